```python
import math
import jax, jax.numpy as jnp
from jax import lax
import numpy as np

D_MODEL = 2048
BATCH = 1
SEQ = 8192
DEPTH = 1

POOL_WIDTH = D_MODEL // 2
POOL_WINDOWS = (2, 4, 8, 16)
N_POOL_GROUPS = len(POOL_WINDOWS)
POOL_GROUP = POOL_WIDTH // N_POOL_GROUPS
N_HEADS = 16
N_KV_GROUPS = 4
HEADS_PER_GROUP = N_HEADS // N_KV_GROUPS
D_QK = 96
D_V = 64
ROT_DIM = D_QK // 4
ROPE_THETA = 500000.0
CMP_LEN = 32
CMP_STRIDE = 16
CMP_HID_K = 4 * D_QK
CMP_HID_V = 4 * D_V
SLC_LEN = 64
N_SELECT = 16
WINDOW = 512
Q_BLOCK = 128
KV_K = N_KV_GROUPS * D_QK
KV_V = N_KV_GROUPS * D_V
N_EXPERTS = 64
TOP_K = 8
N_EXPERT_GROUPS = 8
TOPK_GROUPS = 4
D_EXPERT = D_MODEL // 4
D_SHARED = D_EXPERT
ROUTED_SCALE = 2.5
MOE_BLOCK = 128
RMS_EPS = 1e-6
NEG_INF = -1e30
SEL_FORCE = 1e30
IN_SIZES = (POOL_WIDTH, N_HEADS * D_QK, KV_K, KV_V, KV_K, KV_V, KV_K, KV_V, 3 * N_HEADS, 2 * D_MODEL)
IN_COLS = sum(IN_SIZES)

kernel_name = 'hybrid_pool_nsa_moe_block'


def rms_norm(x, g):
    x32 = x.astype(jnp.float32)
    y = x32 * lax.rsqrt(jnp.mean(x32 * x32, axis=-1, keepdims=True) + RMS_EPS)
    return (y * g.astype(jnp.float32)).astype(x.dtype)


def rope_partial(x, pos):
    half = ROT_DIM // 2
    inv = jnp.power(jnp.float32(ROPE_THETA), -jnp.arange(half, dtype=jnp.float32) * 2.0 / ROT_DIM)
    ang = pos.astype(jnp.float32)[..., None] * inv
    cos = jnp.cos(ang)[:, :, None, :]
    sin = jnp.sin(ang)[:, :, None, :]
    x32 = x.astype(jnp.float32)
    x1 = x32[..., :half]
    x2 = x32[..., half:ROT_DIM]
    out = jnp.concatenate([x1 * cos - x2 * sin, x2 * cos + x1 * sin, x32[..., ROT_DIM:]], axis=-1)
    return out.astype(x.dtype)


def multiscale_pool(u, pool_w, pool_scale):
    B, T, _ = u.shape
    u4 = u.astype(jnp.float32).reshape(B, T, N_POOL_GROUPS, POOL_GROUP)
    cs = jnp.pad(jnp.cumsum(u4, axis=1), ((0, 0), (1, 0), (0, 0), (0, 0)))
    win = jnp.asarray(POOL_WINDOWS, dtype=jnp.int32)
    t = jnp.arange(T, dtype=jnp.int32)
    lo_idx = jnp.maximum(t[:, None] + 1 - win[None, :], 0)
    lo = cs[:, lo_idx, jnp.arange(N_POOL_GROUPS)[None, :], :]
    count = jnp.minimum(t[:, None] + 1, win[None, :]).astype(jnp.float32)
    d = (cs[:, 1:] - lo) / count[None, :, :, None] - u4
    out = jnp.einsum('btgc,gce->btge', d, pool_w.astype(jnp.float32))
    out = out * pool_scale.astype(jnp.float32).reshape(N_POOL_GROUPS, POOL_GROUP)
    return out.reshape(B, T, POOL_WIDTH)


def cmp_to_slc_matrix(n_cmp, n_slc):
    cs = np.arange(n_cmp)[:, None] * CMP_STRIDE
    ss = np.arange(n_slc)[None, :] * SLC_LEN
    ov = np.clip(np.minimum(cs + CMP_LEN, ss + SLC_LEN) - np.maximum(cs, ss), 0, None)
    return jnp.asarray(ov / CMP_LEN, dtype=jnp.float32)


def compress(raw, pos_emb, w1, w2, d):
    B, T = raw.shape[0], raw.shape[1]
    n_cmp = (T - CMP_LEN) // CMP_STRIDE + 1
    blk = np.arange(n_cmp)[:, None] * CMP_STRIDE + np.arange(CMP_LEN)[None, :]
    xb = raw[:, blk] + pos_emb[None, None, :, None, :]
    xb = xb.transpose(0, 1, 3, 2, 4).reshape(B, n_cmp, N_KV_GROUPS, CMP_LEN * d)
    hid = jax.nn.silu(jnp.einsum('bngf,fh->bngh', xb, w1))
    return jnp.einsum('bngh,hd->bngd', hid, w2), blk[:, -1]


def nsa_attention(q_in, kc, vc, ks, vs, kw, vw, g_nsa, positions,
                  cmp_pos_k, cmp_k_w1, cmp_k_w2, cmp_pos_v, cmp_v_w1, cmp_v_w2):
    B, T, _ = q_in.shape
    R = HEADS_PER_GROUP
    G = N_KV_GROUPS
    q = rope_partial(q_in.reshape(B, T, N_HEADS, D_QK), positions) * (D_QK ** -0.5)
    q = q.reshape(B, T, G, R, D_QK).transpose(0, 2, 3, 1, 4)
    k_cmp, cmp_end = compress(kc.reshape(B, T, G, D_QK), cmp_pos_k, cmp_k_w1, cmp_k_w2, D_QK)
    v_cmp, _ = compress(vc.reshape(B, T, G, D_V), cmp_pos_v, cmp_v_w1, cmp_v_w2, D_V)
    k_cmp = rope_partial(k_cmp, positions[:, cmp_end]).transpose(0, 2, 1, 3)
    v_cmp = v_cmp.transpose(0, 2, 1, 3).astype(jnp.float32)
    cmp_end_j = jnp.asarray(cmp_end, dtype=jnp.int32)
    n_slc = T // SLC_LEN
    n_sel = min(N_SELECT, n_slc)
    cmp2slc = cmp_to_slc_matrix(cmp_end.shape[0], n_slc)
    ks_r = rope_partial(ks.reshape(B, T, G, D_QK), positions).transpose(0, 2, 1, 3)
    ks_blk = ks_r.reshape(B, G, n_slc, SLC_LEN, D_QK)
    vs_blk = vs.reshape(B, T, G, D_V).transpose(0, 2, 1, 3).reshape(B, G, n_slc, SLC_LEN, D_V)
    b_ix = jnp.arange(B)[:, None, None, None]
    g_ix = jnp.arange(G)[None, :, None, None]
    kw_r = rope_partial(kw.reshape(B, T, G, D_QK), positions).transpose(0, 2, 1, 3)
    kw_pad = jnp.pad(kw_r, ((0, 0), (0, 0), (WINDOW, 0), (0, 0)))
    vw_pad = jnp.pad(vw.reshape(B, T, G, D_V).transpose(0, 2, 1, 3), ((0, 0), (0, 0), (WINDOW, 0), (0, 0)))
    gates = jax.nn.sigmoid(g_nsa.astype(jnp.float32)).reshape(B, T, G, R, 3).transpose(0, 2, 3, 1, 4)

    def nsa_block(ci):
        s = ci * Q_BLOCK
        t = s + jnp.arange(Q_BLOCK, dtype=jnp.int32)
        qc = lax.dynamic_slice_in_dim(q, s, Q_BLOCK, axis=3)
        gc = lax.dynamic_slice_in_dim(gates, s, Q_BLOCK, axis=3)
        sc = jnp.einsum('bgrtd,bgnd->bgrtn', qc, k_cmp).astype(jnp.float32)
        ok_c = cmp_end_j[None, :] <= t[:, None]
        p_cmp = jax.nn.softmax(jnp.where(ok_c, sc, NEG_INF), axis=-1)
        p_cmp = p_cmp * jnp.any(ok_c, axis=-1)[:, None].astype(jnp.float32)
        o_cmp = jnp.einsum('bgrtn,bgnv->bgrtv', p_cmp, v_cmp)
        imp = jnp.einsum('bgrtn,nj->bgtj', p_cmp, cmp2slc)
        jj = jnp.arange(n_slc, dtype=jnp.int32)[None, :]
        cur = (t // SLC_LEN)[:, None]
        forced = (jj == 0) | (jj == cur) | (jj == cur - 1)
        valid = jj * SLC_LEN <= t[:, None]
        score = jnp.where(valid, jnp.where(forced, SEL_FORCE, imp), NEG_INF)
        top_val, top_idx = lax.top_k(score, n_sel)
        kb = ks_blk[b_ix, g_ix, top_idx]
        vb = vs_blk[b_ix, g_ix, top_idx].astype(jnp.float32)
        kpos = top_idx[..., None] * SLC_LEN + jnp.arange(SLC_LEN, dtype=jnp.int32)
        ok_s = (top_val[..., None] > 0.5 * NEG_INF) & (kpos <= t[None, None, :, None, None])
        ss = jnp.einsum('bgrtd,bgtkld->bgrtkl', qc, kb).astype(jnp.float32)
        ss = jnp.where(ok_s[:, :, None], ss, NEG_INF).reshape(B, G, R, Q_BLOCK, n_sel * SLC_LEN)
        p_s = jax.nn.softmax(ss, axis=-1).reshape(B, G, R, Q_BLOCK, n_sel, SLC_LEN)
        o_slc = jnp.einsum('bgrtkl,bgtklv->bgrtv', p_s, vb)
        kwc = lax.dynamic_slice_in_dim(kw_pad, s, Q_BLOCK + WINDOW, axis=2)
        vwc = lax.dynamic_slice_in_dim(vw_pad, s, Q_BLOCK + WINDOW, axis=2).astype(jnp.float32)
        wpos = (s - WINDOW + jnp.arange(Q_BLOCK + WINDOW, dtype=jnp.int32))[None, :]
        ok_w = (wpos <= t[:, None]) & (wpos > t[:, None] - WINDOW) & (wpos >= 0)
        sw = jnp.einsum('bgrtd,bgsd->bgrts', qc, kwc).astype(jnp.float32)
        p_w = jax.nn.softmax(jnp.where(ok_w, sw, NEG_INF), axis=-1)
        o_win = jnp.einsum('bgrts,bgsv->bgrtv', p_w, vwc)
        return gc[..., 0:1] * o_cmp + gc[..., 1:2] * o_slc + gc[..., 2:3] * o_win

    o = lax.map(nsa_block, jnp.arange(T // Q_BLOCK, dtype=jnp.int32))
    return o.transpose(1, 0, 4, 2, 3, 5).reshape(B, T, N_HEADS * D_V)


def token_mixer(h, positions, w_in, cmp_pos_k, cmp_k_w1, cmp_k_w2, cmp_pos_v, cmp_v_w1, cmp_v_w2,
                pool_w, pool_scale, w_pool_up, w_nsa_up, w_out):
    B, T, D = h.shape
    proj = jnp.einsum('btd,de->bte', h, w_in)
    split_at = [int(v) for v in np.cumsum(IN_SIZES)[:-1]]
    u_pool, q, kc, vc, ks, vs, kw, vw, g_nsa, g_merge = jnp.split(proj, split_at, axis=-1)
    a = multiscale_pool(u_pool, pool_w, pool_scale).astype(h.dtype)
    o = nsa_attention(q, kc, vc, ks, vs, kw, vw, g_nsa, positions,
                      cmp_pos_k, cmp_k_w1, cmp_k_w2, cmp_pos_v, cmp_v_w1, cmp_v_w2).astype(h.dtype)
    gm = jax.nn.sigmoid(g_merge.astype(jnp.float32)).reshape(B, T, 2, D)
    ya = jnp.einsum('btc,cd->btd', a, w_pool_up).astype(jnp.float32)
    yb = jnp.einsum('btc,cd->btd', o, w_nsa_up).astype(jnp.float32)
    y = (gm[:, :, 0] * ya + gm[:, :, 1] * yb).astype(h.dtype)
    return jnp.einsum('btd,de->bte', y, w_out)


def moe_ffn(h, w_router, router_bias, w_exp_gate, w_exp_up, w_exp_down, w_sh_gate, w_sh_up, w_sh_down):
    B, T, D = h.shape
    N = B * T
    xf = h.reshape(N, D)
    aff = jax.nn.sigmoid(jnp.einsum('nd,de->ne', xf.astype(jnp.float32), w_router.astype(jnp.float32)))
    biased = aff + router_bias.astype(jnp.float32)
    per_group = N_EXPERTS // N_EXPERT_GROUPS
    grp_score = lax.top_k(biased.reshape(N, N_EXPERT_GROUPS, per_group), 2)[0].sum(-1)
    _, grp_idx = lax.top_k(grp_score, TOPK_GROUPS)
    grp_keep = jnp.any(grp_idx[:, :, None] == jnp.arange(N_EXPERT_GROUPS)[None, None, :], axis=1)
    keep = jnp.repeat(grp_keep, per_group, axis=1)
    _, e_idx = lax.top_k(jnp.where(keep, biased, NEG_INF), TOP_K)
    w_sel = jnp.take_along_axis(aff, e_idx, axis=1)
    w_sel = w_sel / jnp.sum(w_sel, axis=-1, keepdims=True) * ROUTED_SCALE
    A = N * TOP_K
    e_flat = e_idx.reshape(A)
    tok_flat = jnp.repeat(jnp.arange(N, dtype=jnp.int32), TOP_K)
    w_flat = w_sel.reshape(A)
    order = jnp.argsort(e_flat)
    e_sorted = e_flat[order]
    counts = jnp.bincount(e_flat, length=N_EXPERTS)
    starts = jnp.cumsum(counts) - counts
    padded = (counts + MOE_BLOCK - 1) // MOE_BLOCK * MOE_BLOCK
    pend = jnp.cumsum(padded)
    pstart = pend - padded
    dest = pstart[e_sorted] + jnp.arange(A, dtype=jnp.int32) - starts[e_sorted]
    n_blocks = -(-A // MOE_BLOCK) + N_EXPERTS
    P = n_blocks * MOE_BLOCK
    tok_buf = jnp.full((P,), N, dtype=jnp.int32).at[dest].set(tok_flat[order])
    w_buf = jnp.zeros((P,), jnp.float32).at[dest].set(w_flat[order])
    blk_start = jnp.arange(n_blocks, dtype=pend.dtype) * MOE_BLOCK
    blk_expert = jnp.minimum(jnp.searchsorted(pend, blk_start, side='right'), N_EXPERTS - 1)
    x_pad = jnp.concatenate([xf, jnp.zeros((1, D), xf.dtype)], axis=0)

    def expert_block(args):
        rows, e = args
        xb = x_pad[rows]
        hid = jax.nn.silu(xb @ w_exp_gate[e]) * (xb @ w_exp_up[e])
        return hid @ w_exp_down[e]

    y_buf = lax.map(expert_block, (tok_buf.reshape(n_blocks, MOE_BLOCK), blk_expert))
    routed = jnp.zeros((N + 1, D), jnp.float32).at[tok_buf].add(
        y_buf.reshape(P, D).astype(jnp.float32) * w_buf[:, None])[:N]
    shared = (jax.nn.silu(xf @ w_sh_gate) * (xf @ w_sh_up)) @ w_sh_down
    return (routed + shared.astype(jnp.float32)).astype(h.dtype).reshape(B, T, D)


def setup_inputs(seed: int = 0) -> dict:
    key = jax.random.key(seed)
    k = jax.random.split(key, 32)
    L, D = DEPTH, D_MODEL
    f32 = jnp.float32

    def nrm(kk, shape, fan_in):
        return jax.random.normal(kk, shape, f32) * (fan_in ** -0.5)

    def gain(kk, n):
        return 1.0 + 0.05 * jax.random.normal(kk, (L, n), f32)

    return {
        'x': jax.random.normal(k[0], (BATCH, SEQ, D), f32),
        'c': jax.random.normal(k[1], (BATCH, D), f32),
        'positions': jnp.broadcast_to(jnp.arange(SEQ, dtype=jnp.int32), (BATCH, SEQ)),
        'w_ada': nrm(k[2], (L, D, 6 * D), D),
        'b_ada': 0.01 * jax.random.normal(k[3], (L, 6 * D), f32),
        'g_pre_mix': gain(k[4], D),
        'g_post_mix': gain(k[5], D),
        'g_pre_ffn': gain(k[6], D),
        'g_post_ffn': gain(k[7], D),
        'w_in': nrm(k[8], (L, D, IN_COLS), D),
        'cmp_pos_k': 0.1 * jax.random.normal(k[9], (L, CMP_LEN, D_QK), f32),
        'cmp_k_w1': nrm(k[10], (L, CMP_LEN * D_QK, CMP_HID_K), CMP_LEN * D_QK),
        'cmp_k_w2': nrm(k[11], (L, CMP_HID_K, D_QK), CMP_HID_K),
        'cmp_pos_v': 0.1 * jax.random.normal(k[12], (L, CMP_LEN, D_V), f32),
        'cmp_v_w1': nrm(k[13], (L, CMP_LEN * D_V, CMP_HID_V), CMP_LEN * D_V),
        'cmp_v_w2': nrm(k[14], (L, CMP_HID_V, D_V), CMP_HID_V),
        'pool_w': nrm(k[15], (L, N_POOL_GROUPS, POOL_GROUP, POOL_GROUP), POOL_GROUP),
        'pool_scale': gain(k[16], POOL_WIDTH),
        'w_pool_up': nrm(k[17], (L, POOL_WIDTH, D), POOL_WIDTH),
        'w_nsa_up': nrm(k[18], (L, N_HEADS * D_V, D), N_HEADS * D_V),
        'w_out': nrm(k[19], (L, D, D), D),
        'w_router': nrm(k[20], (L, D, N_EXPERTS), D),
        'router_bias': 0.01 * jax.random.normal(k[21], (L, N_EXPERTS), f32),
        'w_exp_gate': nrm(k[22], (L, N_EXPERTS, D, D_EXPERT), D),
        'w_exp_up': nrm(k[23], (L, N_EXPERTS, D, D_EXPERT), D),
        'w_exp_down': nrm(k[24], (L, N_EXPERTS, D_EXPERT, D), D_EXPERT),
        'w_sh_gate': nrm(k[25], (L, D, D_SHARED), D),
        'w_sh_up': nrm(k[26], (L, D, D_SHARED), D),
        'w_sh_down': nrm(k[27], (L, D_SHARED, D), D_SHARED),
    }


def reference(x, c, positions, w_ada, b_ada, g_pre_mix, g_post_mix, g_pre_ffn, g_post_ffn, w_in,
              cmp_pos_k, cmp_k_w1, cmp_k_w2, cmp_pos_v, cmp_v_w1, cmp_v_w2, pool_w, pool_scale,
              w_pool_up, w_nsa_up, w_out, w_router, router_bias, w_exp_gate, w_exp_up, w_exp_down,
              w_sh_gate, w_sh_up, w_sh_down):
    for l in range(DEPTH):
        mod = jnp.einsum('bd,de->be', jax.nn.silu(c), w_ada[l]) + b_ada[l]
        sh1, sc1, gt1, sh2, sc2, gt2 = jnp.split(mod[:, None, :], 6, axis=-1)
        h = rms_norm(x, g_pre_mix[l]) * (1 + sc1) + sh1
        y = token_mixer(h, positions, w_in[l], cmp_pos_k[l], cmp_k_w1[l], cmp_k_w2[l], cmp_pos_v[l],
                        cmp_v_w1[l], cmp_v_w2[l], pool_w[l], pool_scale[l], w_pool_up[l], w_nsa_up[l], w_out[l])
        x = x + gt1 * rms_norm(y, g_post_mix[l])
        h = rms_norm(x, g_pre_ffn[l]) * (1 + sc2) + sh2
        y = moe_ffn(h, w_router[l], router_bias[l], w_exp_gate[l], w_exp_up[l], w_exp_down[l],
                    w_sh_gate[l], w_sh_up[l], w_sh_down[l])
        x = x + gt2 * rms_norm(y, g_post_ffn[l])
    return x
```

```python
import functools

import numpy as np
import jax
import jax.numpy as jnp
from jax import lax
from jax.experimental import pallas as pl
from jax.experimental.pallas import tpu as pltpu

F32 = jnp.float32
BF16 = jnp.bfloat16

D_MODEL = 2048
POOL_WIDTH = 1024
POOL_WINDOWS = (2, 4, 8, 16)
POOL_GROUP = 256
POOL_HALO = 16
N_HEADS = 16
N_KV_GROUPS = 4
HEADS_PER_GROUP = 4
D_QK = 96
D_V = 64
HEAD_PAD = 128
ROT_HALF = 12
ROPE_THETA = 500000.0
CMP_LEN = 32
CMP_STRIDE = 16
SLC_LEN = 64
N_SELECT = 16
WINDOW = 512
Q_BLOCK = 128
N_EXPERTS = 64
TOP_K = 8
N_EXPERT_GROUPS = 8
TOPK_GROUPS = 4
D_EXPERT = 512
ROUTED_SCALE = 2.5
RMS_EPS = 1e-6
NEG_INF = -1e30
SEL_FORCE = 1e30
REMOVED = -3e38

VMEM_LIMIT = 56 * 1024 * 1024

C_GM = 0
C_Q = 4096
C_U = 6144
C_KC = 7168
C_KS = 7680
C_KW = 8192
C_VC = 8704
C_VS = 9216
C_VW = 9728
C_GN = 10240
N_COLS = 10752


def _params(*sem):
    return pltpu.CompilerParams(dimension_semantics=sem, vmem_limit_bytes=VMEM_LIMIT)


def _const_spec(shape):
    nd = len(shape)
    return pl.BlockSpec(shape, lambda *_: (0,) * nd)


ADA_TN = 1536
ADA_ROWS = 64


def _ada_kernel(c_ref, w_ref, b_ref, o_ref, s_ref):
    c = c_ref[...]
    s_ref[...] = c * jax.nn.sigmoid(c)

    def body(k, acc):
        r = pl.multiple_of(k * ADA_ROWS, ADA_ROWS)
        for j in range(ADA_ROWS // 8):
            acc = acc + w_ref[pl.ds(r + 8 * j, 8), :] * s_ref[pl.ds(r + 8 * j, 8), :]
        return acc

    acc = lax.fori_loop(0, D_MODEL // ADA_ROWS, body, jnp.zeros((8, ADA_TN), F32))
    o_ref[...] = jnp.sum(acc, axis=0, keepdims=True) + b_ref[...]


def _ada(c_col, w_ada, b_ada):
    n = w_ada.shape[1]
    return pl.pallas_call(
        _ada_kernel,
        grid=(n // ADA_TN,),
        in_specs=[_const_spec((D_MODEL, 1)),
                  pl.BlockSpec((D_MODEL, ADA_TN), lambda j: (0, j)),
                  pl.BlockSpec((1, ADA_TN), lambda j: (0, j))],
        out_specs=pl.BlockSpec((1, ADA_TN), lambda j: (0, j)),
        out_shape=jax.ShapeDtypeStruct((1, n), F32),
        scratch_shapes=[pltpu.VMEM((D_MODEL, 1), F32)],
        compiler_params=_params("arbitrary"),
        name="ada",
    )(c_col, w_ada, b_ada)


def _mod_spec(k):
    return pl.BlockSpec((1, D_MODEL), lambda *_: (0, k))


PROJ_TM = 512
PROJ_TN = 1536


def _modulated_norm(x, g, sc, sh):
    y = x * lax.rsqrt(jnp.mean(x * x, axis=-1, keepdims=True) + RMS_EPS)
    return (y * g) * (1.0 + sc) + sh


def _proj_kernel(x_ref, g_ref, sc_ref, sh_ref, w_ref, o_ref, h_ref):
    @pl.when(pl.program_id(1) == 0)
    def _():
        h_ref[...] = _modulated_norm(x_ref[...], g_ref[...], sc_ref[...], sh_ref[...]).astype(BF16)

    o_ref[...] = jnp.dot(h_ref[...], w_ref[...], preferred_element_type=F32)


def _proj(x2, g_pre, mod, w_in_p):
    T = x2.shape[0]
    tm = min(PROJ_TM, T)
    return pl.pallas_call(
        _proj_kernel,
        grid=(T // tm, N_COLS // PROJ_TN),
        in_specs=[pl.BlockSpec((tm, D_MODEL), lambda i, j: (i, 0)),
                  _const_spec((1, D_MODEL)), _mod_spec(1), _mod_spec(0),
                  pl.BlockSpec((D_MODEL, PROJ_TN), lambda i, j: (0, j))],
        out_specs=pl.BlockSpec((tm, PROJ_TN), lambda i, j: (i, j)),
        out_shape=jax.ShapeDtypeStruct((T, N_COLS), F32),
        scratch_shapes=[pltpu.VMEM((tm, D_MODEL), BF16)],
        compiler_params=_params("arbitrary", "arbitrary"),
        name="in_proj",
    )(x2, g_pre, mod, mod, w_in_p)


def _rope_tables(pos_col, invf):
    ang = pos_col.astype(F32) * invf
    cos = jnp.cos(ang)
    sin = jnp.sin(ang)
    lane = lax.broadcasted_iota(jnp.int32, ang.shape, 1)
    sin_lo = jnp.where(lane < ROT_HALF, -sin, 0.0)
    sin_hi = jnp.where((lane >= ROT_HALF) & (lane < 2 * ROT_HALF), sin, 0.0)
    return cos, sin_lo, sin_hi


def _rope_head(x, tabs):
    cos, sin_lo, sin_hi = tabs
    return (x * cos + pltpu.roll(x, HEAD_PAD - ROT_HALF, 1) * sin_lo
            + pltpu.roll(x, ROT_HALF, 1) * sin_hi)


PREP_TM = 256


def _prep_kernel(pos_ref, invf_ref, q_ref, ks_ref, kw_ref, vs_ref, vw_ref,
                 qo_ref, kso_ref, kwo_ref, vso_ref, vwo_ref):
    tabs = _rope_tables(pos_ref[...], invf_ref[...])
    for h in range(N_HEADS):
        sl = slice(h * HEAD_PAD, (h + 1) * HEAD_PAD)
        qo_ref[:, sl] = (_rope_head(q_ref[:, sl], tabs) * (D_QK ** -0.5)).astype(BF16)
    for g in range(N_KV_GROUPS):
        sl = slice(g * HEAD_PAD, (g + 1) * HEAD_PAD)
        kso_ref[:, sl] = _rope_head(ks_ref[:, sl], tabs).astype(BF16)
        kwo_ref[:, sl] = _rope_head(kw_ref[:, sl], tabs).astype(BF16)
    vso_ref[...] = vs_ref[...].astype(BF16)
    vwo_ref[...] = vw_ref[...].astype(BF16)


def _prep(proj, pos_col, invf):
    T = proj.shape[0]
    tm = min(PREP_TM, T)
    kv = N_KV_GROUPS * HEAD_PAD
    qw = N_HEADS * HEAD_PAD

    def col(width, off):
        return pl.BlockSpec((tm, width), lambda i: (i, off // width))

    def out(width):
        return pl.BlockSpec((tm, width), lambda i: (i, 0))

    return pl.pallas_call(
        _prep_kernel,
        grid=(T // tm,),
        in_specs=[pl.BlockSpec((tm, 1), lambda i: (i, 0)), _const_spec((1, HEAD_PAD)),
                  col(qw, C_Q), col(kv, C_KS), col(kv, C_KW), col(kv, C_VS), col(kv, C_VW)],
        out_specs=[out(qw), out(kv), out(kv), out(kv), out(kv)],
        out_shape=[jax.ShapeDtypeStruct((T, qw), BF16)] + [jax.ShapeDtypeStruct((T, kv), BF16)] * 4,
        compiler_params=_params("arbitrary"),
        name="nsa_prep",
    )(pos_col, invf, proj, proj, proj, proj, proj)


def _cmp_kernel(x_ref, pe_ref, w1_ref, w2_ref, pos_ref, invf_ref, o_ref, *, rope):
    nchunk = o_ref.shape[1]
    hid_dim = w1_ref.shape[2]
    first = jnp.zeros((nchunk, hid_dim), F32)
    second = jnp.zeros((nchunk, hid_dim), F32)
    for l in range(CMP_STRIDE):
        rows = x_ref[pl.ds(l, nchunk, stride=CMP_STRIDE), :]
        first += jnp.dot((rows + pe_ref[l:l + 1, :]).astype(BF16), w1_ref[l],
                         preferred_element_type=F32)
        second += jnp.dot((rows + pe_ref[CMP_STRIDE + l:CMP_STRIDE + l + 1, :]).astype(BF16),
                          w1_ref[CMP_STRIDE + l], preferred_element_type=F32)
    second = jnp.concatenate([second[1:], jnp.zeros((1, hid_dim), F32)], axis=0)
    pre = first + second
    hid = pre * jax.nn.sigmoid(pre)
    out = jnp.dot(hid.astype(BF16), w2_ref[...], preferred_element_type=F32)
    if rope:
        out = _rope_head(out, _rope_tables(pos_ref[...], invf_ref[...]))
    o_ref[0] = out.astype(BF16)


def _compress(proj, col_off, pe_p, w1_p, w2_p, pos_cmp, invf, rope):
    T = proj.shape[0]
    nchunk = T // CMP_STRIDE
    hid = w1_p.shape[2]
    return pl.pallas_call(
        functools.partial(_cmp_kernel, rope=rope),
        grid=(N_KV_GROUPS,),
        in_specs=[pl.BlockSpec((T, HEAD_PAD), lambda g: (0, col_off // HEAD_PAD + g)),
                  _const_spec((CMP_LEN, HEAD_PAD)), _const_spec((CMP_LEN, HEAD_PAD, hid)),
                  _const_spec((hid, HEAD_PAD)), _const_spec((nchunk, 1)), _const_spec((1, HEAD_PAD))],
        out_specs=pl.BlockSpec((1, nchunk, HEAD_PAD), lambda g: (g, 0, 0)),
        out_shape=jax.ShapeDtypeStruct((N_KV_GROUPS, nchunk, HEAD_PAD), BF16),
        compiler_params=_params("arbitrary"),
        name="compress",
    )(proj, pe_p, w1_p, w2_p, pos_cmp, invf)


SLC_TK = 512
WIN_KEYS = WINDOW + Q_BLOCK
R = HEADS_PER_GROUP


def _softmax_rows(s):
    m = jnp.max(s, axis=-1, keepdims=True)
    p = jnp.exp(s - m)
    return p / jnp.sum(p, axis=-1, keepdims=True)


def _attn_kernel(q_ref, kc_ref, vc_ref, ks_ref, vs_ref, kw_ref, vw_ref, gate_ref, c2s_ref, o_ref,
                 acc_ref, m_ref, l_ref, *, n_sel):
    ci = pl.program_id(1)
    s = ci * Q_BLOCK
    rows = R * Q_BLOCK
    dn = (((1,), (1,)), ((), ()))
    q4 = jnp.concatenate([q_ref[:, r * HEAD_PAD:(r + 1) * HEAD_PAD] for r in range(R)], axis=0)
    t1 = s + lax.broadcasted_iota(jnp.int32, (Q_BLOCK, 1), 0)
    t4 = jnp.concatenate([t1] * R, axis=0)

    n_cmp = kc_ref.shape[1]
    sc = lax.dot_general(q4, kc_ref[0], dn, preferred_element_type=F32)
    cmp_end = lax.broadcasted_iota(jnp.int32, (1, n_cmp), 1) * CMP_STRIDE + (CMP_LEN - 1)
    p_cmp = _softmax_rows(jnp.where(cmp_end <= t4, sc, NEG_INF))
    p_cmp = p_cmp * (t4 >= CMP_LEN - 1).astype(F32)
    o_cmp = jnp.dot(p_cmp.astype(BF16), vc_ref[0], preferred_element_type=F32)

    p_sum = p_cmp[0:Q_BLOCK]
    for r in range(1, R):
        p_sum = p_sum + p_cmp[r * Q_BLOCK:(r + 1) * Q_BLOCK]
    p_hi = p_sum.astype(BF16)
    p_lo = (p_sum - p_hi.astype(F32)).astype(BF16)
    imp = (jnp.dot(p_hi, c2s_ref[...], preferred_element_type=F32)
           + jnp.dot(p_lo, c2s_ref[...], preferred_element_type=F32))

    n_lane = c2s_ref.shape[1]
    jj = lax.broadcasted_iota(jnp.int32, (1, n_lane), 1)
    cur = t1 >> 6
    forced = (jj == 0) | (jj == cur) | (jj == cur - 1)
    score = jnp.where(jj <= cur, jnp.where(forced, SEL_FORCE, imp), NEG_INF)
    sel = jnp.zeros((Q_BLOCK, n_lane), F32)
    for _ in range(n_sel):
        best = jnp.max(score, axis=-1, keepdims=True)
        idx = jnp.min(jnp.where(score == best, jj, n_lane), axis=-1, keepdims=True)
        hit = jj == idx
        sel = jnp.where(hit & (best > 0.5 * NEG_INF), 1.0, sel)
        score = jnp.where(hit, REMOVED, score)
    sel_bf = sel.astype(BF16)

    acc_ref[...] = jnp.zeros_like(acc_ref)
    m_ref[...] = jnp.full_like(m_ref, NEG_INF)
    l_ref[...] = jnp.zeros_like(l_ref)
    blk_row = lax.broadcasted_iota(jnp.int32, (n_lane, 1), 0)

    def tile(jt, carry):
        k0 = pl.multiple_of(jt * SLC_TK, SLC_TK)
        kpos = k0 + lax.broadcasted_iota(jnp.int32, (1, SLC_TK), 1)
        expand = jnp.where((kpos >> 6) == blk_row, 1.0, 0.0).astype(BF16)
        picked = jnp.dot(sel_bf, expand, preferred_element_type=F32)
        bias = jnp.where((picked > 0.5) & (kpos <= t1), 0.0, NEG_INF)
        st = lax.dot_general(q4, ks_ref[pl.ds(k0, SLC_TK), :], dn, preferred_element_type=F32)
        st = st + jnp.concatenate([bias] * R, axis=0)
        m_old = m_ref[...]
        m_new = jnp.maximum(m_old, jnp.max(st, axis=-1, keepdims=True))
        alpha = jnp.exp(m_old - m_new)
        p = jnp.exp(st - m_new)
        l_ref[...] = alpha * l_ref[...] + jnp.sum(p, axis=-1, keepdims=True)
        acc_ref[...] = alpha * acc_ref[...] + jnp.dot(p.astype(BF16), vs_ref[pl.ds(k0, SLC_TK), :],
                                                      preferred_element_type=F32)
        m_ref[...] = m_new
        return carry

    lax.fori_loop(0, (s + Q_BLOCK + SLC_TK - 1) // SLC_TK, tile, 0)
    o_slc = acc_ref[...] / l_ref[...]

    w0 = pl.multiple_of(jnp.maximum(s - WINDOW, 0), Q_BLOCK)
    wpos = w0 + lax.broadcasted_iota(jnp.int32, (1, WIN_KEYS), 1)
    sw = lax.dot_general(q4, kw_ref[pl.ds(w0, WIN_KEYS), :], dn, preferred_element_type=F32)
    p_w = _softmax_rows(jnp.where((wpos <= t4) & (wpos > t4 - WINDOW), sw, NEG_INF))
    o_win = jnp.dot(p_w.astype(BF16), vw_ref[pl.ds(w0, WIN_KEYS), :], preferred_element_type=F32)

    gate = jax.nn.sigmoid(gate_ref[...])
    for r in range(R):
        rs = slice(r * Q_BLOCK, (r + 1) * Q_BLOCK)
        o = (gate[:, 3 * r:3 * r + 1] * o_cmp[rs] + gate[:, 3 * r + 1:3 * r + 2] * o_slc[rs]
             + gate[:, 3 * r + 2:3 * r + 3] * o_win[rs])
        o_ref[:, r * HEAD_PAD:(r + 1) * HEAD_PAD] = o.astype(BF16)


def _attention(q_r, k_cmp, v_cmp, ks_r, vs_b, kw_r, vw_b, proj, c2s):
    T = q_r.shape[0]
    n_cmp = k_cmp.shape[1]
    n_lane = c2s.shape[1]
    n_sel = min(N_SELECT, T // SLC_LEN)
    gw = R * HEAD_PAD

    def seq(g, ci):
        return (0, g)

    return pl.pallas_call(
        functools.partial(_attn_kernel, n_sel=n_sel),
        grid=(N_KV_GROUPS, T // Q_BLOCK),
        in_specs=[pl.BlockSpec((Q_BLOCK, gw), lambda g, ci: (ci, g)),
                  pl.BlockSpec((1, n_cmp, HEAD_PAD), lambda g, ci: (g, 0, 0)),
                  pl.BlockSpec((1, n_cmp, HEAD_PAD), lambda g, ci: (g, 0, 0)),
                  pl.BlockSpec((T, HEAD_PAD), seq), pl.BlockSpec((T, HEAD_PAD), seq),
                  pl.BlockSpec((T, HEAD_PAD), seq), pl.BlockSpec((T, HEAD_PAD), seq),
                  pl.BlockSpec((Q_BLOCK, HEAD_PAD), lambda g, ci: (ci, C_GN // HEAD_PAD + g)),
                  _const_spec((n_cmp, n_lane))],
        out_specs=pl.BlockSpec((Q_BLOCK, gw), lambda g, ci: (ci, g)),
        out_shape=jax.ShapeDtypeStruct((T, N_HEADS * HEAD_PAD), BF16),
        scratch_shapes=[pltpu.VMEM((R * Q_BLOCK, HEAD_PAD), F32),
                        pltpu.VMEM((R * Q_BLOCK, 1), F32), pltpu.VMEM((R * Q_BLOCK, 1), F32)],
        compiler_params=_params("arbitrary", "arbitrary"),
        name="nsa_attention",
    )(q_r, k_cmp, v_cmp, ks_r, vs_b, kw_r, vw_b, proj, c2s)


MIX_TM = 256


def _mix_kernel(u_ref, up_ref, gm_ref, o_ref, pw_ref, ps_ref, wpu_ref, wnu_ref, y_ref):
    i = pl.program_id(0)
    tm = u_ref.shape[0]
    u = u_ref[...]
    prev = up_ref[...] * (i > 0).astype(F32)
    ext = jnp.concatenate([prev, u], axis=0)
    t = i * tm + lax.broadcasted_iota(jnp.int32, (tm, 1), 0)
    parts = []
    for g, win in enumerate(POOL_WINDOWS):
        cs = slice(g * POOL_GROUP, (g + 1) * POOL_GROUP)
        run = ext[:, cs]
        k = 1
        while k < win:
            run = run[k:] + run[:-k]
            k *= 2
        lo = POOL_HALO - (win - 1)
        total = run[lo:lo + tm]
        count = jnp.minimum(t + 1, win).astype(F32)
        d = total / count - u[:, cs]
        parts.append(jnp.dot(d.astype(BF16), pw_ref[g], preferred_element_type=F32) * ps_ref[:, cs])
    a = jnp.concatenate(parts, axis=1).astype(BF16)
    ya = jnp.dot(a, wpu_ref[...], preferred_element_type=F32)
    yb = jnp.dot(o_ref[...], wnu_ref[...], preferred_element_type=F32)
    gm = jax.nn.sigmoid(gm_ref[...])
    y_ref[...] = (gm[:, :D_MODEL] * ya + gm[:, D_MODEL:] * yb).astype(BF16)


def _mix(proj, o, pool_w_b, pool_scale, w_pool_up_b, w_nsa_up_p):
    T = proj.shape[0]
    tm = min(MIX_TM, T)
    halo_blocks = tm // POOL_HALO
    return pl.pallas_call(
        _mix_kernel,
        grid=(T // tm,),
        in_specs=[pl.BlockSpec((tm, POOL_WIDTH), lambda i: (i, C_U // POOL_WIDTH)),
                  pl.BlockSpec((POOL_HALO, POOL_WIDTH),
                               lambda i: (jnp.maximum(i * halo_blocks - 1, 0), C_U // POOL_WIDTH)),
                  pl.BlockSpec((tm, 2 * D_MODEL), lambda i: (i, 0)),
                  pl.BlockSpec((tm, N_HEADS * HEAD_PAD), lambda i: (i, 0)),
                  _const_spec(pool_w_b.shape), _const_spec((1, POOL_WIDTH)),
                  _const_spec(w_pool_up_b.shape), _const_spec(w_nsa_up_p.shape)],
        out_specs=pl.BlockSpec((tm, D_MODEL), lambda i: (i, 0)),
        out_shape=jax.ShapeDtypeStruct((T, D_MODEL), BF16),
        compiler_params=_params("arbitrary"),
        name="pool_merge",
    )(proj, proj, proj, o, pool_w_b, pool_scale, w_pool_up_b, w_nsa_up_p)


OUT_TM = 256


def _rms(y, g):
    return y * lax.rsqrt(jnp.mean(y * y, axis=-1, keepdims=True) + RMS_EPS) * g


def _outproj_kernel(y_ref, w_ref, x_ref, g_ref, gt_ref, o_ref):
    z = jnp.dot(y_ref[...], w_ref[...], preferred_element_type=F32)
    o_ref[...] = x_ref[...] + gt_ref[...] * _rms(z, g_ref[...])


def _outproj(y, w_out_b, x2, g_post, mod):
    T = y.shape[0]
    tm = min(OUT_TM, T)
    row = pl.BlockSpec((tm, D_MODEL), lambda i: (i, 0))
    return pl.pallas_call(
        _outproj_kernel,
        grid=(T // tm,),
        in_specs=[row, _const_spec((D_MODEL, D_MODEL)), row, _const_spec((1, D_MODEL)), _mod_spec(2)],
        out_specs=row,
        out_shape=jax.ShapeDtypeStruct((T, D_MODEL), F32),
        compiler_params=_params("arbitrary"),
        name="out_proj",
    )(y, w_out_b, x2, g_post, mod)


def _pad_heads(w, n_heads, d_head):
    k = w.shape[0]
    w = w.reshape(k, n_heads, d_head)
    return jnp.pad(w, ((0, 0), (0, 0), (0, HEAD_PAD - d_head))).reshape(k, n_heads * HEAD_PAD)


def _pack_w_in(w_in):
    sizes = (POOL_WIDTH, N_HEADS * D_QK, N_KV_GROUPS * D_QK, N_KV_GROUPS * D_V, N_KV_GROUPS * D_QK,
             N_KV_GROUPS * D_V, N_KV_GROUPS * D_QK, N_KV_GROUPS * D_V, 3 * N_HEADS, 2 * D_MODEL)
    offs = np.cumsum((0,) + sizes)
    u, q, kc, vc, ks, vs, kw, vw, gn, gm = [w_in[:, offs[i]:offs[i + 1]] for i in range(len(sizes))]
    cols = [gm, _pad_heads(q, N_HEADS, D_QK), u,
            _pad_heads(kc, N_KV_GROUPS, D_QK), _pad_heads(ks, N_KV_GROUPS, D_QK),
            _pad_heads(kw, N_KV_GROUPS, D_QK), _pad_heads(vc, N_KV_GROUPS, D_V),
            _pad_heads(vs, N_KV_GROUPS, D_V), _pad_heads(vw, N_KV_GROUPS, D_V),
            _pad_heads(gn, N_KV_GROUPS, 3 * HEADS_PER_GROUP)]
    return jnp.concatenate(cols, axis=1).astype(BF16)


def _cmp_to_slc(n_chunk, n_lane):
    cs = np.arange(n_chunk)[:, None] * CMP_STRIDE
    ss = np.arange(n_lane)[None, :] * SLC_LEN
    ov = np.clip(np.minimum(cs + CMP_LEN, ss + SLC_LEN) - np.maximum(cs, ss), 0, None)
    return jnp.asarray(ov / CMP_LEN, dtype=BF16)


def _token_mixer_block(x2, pos_col, mod, g_pre, g_post, w_in, cmp_pos_k, cmp_k_w1, cmp_k_w2,
                       cmp_pos_v, cmp_v_w1, cmp_v_w2, pool_w, pool_scale, w_pool_up, w_nsa_up, w_out):
    T = x2.shape[0]
    half = jnp.arange(ROT_HALF, dtype=F32)
    inv = jnp.power(jnp.float32(ROPE_THETA), -half * 2.0 / (2 * ROT_HALF))
    invf = jnp.concatenate([inv, inv, jnp.zeros((HEAD_PAD - 2 * ROT_HALF,), F32)])[None, :]

    proj = _proj(x2, g_pre, mod, _pack_w_in(w_in))
    q_r, ks_r, kw_r, vs_b, vw_b = _prep(proj, pos_col, invf)

    n_chunk = T // CMP_STRIDE
    pos_cmp = jnp.pad(pos_col[CMP_LEN - 1::CMP_STRIDE], ((0, 1), (0, 0)))

    def pad_lanes(a, d):
        return jnp.pad(a, [(0, 0)] * (a.ndim - 1) + [(0, HEAD_PAD - d)])

    def w1_pack(w1, d):
        w1 = w1.reshape(CMP_LEN, d, w1.shape[1])
        return jnp.pad(w1, ((0, 0), (0, HEAD_PAD - d), (0, 0))).astype(BF16)

    k_cmp = _compress(proj, C_KC, pad_lanes(cmp_pos_k, D_QK), w1_pack(cmp_k_w1, D_QK),
                      pad_lanes(cmp_k_w2, D_QK).astype(BF16), pos_cmp, invf, True)
    v_cmp = _compress(proj, C_VC, pad_lanes(cmp_pos_v, D_V), w1_pack(cmp_v_w1, D_V),
                      pad_lanes(cmp_v_w2, D_V).astype(BF16), pos_cmp, invf, False)

    n_lane = max(T // SLC_LEN, HEAD_PAD)
    o = _attention(q_r, k_cmp, v_cmp, ks_r, vs_b, kw_r, vw_b, proj, _cmp_to_slc(n_chunk, n_lane))

    w_nsa_up_p = jnp.pad(w_nsa_up.reshape(N_HEADS, D_V, D_MODEL),
                         ((0, 0), (0, HEAD_PAD - D_V), (0, 0))).reshape(N_HEADS * HEAD_PAD, D_MODEL)
    y = _mix(proj, o, pool_w.astype(BF16), pool_scale, w_pool_up.astype(BF16), w_nsa_up_p.astype(BF16))
    return _outproj(y, w_out.astype(BF16), x2, g_post, mod)


ROUTE_TM = 256
EXPERTS_PER_GROUP = N_EXPERTS // N_EXPERT_GROUPS


def _first_index_of(mask, lane, n):
    return jnp.min(jnp.where(mask, lane, n), axis=-1, keepdims=True)


def _route_kernel(x_ref, g_ref, sc_ref, sh_ref, wr_ref, rb_ref,
                  hf_ref, hb_ref, sel_ref, wd_ref, rank_ref, cnt_ref, carry_ref):
    i = pl.program_id(0)
    tm = x_ref.shape[0]

    @pl.when(i == 0)
    def _():
        carry_ref[...] = jnp.zeros_like(carry_ref)

    h = _modulated_norm(x_ref[...], g_ref[...], sc_ref[...], sh_ref[...])
    hf_ref[...] = h
    hb_ref[...] = h.astype(BF16)
    logits = jnp.dot(h, wr_ref[...], precision=lax.Precision.HIGHEST, preferred_element_type=F32)
    aff = jax.nn.sigmoid(logits)
    biased = aff + rb_ref[...]
    lane = lax.broadcasted_iota(jnp.int32, (1, N_EXPERTS), 1)
    grp = lane // EXPERTS_PER_GROUP

    gs = []
    for gi in range(N_EXPERT_GROUPS):
        v = jnp.where(grp == gi, biased, REMOVED)
        m1 = jnp.max(v, axis=-1, keepdims=True)
        i1 = _first_index_of(v == m1, lane, N_EXPERTS)
        m2 = jnp.max(jnp.where(lane == i1, REMOVED, v), axis=-1, keepdims=True)
        gs.append(m1 + m2)
    keep = jnp.zeros((tm, N_EXPERTS), jnp.bool_)
    for gi in range(N_EXPERT_GROUPS):
        ahead = jnp.zeros((tm, 1), jnp.int32)
        for gj in range(N_EXPERT_GROUPS):
            if gj < gi:
                ahead += (gs[gj] >= gs[gi]).astype(jnp.int32)
            elif gj > gi:
                ahead += (gs[gj] > gs[gi]).astype(jnp.int32)
        keep = keep | ((grp == gi) & (ahead < TOPK_GROUPS))
    masked = jnp.where(keep, biased, NEG_INF)
    sel = jnp.zeros((tm, N_EXPERTS), F32)
    for _ in range(TOP_K):
        best = jnp.max(masked, axis=-1, keepdims=True)
        hit = lane == _first_index_of(masked == best, lane, N_EXPERTS)
        sel = jnp.where(hit, 1.0, sel)
        masked = jnp.where(hit, REMOVED, masked)
    w = aff * sel
    wd_ref[...] = w / jnp.sum(w, axis=-1, keepdims=True) * ROUTED_SCALE
    sel_ref[...] = sel

    r_i = lax.broadcasted_iota(jnp.int32, (tm, tm), 0)
    c_i = lax.broadcasted_iota(jnp.int32, (tm, tm), 1)
    before = jnp.where(c_i < r_i, 1.0, 0.0).astype(BF16)
    carry = carry_ref[...]
    rank_ref[...] = jnp.dot(before, sel.astype(BF16), preferred_element_type=F32) + carry
    carry = carry + jnp.sum(sel, axis=0, keepdims=True)
    carry_ref[...] = carry
    cnt_ref[...] = carry


def _route(x1, g_pre, mod, w_router, router_bias):
    T = x1.shape[0]
    tm = min(ROUTE_TM, T)
    row = pl.BlockSpec((tm, D_MODEL), lambda i: (i, 0))
    erow = pl.BlockSpec((tm, N_EXPERTS), lambda i: (i, 0))
    e_shape = jax.ShapeDtypeStruct((T, N_EXPERTS), F32)
    return pl.pallas_call(
        _route_kernel,
        grid=(T // tm,),
        in_specs=[row, _const_spec((1, D_MODEL)), _mod_spec(4), _mod_spec(3),
                  _const_spec((D_MODEL, N_EXPERTS)), _const_spec((1, N_EXPERTS))],
        out_specs=[row, row, erow, erow, erow, _const_spec((1, N_EXPERTS))],
        out_shape=[jax.ShapeDtypeStruct((T, D_MODEL), F32), jax.ShapeDtypeStruct((T, D_MODEL), BF16),
                   e_shape, e_shape, e_shape, jax.ShapeDtypeStruct((1, N_EXPERTS), F32)],
        scratch_shapes=[pltpu.VMEM((1, N_EXPERTS), F32)],
        compiler_params=_params("arbitrary"),
        name="moe_route",
    )(x1, g_pre, mod, mod, w_router, router_bias)


def _dest_kernel(sel_ref, wd_ref, rank_ref, ps_ref, dest_ref, w8_ref):
    tm = sel_ref.shape[0]
    slot = ps_ref[...] + rank_ref[...]
    wd = wd_ref[...]
    lane = lax.broadcasted_iota(jnp.int32, (1, N_EXPERTS), 1)
    lane_k = lax.broadcasted_iota(jnp.int32, (1, TOP_K), 1)
    dest = jnp.zeros((tm, TOP_K), F32)
    w8 = jnp.zeros((tm, TOP_K), F32)
    rem = sel_ref[...]
    for k in range(TOP_K):
        hit = lane == _first_index_of(rem > 0.5, lane, N_EXPERTS)
        dest = jnp.where(lane_k == k, jnp.sum(jnp.where(hit, slot, 0.0), axis=-1, keepdims=True), dest)
        w8 = jnp.where(lane_k == k, jnp.sum(jnp.where(hit, wd, 0.0), axis=-1, keepdims=True), w8)
        rem = jnp.where(hit, 0.0, rem)
    dest_ref[...] = dest.astype(jnp.int32)
    w8_ref[...] = w8


def _dest(sel, wd, rank, pstart):
    T = sel.shape[0]
    tm = min(ROUTE_TM, T)
    erow = pl.BlockSpec((tm, N_EXPERTS), lambda i: (i, 0))
    krow = pl.BlockSpec((tm, TOP_K), lambda i: (i, 0))
    return pl.pallas_call(
        _dest_kernel,
        grid=(T // tm,),
        in_specs=[erow, erow, erow, _const_spec((1, N_EXPERTS))],
        out_specs=[krow, krow],
        out_shape=[jax.ShapeDtypeStruct((T, TOP_K), jnp.int32), jax.ShapeDtypeStruct((T, TOP_K), F32)],
        compiler_params=_params("arbitrary"),
        name="moe_dest",
    )(sel, wd, rank, pstart)


DISPATCH_TM = 128


def _dispatch_kernel(dest_ref, h_ref, xs_hbm, sem):
    tm = h_ref.shape[0]

    def body(t, carry):
        for k in range(TOP_K):
            d = dest_ref[t * TOP_K + k]
            pltpu.make_async_copy(h_ref.at[pl.ds(t, 1), :], xs_hbm.at[pl.ds(d, 1), :], sem).start()
        return carry

    lax.fori_loop(0, tm, body, 0)
    for k in range(TOP_K):
        pltpu.make_async_copy(h_ref, xs_hbm.at[pl.ds(0, tm), :], sem).wait()


def _dispatch(dest_flat, hf, n_slots):
    T = hf.shape[0]
    tm = min(DISPATCH_TM, T)
    return pl.pallas_call(
        _dispatch_kernel,
        grid=(T // tm,),
        in_specs=[pl.BlockSpec((tm * TOP_K,), lambda i: (i,), memory_space=pltpu.SMEM),
                  pl.BlockSpec((tm, D_MODEL), lambda i: (i, 0))],
        out_specs=pl.BlockSpec(memory_space=pl.ANY),
        out_shape=jax.ShapeDtypeStruct((n_slots, D_MODEL), F32),
        scratch_shapes=[pltpu.SemaphoreType.DMA(())],
        compiler_params=pltpu.CompilerParams(dimension_semantics=("arbitrary",), vmem_limit_bytes=VMEM_LIMIT,
                                             has_side_effects=True),
        name="moe_dispatch",
    )(dest_flat, hf)


EXPERT_BLOCK = 256


def _swiglu(x, wg, wu, wd):
    g = jnp.dot(x, wg, preferred_element_type=F32)
    u = jnp.dot(x, wu, preferred_element_type=F32)
    hid = (g * jax.nn.sigmoid(g)) * u
    return jnp.dot(hid.astype(BF16), wd, preferred_element_type=F32)


def _expert_items(counts, n_items):
    ends = jnp.cumsum(counts)
    starts = ends - counts
    first_blk = starts // EXPERT_BLOCK
    last_blk = (ends - 1) // EXPERT_BLOCK
    per_expert = jnp.where(counts > 0, last_blk - first_blk + 1, 0)
    item_end = jnp.cumsum(per_expert)
    n_used = item_end[-1]
    i = jnp.minimum(jnp.arange(n_items, dtype=jnp.int32), n_used - 1)
    e = jnp.minimum(jnp.searchsorted(item_end, i, side='right'), N_EXPERTS - 1).astype(jnp.int32)
    blk = first_blk[e] + i - (item_end[e] - per_expert[e])
    lo = jnp.clip(starts[e] - blk * EXPERT_BLOCK, 0, EXPERT_BLOCK)
    hi = jnp.clip(ends[e] - blk * EXPERT_BLOCK, 0, EXPERT_BLOCK)
    live = jnp.arange(n_items, dtype=jnp.int32) < n_used
    hi = jnp.where(live, hi, lo)
    new_expert = jnp.concatenate([jnp.ones((1,), jnp.bool_), e[1:] != e[:-1]])
    new_block = jnp.concatenate([jnp.ones((1,), jnp.bool_), blk[1:] != blk[:-1]])
    as_i32 = lambda a: a.astype(jnp.int32)
    return (as_i32(blk), e, as_i32(lo), as_i32(hi), as_i32(new_expert & live), as_i32(new_block & live),
            as_i32(live), starts)


def _expert_kernel(blk_ref, e_ref, lo_ref, hi_ref, newe_ref, newb_ref, live_ref,
                   x_ref, wg_ref, wu_ref, wd_ref, y_ref, wgb, wub, wdb):
    i = pl.program_id(0)

    @pl.when(live_ref[i] == 1)
    def _():
        @pl.when(newe_ref[i] == 1)
        def _():
            wgb[...] = wg_ref[0].astype(BF16)
            wub[...] = wu_ref[0].astype(BF16)
            wdb[...] = wd_ref[0].astype(BF16)

        row = lax.broadcasted_iota(jnp.int32, (EXPERT_BLOCK, 1), 0)
        mine = (row >= lo_ref[i]) & (row < hi_ref[i])
        y = _swiglu(jnp.where(mine, x_ref[...], 0.0).astype(BF16), wgb[...], wub[...], wdb[...])

        @pl.when(newb_ref[i] == 1)
        def _():
            y_ref[...] = y

        @pl.when(newb_ref[i] == 0)
        def _():
            y_ref[...] += y


def _experts(items, xs, w_gate, w_up, w_down):
    n_items = items[0].shape[0]
    by_block = lambda i, blk, e, *_: (blk[i], 0)
    by_expert = lambda i, blk, e, *_: (e[i], 0, 0)
    grid_spec = pltpu.PrefetchScalarGridSpec(
        num_scalar_prefetch=len(items),
        grid=(n_items,),
        in_specs=[pl.BlockSpec((EXPERT_BLOCK, D_MODEL), by_block),
                  pl.BlockSpec((1, D_MODEL, D_EXPERT), by_expert),
                  pl.BlockSpec((1, D_MODEL, D_EXPERT), by_expert),
                  pl.BlockSpec((1, D_EXPERT, D_MODEL), by_expert)],
        out_specs=pl.BlockSpec((EXPERT_BLOCK, D_MODEL), by_block),
        scratch_shapes=[pltpu.VMEM((D_MODEL, D_EXPERT), BF16), pltpu.VMEM((D_MODEL, D_EXPERT), BF16),
                        pltpu.VMEM((D_EXPERT, D_MODEL), BF16)],
    )
    return pl.pallas_call(
        _expert_kernel,
        grid_spec=grid_spec,
        out_shape=jax.ShapeDtypeStruct(xs.shape, F32),
        compiler_params=_params("arbitrary"),
        name="moe_experts",
    )(*items, xs, w_gate, w_up, w_down)


COMBINE_TM = 128


def _combine_kernel(dcur_ref, dnext_ref, w8_ref, hb_ref, x_ref, wsg_ref, wsu_ref, wsd_ref, g_ref, gt_ref,
                    y_hbm, o_ref, gbuf, sem):
    i = pl.program_id(0)
    n = pl.num_programs(0)
    tm = x_ref.shape[0]

    def issue(d_ref, slot):
        def body(t, carry):
            for k in range(TOP_K):
                d = d_ref[t * TOP_K + k]
                pltpu.make_async_copy(y_hbm.at[pl.ds(d, 1), :], gbuf.at[slot, pl.ds(k * tm + t, 1), :],
                                      sem.at[slot]).start()
            return carry

        lax.fori_loop(0, tm, body, 0)

    @pl.when(i == 0)
    def _():
        issue(dcur_ref, 0)

    @pl.when(i + 1 < n)
    def _():
        issue(dnext_ref, (i + 1) % 2)

    slot = i % 2
    shared = _swiglu(hb_ref[...], wsg_ref[...], wsu_ref[...], wsd_ref[...])
    for k in range(TOP_K):
        pltpu.make_async_copy(y_hbm.at[pl.ds(0, tm), :], gbuf.at[slot, pl.ds(k * tm, tm), :],
                              sem.at[slot]).wait()
    w8 = w8_ref[...]
    y = shared
    for k in range(TOP_K):
        y = y + w8[:, k:k + 1] * gbuf[slot, pl.ds(k * tm, tm), :]
    o_ref[...] = x_ref[...] + gt_ref[...] * _rms(y, g_ref[...])


def _combine(dest_flat, w8, hb, x1, wsg, wsu, wsd, g_post, mod, y_buf):
    T = x1.shape[0]
    tm = min(COMBINE_TM, T)
    n = T // tm
    row = pl.BlockSpec((tm, D_MODEL), lambda i: (i, 0))
    return pl.pallas_call(
        _combine_kernel,
        grid=(n,),
        in_specs=[pl.BlockSpec((tm * TOP_K,), lambda i: (i,), memory_space=pltpu.SMEM),
                  pl.BlockSpec((tm * TOP_K,), lambda i: (jnp.minimum(i + 1, n - 1),), memory_space=pltpu.SMEM),
                  pl.BlockSpec((tm, TOP_K), lambda i: (i, 0)), row, row,
                  _const_spec(wsg.shape), _const_spec(wsu.shape), _const_spec(wsd.shape),
                  _const_spec((1, D_MODEL)), _mod_spec(5),
                  pl.BlockSpec(memory_space=pl.ANY)],
        out_specs=row,
        out_shape=jax.ShapeDtypeStruct((T, D_MODEL), F32),
        scratch_shapes=[pltpu.VMEM((2, TOP_K * tm, D_MODEL), F32), pltpu.SemaphoreType.DMA((2,))],
        compiler_params=_params("arbitrary"),
        name="moe_combine",
    )(dest_flat, dest_flat, w8, hb, x1, wsg, wsu, wsd, g_post, mod, y_buf)


def _moe_block(x1, mod, g_pre, g_post, w_router, router_bias, w_exp_gate, w_exp_up, w_exp_down,
               w_sh_gate, w_sh_up, w_sh_down):
    T = x1.shape[0]
    hf, hb, sel, wd, rank, counts = _route(x1, g_pre, mod, w_router, router_bias)

    n_slots = T * TOP_K
    *items, starts = _expert_items(counts[0].astype(jnp.int32), n_slots // EXPERT_BLOCK + N_EXPERTS)

    dest, w8 = _dest(sel, wd, rank, starts.astype(F32)[None, :])
    dest_flat = dest.reshape(n_slots)
    xs = _dispatch(dest_flat, hf, n_slots)
    y_buf = _experts(items, xs, w_exp_gate, w_exp_up, w_exp_down)
    return _combine(dest_flat, w8, hb, x1, w_sh_gate.astype(BF16), w_sh_up.astype(BF16),
                    w_sh_down.astype(BF16), g_post, mod, y_buf)


def kernel(x, c, positions, w_ada, b_ada, g_pre_mix, g_post_mix, g_pre_ffn, g_post_ffn, w_in, cmp_pos_k, cmp_k_w1, cmp_k_w2, cmp_pos_v, cmp_v_w1, cmp_v_w2, pool_w, pool_scale, w_pool_up, w_nsa_up, w_out, w_router, router_bias, w_exp_gate, w_exp_up, w_exp_down, w_sh_gate, w_sh_up, w_sh_down):
    B, T, D = x.shape
    assert B == 1 and D == D_MODEL
    x2 = x.reshape(T, D)
    pos_col = positions.reshape(T, 1)
    for l in range(w_ada.shape[0]):
        mod = _ada(c.reshape(D, 1), w_ada[l], b_ada[l][None, :])
        x2 = _token_mixer_block(x2, pos_col, mod, g_pre_mix[l][None, :], g_post_mix[l][None, :], w_in[l],
                                cmp_pos_k[l], cmp_k_w1[l], cmp_k_w2[l], cmp_pos_v[l], cmp_v_w1[l],
                                cmp_v_w2[l], pool_w[l], pool_scale[l][None, :], w_pool_up[l], w_nsa_up[l],
                                w_out[l])
        x2 = _moe_block(x2, mod, g_pre_ffn[l][None, :], g_post_ffn[l][None, :], w_router[l],
                        router_bias[l][None, :], w_exp_gate[l], w_exp_up[l], w_exp_down[l],
                        w_sh_gate[l], w_sh_up[l], w_sh_down[l])
    return x2.reshape(B, T, D)
```

```python
import functools

import numpy as np
import jax
import jax.numpy as jnp
from jax import lax
from jax.experimental import pallas as pl
from jax.experimental.pallas import tpu as pltpu

F32 = jnp.float32
BF16 = jnp.bfloat16

D_MODEL = 2048
POOL_WIDTH = 1024
POOL_WINDOWS = (2, 4, 8, 16)
POOL_GROUP = 256
POOL_HALO = 16
N_HEADS = 16
N_KV_GROUPS = 4
HEADS_PER_GROUP = 4
D_QK = 96
D_V = 64
HEAD_PAD = 128
ROT_HALF = 12
ROPE_THETA = 500000.0
CMP_LEN = 32
CMP_STRIDE = 16
SLC_LEN = 64
N_SELECT = 16
WINDOW = 512
Q_BLOCK = 128
N_EXPERTS = 64
TOP_K = 8
N_EXPERT_GROUPS = 8
TOPK_GROUPS = 4
D_EXPERT = 512
ROUTED_SCALE = 2.5
RMS_EPS = 1e-6
NEG_INF = -1e30
SEL_FORCE = 1e30
REMOVED = -3e38

VMEM_LIMIT = 56 * 1024 * 1024

C_GM = 0
C_Q = 4096
C_U = 6144
C_KC = 7168
C_KS = 7680
C_KW = 8192
C_VC = 8704
C_VS = 9216
C_VW = 9728
C_GN = 10240
N_COLS = 10752


def _params(*sem):
    return pltpu.CompilerParams(dimension_semantics=sem, vmem_limit_bytes=VMEM_LIMIT)


def _const_spec(shape):
    nd = len(shape)
    return pl.BlockSpec(shape, lambda *_: (0,) * nd)


ADA_TN = 1536
ADA_ROWS = 64


def _ada_kernel(c_ref, w_ref, b_ref, o_ref, s_ref):
    c = c_ref[...]
    s_ref[...] = c * jax.nn.sigmoid(c)

    def body(k, acc):
        r = pl.multiple_of(k * ADA_ROWS, ADA_ROWS)
        for j in range(ADA_ROWS // 8):
            acc = acc + w_ref[pl.ds(r + 8 * j, 8), :] * s_ref[pl.ds(r + 8 * j, 8), :]
        return acc

    acc = lax.fori_loop(0, D_MODEL // ADA_ROWS, body, jnp.zeros((8, ADA_TN), F32))
    o_ref[...] = jnp.sum(acc, axis=0, keepdims=True) + b_ref[...]


def _ada(c_col, w_ada, b_ada):
    n = w_ada.shape[1]
    return pl.pallas_call(
        _ada_kernel,
        grid=(n // ADA_TN,),
        in_specs=[_const_spec((D_MODEL, 1)),
                  pl.BlockSpec((D_MODEL, ADA_TN), lambda j: (0, j)),
                  pl.BlockSpec((1, ADA_TN), lambda j: (0, j))],
        out_specs=pl.BlockSpec((1, ADA_TN), lambda j: (0, j)),
        out_shape=jax.ShapeDtypeStruct((1, n), F32),
        scratch_shapes=[pltpu.VMEM((D_MODEL, 1), F32)],
        compiler_params=_params("arbitrary"),
        name="ada",
    )(c_col, w_ada, b_ada)


def _mod_spec(k):
    return pl.BlockSpec((1, D_MODEL), lambda *_: (0, k))


PROJ_TM = 512
PROJ_TN = 1536


def _modulated_norm(x, g, sc, sh):
    y = x * lax.rsqrt(jnp.mean(x * x, axis=-1, keepdims=True) + RMS_EPS)
    return (y * g) * (1.0 + sc) + sh


def _proj_kernel(x_ref, g_ref, sc_ref, sh_ref, w_ref, o_ref, h_ref):
    @pl.when(pl.program_id(1) == 0)
    def _():
        h_ref[...] = _modulated_norm(x_ref[...], g_ref[...], sc_ref[...], sh_ref[...]).astype(BF16)

    o_ref[...] = jnp.dot(h_ref[...], w_ref[...], preferred_element_type=F32)


def _proj(x2, g_pre, mod, w_in_p):
    T = x2.shape[0]
    tm = min(PROJ_TM, T)
    return pl.pallas_call(
        _proj_kernel,
        grid=(T // tm, N_COLS // PROJ_TN),
        in_specs=[pl.BlockSpec((tm, D_MODEL), lambda i, j: (i, 0)),
                  _const_spec((1, D_MODEL)), _mod_spec(1), _mod_spec(0),
                  pl.BlockSpec((D_MODEL, PROJ_TN), lambda i, j: (0, j))],
        out_specs=pl.BlockSpec((tm, PROJ_TN), lambda i, j: (i, j)),
        out_shape=jax.ShapeDtypeStruct((T, N_COLS), F32),
        scratch_shapes=[pltpu.VMEM((tm, D_MODEL), BF16)],
        compiler_params=_params("arbitrary", "arbitrary"),
        name="in_proj",
    )(x2, g_pre, mod, mod, w_in_p)


def _rope_tables(pos_col, invf):
    ang = pos_col.astype(F32) * invf
    cos = jnp.cos(ang)
    sin = jnp.sin(ang)
    lane = lax.broadcasted_iota(jnp.int32, ang.shape, 1)
    sin_lo = jnp.where(lane < ROT_HALF, -sin, 0.0)
    sin_hi = jnp.where((lane >= ROT_HALF) & (lane < 2 * ROT_HALF), sin, 0.0)
    return cos, sin_lo, sin_hi


def _rope_head(x, tabs):
    cos, sin_lo, sin_hi = tabs
    return (x * cos + pltpu.roll(x, HEAD_PAD - ROT_HALF, 1) * sin_lo
            + pltpu.roll(x, ROT_HALF, 1) * sin_hi)


PREP_TM = 256


LOG2E = 1.4426950408889634
Q_SCALE = (D_QK ** -0.5) * LOG2E
MASK_LANE = D_QK
MASK_ROWS = 16
ONES_ROW = D_V
MASK_BIG = 1e30


def _with_ones_row(vt):
    row = lax.broadcasted_iota(jnp.int32, (HEAD_PAD, 1), 0)
    return jnp.where(row == ONES_ROW, 1.0, vt)


def _prep_kernel(pos_ref, invf_ref, q_ref, ks_ref, kw_ref, vs_ref, vw_ref, gn_ref,
                 qt_ref, kso_ref, kwo_ref, vst_ref, vwt_ref, gt_ref):
    tm = q_ref.shape[0]
    tabs = _rope_tables(pos_ref[...], invf_ref[...])
    for h in range(N_HEADS):
        sl = slice(h * HEAD_PAD, (h + 1) * HEAD_PAD)
        qt_ref[sl, :] = (_rope_head(q_ref[:, sl], tabs) * Q_SCALE).T.astype(BF16)
    row = pl.program_id(0) * tm + lax.broadcasted_iota(jnp.int32, (tm, 1), 0)
    lane = lax.broadcasted_iota(jnp.int32, (1, HEAD_PAD), 1)
    block_tag = jnp.where(lane == MASK_LANE + ((row >> 6) & (MASK_ROWS - 1)), 1.0, 0.0)
    for g in range(N_KV_GROUPS):
        sl = slice(g * HEAD_PAD, (g + 1) * HEAD_PAD)
        kso_ref[:, sl] = (_rope_head(ks_ref[:, sl], tabs) + block_tag).astype(BF16)
        kwo_ref[:, sl] = _rope_head(kw_ref[:, sl], tabs).astype(BF16)
        vst_ref[sl, :] = _with_ones_row(vs_ref[:, sl].T).astype(BF16)
        vwt_ref[sl, :] = _with_ones_row(vw_ref[:, sl].T).astype(BF16)
        gt_ref[sl, :] = jax.nn.sigmoid(gn_ref[:, sl]).T


def _prep(proj, pos_col, invf):
    T = proj.shape[0]
    tm = min(PREP_TM, T)
    kv = N_KV_GROUPS * HEAD_PAD
    qw = N_HEADS * HEAD_PAD

    def col(width, off):
        return pl.BlockSpec((tm, width), lambda i: (i, off // width))

    rows = lambda width: pl.BlockSpec((tm, width), lambda i: (i, 0))
    cols = lambda height: pl.BlockSpec((height, tm), lambda i: (0, i))
    return pl.pallas_call(
        _prep_kernel,
        grid=(T // tm,),
        in_specs=[pl.BlockSpec((tm, 1), lambda i: (i, 0)), _const_spec((1, HEAD_PAD)),
                  col(qw, C_Q), col(kv, C_KS), col(kv, C_KW), col(kv, C_VS), col(kv, C_VW), col(kv, C_GN)],
        out_specs=[cols(qw), rows(kv), rows(kv), cols(kv), cols(kv), cols(kv)],
        out_shape=[jax.ShapeDtypeStruct((qw, T), BF16), jax.ShapeDtypeStruct((T, kv), BF16),
                   jax.ShapeDtypeStruct((T, kv), BF16), jax.ShapeDtypeStruct((kv, T), BF16),
                   jax.ShapeDtypeStruct((kv, T), BF16), jax.ShapeDtypeStruct((kv, T), F32)],
        compiler_params=_params("arbitrary"),
        name="nsa_prep",
    )(pos_col, invf, proj, proj, proj, proj, proj, proj)


def _cmp_kernel(x_ref, pe_ref, w1_ref, w2_ref, pos_ref, invf_ref, o_ref, *, is_key):
    nchunk = pos_ref.shape[0]
    hid_dim = w1_ref.shape[2]
    first = jnp.zeros((nchunk, hid_dim), F32)
    second = jnp.zeros((nchunk, hid_dim), F32)
    for l in range(CMP_STRIDE):
        rows = x_ref[pl.ds(l, nchunk, stride=CMP_STRIDE), :]
        first += jnp.dot((rows + pe_ref[l:l + 1, :]).astype(BF16), w1_ref[l],
                         preferred_element_type=F32)
        second += jnp.dot((rows + pe_ref[CMP_STRIDE + l:CMP_STRIDE + l + 1, :]).astype(BF16),
                          w1_ref[CMP_STRIDE + l], preferred_element_type=F32)
    second = jnp.concatenate([second[1:], jnp.zeros((1, hid_dim), F32)], axis=0)
    pre = first + second
    hid = pre * jax.nn.sigmoid(pre)
    if is_key:
        out = jnp.dot(hid.astype(BF16), w2_ref[...], preferred_element_type=F32)
        o_ref[0] = _rope_head(out, _rope_tables(pos_ref[...], invf_ref[...])).astype(BF16)
    else:
        out_t = lax.dot_general(w2_ref[...], hid.astype(BF16), (((1,), (1,)), ((), ())),
                                preferred_element_type=F32)
        o_ref[0] = _with_ones_row(out_t).astype(BF16)


def _compress(proj, col_off, pe_p, w1_p, w2_p, pos_cmp, invf, is_key):
    T = proj.shape[0]
    nchunk = T // CMP_STRIDE
    hid = w1_p.shape[2]
    out_block = (1, nchunk, HEAD_PAD) if is_key else (1, HEAD_PAD, nchunk)
    return pl.pallas_call(
        functools.partial(_cmp_kernel, is_key=is_key),
        grid=(N_KV_GROUPS,),
        in_specs=[pl.BlockSpec((T, HEAD_PAD), lambda g: (0, col_off // HEAD_PAD + g)),
                  _const_spec((CMP_LEN, HEAD_PAD)), _const_spec((CMP_LEN, HEAD_PAD, hid)),
                  _const_spec(w2_p.shape), _const_spec((nchunk, 1)), _const_spec((1, HEAD_PAD))],
        out_specs=pl.BlockSpec(out_block, lambda g: (g, 0, 0)),
        out_shape=jax.ShapeDtypeStruct((N_KV_GROUPS,) + out_block[1:], BF16),
        compiler_params=_params("arbitrary"),
        name="compress",
    )(proj, pe_p, w1_p, w2_p, pos_cmp, invf)


SLC_TK = 512
WIN_KEYS = WINDOW + Q_BLOCK
R = HEADS_PER_GROUP


def _lanes4(a):
    return jnp.concatenate([a] * R, axis=1)


def _attn_kernel(qt_ref, kc_ref, vct_ref, ks_ref, vst_ref, kw_ref, vwt_ref, gate_ref, c2st_ref, o_ref,
                 qa_ref, neg_ref, acc_ref, m_ref, *, n_sel):
    ci = pl.program_id(1)
    s = ci * Q_BLOCK
    qt = jnp.concatenate([qt_ref[r * HEAD_PAD:(r + 1) * HEAD_PAD, :] for r in range(R)], axis=1)
    tok = s + lax.broadcasted_iota(jnp.int32, (1, Q_BLOCK), 1)
    tok4 = _lanes4(tok)

    n_cmp = kc_ref.shape[1]
    cmp_end = lax.broadcasted_iota(jnp.int32, (n_cmp, 1), 0) * CMP_STRIDE + (CMP_LEN - 1)
    sc = jnp.dot(kc_ref[0], qt, preferred_element_type=F32)
    sc = sc + _lanes4(jnp.where(cmp_end <= tok, 0.0, NEG_INF))
    p = jnp.exp2(sc - jnp.max(sc, axis=0, keepdims=True))
    inv = jnp.where(tok4 >= CMP_LEN - 1, 1.0 / jnp.sum(p, axis=0, keepdims=True), 0.0)
    p = p * inv
    o_cmp = jnp.dot(vct_ref[0], p.astype(BF16), preferred_element_type=F32)

    p_sum = p[:, 0:Q_BLOCK]
    for r in range(1, R):
        p_sum = p_sum + p[:, r * Q_BLOCK:(r + 1) * Q_BLOCK]
    p_hi = p_sum.astype(BF16)
    p_lo = (p_sum - p_hi.astype(F32)).astype(BF16)
    imp = (jnp.dot(c2st_ref[...], p_hi, preferred_element_type=F32)
           + jnp.dot(c2st_ref[...], p_lo, preferred_element_type=F32))

    n_blk = c2st_ref.shape[0]
    jj = lax.broadcasted_iota(jnp.int32, (n_blk, 1), 0)
    cur = tok >> 6
    forced = (jj == 0) | (jj == cur) | (jj == cur - 1)
    score = jnp.where(jj <= cur, jnp.where(forced, SEL_FORCE, imp), NEG_INF)
    chosen = jnp.zeros((n_blk, Q_BLOCK), jnp.bool_)
    for _ in range(n_sel):
        best = jnp.max(score, axis=0, keepdims=True)
        hit = jj == jnp.min(jnp.where(score == best, jj, n_blk), axis=0, keepdims=True)
        chosen = chosen | (hit & (best > 0.5 * NEG_INF))
        score = jnp.where(hit, REMOVED, score)
    neg_ref[...] = _lanes4(jnp.where(chosen, 0.0, -MASK_BIG)).astype(BF16)

    qa_ref[...] = qt
    acc_ref[...] = jnp.zeros_like(acc_ref)
    m_ref[...] = jnp.full_like(m_ref, NEG_INF)

    def tile(jt, carry):
        k0 = pl.multiple_of(jt * SLC_TK, SLC_TK)
        blk0 = pl.multiple_of((k0 // (SLC_LEN * MASK_ROWS)) * MASK_ROWS, MASK_ROWS)
        qa_ref[MASK_LANE:MASK_LANE + MASK_ROWS, :] = neg_ref[pl.ds(blk0, MASK_ROWS), :]
        st = jnp.dot(ks_ref[pl.ds(k0, SLC_TK), :], qa_ref[...], preferred_element_type=F32)
        kpos = k0 + lax.broadcasted_iota(jnp.int32, (SLC_TK, 1), 0)
        st = st + _lanes4(jnp.where(kpos <= tok, 0.0, NEG_INF))
        m_old = m_ref[...]
        m_new = jnp.maximum(m_old, jnp.max(st, axis=0, keepdims=True))
        pt = jnp.exp2(st - m_new).astype(BF16)
        acc_ref[...] = (jnp.exp2(m_old - m_new) * acc_ref[...]
                        + jnp.dot(vst_ref[:, pl.ds(k0, SLC_TK)], pt, preferred_element_type=F32))
        m_ref[...] = m_new
        return carry

    lax.fori_loop(0, (s + Q_BLOCK + SLC_TK - 1) // SLC_TK, tile, 0)
    acc = acc_ref[...]
    o_slc = acc * (1.0 / acc[ONES_ROW:ONES_ROW + 1, :])

    w0 = pl.multiple_of(jnp.maximum(s - WINDOW, 0), Q_BLOCK)
    wpos = w0 + lax.broadcasted_iota(jnp.int32, (WIN_KEYS, 1), 0)
    sw = jnp.dot(kw_ref[pl.ds(w0, WIN_KEYS), :], qt, preferred_element_type=F32)
    sw = sw + _lanes4(jnp.where((wpos <= tok) & (wpos > tok - WINDOW), 0.0, NEG_INF))
    pw = jnp.exp2(sw - jnp.max(sw, axis=0, keepdims=True)).astype(BF16)
    o_win = jnp.dot(vwt_ref[:, pl.ds(w0, WIN_KEYS)], pw, preferred_element_type=F32)
    o_win = o_win * (1.0 / o_win[ONES_ROW:ONES_ROW + 1, :])

    gate = gate_ref[...]
    for r in range(R):
        ls = slice(r * Q_BLOCK, (r + 1) * Q_BLOCK)
        o = (gate[3 * r:3 * r + 1, :] * o_cmp[:, ls] + gate[3 * r + 1:3 * r + 2, :] * o_slc[:, ls]
             + gate[3 * r + 2:3 * r + 3, :] * o_win[:, ls])
        o_ref[:, r * HEAD_PAD:(r + 1) * HEAD_PAD] = o.T.astype(BF16)


def _attention(qt, k_cmp, v_cmp_t, ks_r, vs_t, kw_r, vw_t, gate_t, c2st):
    T = ks_r.shape[0]
    n_cmp = k_cmp.shape[1]
    n_blk = c2st.shape[0]
    n_sel = min(N_SELECT, T // SLC_LEN)
    gw = R * HEAD_PAD
    rows = lambda g, ci: (0, g)
    cols = lambda g, ci: (g, 0)
    return pl.pallas_call(
        functools.partial(_attn_kernel, n_sel=n_sel),
        grid=(N_KV_GROUPS, T // Q_BLOCK),
        in_specs=[pl.BlockSpec((gw, Q_BLOCK), lambda g, ci: (g, ci)),
                  pl.BlockSpec((1, n_cmp, HEAD_PAD), lambda g, ci: (g, 0, 0)),
                  pl.BlockSpec((1, HEAD_PAD, n_cmp), lambda g, ci: (g, 0, 0)),
                  pl.BlockSpec((T, HEAD_PAD), rows), pl.BlockSpec((HEAD_PAD, T), cols),
                  pl.BlockSpec((T, HEAD_PAD), rows), pl.BlockSpec((HEAD_PAD, T), cols),
                  pl.BlockSpec((HEAD_PAD, Q_BLOCK), lambda g, ci: (g, ci)),
                  _const_spec((n_blk, n_cmp))],
        out_specs=pl.BlockSpec((Q_BLOCK, gw), lambda g, ci: (ci, g)),
        out_shape=jax.ShapeDtypeStruct((T, N_HEADS * HEAD_PAD), BF16),
        scratch_shapes=[pltpu.VMEM((HEAD_PAD, R * Q_BLOCK), BF16), pltpu.VMEM((n_blk, R * Q_BLOCK), BF16),
                        pltpu.VMEM((HEAD_PAD, R * Q_BLOCK), F32), pltpu.VMEM((1, R * Q_BLOCK), F32)],
        compiler_params=_params("arbitrary", "arbitrary"),
        name="nsa_attention",
    )(qt, k_cmp, v_cmp_t, ks_r, vs_t, kw_r, vw_t, gate_t, c2st)


MIX_TM = 256


def _mix_kernel(u_ref, up_ref, gm_ref, o_ref, pw_ref, ps_ref, wpu_ref, wnu_ref, y_ref):
    i = pl.program_id(0)
    tm = u_ref.shape[0]
    u = u_ref[...]
    prev = up_ref[...] * (i > 0).astype(F32)
    ext = jnp.concatenate([prev, u], axis=0)
    t = i * tm + lax.broadcasted_iota(jnp.int32, (tm, 1), 0)
    parts = []
    for g, win in enumerate(POOL_WINDOWS):
        cs = slice(g * POOL_GROUP, (g + 1) * POOL_GROUP)
        run = ext[:, cs]
        k = 1
        while k < win:
            run = run[k:] + run[:-k]
            k *= 2
        lo = POOL_HALO - (win - 1)
        total = run[lo:lo + tm]
        count = jnp.minimum(t + 1, win).astype(F32)
        d = total / count - u[:, cs]
        parts.append(jnp.dot(d.astype(BF16), pw_ref[g], preferred_element_type=F32) * ps_ref[:, cs])
    a = jnp.concatenate(parts, axis=1).astype(BF16)
    ya = jnp.dot(a, wpu_ref[...], preferred_element_type=F32)
    yb = jnp.dot(o_ref[...], wnu_ref[...], preferred_element_type=F32)
    gm = jax.nn.sigmoid(gm_ref[...])
    y_ref[...] = (gm[:, :D_MODEL] * ya + gm[:, D_MODEL:] * yb).astype(BF16)


def _mix(proj, o, pool_w_b, pool_scale, w_pool_up_b, w_nsa_up_p):
    T = proj.shape[0]
    tm = min(MIX_TM, T)
    halo_blocks = tm // POOL_HALO
    return pl.pallas_call(
        _mix_kernel,
        grid=(T // tm,),
        in_specs=[pl.BlockSpec((tm, POOL_WIDTH), lambda i: (i, C_U // POOL_WIDTH)),
                  pl.BlockSpec((POOL_HALO, POOL_WIDTH),
                               lambda i: (jnp.maximum(i * halo_blocks - 1, 0), C_U // POOL_WIDTH)),
                  pl.BlockSpec((tm, 2 * D_MODEL), lambda i: (i, 0)),
                  pl.BlockSpec((tm, N_HEADS * HEAD_PAD), lambda i: (i, 0)),
                  _const_spec(pool_w_b.shape), _const_spec((1, POOL_WIDTH)),
                  _const_spec(w_pool_up_b.shape), _const_spec(w_nsa_up_p.shape)],
        out_specs=pl.BlockSpec((tm, D_MODEL), lambda i: (i, 0)),
        out_shape=jax.ShapeDtypeStruct((T, D_MODEL), BF16),
        compiler_params=_params("arbitrary"),
        name="pool_merge",
    )(proj, proj, proj, o, pool_w_b, pool_scale, w_pool_up_b, w_nsa_up_p)


OUT_TM = 256


def _rms(y, g):
    return y * lax.rsqrt(jnp.mean(y * y, axis=-1, keepdims=True) + RMS_EPS) * g


def _outproj_kernel(y_ref, w_ref, x_ref, g_ref, gt_ref, o_ref):
    z = jnp.dot(y_ref[...], w_ref[...], preferred_element_type=F32)
    o_ref[...] = x_ref[...] + gt_ref[...] * _rms(z, g_ref[...])


def _outproj(y, w_out_b, x2, g_post, mod):
    T = y.shape[0]
    tm = min(OUT_TM, T)
    row = pl.BlockSpec((tm, D_MODEL), lambda i: (i, 0))
    return pl.pallas_call(
        _outproj_kernel,
        grid=(T // tm,),
        in_specs=[row, _const_spec((D_MODEL, D_MODEL)), row, _const_spec((1, D_MODEL)), _mod_spec(2)],
        out_specs=row,
        out_shape=jax.ShapeDtypeStruct((T, D_MODEL), F32),
        compiler_params=_params("arbitrary"),
        name="out_proj",
    )(y, w_out_b, x2, g_post, mod)


def _pad_heads(w, n_heads, d_head):
    k = w.shape[0]
    w = w.reshape(k, n_heads, d_head)
    return jnp.pad(w, ((0, 0), (0, 0), (0, HEAD_PAD - d_head))).reshape(k, n_heads * HEAD_PAD)


def _pack_w_in(w_in):
    sizes = (POOL_WIDTH, N_HEADS * D_QK, N_KV_GROUPS * D_QK, N_KV_GROUPS * D_V, N_KV_GROUPS * D_QK,
             N_KV_GROUPS * D_V, N_KV_GROUPS * D_QK, N_KV_GROUPS * D_V, 3 * N_HEADS, 2 * D_MODEL)
    offs = np.cumsum((0,) + sizes)
    u, q, kc, vc, ks, vs, kw, vw, gn, gm = [w_in[:, offs[i]:offs[i + 1]] for i in range(len(sizes))]
    cols = [gm, _pad_heads(q, N_HEADS, D_QK), u,
            _pad_heads(kc, N_KV_GROUPS, D_QK), _pad_heads(ks, N_KV_GROUPS, D_QK),
            _pad_heads(kw, N_KV_GROUPS, D_QK), _pad_heads(vc, N_KV_GROUPS, D_V),
            _pad_heads(vs, N_KV_GROUPS, D_V), _pad_heads(vw, N_KV_GROUPS, D_V),
            _pad_heads(gn, N_KV_GROUPS, 3 * HEADS_PER_GROUP)]
    return jnp.concatenate(cols, axis=1).astype(BF16)


def _slc_from_cmp(n_blk, n_chunk):
    cs = np.arange(n_chunk)[None, :] * CMP_STRIDE
    ss = np.arange(n_blk)[:, None] * SLC_LEN
    ov = np.clip(np.minimum(cs + CMP_LEN, ss + SLC_LEN) - np.maximum(cs, ss), 0, None)
    return jnp.asarray(ov / CMP_LEN, dtype=BF16)


def _token_mixer_block(x2, pos_col, mod, g_pre, g_post, w_in, cmp_pos_k, cmp_k_w1, cmp_k_w2,
                       cmp_pos_v, cmp_v_w1, cmp_v_w2, pool_w, pool_scale, w_pool_up, w_nsa_up, w_out):
    T = x2.shape[0]
    half = jnp.arange(ROT_HALF, dtype=F32)
    inv = jnp.power(jnp.float32(ROPE_THETA), -half * 2.0 / (2 * ROT_HALF))
    invf = jnp.concatenate([inv, inv, jnp.zeros((HEAD_PAD - 2 * ROT_HALF,), F32)])[None, :]

    proj = _proj(x2, g_pre, mod, _pack_w_in(w_in))
    qt, ks_r, kw_r, vs_t, vw_t, gate_t = _prep(proj, pos_col, invf)

    n_chunk = T // CMP_STRIDE
    pos_cmp = jnp.pad(pos_col[CMP_LEN - 1::CMP_STRIDE], ((0, 1), (0, 0)))

    def pad_lanes(a, d):
        return jnp.pad(a, [(0, 0)] * (a.ndim - 1) + [(0, HEAD_PAD - d)])

    def w1_pack(w1, d):
        w1 = w1.reshape(CMP_LEN, d, w1.shape[1])
        return jnp.pad(w1, ((0, 0), (0, HEAD_PAD - d), (0, 0))).astype(BF16)

    k_cmp = _compress(proj, C_KC, pad_lanes(cmp_pos_k, D_QK), w1_pack(cmp_k_w1, D_QK),
                      pad_lanes(cmp_k_w2, D_QK).astype(BF16), pos_cmp, invf, True)
    v_cmp = _compress(proj, C_VC, pad_lanes(cmp_pos_v, D_V), w1_pack(cmp_v_w1, D_V),
                      pad_lanes(cmp_v_w2, D_V).T.astype(BF16), pos_cmp, invf, False)

    n_blk = max(T // SLC_LEN, MASK_ROWS)
    o = _attention(qt, k_cmp, v_cmp, ks_r, vs_t, kw_r, vw_t, gate_t, _slc_from_cmp(n_blk, n_chunk))

    w_nsa_up_p = jnp.pad(w_nsa_up.reshape(N_HEADS, D_V, D_MODEL),
                         ((0, 0), (0, HEAD_PAD - D_V), (0, 0))).reshape(N_HEADS * HEAD_PAD, D_MODEL)
    y = _mix(proj, o, pool_w.astype(BF16), pool_scale, w_pool_up.astype(BF16), w_nsa_up_p.astype(BF16))
    return _outproj(y, w_out.astype(BF16), x2, g_post, mod)


ROUTE_TM = 256
EXPERTS_PER_GROUP = N_EXPERTS // N_EXPERT_GROUPS


def _first_index_of(mask, lane, n):
    return jnp.min(jnp.where(mask, lane, n), axis=-1, keepdims=True)


def _route_kernel(x_ref, g_ref, sc_ref, sh_ref, wr_ref, rb_ref,
                  hf_ref, hb_ref, sel_ref, wd_ref, rank_ref, cnt_ref, carry_ref):
    i = pl.program_id(0)
    tm = x_ref.shape[0]

    @pl.when(i == 0)
    def _():
        carry_ref[...] = jnp.zeros_like(carry_ref)

    h = _modulated_norm(x_ref[...], g_ref[...], sc_ref[...], sh_ref[...])
    hf_ref[...] = h
    hb_ref[...] = h.astype(BF16)
    logits = jnp.dot(h, wr_ref[...], precision=lax.Precision.HIGHEST, preferred_element_type=F32)
    aff = jax.nn.sigmoid(logits)
    biased = aff + rb_ref[...]
    lane = lax.broadcasted_iota(jnp.int32, (1, N_EXPERTS), 1)
    grp = lane // EXPERTS_PER_GROUP

    gs = []
    for gi in range(N_EXPERT_GROUPS):
        v = jnp.where(grp == gi, biased, REMOVED)
        m1 = jnp.max(v, axis=-1, keepdims=True)
        i1 = _first_index_of(v == m1, lane, N_EXPERTS)
        m2 = jnp.max(jnp.where(lane == i1, REMOVED, v), axis=-1, keepdims=True)
        gs.append(m1 + m2)
    keep = jnp.zeros((tm, N_EXPERTS), jnp.bool_)
    for gi in range(N_EXPERT_GROUPS):
        ahead = jnp.zeros((tm, 1), jnp.int32)
        for gj in range(N_EXPERT_GROUPS):
            if gj < gi:
                ahead += (gs[gj] >= gs[gi]).astype(jnp.int32)
            elif gj > gi:
                ahead += (gs[gj] > gs[gi]).astype(jnp.int32)
        keep = keep | ((grp == gi) & (ahead < TOPK_GROUPS))
    masked = jnp.where(keep, biased, NEG_INF)
    sel = jnp.zeros((tm, N_EXPERTS), F32)
    for _ in range(TOP_K):
        best = jnp.max(masked, axis=-1, keepdims=True)
        hit = lane == _first_index_of(masked == best, lane, N_EXPERTS)
        sel = jnp.where(hit, 1.0, sel)
        masked = jnp.where(hit, REMOVED, masked)
    w = aff * sel
    wd_ref[...] = w / jnp.sum(w, axis=-1, keepdims=True) * ROUTED_SCALE
    sel_ref[...] = sel

    r_i = lax.broadcasted_iota(jnp.int32, (tm, tm), 0)
    c_i = lax.broadcasted_iota(jnp.int32, (tm, tm), 1)
    before = jnp.where(c_i < r_i, 1.0, 0.0).astype(BF16)
    carry = carry_ref[...]
    rank_ref[...] = jnp.dot(before, sel.astype(BF16), preferred_element_type=F32) + carry
    carry = carry + jnp.sum(sel, axis=0, keepdims=True)
    carry_ref[...] = carry
    cnt_ref[...] = carry


def _route(x1, g_pre, mod, w_router, router_bias):
    T = x1.shape[0]
    tm = min(ROUTE_TM, T)
    row = pl.BlockSpec((tm, D_MODEL), lambda i: (i, 0))
    erow = pl.BlockSpec((tm, N_EXPERTS), lambda i: (i, 0))
    e_shape = jax.ShapeDtypeStruct((T, N_EXPERTS), F32)
    return pl.pallas_call(
        _route_kernel,
        grid=(T // tm,),
        in_specs=[row, _const_spec((1, D_MODEL)), _mod_spec(4), _mod_spec(3),
                  _const_spec((D_MODEL, N_EXPERTS)), _const_spec((1, N_EXPERTS))],
        out_specs=[row, row, erow, erow, erow, _const_spec((1, N_EXPERTS))],
        out_shape=[jax.ShapeDtypeStruct((T, D_MODEL), F32), jax.ShapeDtypeStruct((T, D_MODEL), BF16),
                   e_shape, e_shape, e_shape, jax.ShapeDtypeStruct((1, N_EXPERTS), F32)],
        scratch_shapes=[pltpu.VMEM((1, N_EXPERTS), F32)],
        compiler_params=_params("arbitrary"),
        name="moe_route",
    )(x1, g_pre, mod, mod, w_router, router_bias)


def _dest_kernel(sel_ref, wd_ref, rank_ref, ps_ref, dest_ref, w8_ref):
    tm = sel_ref.shape[0]
    slot = ps_ref[...] + rank_ref[...]
    wd = wd_ref[...]
    lane = lax.broadcasted_iota(jnp.int32, (1, N_EXPERTS), 1)
    lane_k = lax.broadcasted_iota(jnp.int32, (1, TOP_K), 1)
    dest = jnp.zeros((tm, TOP_K), F32)
    w8 = jnp.zeros((tm, TOP_K), F32)
    rem = sel_ref[...]
    for k in range(TOP_K):
        hit = lane == _first_index_of(rem > 0.5, lane, N_EXPERTS)
        dest = jnp.where(lane_k == k, jnp.sum(jnp.where(hit, slot, 0.0), axis=-1, keepdims=True), dest)
        w8 = jnp.where(lane_k == k, jnp.sum(jnp.where(hit, wd, 0.0), axis=-1, keepdims=True), w8)
        rem = jnp.where(hit, 0.0, rem)
    dest_ref[...] = dest.astype(jnp.int32)
    w8_ref[...] = w8


def _dest(sel, wd, rank, pstart):
    T = sel.shape[0]
    tm = min(ROUTE_TM, T)
    erow = pl.BlockSpec((tm, N_EXPERTS), lambda i: (i, 0))
    krow = pl.BlockSpec((tm, TOP_K), lambda i: (i, 0))
    return pl.pallas_call(
        _dest_kernel,
        grid=(T // tm,),
        in_specs=[erow, erow, erow, _const_spec((1, N_EXPERTS))],
        out_specs=[krow, krow],
        out_shape=[jax.ShapeDtypeStruct((T, TOP_K), jnp.int32), jax.ShapeDtypeStruct((T, TOP_K), F32)],
        compiler_params=_params("arbitrary"),
        name="moe_dest",
    )(sel, wd, rank, pstart)


DISPATCH_TM = 128


def _dispatch_kernel(dest_ref, h_ref, xs_hbm, sem):
    tm = h_ref.shape[0]

    def body(t, carry):
        for k in range(TOP_K):
            d = dest_ref[t * TOP_K + k]
            pltpu.make_async_copy(h_ref.at[pl.ds(t, 1), :], xs_hbm.at[pl.ds(d, 1), :], sem).start()
        return carry

    lax.fori_loop(0, tm, body, 0)
    for k in range(TOP_K):
        pltpu.make_async_copy(h_ref, xs_hbm.at[pl.ds(0, tm), :], sem).wait()


def _dispatch(dest_flat, hf, n_slots):
    T = hf.shape[0]
    tm = min(DISPATCH_TM, T)
    return pl.pallas_call(
        _dispatch_kernel,
        grid=(T // tm,),
        in_specs=[pl.BlockSpec((tm * TOP_K,), lambda i: (i,), memory_space=pltpu.SMEM),
                  pl.BlockSpec((tm, D_MODEL), lambda i: (i, 0))],
        out_specs=pl.BlockSpec(memory_space=pl.ANY),
        out_shape=jax.ShapeDtypeStruct((n_slots, D_MODEL), F32),
        scratch_shapes=[pltpu.SemaphoreType.DMA(())],
        compiler_params=pltpu.CompilerParams(dimension_semantics=("arbitrary",), vmem_limit_bytes=VMEM_LIMIT,
                                             has_side_effects=True),
        name="moe_dispatch",
    )(dest_flat, hf)


EXPERT_BLOCK = 256


def _swiglu(x, wg, wu, wd):
    g = jnp.dot(x, wg, preferred_element_type=F32)
    u = jnp.dot(x, wu, preferred_element_type=F32)
    hid = (g * jax.nn.sigmoid(g)) * u
    return jnp.dot(hid.astype(BF16), wd, preferred_element_type=F32)


def _expert_items(counts, n_items):
    ends = jnp.cumsum(counts)
    starts = ends - counts
    first_blk = starts // EXPERT_BLOCK
    last_blk = (ends - 1) // EXPERT_BLOCK
    per_expert = jnp.where(counts > 0, last_blk - first_blk + 1, 0)
    item_end = jnp.cumsum(per_expert)
    n_used = item_end[-1]
    i = jnp.minimum(jnp.arange(n_items, dtype=jnp.int32), n_used - 1)
    e = jnp.sum((item_end[None, :] <= i[:, None]).astype(jnp.int32), axis=1)
    e = jnp.minimum(e, N_EXPERTS - 1)
    blk = first_blk[e] + i - (item_end[e] - per_expert[e])
    lo = jnp.clip(starts[e] - blk * EXPERT_BLOCK, 0, EXPERT_BLOCK)
    hi = jnp.clip(ends[e] - blk * EXPERT_BLOCK, 0, EXPERT_BLOCK)
    live = jnp.arange(n_items, dtype=jnp.int32) < n_used
    hi = jnp.where(live, hi, lo)
    new_expert = jnp.concatenate([jnp.ones((1,), jnp.bool_), e[1:] != e[:-1]])
    new_block = jnp.concatenate([jnp.ones((1,), jnp.bool_), blk[1:] != blk[:-1]])
    as_i32 = lambda a: a.astype(jnp.int32)
    return (as_i32(blk), e, as_i32(lo), as_i32(hi), as_i32(new_expert & live), as_i32(new_block & live),
            as_i32(live), starts)


def _expert_kernel(blk_ref, e_ref, lo_ref, hi_ref, newe_ref, newb_ref, live_ref,
                   x_ref, wg_ref, wu_ref, wd_ref, y_ref, wgb, wub, wdb):
    i = pl.program_id(0)

    @pl.when(live_ref[i] == 1)
    def _():
        @pl.when(newe_ref[i] == 1)
        def _():
            wgb[...] = wg_ref[0].astype(BF16)
            wub[...] = wu_ref[0].astype(BF16)
            wdb[...] = wd_ref[0].astype(BF16)

        row = lax.broadcasted_iota(jnp.int32, (EXPERT_BLOCK, 1), 0)
        mine = (row >= lo_ref[i]) & (row < hi_ref[i])
        y = _swiglu(jnp.where(mine, x_ref[...], 0.0).astype(BF16), wgb[...], wub[...], wdb[...])

        @pl.when(newb_ref[i] == 1)
        def _():
            y_ref[...] = y

        @pl.when(newb_ref[i] == 0)
        def _():
            y_ref[...] += y


def _experts(items, xs, w_gate, w_up, w_down):
    n_items = items[0].shape[0]
    by_block = lambda i, blk, e, *_: (blk[i], 0)
    by_expert = lambda i, blk, e, *_: (e[i], 0, 0)
    grid_spec = pltpu.PrefetchScalarGridSpec(
        num_scalar_prefetch=len(items),
        grid=(n_items,),
        in_specs=[pl.BlockSpec((EXPERT_BLOCK, D_MODEL), by_block),
                  pl.BlockSpec((1, D_MODEL, D_EXPERT), by_expert),
                  pl.BlockSpec((1, D_MODEL, D_EXPERT), by_expert),
                  pl.BlockSpec((1, D_EXPERT, D_MODEL), by_expert)],
        out_specs=pl.BlockSpec((EXPERT_BLOCK, D_MODEL), by_block),
        scratch_shapes=[pltpu.VMEM((D_MODEL, D_EXPERT), BF16), pltpu.VMEM((D_MODEL, D_EXPERT), BF16),
                        pltpu.VMEM((D_EXPERT, D_MODEL), BF16)],
    )
    return pl.pallas_call(
        _expert_kernel,
        grid_spec=grid_spec,
        out_shape=jax.ShapeDtypeStruct(xs.shape, F32),
        compiler_params=_params("arbitrary"),
        name="moe_experts",
    )(*items, xs, w_gate, w_up, w_down)


COMBINE_TM = 128


def _combine_kernel(dcur_ref, dnext_ref, w8_ref, hb_ref, x_ref, wsg_ref, wsu_ref, wsd_ref, g_ref, gt_ref,
                    y_hbm, o_ref, gbuf, sem):
    i = pl.program_id(0)
    n = pl.num_programs(0)
    tm = x_ref.shape[0]

    def issue(d_ref, slot):
        def body(t, carry):
            for k in range(TOP_K):
                d = d_ref[t * TOP_K + k]
                pltpu.make_async_copy(y_hbm.at[pl.ds(d, 1), :], gbuf.at[slot, pl.ds(k * tm + t, 1), :],
                                      sem.at[slot]).start()
            return carry

        lax.fori_loop(0, tm, body, 0)

    @pl.when(i == 0)
    def _():
        issue(dcur_ref, 0)

    @pl.when(i + 1 < n)
    def _():
        issue(dnext_ref, (i + 1) % 2)

    slot = i % 2
    shared = _swiglu(hb_ref[...], wsg_ref[...], wsu_ref[...], wsd_ref[...])
    for k in range(TOP_K):
        pltpu.make_async_copy(y_hbm.at[pl.ds(0, tm), :], gbuf.at[slot, pl.ds(k * tm, tm), :],
                              sem.at[slot]).wait()
    w8 = w8_ref[...]
    y = shared
    for k in range(TOP_K):
        y = y + w8[:, k:k + 1] * gbuf[slot, pl.ds(k * tm, tm), :]
    o_ref[...] = x_ref[...] + gt_ref[...] * _rms(y, g_ref[...])


def _combine(dest_flat, w8, hb, x1, wsg, wsu, wsd, g_post, mod, y_buf):
    T = x1.shape[0]
    tm = min(COMBINE_TM, T)
    n = T // tm
    row = pl.BlockSpec((tm, D_MODEL), lambda i: (i, 0))
    return pl.pallas_call(
        _combine_kernel,
        grid=(n,),
        in_specs=[pl.BlockSpec((tm * TOP_K,), lambda i: (i,), memory_space=pltpu.SMEM),
                  pl.BlockSpec((tm * TOP_K,), lambda i: (jnp.minimum(i + 1, n - 1),), memory_space=pltpu.SMEM),
                  pl.BlockSpec((tm, TOP_K), lambda i: (i, 0)), row, row,
                  _const_spec(wsg.shape), _const_spec(wsu.shape), _const_spec(wsd.shape),
                  _const_spec((1, D_MODEL)), _mod_spec(5),
                  pl.BlockSpec(memory_space=pl.ANY)],
        out_specs=row,
        out_shape=jax.ShapeDtypeStruct((T, D_MODEL), F32),
        scratch_shapes=[pltpu.VMEM((2, TOP_K * tm, D_MODEL), F32), pltpu.SemaphoreType.DMA((2,))],
        compiler_params=_params("arbitrary"),
        name="moe_combine",
    )(dest_flat, dest_flat, w8, hb, x1, wsg, wsu, wsd, g_post, mod, y_buf)


def _moe_block(x1, mod, g_pre, g_post, w_router, router_bias, w_exp_gate, w_exp_up, w_exp_down,
               w_sh_gate, w_sh_up, w_sh_down):
    T = x1.shape[0]
    hf, hb, sel, wd, rank, counts = _route(x1, g_pre, mod, w_router, router_bias)

    n_slots = T * TOP_K
    *items, starts = _expert_items(counts[0].astype(jnp.int32), n_slots // EXPERT_BLOCK + N_EXPERTS)

    dest, w8 = _dest(sel, wd, rank, starts.astype(F32)[None, :])
    dest_flat = dest.reshape(n_slots)
    xs = _dispatch(dest_flat, hf, n_slots)
    y_buf = _experts(items, xs, w_exp_gate, w_exp_up, w_exp_down)
    return _combine(dest_flat, w8, hb, x1, w_sh_gate.astype(BF16), w_sh_up.astype(BF16),
                    w_sh_down.astype(BF16), g_post, mod, y_buf)


def kernel(x, c, positions, w_ada, b_ada, g_pre_mix, g_post_mix, g_pre_ffn, g_post_ffn, w_in, cmp_pos_k, cmp_k_w1, cmp_k_w2, cmp_pos_v, cmp_v_w1, cmp_v_w2, pool_w, pool_scale, w_pool_up, w_nsa_up, w_out, w_router, router_bias, w_exp_gate, w_exp_up, w_exp_down, w_sh_gate, w_sh_up, w_sh_down):
    B, T, D = x.shape
    assert B == 1 and D == D_MODEL
    x2 = x.reshape(T, D)
    pos_col = positions.reshape(T, 1)
    for l in range(w_ada.shape[0]):
        mod = _ada(c.reshape(D, 1), w_ada[l], b_ada[l][None, :])
        x2 = _token_mixer_block(x2, pos_col, mod, g_pre_mix[l][None, :], g_post_mix[l][None, :], w_in[l],
                                cmp_pos_k[l], cmp_k_w1[l], cmp_k_w2[l], cmp_pos_v[l], cmp_v_w1[l],
                                cmp_v_w2[l], pool_w[l], pool_scale[l][None, :], w_pool_up[l], w_nsa_up[l],
                                w_out[l])
        x2 = _moe_block(x2, mod, g_pre_ffn[l][None, :], g_post_ffn[l][None, :], w_router[l],
                        router_bias[l][None, :], w_exp_gate[l], w_exp_up[l], w_exp_down[l],
                        w_sh_gate[l], w_sh_up[l], w_sh_down[l])
    return x2.reshape(B, T, D)
```

```python
import functools

import numpy as np
import jax
import jax.numpy as jnp
from jax import lax
from jax.experimental import pallas as pl
from jax.experimental.pallas import tpu as pltpu

F32 = jnp.float32
BF16 = jnp.bfloat16

D_MODEL = 2048
POOL_WIDTH = 1024
POOL_WINDOWS = (2, 4, 8, 16)
POOL_GROUP = 256
POOL_HALO = 16
N_HEADS = 16
N_KV_GROUPS = 4
HEADS_PER_GROUP = 4
D_QK = 96
D_V = 64
HEAD_PAD = 128
ROT_HALF = 12
ROPE_THETA = 500000.0
CMP_LEN = 32
CMP_STRIDE = 16
SLC_LEN = 64
N_SELECT = 16
WINDOW = 512
Q_BLOCK = 256
N_EXPERTS = 64
TOP_K = 8
N_EXPERT_GROUPS = 8
TOPK_GROUPS = 4
D_EXPERT = 512
ROUTED_SCALE = 2.5
RMS_EPS = 1e-6
NEG_INF = -1e30
SEL_FORCE = 1e30
REMOVED = -3e38

VMEM_LIMIT = 56 * 1024 * 1024

C_GM = 0
C_Q = 4096
C_U = 6144
C_KC = 7168
C_KS = 7680
C_KW = 8192
C_VC = 8704
C_VS = 9216
C_VW = 9728
C_GN = 10240
N_COLS = 10752


def _params(*sem):
    return pltpu.CompilerParams(dimension_semantics=sem, vmem_limit_bytes=VMEM_LIMIT)


def _const_spec(shape):
    nd = len(shape)
    return pl.BlockSpec(shape, lambda *_: (0,) * nd)


ADA_TN = 1536
ADA_ROWS = 64


def _ada_kernel(c_ref, w_ref, b_ref, o_ref, s_ref):
    c = c_ref[...]
    s_ref[...] = c * jax.nn.sigmoid(c)

    def body(k, acc):
        r = pl.multiple_of(k * ADA_ROWS, ADA_ROWS)
        for j in range(ADA_ROWS // 8):
            acc = acc + w_ref[pl.ds(r + 8 * j, 8), :] * s_ref[pl.ds(r + 8 * j, 8), :]
        return acc

    acc = lax.fori_loop(0, D_MODEL // ADA_ROWS, body, jnp.zeros((8, ADA_TN), F32))
    o_ref[...] = jnp.sum(acc, axis=0, keepdims=True) + b_ref[...]


def _ada(c_col, w_ada, b_ada):
    n = w_ada.shape[1]
    return pl.pallas_call(
        _ada_kernel,
        grid=(n // ADA_TN,),
        in_specs=[_const_spec((D_MODEL, 1)),
                  pl.BlockSpec((D_MODEL, ADA_TN), lambda j: (0, j)),
                  pl.BlockSpec((1, ADA_TN), lambda j: (0, j))],
        out_specs=pl.BlockSpec((1, ADA_TN), lambda j: (0, j)),
        out_shape=jax.ShapeDtypeStruct((1, n), F32),
        scratch_shapes=[pltpu.VMEM((D_MODEL, 1), F32)],
        compiler_params=_params("arbitrary"),
        name="ada",
    )(c_col, w_ada, b_ada)


def _mod_spec(k):
    return pl.BlockSpec((1, D_MODEL), lambda *_: (0, k))


PROJ_TM = 512
PROJ_TN = 1536


def _modulated_norm(x, g, sc, sh):
    y = x * lax.rsqrt(jnp.mean(x * x, axis=-1, keepdims=True) + RMS_EPS)
    return (y * g) * (1.0 + sc) + sh


def _proj_kernel(x_ref, g_ref, sc_ref, sh_ref, w_ref, o_ref, h_ref):
    @pl.when(pl.program_id(1) == 0)
    def _():
        h_ref[...] = _modulated_norm(x_ref[...], g_ref[...], sc_ref[...], sh_ref[...]).astype(BF16)

    o_ref[...] = jnp.dot(h_ref[...], w_ref[...], preferred_element_type=F32)


def _proj(x2, g_pre, mod, w_in_p):
    T = x2.shape[0]
    tm = min(PROJ_TM, T)
    return pl.pallas_call(
        _proj_kernel,
        grid=(T // tm, N_COLS // PROJ_TN),
        in_specs=[pl.BlockSpec((tm, D_MODEL), lambda i, j: (i, 0)),
                  _const_spec((1, D_MODEL)), _mod_spec(1), _mod_spec(0),
                  pl.BlockSpec((D_MODEL, PROJ_TN), lambda i, j: (0, j))],
        out_specs=pl.BlockSpec((tm, PROJ_TN), lambda i, j: (i, j)),
        out_shape=jax.ShapeDtypeStruct((T, N_COLS), F32),
        scratch_shapes=[pltpu.VMEM((tm, D_MODEL), BF16)],
        compiler_params=_params("arbitrary", "arbitrary"),
        name="in_proj",
    )(x2, g_pre, mod, mod, w_in_p)


def _rope_tables(pos_col, invf):
    ang = pos_col.astype(F32) * invf
    cos = jnp.cos(ang)
    sin = jnp.sin(ang)
    lane = lax.broadcasted_iota(jnp.int32, ang.shape, 1)
    sin_lo = jnp.where(lane < ROT_HALF, -sin, 0.0)
    sin_hi = jnp.where((lane >= ROT_HALF) & (lane < 2 * ROT_HALF), sin, 0.0)
    return cos, sin_lo, sin_hi


def _rope_head(x, tabs):
    cos, sin_lo, sin_hi = tabs
    return (x * cos + pltpu.roll(x, HEAD_PAD - ROT_HALF, 1) * sin_lo
            + pltpu.roll(x, ROT_HALF, 1) * sin_hi)


PREP_TM = 256


LOG2E = 1.4426950408889634
Q_SCALE = (D_QK ** -0.5) * LOG2E
MASK_LANE = D_QK
MASK_ROWS = 16
ONES_ROW = D_V
MASK_BIG = 1e30


def _with_ones_row(vt):
    row = lax.broadcasted_iota(jnp.int32, (HEAD_PAD, 1), 0)
    return jnp.where(row == ONES_ROW, 1.0, vt)


def _prep_kernel(pos_ref, invf_ref, q_ref, ks_ref, kw_ref, vs_ref, vw_ref, gn_ref,
                 qt_ref, kso_ref, kwo_ref, vst_ref, vwt_ref, gt_ref):
    tm = q_ref.shape[0]
    tabs = _rope_tables(pos_ref[...], invf_ref[...])
    for h in range(N_HEADS):
        sl = slice(h * HEAD_PAD, (h + 1) * HEAD_PAD)
        qt_ref[sl, :] = (_rope_head(q_ref[:, sl], tabs) * Q_SCALE).T.astype(BF16)
    row = pl.program_id(0) * tm + lax.broadcasted_iota(jnp.int32, (tm, 1), 0)
    lane = lax.broadcasted_iota(jnp.int32, (1, HEAD_PAD), 1)
    block_tag = jnp.where(lane == MASK_LANE + ((row >> 6) & (MASK_ROWS - 1)), 1.0, 0.0)
    for g in range(N_KV_GROUPS):
        sl = slice(g * HEAD_PAD, (g + 1) * HEAD_PAD)
        kso_ref[:, sl] = (_rope_head(ks_ref[:, sl], tabs) + block_tag).astype(BF16)
        kwo_ref[:, sl] = _rope_head(kw_ref[:, sl], tabs).astype(BF16)
        vst_ref[sl, :] = _with_ones_row(vs_ref[:, sl].T).astype(BF16)
        vwt_ref[sl, :] = _with_ones_row(vw_ref[:, sl].T).astype(BF16)
        gt_ref[sl, :] = jax.nn.sigmoid(gn_ref[:, sl]).T


def _prep(proj, pos_col, invf):
    T = proj.shape[0]
    tm = min(PREP_TM, T)
    kv = N_KV_GROUPS * HEAD_PAD
    qw = N_HEADS * HEAD_PAD

    def col(width, off):
        return pl.BlockSpec((tm, width), lambda i: (i, off // width))

    rows = lambda width: pl.BlockSpec((tm, width), lambda i: (i, 0))
    cols = lambda height: pl.BlockSpec((height, tm), lambda i: (0, i))
    return pl.pallas_call(
        _prep_kernel,
        grid=(T // tm,),
        in_specs=[pl.BlockSpec((tm, 1), lambda i: (i, 0)), _const_spec((1, HEAD_PAD)),
                  col(qw, C_Q), col(kv, C_KS), col(kv, C_KW), col(kv, C_VS), col(kv, C_VW), col(kv, C_GN)],
        out_specs=[cols(qw), rows(kv), rows(kv), cols(kv), cols(kv), cols(kv)],
        out_shape=[jax.ShapeDtypeStruct((qw, T), BF16), jax.ShapeDtypeStruct((T, kv), BF16),
                   jax.ShapeDtypeStruct((T, kv), BF16), jax.ShapeDtypeStruct((kv, T), BF16),
                   jax.ShapeDtypeStruct((kv, T), BF16), jax.ShapeDtypeStruct((kv, T), F32)],
        compiler_params=_params("arbitrary"),
        name="nsa_prep",
    )(pos_col, invf, proj, proj, proj, proj, proj, proj)


def _cmp_kernel(x_ref, pe_ref, w1_ref, w2_ref, pos_ref, invf_ref, o_ref, *, is_key):
    nchunk = pos_ref.shape[0]
    hid_dim = w1_ref.shape[2]
    first = jnp.zeros((nchunk, hid_dim), F32)
    second = jnp.zeros((nchunk, hid_dim), F32)
    for l in range(CMP_STRIDE):
        rows = x_ref[pl.ds(l, nchunk, stride=CMP_STRIDE), :]
        first += jnp.dot((rows + pe_ref[l:l + 1, :]).astype(BF16), w1_ref[l],
                         preferred_element_type=F32)
        second += jnp.dot((rows + pe_ref[CMP_STRIDE + l:CMP_STRIDE + l + 1, :]).astype(BF16),
                          w1_ref[CMP_STRIDE + l], preferred_element_type=F32)
    second = jnp.concatenate([second[1:], jnp.zeros((1, hid_dim), F32)], axis=0)
    pre = first + second
    hid = pre * jax.nn.sigmoid(pre)
    if is_key:
        out = jnp.dot(hid.astype(BF16), w2_ref[...], preferred_element_type=F32)
        o_ref[0] = _rope_head(out, _rope_tables(pos_ref[...], invf_ref[...])).astype(BF16)
    else:
        out_t = lax.dot_general(w2_ref[...], hid.astype(BF16), (((1,), (1,)), ((), ())),
                                preferred_element_type=F32)
        o_ref[0] = _with_ones_row(out_t).astype(BF16)


def _compress(proj, col_off, pe_p, w1_p, w2_p, pos_cmp, invf, is_key):
    T = proj.shape[0]
    nchunk = T // CMP_STRIDE
    hid = w1_p.shape[2]
    out_block = (1, nchunk, HEAD_PAD) if is_key else (1, HEAD_PAD, nchunk)
    return pl.pallas_call(
        functools.partial(_cmp_kernel, is_key=is_key),
        grid=(N_KV_GROUPS,),
        in_specs=[pl.BlockSpec((T, HEAD_PAD), lambda g: (0, col_off // HEAD_PAD + g)),
                  _const_spec((CMP_LEN, HEAD_PAD)), _const_spec((CMP_LEN, HEAD_PAD, hid)),
                  _const_spec(w2_p.shape), _const_spec((nchunk, 1)), _const_spec((1, HEAD_PAD))],
        out_specs=pl.BlockSpec(out_block, lambda g: (g, 0, 0)),
        out_shape=jax.ShapeDtypeStruct((N_KV_GROUPS,) + out_block[1:], BF16),
        compiler_params=_params("arbitrary"),
        name="compress",
    )(proj, pe_p, w1_p, w2_p, pos_cmp, invf)


SLC_TK = 1024
WIN_KEYS = WINDOW + Q_BLOCK
R = HEADS_PER_GROUP


def _lanes4(a):
    return jnp.concatenate([a] * R, axis=1)


def _attn_kernel(qt_ref, kc_ref, vct_ref, ks_ref, vst_ref, kw_ref, vwt_ref, gate_ref, c2st_ref, o_ref,
                 qa_ref, neg_ref, acc_ref, m_ref, *, n_sel):
    ci = pl.program_id(1)
    s = ci * Q_BLOCK
    qt = jnp.concatenate([qt_ref[r * HEAD_PAD:(r + 1) * HEAD_PAD, :] for r in range(R)], axis=1)
    tok = s + lax.broadcasted_iota(jnp.int32, (1, Q_BLOCK), 1)
    tok4 = _lanes4(tok)

    n_cmp = kc_ref.shape[1]
    cmp_end = lax.broadcasted_iota(jnp.int32, (n_cmp, 1), 0) * CMP_STRIDE + (CMP_LEN - 1)
    sc = jnp.dot(kc_ref[0], qt, preferred_element_type=F32)
    sc = sc + _lanes4(jnp.where(cmp_end <= tok, 0.0, NEG_INF))
    p = jnp.exp2(sc - jnp.max(sc, axis=0, keepdims=True))
    inv = jnp.where(tok4 >= CMP_LEN - 1, 1.0 / jnp.sum(p, axis=0, keepdims=True), 0.0)
    p = p * inv
    o_cmp = jnp.dot(vct_ref[0], p.astype(BF16), preferred_element_type=F32)

    p_sum = p[:, 0:Q_BLOCK]
    for r in range(1, R):
        p_sum = p_sum + p[:, r * Q_BLOCK:(r + 1) * Q_BLOCK]
    p_hi = p_sum.astype(BF16)
    p_lo = (p_sum - p_hi.astype(F32)).astype(BF16)
    imp = (jnp.dot(c2st_ref[...], p_hi, preferred_element_type=F32)
           + jnp.dot(c2st_ref[...], p_lo, preferred_element_type=F32))

    n_blk = c2st_ref.shape[0]
    jj = lax.broadcasted_iota(jnp.int32, (n_blk, 1), 0)
    cur = tok >> 6
    forced = (jj == 0) | (jj == cur) | (jj == cur - 1)
    score = jnp.where(jj <= cur, jnp.where(forced, SEL_FORCE, imp), NEG_INF)
    chosen = jnp.zeros((n_blk, Q_BLOCK), jnp.bool_)
    for _ in range(n_sel):
        best = jnp.max(score, axis=0, keepdims=True)
        hit = jj == jnp.min(jnp.where(score == best, jj, n_blk), axis=0, keepdims=True)
        chosen = chosen | (hit & (best > 0.5 * NEG_INF))
        score = jnp.where(hit, REMOVED, score)
    neg_ref[...] = _lanes4(jnp.where(chosen, 0.0, -MASK_BIG)).astype(BF16)

    qa_ref[...] = qt
    acc_ref[...] = jnp.zeros_like(acc_ref)
    m_ref[...] = jnp.full_like(m_ref, NEG_INF)

    def tile(jt, causal):
        k0 = pl.multiple_of(jt * SLC_TK, SLC_TK)
        blk0 = pl.multiple_of((k0 // (SLC_LEN * MASK_ROWS)) * MASK_ROWS, MASK_ROWS)
        qa_ref[MASK_LANE:MASK_LANE + MASK_ROWS, :] = neg_ref[pl.ds(blk0, MASK_ROWS), :]
        st = jnp.dot(ks_ref[pl.ds(k0, SLC_TK), :], qa_ref[...], preferred_element_type=F32)
        if causal:
            kpos = k0 + lax.broadcasted_iota(jnp.int32, (SLC_TK, 1), 0)
            st = st + _lanes4(jnp.where(kpos <= tok, 0.0, NEG_INF))
        m_old = m_ref[...]
        m_new = jnp.maximum(m_old, jnp.max(st, axis=0, keepdims=True))
        pt = jnp.exp2(st - m_new).astype(BF16)
        acc_ref[...] = (jnp.exp2(m_old - m_new) * acc_ref[...]
                        + jnp.dot(vst_ref[:, pl.ds(k0, SLC_TK)], pt, preferred_element_type=F32))
        m_ref[...] = m_new

    def full_tile(jt, carry):
        tile(jt, False)
        return carry

    n_full = s // SLC_TK
    lax.fori_loop(0, n_full, full_tile, 0)
    tile(n_full, True)
    acc = acc_ref[...]
    o_slc = acc * (1.0 / acc[ONES_ROW:ONES_ROW + 1, :])

    w0 = pl.multiple_of(jnp.maximum(s - WINDOW, 0), Q_BLOCK)
    wpos = w0 + lax.broadcasted_iota(jnp.int32, (WIN_KEYS, 1), 0)
    sw = jnp.dot(kw_ref[pl.ds(w0, WIN_KEYS), :], qt, preferred_element_type=F32)
    sw = sw + _lanes4(jnp.where((wpos <= tok) & (wpos > tok - WINDOW), 0.0, NEG_INF))
    pw = jnp.exp2(sw - jnp.max(sw, axis=0, keepdims=True)).astype(BF16)
    o_win = jnp.dot(vwt_ref[:, pl.ds(w0, WIN_KEYS)], pw, preferred_element_type=F32)
    o_win = o_win * (1.0 / o_win[ONES_ROW:ONES_ROW + 1, :])

    gate = gate_ref[...]
    for r in range(R):
        ls = slice(r * Q_BLOCK, (r + 1) * Q_BLOCK)
        o = (gate[3 * r:3 * r + 1, :] * o_cmp[:, ls] + gate[3 * r + 1:3 * r + 2, :] * o_slc[:, ls]
             + gate[3 * r + 2:3 * r + 3, :] * o_win[:, ls])
        o_ref[:, r * HEAD_PAD:(r + 1) * HEAD_PAD] = o.T.astype(BF16)


def _attention(qt, k_cmp, v_cmp_t, ks_r, vs_t, kw_r, vw_t, gate_t, c2st):
    T = ks_r.shape[0]
    n_cmp = k_cmp.shape[1]
    n_blk = c2st.shape[0]
    n_sel = min(N_SELECT, T // SLC_LEN)
    gw = R * HEAD_PAD
    rows = lambda g, ci: (0, g)
    cols = lambda g, ci: (g, 0)
    return pl.pallas_call(
        functools.partial(_attn_kernel, n_sel=n_sel),
        grid=(N_KV_GROUPS, T // Q_BLOCK),
        in_specs=[pl.BlockSpec((gw, Q_BLOCK), lambda g, ci: (g, ci)),
                  pl.BlockSpec((1, n_cmp, HEAD_PAD), lambda g, ci: (g, 0, 0)),
                  pl.BlockSpec((1, HEAD_PAD, n_cmp), lambda g, ci: (g, 0, 0)),
                  pl.BlockSpec((T, HEAD_PAD), rows), pl.BlockSpec((HEAD_PAD, T), cols),
                  pl.BlockSpec((T, HEAD_PAD), rows), pl.BlockSpec((HEAD_PAD, T), cols),
                  pl.BlockSpec((HEAD_PAD, Q_BLOCK), lambda g, ci: (g, ci)),
                  _const_spec((n_blk, n_cmp))],
        out_specs=pl.BlockSpec((Q_BLOCK, gw), lambda g, ci: (ci, g)),
        out_shape=jax.ShapeDtypeStruct((T, N_HEADS * HEAD_PAD), BF16),
        scratch_shapes=[pltpu.VMEM((HEAD_PAD, R * Q_BLOCK), BF16), pltpu.VMEM((n_blk, R * Q_BLOCK), BF16),
                        pltpu.VMEM((HEAD_PAD, R * Q_BLOCK), F32), pltpu.VMEM((1, R * Q_BLOCK), F32)],
        compiler_params=_params("arbitrary", "arbitrary"),
        name="nsa_attention",
    )(qt, k_cmp, v_cmp_t, ks_r, vs_t, kw_r, vw_t, gate_t, c2st)


MIX_TM = 256


def _mix_kernel(u_ref, up_ref, gm_ref, o_ref, pw_ref, ps_ref, wpu_ref, wnu_ref, y_ref):
    i = pl.program_id(0)
    tm = u_ref.shape[0]
    u = u_ref[...]
    prev = up_ref[...] * (i > 0).astype(F32)
    ext = jnp.concatenate([prev, u], axis=0)
    t = i * tm + lax.broadcasted_iota(jnp.int32, (tm, 1), 0)
    parts = []
    for g, win in enumerate(POOL_WINDOWS):
        cs = slice(g * POOL_GROUP, (g + 1) * POOL_GROUP)
        run = ext[:, cs]
        k = 1
        while k < win:
            run = run[k:] + run[:-k]
            k *= 2
        lo = POOL_HALO - (win - 1)
        total = run[lo:lo + tm]
        count = jnp.minimum(t + 1, win).astype(F32)
        d = total / count - u[:, cs]
        parts.append(jnp.dot(d.astype(BF16), pw_ref[g], preferred_element_type=F32) * ps_ref[:, cs])
    a = jnp.concatenate(parts, axis=1).astype(BF16)
    ya = jnp.dot(a, wpu_ref[...], preferred_element_type=F32)
    yb = jnp.dot(o_ref[...], wnu_ref[...], preferred_element_type=F32)
    gm = jax.nn.sigmoid(gm_ref[...])
    y_ref[...] = (gm[:, :D_MODEL] * ya + gm[:, D_MODEL:] * yb).astype(BF16)


def _mix(proj, o, pool_w_b, pool_scale, w_pool_up_b, w_nsa_up_p):
    T = proj.shape[0]
    tm = min(MIX_TM, T)
    halo_blocks = tm // POOL_HALO
    return pl.pallas_call(
        _mix_kernel,
        grid=(T // tm,),
        in_specs=[pl.BlockSpec((tm, POOL_WIDTH), lambda i: (i, C_U // POOL_WIDTH)),
                  pl.BlockSpec((POOL_HALO, POOL_WIDTH),
                               lambda i: (jnp.maximum(i * halo_blocks - 1, 0), C_U // POOL_WIDTH)),
                  pl.BlockSpec((tm, 2 * D_MODEL), lambda i: (i, 0)),
                  pl.BlockSpec((tm, N_HEADS * HEAD_PAD), lambda i: (i, 0)),
                  _const_spec(pool_w_b.shape), _const_spec((1, POOL_WIDTH)),
                  _const_spec(w_pool_up_b.shape), _const_spec(w_nsa_up_p.shape)],
        out_specs=pl.BlockSpec((tm, D_MODEL), lambda i: (i, 0)),
        out_shape=jax.ShapeDtypeStruct((T, D_MODEL), BF16),
        compiler_params=_params("arbitrary"),
        name="pool_merge",
    )(proj, proj, proj, o, pool_w_b, pool_scale, w_pool_up_b, w_nsa_up_p)


OUT_TM = 256


def _rms(y, g):
    return y * lax.rsqrt(jnp.mean(y * y, axis=-1, keepdims=True) + RMS_EPS) * g


def _outproj_kernel(y_ref, w_ref, x_ref, g_ref, gt_ref, o_ref):
    z = jnp.dot(y_ref[...], w_ref[...], preferred_element_type=F32)
    o_ref[...] = x_ref[...] + gt_ref[...] * _rms(z, g_ref[...])


def _outproj(y, w_out_b, x2, g_post, mod):
    T = y.shape[0]
    tm = min(OUT_TM, T)
    row = pl.BlockSpec((tm, D_MODEL), lambda i: (i, 0))
    return pl.pallas_call(
        _outproj_kernel,
        grid=(T // tm,),
        in_specs=[row, _const_spec((D_MODEL, D_MODEL)), row, _const_spec((1, D_MODEL)), _mod_spec(2)],
        out_specs=row,
        out_shape=jax.ShapeDtypeStruct((T, D_MODEL), F32),
        compiler_params=_params("arbitrary"),
        name="out_proj",
    )(y, w_out_b, x2, g_post, mod)


def _pad_heads(w, n_heads, d_head):
    k = w.shape[0]
    w = w.reshape(k, n_heads, d_head)
    return jnp.pad(w, ((0, 0), (0, 0), (0, HEAD_PAD - d_head))).reshape(k, n_heads * HEAD_PAD)


def _pack_w_in(w_in):
    sizes = (POOL_WIDTH, N_HEADS * D_QK, N_KV_GROUPS * D_QK, N_KV_GROUPS * D_V, N_KV_GROUPS * D_QK,
             N_KV_GROUPS * D_V, N_KV_GROUPS * D_QK, N_KV_GROUPS * D_V, 3 * N_HEADS, 2 * D_MODEL)
    offs = np.cumsum((0,) + sizes)
    u, q, kc, vc, ks, vs, kw, vw, gn, gm = [w_in[:, offs[i]:offs[i + 1]] for i in range(len(sizes))]
    cols = [gm, _pad_heads(q, N_HEADS, D_QK), u,
            _pad_heads(kc, N_KV_GROUPS, D_QK), _pad_heads(ks, N_KV_GROUPS, D_QK),
            _pad_heads(kw, N_KV_GROUPS, D_QK), _pad_heads(vc, N_KV_GROUPS, D_V),
            _pad_heads(vs, N_KV_GROUPS, D_V), _pad_heads(vw, N_KV_GROUPS, D_V),
            _pad_heads(gn, N_KV_GROUPS, 3 * HEADS_PER_GROUP)]
    return jnp.concatenate(cols, axis=1).astype(BF16)


def _slc_from_cmp(n_blk, n_chunk):
    cs = np.arange(n_chunk)[None, :] * CMP_STRIDE
    ss = np.arange(n_blk)[:, None] * SLC_LEN
    ov = np.clip(np.minimum(cs + CMP_LEN, ss + SLC_LEN) - np.maximum(cs, ss), 0, None)
    return jnp.asarray(ov / CMP_LEN, dtype=BF16)


def _token_mixer_block(x2, pos_col, mod, g_pre, g_post, w_in, cmp_pos_k, cmp_k_w1, cmp_k_w2,
                       cmp_pos_v, cmp_v_w1, cmp_v_w2, pool_w, pool_scale, w_pool_up, w_nsa_up, w_out):
    T = x2.shape[0]
    half = jnp.arange(ROT_HALF, dtype=F32)
    inv = jnp.power(jnp.float32(ROPE_THETA), -half * 2.0 / (2 * ROT_HALF))
    invf = jnp.concatenate([inv, inv, jnp.zeros((HEAD_PAD - 2 * ROT_HALF,), F32)])[None, :]

    proj = _proj(x2, g_pre, mod, _pack_w_in(w_in))
    qt, ks_r, kw_r, vs_t, vw_t, gate_t = _prep(proj, pos_col, invf)

    n_chunk = T // CMP_STRIDE
    pos_cmp = jnp.pad(pos_col[CMP_LEN - 1::CMP_STRIDE], ((0, 1), (0, 0)))

    def pad_lanes(a, d):
        return jnp.pad(a, [(0, 0)] * (a.ndim - 1) + [(0, HEAD_PAD - d)])

    def w1_pack(w1, d):
        w1 = w1.reshape(CMP_LEN, d, w1.shape[1])
        return jnp.pad(w1, ((0, 0), (0, HEAD_PAD - d), (0, 0))).astype(BF16)

    k_cmp = _compress(proj, C_KC, pad_lanes(cmp_pos_k, D_QK), w1_pack(cmp_k_w1, D_QK),
                      pad_lanes(cmp_k_w2, D_QK).astype(BF16), pos_cmp, invf, True)
    v_cmp = _compress(proj, C_VC, pad_lanes(cmp_pos_v, D_V), w1_pack(cmp_v_w1, D_V),
                      pad_lanes(cmp_v_w2, D_V).T.astype(BF16), pos_cmp, invf, False)

    n_blk = max(T // SLC_LEN, MASK_ROWS)
    o = _attention(qt, k_cmp, v_cmp, ks_r, vs_t, kw_r, vw_t, gate_t, _slc_from_cmp(n_blk, n_chunk))

    w_nsa_up_p = jnp.pad(w_nsa_up.reshape(N_HEADS, D_V, D_MODEL),
                         ((0, 0), (0, HEAD_PAD - D_V), (0, 0))).reshape(N_HEADS * HEAD_PAD, D_MODEL)
    y = _mix(proj, o, pool_w.astype(BF16), pool_scale, w_pool_up.astype(BF16), w_nsa_up_p.astype(BF16))
    return _outproj(y, w_out.astype(BF16), x2, g_post, mod)


ROUTE_TM = 256
EXPERTS_PER_GROUP = N_EXPERTS // N_EXPERT_GROUPS


def _split3(a):
    hi = a.astype(BF16)
    r1 = a - hi.astype(F32)
    mid = r1.astype(BF16)
    lo = (r1 - mid.astype(F32)).astype(BF16)
    return hi, mid, lo


def _dot_f32(a, b):
    a3, b3 = _split3(a), _split3(b)
    out = None
    for i, j in ((2, 0), (1, 1), (0, 2), (1, 0), (0, 1), (0, 0)):
        term = jnp.dot(a3[i], b3[j], preferred_element_type=F32)
        out = term if out is None else out + term
    return out


def _first_index_of(mask, lane, n):
    return jnp.min(jnp.where(mask, lane, n), axis=-1, keepdims=True)


def _route_kernel(x_ref, g_ref, sc_ref, sh_ref, wr_ref, rb_ref,
                  hf_ref, hb_ref, sel_ref, wd_ref, rank_ref, cnt_ref, carry_ref):
    i = pl.program_id(0)
    tm = x_ref.shape[0]

    @pl.when(i == 0)
    def _():
        carry_ref[...] = jnp.zeros_like(carry_ref)

    h = _modulated_norm(x_ref[...], g_ref[...], sc_ref[...], sh_ref[...])
    hf_ref[...] = h
    hb_ref[...] = h.astype(BF16)
    logits = _dot_f32(h, wr_ref[...])
    aff = jax.nn.sigmoid(logits)
    biased = aff + rb_ref[...]
    lane = lax.broadcasted_iota(jnp.int32, (1, N_EXPERTS), 1)
    grp = lane // EXPERTS_PER_GROUP

    gs = []
    for gi in range(N_EXPERT_GROUPS):
        v = jnp.where(grp == gi, biased, REMOVED)
        m1 = jnp.max(v, axis=-1, keepdims=True)
        i1 = _first_index_of(v == m1, lane, N_EXPERTS)
        m2 = jnp.max(jnp.where(lane == i1, REMOVED, v), axis=-1, keepdims=True)
        gs.append(m1 + m2)
    keep = jnp.zeros((tm, N_EXPERTS), jnp.bool_)
    for gi in range(N_EXPERT_GROUPS):
        ahead = jnp.zeros((tm, 1), jnp.int32)
        for gj in range(N_EXPERT_GROUPS):
            if gj < gi:
                ahead += (gs[gj] >= gs[gi]).astype(jnp.int32)
            elif gj > gi:
                ahead += (gs[gj] > gs[gi]).astype(jnp.int32)
        keep = keep | ((grp == gi) & (ahead < TOPK_GROUPS))
    masked = jnp.where(keep, biased, NEG_INF)
    sel = jnp.zeros((tm, N_EXPERTS), F32)
    for _ in range(TOP_K):
        best = jnp.max(masked, axis=-1, keepdims=True)
        hit = lane == _first_index_of(masked == best, lane, N_EXPERTS)
        sel = jnp.where(hit, 1.0, sel)
        masked = jnp.where(hit, REMOVED, masked)
    w = aff * sel
    wd_ref[...] = w / jnp.sum(w, axis=-1, keepdims=True) * ROUTED_SCALE
    sel_ref[...] = sel

    r_i = lax.broadcasted_iota(jnp.int32, (tm, tm), 0)
    c_i = lax.broadcasted_iota(jnp.int32, (tm, tm), 1)
    before = jnp.where(c_i < r_i, 1.0, 0.0).astype(BF16)
    carry = carry_ref[...]
    rank_ref[...] = jnp.dot(before, sel.astype(BF16), preferred_element_type=F32) + carry
    carry = carry + jnp.sum(sel, axis=0, keepdims=True)
    carry_ref[...] = carry
    cnt_ref[...] = carry


def _route(x1, g_pre, mod, w_router, router_bias):
    T = x1.shape[0]
    tm = min(ROUTE_TM, T)
    row = pl.BlockSpec((tm, D_MODEL), lambda i: (i, 0))
    erow = pl.BlockSpec((tm, N_EXPERTS), lambda i: (i, 0))
    e_shape = jax.ShapeDtypeStruct((T, N_EXPERTS), F32)
    return pl.pallas_call(
        _route_kernel,
        grid=(T // tm,),
        in_specs=[row, _const_spec((1, D_MODEL)), _mod_spec(4), _mod_spec(3),
                  _const_spec((D_MODEL, N_EXPERTS)), _const_spec((1, N_EXPERTS))],
        out_specs=[row, row, erow, erow, erow, _const_spec((1, N_EXPERTS))],
        out_shape=[jax.ShapeDtypeStruct((T, D_MODEL), F32), jax.ShapeDtypeStruct((T, D_MODEL), BF16),
                   e_shape, e_shape, e_shape, jax.ShapeDtypeStruct((1, N_EXPERTS), F32)],
        scratch_shapes=[pltpu.VMEM((1, N_EXPERTS), F32)],
        compiler_params=_params("arbitrary"),
        name="moe_route",
    )(x1, g_pre, mod, mod, w_router, router_bias)


def _dest_kernel(sel_ref, wd_ref, rank_ref, ps_ref, dest_ref, w8_ref):
    tm = sel_ref.shape[0]
    slot = ps_ref[...] + rank_ref[...]
    wd = wd_ref[...]
    lane = lax.broadcasted_iota(jnp.int32, (1, N_EXPERTS), 1)
    lane_k = lax.broadcasted_iota(jnp.int32, (1, TOP_K), 1)
    dest = jnp.zeros((tm, TOP_K), F32)
    w8 = jnp.zeros((tm, TOP_K), F32)
    rem = sel_ref[...]
    for k in range(TOP_K):
        hit = lane == _first_index_of(rem > 0.5, lane, N_EXPERTS)
        dest = jnp.where(lane_k == k, jnp.sum(jnp.where(hit, slot, 0.0), axis=-1, keepdims=True), dest)
        w8 = jnp.where(lane_k == k, jnp.sum(jnp.where(hit, wd, 0.0), axis=-1, keepdims=True), w8)
        rem = jnp.where(hit, 0.0, rem)
    dest_ref[...] = dest.astype(jnp.int32)
    w8_ref[...] = w8


def _dest(sel, wd, rank, pstart):
    T = sel.shape[0]
    tm = min(ROUTE_TM, T)
    erow = pl.BlockSpec((tm, N_EXPERTS), lambda i: (i, 0))
    krow = pl.BlockSpec((tm, TOP_K), lambda i: (i, 0))
    return pl.pallas_call(
        _dest_kernel,
        grid=(T // tm,),
        in_specs=[erow, erow, erow, _const_spec((1, N_EXPERTS))],
        out_specs=[krow, krow],
        out_shape=[jax.ShapeDtypeStruct((T, TOP_K), jnp.int32), jax.ShapeDtypeStruct((T, TOP_K), F32)],
        compiler_params=_params("arbitrary"),
        name="moe_dest",
    )(sel, wd, rank, pstart)


DISPATCH_TM = 128


def _dispatch_kernel(dest_ref, h_ref, xs_hbm, sem):
    tm = h_ref.shape[0]

    def body(t, carry):
        for k in range(TOP_K):
            d = dest_ref[t * TOP_K + k]
            pltpu.make_async_copy(h_ref.at[pl.ds(t, 1), :], xs_hbm.at[pl.ds(d, 1), :], sem).start()
        return carry

    lax.fori_loop(0, tm, body, 0)
    for k in range(TOP_K):
        pltpu.make_async_copy(h_ref, xs_hbm.at[pl.ds(0, tm), :], sem).wait()


def _dispatch(dest_flat, hf, n_slots):
    T = hf.shape[0]
    tm = min(DISPATCH_TM, T)
    return pl.pallas_call(
        _dispatch_kernel,
        grid=(T // tm,),
        in_specs=[pl.BlockSpec((tm * TOP_K,), lambda i: (i,), memory_space=pltpu.SMEM),
                  pl.BlockSpec((tm, D_MODEL), lambda i: (i, 0))],
        out_specs=pl.BlockSpec(memory_space=pl.ANY),
        out_shape=jax.ShapeDtypeStruct((n_slots, D_MODEL), F32),
        scratch_shapes=[pltpu.SemaphoreType.DMA(())],
        compiler_params=pltpu.CompilerParams(dimension_semantics=("arbitrary",), vmem_limit_bytes=VMEM_LIMIT,
                                             has_side_effects=True),
        name="moe_dispatch",
    )(dest_flat, hf)


EXPERT_BLOCK = 256


def _swiglu(x, wg, wu, wd):
    g = jnp.dot(x, wg, preferred_element_type=F32)
    u = jnp.dot(x, wu, preferred_element_type=F32)
    hid = (g * jax.nn.sigmoid(g)) * u
    return jnp.dot(hid.astype(BF16), wd, preferred_element_type=F32)


def _expert_items(counts, n_items):
    ends = jnp.cumsum(counts)
    starts = ends - counts
    first_blk = starts // EXPERT_BLOCK
    last_blk = (ends - 1) // EXPERT_BLOCK
    per_expert = jnp.where(counts > 0, last_blk - first_blk + 1, 0)
    item_end = jnp.cumsum(per_expert)
    n_used = item_end[-1]
    i = jnp.minimum(jnp.arange(n_items, dtype=jnp.int32), jnp.maximum(n_used - 1, 0))
    e = jnp.sum((item_end[None, :] <= i[:, None]).astype(jnp.int32), axis=1)
    e = jnp.minimum(e, N_EXPERTS - 1)
    blk = first_blk[e] + i - (item_end[e] - per_expert[e])
    lo = jnp.clip(starts[e] - blk * EXPERT_BLOCK, 0, EXPERT_BLOCK)
    hi = jnp.clip(ends[e] - blk * EXPERT_BLOCK, 0, EXPERT_BLOCK)
    live = jnp.arange(n_items, dtype=jnp.int32) < n_used
    hi = jnp.where(live, hi, lo)
    new_expert = jnp.concatenate([jnp.ones((1,), jnp.bool_), e[1:] != e[:-1]])
    new_block = jnp.concatenate([jnp.ones((1,), jnp.bool_), blk[1:] != blk[:-1]])
    as_i32 = lambda a: a.astype(jnp.int32)
    return (as_i32(blk), e, as_i32(lo), as_i32(hi), as_i32(new_expert & live), as_i32(new_block & live),
            as_i32(live), starts)


def _expert_kernel(blk_ref, e_ref, lo_ref, hi_ref, newe_ref, newb_ref, live_ref,
                   x_ref, wg_ref, wu_ref, wd_ref, y_ref, wgb, wub, wdb):
    i = pl.program_id(0)

    @pl.when(live_ref[i] == 1)
    def _():
        @pl.when(newe_ref[i] == 1)
        def _():
            wgb[...] = wg_ref[0].astype(BF16)
            wub[...] = wu_ref[0].astype(BF16)
            wdb[...] = wd_ref[0].astype(BF16)

        row = lax.broadcasted_iota(jnp.int32, (EXPERT_BLOCK, 1), 0)
        mine = (row >= lo_ref[i]) & (row < hi_ref[i])
        y = _swiglu(jnp.where(mine, x_ref[...], 0.0).astype(BF16), wgb[...], wub[...], wdb[...])

        @pl.when(newb_ref[i] == 1)
        def _():
            y_ref[...] = y

        @pl.when(newb_ref[i] == 0)
        def _():
            y_ref[...] += y


def _experts(items, xs, w_gate, w_up, w_down):
    n_items = items[0].shape[0]
    by_block = lambda i, blk, e, *_: (blk[i], 0)
    by_expert = lambda i, blk, e, *_: (e[i], 0, 0)
    grid_spec = pltpu.PrefetchScalarGridSpec(
        num_scalar_prefetch=len(items),
        grid=(n_items,),
        in_specs=[pl.BlockSpec((EXPERT_BLOCK, D_MODEL), by_block),
                  pl.BlockSpec((1, D_MODEL, D_EXPERT), by_expert),
                  pl.BlockSpec((1, D_MODEL, D_EXPERT), by_expert),
                  pl.BlockSpec((1, D_EXPERT, D_MODEL), by_expert)],
        out_specs=pl.BlockSpec((EXPERT_BLOCK, D_MODEL), by_block),
        scratch_shapes=[pltpu.VMEM((D_MODEL, D_EXPERT), BF16), pltpu.VMEM((D_MODEL, D_EXPERT), BF16),
                        pltpu.VMEM((D_EXPERT, D_MODEL), BF16)],
    )
    return pl.pallas_call(
        _expert_kernel,
        grid_spec=grid_spec,
        out_shape=jax.ShapeDtypeStruct(xs.shape, F32),
        compiler_params=_params("arbitrary"),
        name="moe_experts",
    )(*items, xs, w_gate, w_up, w_down)


COMBINE_TM = 128


def _combine_kernel(dcur_ref, dnext_ref, w8_ref, hb_ref, x_ref, wsg_ref, wsu_ref, wsd_ref, g_ref, gt_ref,
                    y_hbm, o_ref, gbuf, sem):
    i = pl.program_id(0)
    n = pl.num_programs(0)
    tm = x_ref.shape[0]

    def issue(d_ref, slot):
        def body(t, carry):
            for k in range(TOP_K):
                d = d_ref[t * TOP_K + k]
                pltpu.make_async_copy(y_hbm.at[pl.ds(d, 1), :], gbuf.at[slot, pl.ds(k * tm + t, 1), :],
                                      sem.at[slot]).start()
            return carry

        lax.fori_loop(0, tm, body, 0)

    @pl.when(i == 0)
    def _():
        issue(dcur_ref, 0)

    @pl.when(i + 1 < n)
    def _():
        issue(dnext_ref, (i + 1) % 2)

    slot = i % 2
    shared = _swiglu(hb_ref[...], wsg_ref[...], wsu_ref[...], wsd_ref[...])
    for k in range(TOP_K):
        pltpu.make_async_copy(y_hbm.at[pl.ds(0, tm), :], gbuf.at[slot, pl.ds(k * tm, tm), :],
                              sem.at[slot]).wait()
    w8 = w8_ref[...]
    y = shared
    for k in range(TOP_K):
        y = y + w8[:, k:k + 1] * gbuf[slot, pl.ds(k * tm, tm), :]
    o_ref[...] = x_ref[...] + gt_ref[...] * _rms(y, g_ref[...])


def _combine(dest_flat, w8, hb, x1, wsg, wsu, wsd, g_post, mod, y_buf):
    T = x1.shape[0]
    tm = min(COMBINE_TM, T)
    n = T // tm
    row = pl.BlockSpec((tm, D_MODEL), lambda i: (i, 0))
    return pl.pallas_call(
        _combine_kernel,
        grid=(n,),
        in_specs=[pl.BlockSpec((tm * TOP_K,), lambda i: (i,), memory_space=pltpu.SMEM),
                  pl.BlockSpec((tm * TOP_K,), lambda i: (jnp.minimum(i + 1, n - 1),), memory_space=pltpu.SMEM),
                  pl.BlockSpec((tm, TOP_K), lambda i: (i, 0)), row, row,
                  _const_spec(wsg.shape), _const_spec(wsu.shape), _const_spec(wsd.shape),
                  _const_spec((1, D_MODEL)), _mod_spec(5),
                  pl.BlockSpec(memory_space=pl.ANY)],
        out_specs=row,
        out_shape=jax.ShapeDtypeStruct((T, D_MODEL), F32),
        scratch_shapes=[pltpu.VMEM((2, TOP_K * tm, D_MODEL), F32), pltpu.SemaphoreType.DMA((2,))],
        compiler_params=_params("arbitrary"),
        name="moe_combine",
    )(dest_flat, dest_flat, w8, hb, x1, wsg, wsu, wsd, g_post, mod, y_buf)


def _moe_block(x1, mod, g_pre, g_post, w_router, router_bias, w_exp_gate, w_exp_up, w_exp_down,
               w_sh_gate, w_sh_up, w_sh_down):
    T = x1.shape[0]
    hf, hb, sel, wd, rank, counts = _route(x1, g_pre, mod, w_router, router_bias)

    n_slots = T * TOP_K
    *items, starts = _expert_items(counts[0].astype(jnp.int32), n_slots // EXPERT_BLOCK + N_EXPERTS)

    dest, w8 = _dest(sel, wd, rank, starts.astype(F32)[None, :])
    dest_flat = dest.reshape(n_slots)
    xs = _dispatch(dest_flat, hf, n_slots)
    y_buf = _experts(items, xs, w_exp_gate, w_exp_up, w_exp_down)
    return _combine(dest_flat, w8, hb, x1, w_sh_gate.astype(BF16), w_sh_up.astype(BF16),
                    w_sh_down.astype(BF16), g_post, mod, y_buf)


def kernel(x, c, positions, w_ada, b_ada, g_pre_mix, g_post_mix, g_pre_ffn, g_post_ffn, w_in, cmp_pos_k, cmp_k_w1, cmp_k_w2, cmp_pos_v, cmp_v_w1, cmp_v_w2, pool_w, pool_scale, w_pool_up, w_nsa_up, w_out, w_router, router_bias, w_exp_gate, w_exp_up, w_exp_down, w_sh_gate, w_sh_up, w_sh_down):
    B, T, D = x.shape
    assert B == 1 and D == D_MODEL
    x2 = x.reshape(T, D)
    pos_col = positions.reshape(T, 1)
    for l in range(w_ada.shape[0]):
        mod = _ada(c.reshape(D, 1), w_ada[l], b_ada[l][None, :])
        x2 = _token_mixer_block(x2, pos_col, mod, g_pre_mix[l][None, :], g_post_mix[l][None, :], w_in[l],
                                cmp_pos_k[l], cmp_k_w1[l], cmp_k_w2[l], cmp_pos_v[l], cmp_v_w1[l],
                                cmp_v_w2[l], pool_w[l], pool_scale[l][None, :], w_pool_up[l], w_nsa_up[l],
                                w_out[l])
        x2 = _moe_block(x2, mod, g_pre_ffn[l][None, :], g_post_ffn[l][None, :], w_router[l],
                        router_bias[l][None, :], w_exp_gate[l], w_exp_up[l], w_exp_down[l],
                        w_sh_gate[l], w_sh_up[l], w_sh_down[l])
    return x2.reshape(B, T, D)
```

```python
import functools

import numpy as np
import jax
import jax.numpy as jnp
from jax import lax
from jax.experimental import pallas as pl
from jax.experimental.pallas import tpu as pltpu

F32 = jnp.float32
BF16 = jnp.bfloat16

D_MODEL = 2048
POOL_WIDTH = 1024
POOL_WINDOWS = (2, 4, 8, 16)
POOL_GROUP = 256
POOL_HALO = 16
N_HEADS = 16
N_KV_GROUPS = 4
HEADS_PER_GROUP = 4
D_QK = 96
D_V = 64
HEAD_PAD = 128
ROT_HALF = 12
ROPE_THETA = 500000.0
CMP_LEN = 32
CMP_STRIDE = 16
SLC_LEN = 64
N_SELECT = 16
WINDOW = 512
Q_BLOCK = 512
N_EXPERTS = 64
TOP_K = 8
N_EXPERT_GROUPS = 8
TOPK_GROUPS = 4
D_EXPERT = 512
ROUTED_SCALE = 2.5
RMS_EPS = 1e-6
NEG_INF = -1e30
SEL_FORCE = 1e30
REMOVED = -3e38

VMEM_LIMIT = 56 * 1024 * 1024

C_GM = 0
C_Q = 4096
C_U = 6144
C_KC = 7168
C_KS = 7680
C_KW = 8192
C_VC = 8704
C_VS = 9216
C_VW = 9728
C_GN = 10240
N_COLS = 10752


def _params(*sem):
    return pltpu.CompilerParams(dimension_semantics=sem, vmem_limit_bytes=VMEM_LIMIT)


def _const_spec(shape):
    nd = len(shape)
    return pl.BlockSpec(shape, lambda *_: (0,) * nd)


ADA_TN = 1536
ADA_ROWS = 64


def _ada_kernel(c_ref, w_ref, b_ref, o_ref, s_ref):
    c = c_ref[...]
    s_ref[...] = c * jax.nn.sigmoid(c)

    def body(k, acc):
        r = pl.multiple_of(k * ADA_ROWS, ADA_ROWS)
        for j in range(ADA_ROWS // 8):
            acc = acc + w_ref[pl.ds(r + 8 * j, 8), :] * s_ref[pl.ds(r + 8 * j, 8), :]
        return acc

    acc = lax.fori_loop(0, D_MODEL // ADA_ROWS, body, jnp.zeros((8, ADA_TN), F32))
    o_ref[...] = jnp.sum(acc, axis=0, keepdims=True) + b_ref[...]


def _ada(c_col, w_ada, b_ada):
    n = w_ada.shape[1]
    return pl.pallas_call(
        _ada_kernel,
        grid=(n // ADA_TN,),
        in_specs=[_const_spec((D_MODEL, 1)),
                  pl.BlockSpec((D_MODEL, ADA_TN), lambda j: (0, j)),
                  pl.BlockSpec((1, ADA_TN), lambda j: (0, j))],
        out_specs=pl.BlockSpec((1, ADA_TN), lambda j: (0, j)),
        out_shape=jax.ShapeDtypeStruct((1, n), F32),
        scratch_shapes=[pltpu.VMEM((D_MODEL, 1), F32)],
        compiler_params=_params("arbitrary"),
        name="ada",
    )(c_col, w_ada, b_ada)


def _mod_spec(k):
    return pl.BlockSpec((1, D_MODEL), lambda *_: (0, k))


PROJ_TM = 512
PROJ_TN = 1536


def _modulated_norm(x, g, sc, sh):
    y = x * lax.rsqrt(jnp.mean(x * x, axis=-1, keepdims=True) + RMS_EPS)
    return (y * g) * (1.0 + sc) + sh


def _proj_kernel(x_ref, g_ref, sc_ref, sh_ref, w_ref, o_ref, h_ref):
    @pl.when(pl.program_id(1) == 0)
    def _():
        h_ref[...] = _modulated_norm(x_ref[...], g_ref[...], sc_ref[...], sh_ref[...]).astype(BF16)

    o_ref[...] = jnp.dot(h_ref[...], w_ref[...], preferred_element_type=F32)


def _proj(x2, g_pre, mod, w_in_p):
    T = x2.shape[0]
    tm = min(PROJ_TM, T)
    return pl.pallas_call(
        _proj_kernel,
        grid=(T // tm, N_COLS // PROJ_TN),
        in_specs=[pl.BlockSpec((tm, D_MODEL), lambda i, j: (i, 0)),
                  _const_spec((1, D_MODEL)), _mod_spec(1), _mod_spec(0),
                  pl.BlockSpec((D_MODEL, PROJ_TN), lambda i, j: (0, j))],
        out_specs=pl.BlockSpec((tm, PROJ_TN), lambda i, j: (i, j)),
        out_shape=jax.ShapeDtypeStruct((T, N_COLS), F32),
        scratch_shapes=[pltpu.VMEM((tm, D_MODEL), BF16)],
        compiler_params=_params("arbitrary", "arbitrary"),
        name="in_proj",
    )(x2, g_pre, mod, mod, w_in_p)


def _rope_tables(pos_col, invf):
    ang = pos_col.astype(F32) * invf
    cos = jnp.cos(ang)
    sin = jnp.sin(ang)
    lane = lax.broadcasted_iota(jnp.int32, ang.shape, 1)
    sin_lo = jnp.where(lane < ROT_HALF, -sin, 0.0)
    sin_hi = jnp.where((lane >= ROT_HALF) & (lane < 2 * ROT_HALF), sin, 0.0)
    return cos, sin_lo, sin_hi


def _rope_head(x, tabs):
    cos, sin_lo, sin_hi = tabs
    return (x * cos + pltpu.roll(x, HEAD_PAD - ROT_HALF, 1) * sin_lo
            + pltpu.roll(x, ROT_HALF, 1) * sin_hi)


PREP_TM = 256


LOG2E = 1.4426950408889634
Q_SCALE = (D_QK ** -0.5) * LOG2E
MASK_LANE = D_QK
MASK_ROWS = 16
ONES_ROW = D_V
V_ROWS = 80
MASK_BIG = 1e30


def _with_ones_row(vt):
    row = lax.broadcasted_iota(jnp.int32, (HEAD_PAD, 1), 0)
    return jnp.where(row == ONES_ROW, 1.0, vt)


def _prep_kernel(pos_ref, invf_ref, q_ref, ks_ref, kw_ref, vs_ref, vw_ref, gn_ref,
                 qt_ref, kso_ref, kwo_ref, vst_ref, vwt_ref, gt_ref):
    tm = q_ref.shape[0]
    tabs = _rope_tables(pos_ref[...], invf_ref[...])
    for h in range(N_HEADS):
        sl = slice(h * HEAD_PAD, (h + 1) * HEAD_PAD)
        qt_ref[sl, :] = (_rope_head(q_ref[:, sl], tabs) * Q_SCALE).T.astype(BF16)
    row = pl.program_id(0) * tm + lax.broadcasted_iota(jnp.int32, (tm, 1), 0)
    lane = lax.broadcasted_iota(jnp.int32, (1, HEAD_PAD), 1)
    block_tag = jnp.where(lane == MASK_LANE + ((row >> 6) & (MASK_ROWS - 1)), 1.0, 0.0)
    for g in range(N_KV_GROUPS):
        sl = slice(g * HEAD_PAD, (g + 1) * HEAD_PAD)
        kso_ref[:, sl] = (_rope_head(ks_ref[:, sl], tabs) + block_tag).astype(BF16)
        kwo_ref[:, sl] = _rope_head(kw_ref[:, sl], tabs).astype(BF16)
        vst_ref[sl, :] = _with_ones_row(vs_ref[:, sl].T).astype(BF16)
        vwt_ref[sl, :] = _with_ones_row(vw_ref[:, sl].T).astype(BF16)
        gt_ref[sl, :] = jax.nn.sigmoid(gn_ref[:, sl]).T


def _prep(proj, pos_col, invf):
    T = proj.shape[0]
    tm = min(PREP_TM, T)
    kv = N_KV_GROUPS * HEAD_PAD
    qw = N_HEADS * HEAD_PAD

    def col(width, off):
        return pl.BlockSpec((tm, width), lambda i: (i, off // width))

    rows = lambda width: pl.BlockSpec((tm, width), lambda i: (i, 0))
    cols = lambda height: pl.BlockSpec((height, tm), lambda i: (0, i))
    return pl.pallas_call(
        _prep_kernel,
        grid=(T // tm,),
        in_specs=[pl.BlockSpec((tm, 1), lambda i: (i, 0)), _const_spec((1, HEAD_PAD)),
                  col(qw, C_Q), col(kv, C_KS), col(kv, C_KW), col(kv, C_VS), col(kv, C_VW), col(kv, C_GN)],
        out_specs=[cols(qw), rows(kv), rows(kv), cols(kv), cols(kv), cols(kv)],
        out_shape=[jax.ShapeDtypeStruct((qw, T), BF16), jax.ShapeDtypeStruct((T, kv), BF16),
                   jax.ShapeDtypeStruct((T, kv), BF16), jax.ShapeDtypeStruct((kv, T), BF16),
                   jax.ShapeDtypeStruct((kv, T), BF16), jax.ShapeDtypeStruct((kv, T), F32)],
        compiler_params=_params("arbitrary"),
        name="nsa_prep",
    )(pos_col, invf, proj, proj, proj, proj, proj, proj)


def _cmp_kernel(x_ref, pe_ref, w1_ref, w2_ref, pos_ref, invf_ref, o_ref, *, is_key):
    nchunk = pos_ref.shape[0]
    hid_dim = w1_ref.shape[2]
    first = jnp.zeros((nchunk, hid_dim), F32)
    second = jnp.zeros((nchunk, hid_dim), F32)
    for l in range(CMP_STRIDE):
        rows = x_ref[pl.ds(l, nchunk, stride=CMP_STRIDE), :]
        first += jnp.dot((rows + pe_ref[l:l + 1, :]).astype(BF16), w1_ref[l],
                         preferred_element_type=F32)
        second += jnp.dot((rows + pe_ref[CMP_STRIDE + l:CMP_STRIDE + l + 1, :]).astype(BF16),
                          w1_ref[CMP_STRIDE + l], preferred_element_type=F32)
    second = jnp.concatenate([second[1:], jnp.zeros((1, hid_dim), F32)], axis=0)
    pre = first + second
    hid = pre * jax.nn.sigmoid(pre)
    if is_key:
        out = jnp.dot(hid.astype(BF16), w2_ref[...], preferred_element_type=F32)
        o_ref[0] = _rope_head(out, _rope_tables(pos_ref[...], invf_ref[...])).astype(BF16)
    else:
        out_t = lax.dot_general(w2_ref[...], hid.astype(BF16), (((1,), (1,)), ((), ())),
                                preferred_element_type=F32)
        o_ref[0] = _with_ones_row(out_t).astype(BF16)


def _compress(proj, col_off, pe_p, w1_p, w2_p, pos_cmp, invf, is_key):
    T = proj.shape[0]
    nchunk = T // CMP_STRIDE
    hid = w1_p.shape[2]
    out_block = (1, nchunk, HEAD_PAD) if is_key else (1, HEAD_PAD, nchunk)
    return pl.pallas_call(
        functools.partial(_cmp_kernel, is_key=is_key),
        grid=(N_KV_GROUPS,),
        in_specs=[pl.BlockSpec((T, HEAD_PAD), lambda g: (0, col_off // HEAD_PAD + g)),
                  _const_spec((CMP_LEN, HEAD_PAD)), _const_spec((CMP_LEN, HEAD_PAD, hid)),
                  _const_spec(w2_p.shape), _const_spec((nchunk, 1)), _const_spec((1, HEAD_PAD))],
        out_specs=pl.BlockSpec(out_block, lambda g: (g, 0, 0)),
        out_shape=jax.ShapeDtypeStruct((N_KV_GROUPS,) + out_block[1:], BF16),
        compiler_params=_params("arbitrary"),
        name="compress",
    )(proj, pe_p, w1_p, w2_p, pos_cmp, invf)


SLC_TK = 1024
WIN_KEYS = WINDOW + Q_BLOCK
R = HEADS_PER_GROUP


def _lanes4(a):
    return jnp.concatenate([a] * R, axis=1)


def _attn_kernel(qt_ref, kc_ref, vct_ref, ks_ref, vst_ref, kw_ref, vwt_ref, gate_ref, c2st_ref, o_ref,
                 qa_ref, neg_ref, acc_ref, m_ref, *, n_sel):
    ci = pl.program_id(1)
    s = ci * Q_BLOCK
    qt = jnp.concatenate([qt_ref[r * HEAD_PAD:(r + 1) * HEAD_PAD, :] for r in range(R)], axis=1)
    tok = s + lax.broadcasted_iota(jnp.int32, (1, Q_BLOCK), 1)
    tok4 = _lanes4(tok)

    n_cmp = kc_ref.shape[1]
    cmp_end = lax.broadcasted_iota(jnp.int32, (n_cmp, 1), 0) * CMP_STRIDE + (CMP_LEN - 1)
    sc = jnp.dot(kc_ref[0], qt, preferred_element_type=F32)
    sc = sc + _lanes4(jnp.where(cmp_end <= tok, 0.0, NEG_INF))
    p = jnp.exp2(sc - jnp.max(sc, axis=0, keepdims=True))
    inv = jnp.where(tok4 >= CMP_LEN - 1, 1.0 / jnp.sum(p, axis=0, keepdims=True), 0.0)
    p = p * inv
    o_cmp = jnp.dot(vct_ref[0, 0:V_ROWS, :], p.astype(BF16), preferred_element_type=F32)

    p_sum = p[:, 0:Q_BLOCK]
    for r in range(1, R):
        p_sum = p_sum + p[:, r * Q_BLOCK:(r + 1) * Q_BLOCK]
    p_hi = p_sum.astype(BF16)
    p_lo = (p_sum - p_hi.astype(F32)).astype(BF16)
    imp = (jnp.dot(c2st_ref[...], p_hi, preferred_element_type=F32)
           + jnp.dot(c2st_ref[...], p_lo, preferred_element_type=F32))

    n_blk = c2st_ref.shape[0]
    jj = lax.broadcasted_iota(jnp.int32, (n_blk, 1), 0)
    cur = tok >> 6
    forced = (jj == 0) | (jj == cur) | (jj == cur - 1)
    score = jnp.where(jj <= cur, jnp.where(forced, SEL_FORCE, imp), NEG_INF)
    chosen = jnp.zeros((n_blk, Q_BLOCK), jnp.bool_)
    for _ in range(n_sel):
        best = jnp.max(score, axis=0, keepdims=True)
        hit = jj == jnp.min(jnp.where(score == best, jj, n_blk), axis=0, keepdims=True)
        chosen = chosen | (hit & (best > 0.5 * NEG_INF))
        score = jnp.where(hit, REMOVED, score)
    neg_ref[...] = _lanes4(jnp.where(chosen, 0.0, -MASK_BIG)).astype(BF16)

    qa_ref[...] = qt
    acc_ref[...] = jnp.zeros_like(acc_ref)
    m_ref[...] = jnp.full_like(m_ref, NEG_INF)

    def tile(jt, causal):
        k0 = pl.multiple_of(jt * SLC_TK, SLC_TK)
        blk0 = pl.multiple_of((k0 // (SLC_LEN * MASK_ROWS)) * MASK_ROWS, MASK_ROWS)
        qa_ref[MASK_LANE:MASK_LANE + MASK_ROWS, :] = neg_ref[pl.ds(blk0, MASK_ROWS), :]
        st = jnp.dot(ks_ref[pl.ds(k0, SLC_TK), :], qa_ref[...], preferred_element_type=F32)
        if causal:
            kpos = k0 + lax.broadcasted_iota(jnp.int32, (SLC_TK, 1), 0)
            st = st + _lanes4(jnp.where(kpos <= tok, 0.0, NEG_INF))
        m_old = m_ref[...]
        m_new = jnp.maximum(m_old, jnp.max(st, axis=0, keepdims=True))
        pt = jnp.exp2(st - m_new).astype(BF16)
        acc_ref[...] = (jnp.exp2(m_old - m_new) * acc_ref[...]
                        + jnp.dot(vst_ref[0:V_ROWS, pl.ds(k0, SLC_TK)], pt, preferred_element_type=F32))
        m_ref[...] = m_new

    def full_tile(jt, carry):
        tile(jt, False)
        return carry

    n_full = s // SLC_TK
    lax.fori_loop(0, n_full, full_tile, 0)
    tile(n_full, True)
    acc = acc_ref[...]
    o_slc = acc * (1.0 / acc[ONES_ROW:ONES_ROW + 1, :])

    w0 = pl.multiple_of(jnp.maximum(s - WINDOW, 0), Q_BLOCK)
    wpos = w0 + lax.broadcasted_iota(jnp.int32, (WIN_KEYS, 1), 0)
    sw = jnp.dot(kw_ref[pl.ds(w0, WIN_KEYS), :], qt, preferred_element_type=F32)
    sw = sw + _lanes4(jnp.where((wpos <= tok) & (wpos > tok - WINDOW), 0.0, NEG_INF))
    pw = jnp.exp2(sw - jnp.max(sw, axis=0, keepdims=True)).astype(BF16)
    o_win = jnp.dot(vwt_ref[0:V_ROWS, pl.ds(w0, WIN_KEYS)], pw, preferred_element_type=F32)
    o_win = o_win * (1.0 / o_win[ONES_ROW:ONES_ROW + 1, :])

    gate = gate_ref[...]
    for r in range(R):
        ls = slice(r * Q_BLOCK, (r + 1) * Q_BLOCK)
        o = (gate[3 * r:3 * r + 1, :] * o_cmp[:, ls] + gate[3 * r + 1:3 * r + 2, :] * o_slc[:, ls]
             + gate[3 * r + 2:3 * r + 3, :] * o_win[:, ls])
        o = jnp.concatenate([o, jnp.zeros((HEAD_PAD - V_ROWS, Q_BLOCK), F32)], axis=0)
        o_ref[:, r * HEAD_PAD:(r + 1) * HEAD_PAD] = o.T.astype(BF16)


def _attention(qt, k_cmp, v_cmp_t, ks_r, vs_t, kw_r, vw_t, gate_t, c2st):
    T = ks_r.shape[0]
    n_cmp = k_cmp.shape[1]
    n_blk = c2st.shape[0]
    n_sel = min(N_SELECT, T // SLC_LEN)
    gw = R * HEAD_PAD
    rows = lambda g, ci: (0, g)
    cols = lambda g, ci: (g, 0)
    return pl.pallas_call(
        functools.partial(_attn_kernel, n_sel=n_sel),
        grid=(N_KV_GROUPS, T // Q_BLOCK),
        in_specs=[pl.BlockSpec((gw, Q_BLOCK), lambda g, ci: (g, ci)),
                  pl.BlockSpec((1, n_cmp, HEAD_PAD), lambda g, ci: (g, 0, 0)),
                  pl.BlockSpec((1, HEAD_PAD, n_cmp), lambda g, ci: (g, 0, 0)),
                  pl.BlockSpec((T, HEAD_PAD), rows), pl.BlockSpec((HEAD_PAD, T), cols),
                  pl.BlockSpec((T, HEAD_PAD), rows), pl.BlockSpec((HEAD_PAD, T), cols),
                  pl.BlockSpec((HEAD_PAD, Q_BLOCK), lambda g, ci: (g, ci)),
                  _const_spec((n_blk, n_cmp))],
        out_specs=pl.BlockSpec((Q_BLOCK, gw), lambda g, ci: (ci, g)),
        out_shape=jax.ShapeDtypeStruct((T, N_HEADS * HEAD_PAD), BF16),
        scratch_shapes=[pltpu.VMEM((HEAD_PAD, R * Q_BLOCK), BF16), pltpu.VMEM((n_blk, R * Q_BLOCK), BF16),
                        pltpu.VMEM((V_ROWS, R * Q_BLOCK), F32), pltpu.VMEM((1, R * Q_BLOCK), F32)],
        compiler_params=_params("arbitrary", "arbitrary"),
        name="nsa_attention",
    )(qt, k_cmp, v_cmp_t, ks_r, vs_t, kw_r, vw_t, gate_t, c2st)


MIX_TM = 256


def _mix_kernel(u_ref, up_ref, gm_ref, o_ref, pw_ref, ps_ref, wpu_ref, wnu_ref, y_ref):
    i = pl.program_id(0)
    tm = u_ref.shape[0]
    u = u_ref[...]
    prev = up_ref[...] * (i > 0).astype(F32)
    ext = jnp.concatenate([prev, u], axis=0)
    t = i * tm + lax.broadcasted_iota(jnp.int32, (tm, 1), 0)
    parts = []
    for g, win in enumerate(POOL_WINDOWS):
        cs = slice(g * POOL_GROUP, (g + 1) * POOL_GROUP)
        run = ext[:, cs]
        k = 1
        while k < win:
            run = run[k:] + run[:-k]
            k *= 2
        lo = POOL_HALO - (win - 1)
        total = run[lo:lo + tm]
        count = jnp.minimum(t + 1, win).astype(F32)
        d = total / count - u[:, cs]
        parts.append(jnp.dot(d.astype(BF16), pw_ref[g], preferred_element_type=F32) * ps_ref[:, cs])
    a = jnp.concatenate(parts, axis=1).astype(BF16)
    ya = jnp.dot(a, wpu_ref[...], preferred_element_type=F32)
    yb = jnp.dot(o_ref[...], wnu_ref[...], preferred_element_type=F32)
    gm = jax.nn.sigmoid(gm_ref[...])
    y_ref[...] = (gm[:, :D_MODEL] * ya + gm[:, D_MODEL:] * yb).astype(BF16)


def _mix(proj, o, pool_w_b, pool_scale, w_pool_up_b, w_nsa_up_p):
    T = proj.shape[0]
    tm = min(MIX_TM, T)
    halo_blocks = tm // POOL_HALO
    return pl.pallas_call(
        _mix_kernel,
        grid=(T // tm,),
        in_specs=[pl.BlockSpec((tm, POOL_WIDTH), lambda i: (i, C_U // POOL_WIDTH)),
                  pl.BlockSpec((POOL_HALO, POOL_WIDTH),
                               lambda i: (jnp.maximum(i * halo_blocks - 1, 0), C_U // POOL_WIDTH)),
                  pl.BlockSpec((tm, 2 * D_MODEL), lambda i: (i, 0)),
                  pl.BlockSpec((tm, N_HEADS * HEAD_PAD), lambda i: (i, 0)),
                  _const_spec(pool_w_b.shape), _const_spec((1, POOL_WIDTH)),
                  _const_spec(w_pool_up_b.shape), _const_spec(w_nsa_up_p.shape)],
        out_specs=pl.BlockSpec((tm, D_MODEL), lambda i: (i, 0)),
        out_shape=jax.ShapeDtypeStruct((T, D_MODEL), BF16),
        compiler_params=_params("arbitrary"),
        name="pool_merge",
    )(proj, proj, proj, o, pool_w_b, pool_scale, w_pool_up_b, w_nsa_up_p)


OUT_TM = 256


def _rms(y, g):
    return y * lax.rsqrt(jnp.mean(y * y, axis=-1, keepdims=True) + RMS_EPS) * g


def _outproj_kernel(y_ref, w_ref, x_ref, g_ref, gt_ref, o_ref):
    z = jnp.dot(y_ref[...], w_ref[...], preferred_element_type=F32)
    o_ref[...] = x_ref[...] + gt_ref[...] * _rms(z, g_ref[...])


def _outproj(y, w_out_b, x2, g_post, mod):
    T = y.shape[0]
    tm = min(OUT_TM, T)
    row = pl.BlockSpec((tm, D_MODEL), lambda i: (i, 0))
    return pl.pallas_call(
        _outproj_kernel,
        grid=(T // tm,),
        in_specs=[row, _const_spec((D_MODEL, D_MODEL)), row, _const_spec((1, D_MODEL)), _mod_spec(2)],
        out_specs=row,
        out_shape=jax.ShapeDtypeStruct((T, D_MODEL), F32),
        compiler_params=_params("arbitrary"),
        name="out_proj",
    )(y, w_out_b, x2, g_post, mod)


def _pad_heads(w, n_heads, d_head):
    k = w.shape[0]
    w = w.reshape(k, n_heads, d_head)
    return jnp.pad(w, ((0, 0), (0, 0), (0, HEAD_PAD - d_head))).reshape(k, n_heads * HEAD_PAD)


def _pack_w_in(w_in):
    sizes = (POOL_WIDTH, N_HEADS * D_QK, N_KV_GROUPS * D_QK, N_KV_GROUPS * D_V, N_KV_GROUPS * D_QK,
             N_KV_GROUPS * D_V, N_KV_GROUPS * D_QK, N_KV_GROUPS * D_V, 3 * N_HEADS, 2 * D_MODEL)
    offs = np.cumsum((0,) + sizes)
    u, q, kc, vc, ks, vs, kw, vw, gn, gm = [w_in[:, offs[i]:offs[i + 1]] for i in range(len(sizes))]
    cols = [gm, _pad_heads(q, N_HEADS, D_QK), u,
            _pad_heads(kc, N_KV_GROUPS, D_QK), _pad_heads(ks, N_KV_GROUPS, D_QK),
            _pad_heads(kw, N_KV_GROUPS, D_QK), _pad_heads(vc, N_KV_GROUPS, D_V),
            _pad_heads(vs, N_KV_GROUPS, D_V), _pad_heads(vw, N_KV_GROUPS, D_V),
            _pad_heads(gn, N_KV_GROUPS, 3 * HEADS_PER_GROUP)]
    return jnp.concatenate(cols, axis=1).astype(BF16)


def _slc_from_cmp(n_blk, n_chunk):
    cs = np.arange(n_chunk)[None, :] * CMP_STRIDE
    ss = np.arange(n_blk)[:, None] * SLC_LEN
    ov = np.clip(np.minimum(cs + CMP_LEN, ss + SLC_LEN) - np.maximum(cs, ss), 0, None)
    return jnp.asarray(ov / CMP_LEN, dtype=BF16)


def _token_mixer_block(x2, pos_col, mod, g_pre, g_post, w_in, cmp_pos_k, cmp_k_w1, cmp_k_w2,
                       cmp_pos_v, cmp_v_w1, cmp_v_w2, pool_w, pool_scale, w_pool_up, w_nsa_up, w_out):
    T = x2.shape[0]
    half = jnp.arange(ROT_HALF, dtype=F32)
    inv = jnp.power(jnp.float32(ROPE_THETA), -half * 2.0 / (2 * ROT_HALF))
    invf = jnp.concatenate([inv, inv, jnp.zeros((HEAD_PAD - 2 * ROT_HALF,), F32)])[None, :]

    proj = _proj(x2, g_pre, mod, _pack_w_in(w_in))
    qt, ks_r, kw_r, vs_t, vw_t, gate_t = _prep(proj, pos_col, invf)

    n_chunk = T // CMP_STRIDE
    pos_cmp = jnp.pad(pos_col[CMP_LEN - 1::CMP_STRIDE], ((0, 1), (0, 0)))

    def pad_lanes(a, d):
        return jnp.pad(a, [(0, 0)] * (a.ndim - 1) + [(0, HEAD_PAD - d)])

    def w1_pack(w1, d):
        w1 = w1.reshape(CMP_LEN, d, w1.shape[1])
        return jnp.pad(w1, ((0, 0), (0, HEAD_PAD - d), (0, 0))).astype(BF16)

    k_cmp = _compress(proj, C_KC, pad_lanes(cmp_pos_k, D_QK), w1_pack(cmp_k_w1, D_QK),
                      pad_lanes(cmp_k_w2, D_QK).astype(BF16), pos_cmp, invf, True)
    v_cmp = _compress(proj, C_VC, pad_lanes(cmp_pos_v, D_V), w1_pack(cmp_v_w1, D_V),
                      pad_lanes(cmp_v_w2, D_V).T.astype(BF16), pos_cmp, invf, False)

    n_blk = max(T // SLC_LEN, MASK_ROWS)
    o = _attention(qt, k_cmp, v_cmp, ks_r, vs_t, kw_r, vw_t, gate_t, _slc_from_cmp(n_blk, n_chunk))

    w_nsa_up_p = jnp.pad(w_nsa_up.reshape(N_HEADS, D_V, D_MODEL),
                         ((0, 0), (0, HEAD_PAD - D_V), (0, 0))).reshape(N_HEADS * HEAD_PAD, D_MODEL)
    y = _mix(proj, o, pool_w.astype(BF16), pool_scale, w_pool_up.astype(BF16), w_nsa_up_p.astype(BF16))
    return _outproj(y, w_out.astype(BF16), x2, g_post, mod)


ROUTE_TM = 256
EXPERTS_PER_GROUP = N_EXPERTS // N_EXPERT_GROUPS


def _split3(a):
    hi = a.astype(BF16)
    r1 = a - hi.astype(F32)
    mid = r1.astype(BF16)
    lo = (r1 - mid.astype(F32)).astype(BF16)
    return hi, mid, lo


def _dot_f32(a, b):
    a3, b3 = _split3(a), _split3(b)
    out = None
    for i, j in ((2, 0), (1, 1), (0, 2), (1, 0), (0, 1), (0, 0)):
        term = jnp.dot(a3[i], b3[j], preferred_element_type=F32)
        out = term if out is None else out + term
    return out


def _first_index_of(mask, lane, n):
    return jnp.min(jnp.where(mask, lane, n), axis=-1, keepdims=True)


def _route_kernel(x_ref, g_ref, sc_ref, sh_ref, wr_ref, rb_ref,
                  hf_ref, hb_ref, sel_ref, wd_ref, rank_ref, cnt_ref, carry_ref):
    i = pl.program_id(0)
    tm = x_ref.shape[0]

    @pl.when(i == 0)
    def _():
        carry_ref[...] = jnp.zeros_like(carry_ref)

    h = _modulated_norm(x_ref[...], g_ref[...], sc_ref[...], sh_ref[...])
    hf_ref[...] = h
    hb_ref[...] = h.astype(BF16)
    logits = _dot_f32(h, wr_ref[...])
    aff = jax.nn.sigmoid(logits)
    biased = aff + rb_ref[...]
    lane = lax.broadcasted_iota(jnp.int32, (1, N_EXPERTS), 1)
    grp = lane // EXPERTS_PER_GROUP

    gs = []
    for gi in range(N_EXPERT_GROUPS):
        v = jnp.where(grp == gi, biased, REMOVED)
        m1 = jnp.max(v, axis=-1, keepdims=True)
        i1 = _first_index_of(v == m1, lane, N_EXPERTS)
        m2 = jnp.max(jnp.where(lane == i1, REMOVED, v), axis=-1, keepdims=True)
        gs.append(m1 + m2)
    keep = jnp.zeros((tm, N_EXPERTS), jnp.bool_)
    for gi in range(N_EXPERT_GROUPS):
        ahead = jnp.zeros((tm, 1), jnp.int32)
        for gj in range(N_EXPERT_GROUPS):
            if gj < gi:
                ahead += (gs[gj] >= gs[gi]).astype(jnp.int32)
            elif gj > gi:
                ahead += (gs[gj] > gs[gi]).astype(jnp.int32)
        keep = keep | ((grp == gi) & (ahead < TOPK_GROUPS))
    masked = jnp.where(keep, biased, NEG_INF)
    sel = jnp.zeros((tm, N_EXPERTS), F32)
    for _ in range(TOP_K):
        best = jnp.max(masked, axis=-1, keepdims=True)
        hit = lane == _first_index_of(masked == best, lane, N_EXPERTS)
        sel = jnp.where(hit, 1.0, sel)
        masked = jnp.where(hit, REMOVED, masked)
    w = aff * sel
    wd_ref[...] = w / jnp.sum(w, axis=-1, keepdims=True) * ROUTED_SCALE
    sel_ref[...] = sel

    r_i = lax.broadcasted_iota(jnp.int32, (tm, tm), 0)
    c_i = lax.broadcasted_iota(jnp.int32, (tm, tm), 1)
    before = jnp.where(c_i < r_i, 1.0, 0.0).astype(BF16)
    carry = carry_ref[...]
    rank_ref[...] = jnp.dot(before, sel.astype(BF16), preferred_element_type=F32) + carry
    carry = carry + jnp.sum(sel, axis=0, keepdims=True)
    carry_ref[...] = carry
    cnt_ref[...] = carry


def _route(x1, g_pre, mod, w_router, router_bias):
    T = x1.shape[0]
    tm = min(ROUTE_TM, T)
    row = pl.BlockSpec((tm, D_MODEL), lambda i: (i, 0))
    erow = pl.BlockSpec((tm, N_EXPERTS), lambda i: (i, 0))
    e_shape = jax.ShapeDtypeStruct((T, N_EXPERTS), F32)
    return pl.pallas_call(
        _route_kernel,
        grid=(T // tm,),
        in_specs=[row, _const_spec((1, D_MODEL)), _mod_spec(4), _mod_spec(3),
                  _const_spec((D_MODEL, N_EXPERTS)), _const_spec((1, N_EXPERTS))],
        out_specs=[row, row, erow, erow, erow, _const_spec((1, N_EXPERTS))],
        out_shape=[jax.ShapeDtypeStruct((T, D_MODEL), F32), jax.ShapeDtypeStruct((T, D_MODEL), BF16),
                   e_shape, e_shape, e_shape, jax.ShapeDtypeStruct((1, N_EXPERTS), F32)],
        scratch_shapes=[pltpu.VMEM((1, N_EXPERTS), F32)],
        compiler_params=_params("arbitrary"),
        name="moe_route",
    )(x1, g_pre, mod, mod, w_router, router_bias)


def _dest_kernel(sel_ref, wd_ref, rank_ref, ps_ref, dest_ref, w8_ref):
    tm = sel_ref.shape[0]
    slot = ps_ref[...] + rank_ref[...]
    wd = wd_ref[...]
    lane = lax.broadcasted_iota(jnp.int32, (1, N_EXPERTS), 1)
    lane_k = lax.broadcasted_iota(jnp.int32, (1, TOP_K), 1)
    dest = jnp.zeros((tm, TOP_K), F32)
    w8 = jnp.zeros((tm, TOP_K), F32)
    rem = sel_ref[...]
    for k in range(TOP_K):
        hit = lane == _first_index_of(rem > 0.5, lane, N_EXPERTS)
        dest = jnp.where(lane_k == k, jnp.sum(jnp.where(hit, slot, 0.0), axis=-1, keepdims=True), dest)
        w8 = jnp.where(lane_k == k, jnp.sum(jnp.where(hit, wd, 0.0), axis=-1, keepdims=True), w8)
        rem = jnp.where(hit, 0.0, rem)
    dest_ref[...] = dest.astype(jnp.int32)
    w8_ref[...] = w8


def _dest(sel, wd, rank, pstart):
    T = sel.shape[0]
    tm = min(ROUTE_TM, T)
    erow = pl.BlockSpec((tm, N_EXPERTS), lambda i: (i, 0))
    krow = pl.BlockSpec((tm, TOP_K), lambda i: (i, 0))
    return pl.pallas_call(
        _dest_kernel,
        grid=(T // tm,),
        in_specs=[erow, erow, erow, _const_spec((1, N_EXPERTS))],
        out_specs=[krow, krow],
        out_shape=[jax.ShapeDtypeStruct((T, TOP_K), jnp.int32), jax.ShapeDtypeStruct((T, TOP_K), F32)],
        compiler_params=_params("arbitrary"),
        name="moe_dest",
    )(sel, wd, rank, pstart)


DISPATCH_TM = 128


def _dispatch_kernel(dest_ref, h_ref, xs_hbm, sem):
    tm = h_ref.shape[0]

    def body(t, carry):
        for k in range(TOP_K):
            d = dest_ref[t * TOP_K + k]
            pltpu.make_async_copy(h_ref.at[pl.ds(t, 1), :], xs_hbm.at[pl.ds(d, 1), :], sem).start()
        return carry

    lax.fori_loop(0, tm, body, 0)
    for k in range(TOP_K):
        pltpu.make_async_copy(h_ref, xs_hbm.at[pl.ds(0, tm), :], sem).wait()


def _dispatch(dest_flat, hf, n_slots):
    T = hf.shape[0]
    tm = min(DISPATCH_TM, T)
    return pl.pallas_call(
        _dispatch_kernel,
        grid=(T // tm,),
        in_specs=[pl.BlockSpec((tm * TOP_K,), lambda i: (i,), memory_space=pltpu.SMEM),
                  pl.BlockSpec((tm, D_MODEL), lambda i: (i, 0))],
        out_specs=pl.BlockSpec(memory_space=pl.ANY),
        out_shape=jax.ShapeDtypeStruct((n_slots, D_MODEL), F32),
        scratch_shapes=[pltpu.SemaphoreType.DMA(())],
        compiler_params=pltpu.CompilerParams(dimension_semantics=("arbitrary",), vmem_limit_bytes=VMEM_LIMIT,
                                             has_side_effects=True),
        name="moe_dispatch",
    )(dest_flat, hf)


EXPERT_BLOCK = 256


def _swiglu(x, wg, wu, wd):
    g = jnp.dot(x, wg, preferred_element_type=F32)
    u = jnp.dot(x, wu, preferred_element_type=F32)
    hid = (g * jax.nn.sigmoid(g)) * u
    return jnp.dot(hid.astype(BF16), wd, preferred_element_type=F32)


def _expert_items(counts, n_items):
    ends = jnp.cumsum(counts)
    starts = ends - counts
    first_blk = starts // EXPERT_BLOCK
    last_blk = (ends - 1) // EXPERT_BLOCK
    per_expert = jnp.where(counts > 0, last_blk - first_blk + 1, 0)
    item_end = jnp.cumsum(per_expert)
    n_used = item_end[-1]
    i = jnp.minimum(jnp.arange(n_items, dtype=jnp.int32), jnp.maximum(n_used - 1, 0))
    e = jnp.sum((item_end[None, :] <= i[:, None]).astype(jnp.int32), axis=1)
    e = jnp.minimum(e, N_EXPERTS - 1)
    blk = first_blk[e] + i - (item_end[e] - per_expert[e])
    lo = jnp.clip(starts[e] - blk * EXPERT_BLOCK, 0, EXPERT_BLOCK)
    hi = jnp.clip(ends[e] - blk * EXPERT_BLOCK, 0, EXPERT_BLOCK)
    live = jnp.arange(n_items, dtype=jnp.int32) < n_used
    hi = jnp.where(live, hi, lo)
    new_expert = jnp.concatenate([jnp.ones((1,), jnp.bool_), e[1:] != e[:-1]])
    new_block = jnp.concatenate([jnp.ones((1,), jnp.bool_), blk[1:] != blk[:-1]])
    as_i32 = lambda a: a.astype(jnp.int32)
    ids = jnp.arange(N_EXPERTS, dtype=jnp.int32)
    used = counts > 0
    slot_of = (jnp.cumsum(as_i32(used)) - 1) % 2
    later = (ids[None, :] > ids[:, None]) & used[None, :]
    next_of = jnp.min(jnp.where(later, ids[None, :], N_EXPERTS), axis=1)
    next_of = jnp.where(next_of < N_EXPERTS, next_of, -1)
    return (as_i32(blk), as_i32(e), as_i32(lo), as_i32(hi), as_i32(new_expert & live),
            as_i32(new_block & live), as_i32(live), as_i32(slot_of[e]), as_i32(next_of[e]), starts)


def _expert_kernel(blk_ref, e_ref, lo_ref, hi_ref, newe_ref, newb_ref, live_ref, slot_ref, nexte_ref,
                   x_ref, wg_hbm, wu_hbm, wd_hbm, y_ref, wg32, wu32, wd32, wgb, wub, wdb, sem):
    i = pl.program_id(0)

    def weight_copies(e, slot):
        return (pltpu.make_async_copy(wg_hbm.at[e], wg32.at[slot], sem.at[slot]),
                pltpu.make_async_copy(wu_hbm.at[e], wu32.at[slot], sem.at[slot]),
                pltpu.make_async_copy(wd_hbm.at[e], wd32.at[slot], sem.at[slot]))

    @pl.when(live_ref[i] == 1)
    def _():
        @pl.when(newe_ref[i] == 1)
        def _():
            slot = slot_ref[i]

            @pl.when(i == 0)
            def _():
                for c in weight_copies(e_ref[i], slot):
                    c.start()

            for c in weight_copies(e_ref[i], slot):
                c.wait()
            wgb[...] = wg32[slot].astype(BF16)
            wub[...] = wu32[slot].astype(BF16)
            wdb[...] = wd32[slot].astype(BF16)

            @pl.when(nexte_ref[i] >= 0)
            def _():
                for c in weight_copies(nexte_ref[i], 1 - slot):
                    c.start()

        row = lax.broadcasted_iota(jnp.int32, (EXPERT_BLOCK, 1), 0)
        mine = (row >= lo_ref[i]) & (row < hi_ref[i])
        y = _swiglu(jnp.where(mine, x_ref[...], 0.0).astype(BF16), wgb[...], wub[...], wdb[...])

        @pl.when(newb_ref[i] == 1)
        def _():
            y_ref[...] = y

        @pl.when(newb_ref[i] == 0)
        def _():
            y_ref[...] += y


def _experts(items, xs, w_gate, w_up, w_down):
    n_items = items[0].shape[0]
    by_block = lambda i, blk, *_: (blk[i], 0)
    hbm = pl.BlockSpec(memory_space=pl.ANY)
    grid_spec = pltpu.PrefetchScalarGridSpec(
        num_scalar_prefetch=len(items),
        grid=(n_items,),
        in_specs=[pl.BlockSpec((EXPERT_BLOCK, D_MODEL), by_block), hbm, hbm, hbm],
        out_specs=pl.BlockSpec((EXPERT_BLOCK, D_MODEL), by_block),
        scratch_shapes=[pltpu.VMEM((2, D_MODEL, D_EXPERT), F32), pltpu.VMEM((2, D_MODEL, D_EXPERT), F32),
                        pltpu.VMEM((2, D_EXPERT, D_MODEL), F32),
                        pltpu.VMEM((D_MODEL, D_EXPERT), BF16), pltpu.VMEM((D_MODEL, D_EXPERT), BF16),
                        pltpu.VMEM((D_EXPERT, D_MODEL), BF16), pltpu.SemaphoreType.DMA((2,))],
    )
    return pl.pallas_call(
        _expert_kernel,
        grid_spec=grid_spec,
        out_shape=jax.ShapeDtypeStruct(xs.shape, F32),
        compiler_params=_params("arbitrary"),
        name="moe_experts",
    )(*items, xs, w_gate, w_up, w_down)


COMBINE_TM = 128


def _combine_kernel(dcur_ref, dnext_ref, w8_ref, hb_ref, x_ref, wsg_ref, wsu_ref, wsd_ref, g_ref, gt_ref,
                    y_hbm, o_ref, gbuf, sem):
    i = pl.program_id(0)
    n = pl.num_programs(0)
    tm = x_ref.shape[0]

    rows = TOP_K * tm

    def issue(d_ref, slot):
        base = slot * rows

        def body(t, carry):
            for k in range(TOP_K):
                pltpu.make_async_copy(y_hbm.at[pl.ds(d_ref[t * TOP_K + k], 1), :],
                                      gbuf.at[pl.ds(base + k * tm + t, 1), :], sem.at[slot]).start()
            return carry

        lax.fori_loop(0, tm, body, 0)

    @pl.when(i == 0)
    def _():
        issue(dcur_ref, 0)

    @pl.when(i + 1 < n)
    def _():
        issue(dnext_ref, (i + 1) % 2)

    slot = i % 2
    base = pl.multiple_of(slot * rows, rows)
    shared = _swiglu(hb_ref[...], wsg_ref[...], wsu_ref[...], wsd_ref[...])
    for k in range(TOP_K):
        pltpu.make_async_copy(y_hbm.at[pl.ds(0, tm), :], gbuf.at[pl.ds(base + k * tm, tm), :],
                              sem.at[slot]).wait()
    w8 = w8_ref[...]
    y = shared
    for k in range(TOP_K):
        y = y + w8[:, k:k + 1] * gbuf[pl.ds(base + k * tm, tm), :]
    o_ref[...] = x_ref[...] + gt_ref[...] * _rms(y, g_ref[...])


def _combine(dest_flat, w8, hb, x1, wsg, wsu, wsd, g_post, mod, y_buf):
    T = x1.shape[0]
    tm = min(COMBINE_TM, T)
    n = T // tm
    row = pl.BlockSpec((tm, D_MODEL), lambda i: (i, 0))
    return pl.pallas_call(
        _combine_kernel,
        grid=(n,),
        in_specs=[pl.BlockSpec((tm * TOP_K,), lambda i: (i,), memory_space=pltpu.SMEM),
                  pl.BlockSpec((tm * TOP_K,), lambda i: (jnp.minimum(i + 1, n - 1),), memory_space=pltpu.SMEM),
                  pl.BlockSpec((tm, TOP_K), lambda i: (i, 0)), row, row,
                  _const_spec(wsg.shape), _const_spec(wsu.shape), _const_spec(wsd.shape),
                  _const_spec((1, D_MODEL)), _mod_spec(5),
                  pl.BlockSpec(memory_space=pl.ANY)],
        out_specs=row,
        out_shape=jax.ShapeDtypeStruct((T, D_MODEL), F32),
        scratch_shapes=[pltpu.VMEM((2 * TOP_K * tm, D_MODEL), F32), pltpu.SemaphoreType.DMA((2,))],
        compiler_params=_params("arbitrary"),
        name="moe_combine",
    )(dest_flat, dest_flat, w8, hb, x1, wsg, wsu, wsd, g_post, mod, y_buf)


def _moe_block(x1, mod, g_pre, g_post, w_router, router_bias, w_exp_gate, w_exp_up, w_exp_down,
               w_sh_gate, w_sh_up, w_sh_down):
    T = x1.shape[0]
    hf, hb, sel, wd, rank, counts = _route(x1, g_pre, mod, w_router, router_bias)

    n_slots = T * TOP_K
    *items, starts = _expert_items(counts[0].astype(jnp.int32), n_slots // EXPERT_BLOCK + N_EXPERTS)

    dest, w8 = _dest(sel, wd, rank, starts.astype(F32)[None, :])
    dest_flat = dest.reshape(n_slots)
    xs = _dispatch(dest_flat, hf, n_slots)
    y_buf = _experts(items, xs, w_exp_gate, w_exp_up, w_exp_down)
    return _combine(dest_flat, w8, hb, x1, w_sh_gate.astype(BF16), w_sh_up.astype(BF16),
                    w_sh_down.astype(BF16), g_post, mod, y_buf)


def kernel(x, c, positions, w_ada, b_ada, g_pre_mix, g_post_mix, g_pre_ffn, g_post_ffn, w_in, cmp_pos_k, cmp_k_w1, cmp_k_w2, cmp_pos_v, cmp_v_w1, cmp_v_w2, pool_w, pool_scale, w_pool_up, w_nsa_up, w_out, w_router, router_bias, w_exp_gate, w_exp_up, w_exp_down, w_sh_gate, w_sh_up, w_sh_down):
    B, T, D = x.shape
    assert B == 1 and D == D_MODEL
    x2 = x.reshape(T, D)
    pos_col = positions.reshape(T, 1)
    for l in range(w_ada.shape[0]):
        mod = _ada(c.reshape(D, 1), w_ada[l], b_ada[l][None, :])
        x2 = _token_mixer_block(x2, pos_col, mod, g_pre_mix[l][None, :], g_post_mix[l][None, :], w_in[l],
                                cmp_pos_k[l], cmp_k_w1[l], cmp_k_w2[l], cmp_pos_v[l], cmp_v_w1[l],
                                cmp_v_w2[l], pool_w[l], pool_scale[l][None, :], w_pool_up[l], w_nsa_up[l],
                                w_out[l])
        x2 = _moe_block(x2, mod, g_pre_ffn[l][None, :], g_post_ffn[l][None, :], w_router[l],
                        router_bias[l][None, :], w_exp_gate[l], w_exp_up[l], w_exp_down[l],
                        w_sh_gate[l], w_sh_up[l], w_sh_down[l])
    return x2.reshape(B, T, D)
```

```python
import functools

import numpy as np
import jax
import jax.numpy as jnp
from jax import lax
from jax.experimental import pallas as pl
from jax.experimental.pallas import tpu as pltpu

F32 = jnp.float32
BF16 = jnp.bfloat16

D_MODEL = 2048
POOL_WIDTH = 1024
POOL_WINDOWS = (2, 4, 8, 16)
POOL_GROUP = 256
POOL_HALO = 16
N_HEADS = 16
N_KV_GROUPS = 4
HEADS_PER_GROUP = 4
D_QK = 96
D_V = 64
HEAD_PAD = 128
ROT_HALF = 12
ROPE_THETA = 500000.0
CMP_LEN = 32
CMP_STRIDE = 16
SLC_LEN = 64
N_SELECT = 16
WINDOW = 512
Q_BLOCK = 512
N_EXPERTS = 64
TOP_K = 8
N_EXPERT_GROUPS = 8
TOPK_GROUPS = 4
D_EXPERT = 512
ROUTED_SCALE = 2.5
RMS_EPS = 1e-6
NEG_INF = -1e30
SEL_FORCE = 1e30
REMOVED = -3e38

VMEM_LIMIT = 56 * 1024 * 1024

C_GM = 0
C_Q = 4096
C_U = 6144
C_KC = 7168
C_KS = 7680
C_KW = 8192
C_VC = 8704
C_VS = 9216
C_VW = 9728
C_GN = 10240
N_COLS = 10752


def _params(*sem):
    return pltpu.CompilerParams(dimension_semantics=sem, vmem_limit_bytes=VMEM_LIMIT)


def _const_spec(shape):
    nd = len(shape)
    return pl.BlockSpec(shape, lambda *_: (0,) * nd)


ADA_TN = 1536
ADA_ROWS = 64


def _ada_kernel(c_ref, w_ref, b_ref, o_ref, s_ref):
    c = c_ref[...]
    s_ref[...] = c * jax.nn.sigmoid(c)

    def body(k, acc):
        r = pl.multiple_of(k * ADA_ROWS, ADA_ROWS)
        for j in range(ADA_ROWS // 8):
            acc = acc + w_ref[pl.ds(r + 8 * j, 8), :] * s_ref[pl.ds(r + 8 * j, 8), :]
        return acc

    acc = lax.fori_loop(0, D_MODEL // ADA_ROWS, body, jnp.zeros((8, ADA_TN), F32))
    o_ref[...] = jnp.sum(acc, axis=0, keepdims=True) + b_ref[...]


def _ada(c_col, w_ada, b_ada):
    n = w_ada.shape[1]
    return pl.pallas_call(
        _ada_kernel,
        grid=(n // ADA_TN,),
        in_specs=[_const_spec((D_MODEL, 1)),
                  pl.BlockSpec((D_MODEL, ADA_TN), lambda j: (0, j)),
                  pl.BlockSpec((1, ADA_TN), lambda j: (0, j))],
        out_specs=pl.BlockSpec((1, ADA_TN), lambda j: (0, j)),
        out_shape=jax.ShapeDtypeStruct((1, n), F32),
        scratch_shapes=[pltpu.VMEM((D_MODEL, 1), F32)],
        compiler_params=_params("arbitrary"),
        name="ada",
    )(c_col, w_ada, b_ada)


def _mod_spec(k):
    return pl.BlockSpec((1, D_MODEL), lambda *_: (0, k))


PROJ_TM = 512
PROJ_TN = 1536


def _modulated_norm(x, g, sc, sh):
    y = x * lax.rsqrt(jnp.mean(x * x, axis=-1, keepdims=True) + RMS_EPS)
    return (y * g) * (1.0 + sc) + sh


def _proj_kernel(x_ref, g_ref, sc_ref, sh_ref, w_ref, o_ref, h_ref):
    @pl.when(pl.program_id(1) == 0)
    def _():
        h_ref[...] = _modulated_norm(x_ref[...], g_ref[...], sc_ref[...], sh_ref[...]).astype(BF16)

    o_ref[...] = jnp.dot(h_ref[...], w_ref[...], preferred_element_type=F32)


def _proj(x2, g_pre, mod, w_in_p):
    T = x2.shape[0]
    tm = min(PROJ_TM, T)
    return pl.pallas_call(
        _proj_kernel,
        grid=(T // tm, N_COLS // PROJ_TN),
        in_specs=[pl.BlockSpec((tm, D_MODEL), lambda i, j: (i, 0)),
                  _const_spec((1, D_MODEL)), _mod_spec(1), _mod_spec(0),
                  pl.BlockSpec((D_MODEL, PROJ_TN), lambda i, j: (0, j))],
        out_specs=pl.BlockSpec((tm, PROJ_TN), lambda i, j: (i, j)),
        out_shape=jax.ShapeDtypeStruct((T, N_COLS), F32),
        scratch_shapes=[pltpu.VMEM((tm, D_MODEL), BF16)],
        compiler_params=_params("arbitrary", "arbitrary"),
        name="in_proj",
    )(x2, g_pre, mod, mod, w_in_p)


def _rope_tables(pos_col, invf):
    ang = pos_col.astype(F32) * invf
    cos = jnp.cos(ang)
    sin = jnp.sin(ang)
    lane = lax.broadcasted_iota(jnp.int32, ang.shape, 1)
    sin_lo = jnp.where(lane < ROT_HALF, -sin, 0.0)
    sin_hi = jnp.where((lane >= ROT_HALF) & (lane < 2 * ROT_HALF), sin, 0.0)
    return cos, sin_lo, sin_hi


def _rope_head(x, tabs):
    cos, sin_lo, sin_hi = tabs
    return (x * cos + pltpu.roll(x, HEAD_PAD - ROT_HALF, 1) * sin_lo
            + pltpu.roll(x, ROT_HALF, 1) * sin_hi)


PREP_TM = 256


LOG2E = 1.4426950408889634
Q_SCALE = (D_QK ** -0.5) * LOG2E
MASK_LANE = D_QK
MASK_ROWS = 16
ONES_ROW = D_V
V_ROWS = 80
MASK_BIG = 1e30


def _with_ones_row(vt):
    row = lax.broadcasted_iota(jnp.int32, (HEAD_PAD, 1), 0)
    return jnp.where(row == ONES_ROW, 1.0, vt)


def _prep_kernel(pos_ref, invf_ref, q_ref, ks_ref, kw_ref, vs_ref, vw_ref, gn_ref,
                 qt_ref, kso_ref, kwo_ref, vst_ref, vwt_ref, gt_ref):
    tm = q_ref.shape[0]
    tabs = _rope_tables(pos_ref[...], invf_ref[...])
    for h in range(N_HEADS):
        sl = slice(h * HEAD_PAD, (h + 1) * HEAD_PAD)
        qt_ref[sl, :] = (_rope_head(q_ref[:, sl], tabs) * Q_SCALE).T.astype(BF16)
    row = pl.program_id(0) * tm + lax.broadcasted_iota(jnp.int32, (tm, 1), 0)
    lane = lax.broadcasted_iota(jnp.int32, (1, HEAD_PAD), 1)
    block_tag = jnp.where(lane == MASK_LANE + ((row >> 6) & (MASK_ROWS - 1)), 1.0, 0.0)
    for g in range(N_KV_GROUPS):
        sl = slice(g * HEAD_PAD, (g + 1) * HEAD_PAD)
        kso_ref[:, sl] = (_rope_head(ks_ref[:, sl], tabs) + block_tag).astype(BF16)
        kwo_ref[:, sl] = _rope_head(kw_ref[:, sl], tabs).astype(BF16)
        vst_ref[sl, :] = _with_ones_row(vs_ref[:, sl].T).astype(BF16)
        vwt_ref[sl, :] = _with_ones_row(vw_ref[:, sl].T).astype(BF16)
        gt_ref[sl, :] = jax.nn.sigmoid(gn_ref[:, sl]).T


def _prep(proj, pos_col, invf):
    T = proj.shape[0]
    tm = min(PREP_TM, T)
    kv = N_KV_GROUPS * HEAD_PAD
    qw = N_HEADS * HEAD_PAD

    def col(width, off):
        return pl.BlockSpec((tm, width), lambda i: (i, off // width))

    rows = lambda width: pl.BlockSpec((tm, width), lambda i: (i, 0))
    cols = lambda height: pl.BlockSpec((height, tm), lambda i: (0, i))
    return pl.pallas_call(
        _prep_kernel,
        grid=(T // tm,),
        in_specs=[pl.BlockSpec((tm, 1), lambda i: (i, 0)), _const_spec((1, HEAD_PAD)),
                  col(qw, C_Q), col(kv, C_KS), col(kv, C_KW), col(kv, C_VS), col(kv, C_VW), col(kv, C_GN)],
        out_specs=[cols(qw), rows(kv), rows(kv), cols(kv), cols(kv), cols(kv)],
        out_shape=[jax.ShapeDtypeStruct((qw, T), BF16), jax.ShapeDtypeStruct((T, kv), BF16),
                   jax.ShapeDtypeStruct((T, kv), BF16), jax.ShapeDtypeStruct((kv, T), BF16),
                   jax.ShapeDtypeStruct((kv, T), BF16), jax.ShapeDtypeStruct((kv, T), F32)],
        compiler_params=_params("arbitrary"),
        name="nsa_prep",
    )(pos_col, invf, proj, proj, proj, proj, proj, proj)


def _cmp_kernel(x_ref, pe_ref, w1_ref, w2_ref, pos_ref, invf_ref, o_ref, *, is_key):
    nchunk = pos_ref.shape[0]
    hid_dim = w1_ref.shape[2]
    first = jnp.zeros((nchunk, hid_dim), F32)
    second = jnp.zeros((nchunk, hid_dim), F32)
    for l in range(CMP_STRIDE):
        rows = x_ref[pl.ds(l, nchunk, stride=CMP_STRIDE), :]
        first += jnp.dot((rows + pe_ref[l:l + 1, :]).astype(BF16), w1_ref[l],
                         preferred_element_type=F32)
        second += jnp.dot((rows + pe_ref[CMP_STRIDE + l:CMP_STRIDE + l + 1, :]).astype(BF16),
                          w1_ref[CMP_STRIDE + l], preferred_element_type=F32)
    second = jnp.concatenate([second[1:], jnp.zeros((1, hid_dim), F32)], axis=0)
    pre = first + second
    hid = pre * jax.nn.sigmoid(pre)
    if is_key:
        out = jnp.dot(hid.astype(BF16), w2_ref[...], preferred_element_type=F32)
        o_ref[0] = _rope_head(out, _rope_tables(pos_ref[...], invf_ref[...])).astype(BF16)
    else:
        out_t = lax.dot_general(w2_ref[...], hid.astype(BF16), (((1,), (1,)), ((), ())),
                                preferred_element_type=F32)
        o_ref[0] = _with_ones_row(out_t).astype(BF16)


def _compress(proj, col_off, pe_p, w1_p, w2_p, pos_cmp, invf, is_key):
    T = proj.shape[0]
    nchunk = T // CMP_STRIDE
    hid = w1_p.shape[2]
    out_block = (1, nchunk, HEAD_PAD) if is_key else (1, HEAD_PAD, nchunk)
    return pl.pallas_call(
        functools.partial(_cmp_kernel, is_key=is_key),
        grid=(N_KV_GROUPS,),
        in_specs=[pl.BlockSpec((T, HEAD_PAD), lambda g: (0, col_off // HEAD_PAD + g)),
                  _const_spec((CMP_LEN, HEAD_PAD)), _const_spec((CMP_LEN, HEAD_PAD, hid)),
                  _const_spec(w2_p.shape), _const_spec((nchunk, 1)), _const_spec((1, HEAD_PAD))],
        out_specs=pl.BlockSpec(out_block, lambda g: (g, 0, 0)),
        out_shape=jax.ShapeDtypeStruct((N_KV_GROUPS,) + out_block[1:], BF16),
        compiler_params=_params("arbitrary"),
        name="compress",
    )(proj, pe_p, w1_p, w2_p, pos_cmp, invf)


SLC_TK = 1024
WIN_SUB = 256
WIN_KEYS = WINDOW + WIN_SUB
R = HEADS_PER_GROUP


def _lanes4(a):
    return jnp.concatenate([a] * R, axis=1)


def _attn_kernel(qt_ref, kc_ref, vct_ref, ks_ref, vst_ref, kw_ref, vwt_ref, gate_ref, c2st_ref, o_ref,
                 qa_ref, neg_ref, acc_ref, m_ref, *, n_sel):
    ci = pl.program_id(1)
    s = ci * Q_BLOCK
    qt = jnp.concatenate([qt_ref[r * HEAD_PAD:(r + 1) * HEAD_PAD, :] for r in range(R)], axis=1)
    tok = s + lax.broadcasted_iota(jnp.int32, (1, Q_BLOCK), 1)
    tok4 = _lanes4(tok)

    n_cmp = kc_ref.shape[1]
    cmp_end = lax.broadcasted_iota(jnp.int32, (n_cmp, 1), 0) * CMP_STRIDE + (CMP_LEN - 1)
    sc = jnp.dot(kc_ref[0], qt, preferred_element_type=F32)
    sc = sc + _lanes4(jnp.where(cmp_end <= tok, 0.0, NEG_INF))
    p = jnp.exp2(sc - jnp.max(sc, axis=0, keepdims=True))
    inv = jnp.where(tok4 >= CMP_LEN - 1, 1.0 / jnp.sum(p, axis=0, keepdims=True), 0.0)
    p = p * inv
    o_cmp = jnp.dot(vct_ref[0, 0:V_ROWS, :], p.astype(BF16), preferred_element_type=F32)

    p_sum = p[:, 0:Q_BLOCK]
    for r in range(1, R):
        p_sum = p_sum + p[:, r * Q_BLOCK:(r + 1) * Q_BLOCK]
    p_hi = p_sum.astype(BF16)
    p_lo = (p_sum - p_hi.astype(F32)).astype(BF16)
    imp = (jnp.dot(c2st_ref[...], p_hi, preferred_element_type=F32)
           + jnp.dot(c2st_ref[...], p_lo, preferred_element_type=F32))

    n_blk = c2st_ref.shape[0]
    jj = lax.broadcasted_iota(jnp.int32, (n_blk, 1), 0)
    cur = tok >> 6
    forced = (jj == 0) | (jj == cur) | (jj == cur - 1)
    score = jnp.where(jj <= cur, jnp.where(forced, SEL_FORCE, imp), NEG_INF)
    chosen = jnp.zeros((n_blk, Q_BLOCK), jnp.bool_)
    for _ in range(n_sel):
        best = jnp.max(score, axis=0, keepdims=True)
        hit = jj == jnp.min(jnp.where(score == best, jj, n_blk), axis=0, keepdims=True)
        chosen = chosen | (hit & (best > 0.5 * NEG_INF))
        score = jnp.where(hit, REMOVED, score)
    neg_ref[...] = _lanes4(jnp.where(chosen, 0.0, -MASK_BIG)).astype(BF16)

    qa_ref[...] = qt
    acc_ref[...] = jnp.zeros_like(acc_ref)
    m_ref[...] = jnp.full_like(m_ref, NEG_INF)

    def tile(jt, causal):
        k0 = pl.multiple_of(jt * SLC_TK, SLC_TK)
        blk0 = pl.multiple_of((k0 // (SLC_LEN * MASK_ROWS)) * MASK_ROWS, MASK_ROWS)
        qa_ref[MASK_LANE:MASK_LANE + MASK_ROWS, :] = neg_ref[pl.ds(blk0, MASK_ROWS), :]
        st = jnp.dot(ks_ref[pl.ds(k0, SLC_TK), :], qa_ref[...], preferred_element_type=F32)
        if causal:
            kpos = k0 + lax.broadcasted_iota(jnp.int32, (SLC_TK, 1), 0)
            st = st + _lanes4(jnp.where(kpos <= tok, 0.0, NEG_INF))
        m_old = m_ref[...]
        m_new = jnp.maximum(m_old, jnp.max(st, axis=0, keepdims=True))
        pt = jnp.exp2(st - m_new).astype(BF16)
        acc_ref[...] = (jnp.exp2(m_old - m_new) * acc_ref[...]
                        + jnp.dot(vst_ref[0:V_ROWS, pl.ds(k0, SLC_TK)], pt, preferred_element_type=F32))
        m_ref[...] = m_new

    def full_tile(jt, carry):
        tile(jt, False)
        return carry

    n_full = s // SLC_TK
    lax.fori_loop(0, n_full, full_tile, 0)
    tile(n_full, True)
    acc = acc_ref[...]
    o_slc = acc * (1.0 / acc[ONES_ROW:ONES_ROW + 1, :])

    win_parts = []
    for b in range(Q_BLOCK // WIN_SUB):
        sb = s + b * WIN_SUB
        tok_b = sb + lax.broadcasted_iota(jnp.int32, (1, WIN_SUB), 1)
        qt_b = jnp.concatenate([qt[:, r * Q_BLOCK + b * WIN_SUB:r * Q_BLOCK + (b + 1) * WIN_SUB]
                                for r in range(R)], axis=1)
        w0 = pl.multiple_of(jnp.maximum(sb - WINDOW, 0), WIN_SUB)
        wpos = w0 + lax.broadcasted_iota(jnp.int32, (WIN_KEYS, 1), 0)
        sw = jnp.dot(kw_ref[pl.ds(w0, WIN_KEYS), :], qt_b, preferred_element_type=F32)
        sw = sw + _lanes4(jnp.where((wpos <= tok_b) & (wpos > tok_b - WINDOW), 0.0, NEG_INF))
        pw = jnp.exp2(sw - jnp.max(sw, axis=0, keepdims=True)).astype(BF16)
        ow = jnp.dot(vwt_ref[0:V_ROWS, pl.ds(w0, WIN_KEYS)], pw, preferred_element_type=F32)
        win_parts.append(ow * (1.0 / ow[ONES_ROW:ONES_ROW + 1, :]))

    gate = gate_ref[...]
    for r in range(R):
        ls = slice(r * Q_BLOCK, (r + 1) * Q_BLOCK)
        o_win = jnp.concatenate([part[:, r * WIN_SUB:(r + 1) * WIN_SUB] for part in win_parts], axis=1)
        o = (gate[3 * r:3 * r + 1, :] * o_cmp[:, ls] + gate[3 * r + 1:3 * r + 2, :] * o_slc[:, ls]
             + gate[3 * r + 2:3 * r + 3, :] * o_win)
        o = jnp.concatenate([o, jnp.zeros((HEAD_PAD - V_ROWS, Q_BLOCK), F32)], axis=0)
        o_ref[:, r * HEAD_PAD:(r + 1) * HEAD_PAD] = o.T.astype(BF16)


def _attention(qt, k_cmp, v_cmp_t, ks_r, vs_t, kw_r, vw_t, gate_t, c2st):
    T = ks_r.shape[0]
    n_cmp = k_cmp.shape[1]
    n_blk = c2st.shape[0]
    n_sel = min(N_SELECT, T // SLC_LEN)
    gw = R * HEAD_PAD
    rows = lambda g, ci: (0, g)
    cols = lambda g, ci: (g, 0)
    return pl.pallas_call(
        functools.partial(_attn_kernel, n_sel=n_sel),
        grid=(N_KV_GROUPS, T // Q_BLOCK),
        in_specs=[pl.BlockSpec((gw, Q_BLOCK), lambda g, ci: (g, ci)),
                  pl.BlockSpec((1, n_cmp, HEAD_PAD), lambda g, ci: (g, 0, 0)),
                  pl.BlockSpec((1, HEAD_PAD, n_cmp), lambda g, ci: (g, 0, 0)),
                  pl.BlockSpec((T, HEAD_PAD), rows), pl.BlockSpec((HEAD_PAD, T), cols),
                  pl.BlockSpec((T, HEAD_PAD), rows), pl.BlockSpec((HEAD_PAD, T), cols),
                  pl.BlockSpec((HEAD_PAD, Q_BLOCK), lambda g, ci: (g, ci)),
                  _const_spec((n_blk, n_cmp))],
        out_specs=pl.BlockSpec((Q_BLOCK, gw), lambda g, ci: (ci, g)),
        out_shape=jax.ShapeDtypeStruct((T, N_HEADS * HEAD_PAD), BF16),
        scratch_shapes=[pltpu.VMEM((HEAD_PAD, R * Q_BLOCK), BF16), pltpu.VMEM((n_blk, R * Q_BLOCK), BF16),
                        pltpu.VMEM((V_ROWS, R * Q_BLOCK), F32), pltpu.VMEM((1, R * Q_BLOCK), F32)],
        compiler_params=_params("arbitrary", "arbitrary"),
        name="nsa_attention",
    )(qt, k_cmp, v_cmp_t, ks_r, vs_t, kw_r, vw_t, gate_t, c2st)


MIX_TM = 256


def _mix_kernel(u_ref, up_ref, gm_ref, o_ref, pw_ref, ps_ref, wpu_ref, wnu_ref, y_ref):
    i = pl.program_id(0)
    tm = u_ref.shape[0]
    u = u_ref[...]
    prev = up_ref[...] * (i > 0).astype(F32)
    ext = jnp.concatenate([prev, u], axis=0)
    t = i * tm + lax.broadcasted_iota(jnp.int32, (tm, 1), 0)
    parts = []
    for g, win in enumerate(POOL_WINDOWS):
        cs = slice(g * POOL_GROUP, (g + 1) * POOL_GROUP)
        run = ext[:, cs]
        k = 1
        while k < win:
            run = run[k:] + run[:-k]
            k *= 2
        lo = POOL_HALO - (win - 1)
        total = run[lo:lo + tm]
        count = jnp.minimum(t + 1, win).astype(F32)
        d = total / count - u[:, cs]
        parts.append(jnp.dot(d.astype(BF16), pw_ref[g], preferred_element_type=F32) * ps_ref[:, cs])
    a = jnp.concatenate(parts, axis=1).astype(BF16)
    ya = jnp.dot(a, wpu_ref[...], preferred_element_type=F32)
    yb = jnp.dot(o_ref[...], wnu_ref[...], preferred_element_type=F32)
    gm = jax.nn.sigmoid(gm_ref[...])
    y_ref[...] = (gm[:, :D_MODEL] * ya + gm[:, D_MODEL:] * yb).astype(BF16)


def _mix(proj, o, pool_w_b, pool_scale, w_pool_up_b, w_nsa_up_p):
    T = proj.shape[0]
    tm = min(MIX_TM, T)
    halo_blocks = tm // POOL_HALO
    return pl.pallas_call(
        _mix_kernel,
        grid=(T // tm,),
        in_specs=[pl.BlockSpec((tm, POOL_WIDTH), lambda i: (i, C_U // POOL_WIDTH)),
                  pl.BlockSpec((POOL_HALO, POOL_WIDTH),
                               lambda i: (jnp.maximum(i * halo_blocks - 1, 0), C_U // POOL_WIDTH)),
                  pl.BlockSpec((tm, 2 * D_MODEL), lambda i: (i, 0)),
                  pl.BlockSpec((tm, N_HEADS * HEAD_PAD), lambda i: (i, 0)),
                  _const_spec(pool_w_b.shape), _const_spec((1, POOL_WIDTH)),
                  _const_spec(w_pool_up_b.shape), _const_spec(w_nsa_up_p.shape)],
        out_specs=pl.BlockSpec((tm, D_MODEL), lambda i: (i, 0)),
        out_shape=jax.ShapeDtypeStruct((T, D_MODEL), BF16),
        compiler_params=_params("arbitrary"),
        name="pool_merge",
    )(proj, proj, proj, o, pool_w_b, pool_scale, w_pool_up_b, w_nsa_up_p)


OUT_TM = 256


def _rms(y, g):
    return y * lax.rsqrt(jnp.mean(y * y, axis=-1, keepdims=True) + RMS_EPS) * g


def _outproj_kernel(y_ref, w_ref, x_ref, g_ref, gt_ref, o_ref):
    z = jnp.dot(y_ref[...], w_ref[...], preferred_element_type=F32)
    o_ref[...] = x_ref[...] + gt_ref[...] * _rms(z, g_ref[...])


def _outproj(y, w_out_b, x2, g_post, mod):
    T = y.shape[0]
    tm = min(OUT_TM, T)
    row = pl.BlockSpec((tm, D_MODEL), lambda i: (i, 0))
    return pl.pallas_call(
        _outproj_kernel,
        grid=(T // tm,),
        in_specs=[row, _const_spec((D_MODEL, D_MODEL)), row, _const_spec((1, D_MODEL)), _mod_spec(2)],
        out_specs=row,
        out_shape=jax.ShapeDtypeStruct((T, D_MODEL), F32),
        compiler_params=_params("arbitrary"),
        name="out_proj",
    )(y, w_out_b, x2, g_post, mod)


def _pack_table():
    sizes = (POOL_WIDTH, N_HEADS * D_QK, N_KV_GROUPS * D_QK, N_KV_GROUPS * D_V, N_KV_GROUPS * D_QK,
             N_KV_GROUPS * D_V, N_KV_GROUPS * D_QK, N_KV_GROUPS * D_V, 3 * N_HEADS, 2 * D_MODEL)
    u, q, kc, vc, ks, vs, kw, vw, gn, gm = [int(o) for o in np.cumsum((0,) + sizes)[:-1]]
    gates = 3 * HEADS_PER_GROUP
    segments = [(gm, HEAD_PAD, 2 * D_MODEL // HEAD_PAD), (q, D_QK, N_HEADS), (u, HEAD_PAD, POOL_WIDTH // HEAD_PAD),
                (kc, D_QK, N_KV_GROUPS), (ks, D_QK, N_KV_GROUPS), (kw, D_QK, N_KV_GROUPS),
                (vc, D_V, N_KV_GROUPS), (vs, D_V, N_KV_GROUPS), (vw, D_V, N_KV_GROUPS),
                (gn, gates, N_KV_GROUPS)]
    tiles = [(start + i * width, width) for start, width, count in segments for i in range(count)]
    assert len(tiles) * HEAD_PAD == N_COLS
    return np.asarray(tiles, np.int32)


def _repack_kernel(blk_ref, shift_ref, width_ref, a_ref, b_ref, o_ref):
    j = pl.program_id(0)
    shift = shift_ref[j]
    lane = lax.broadcasted_iota(jnp.int32, (1, HEAD_PAD), 1)
    back = (HEAD_PAD - shift) & (HEAD_PAD - 1)
    merged = jnp.where(lane < HEAD_PAD - shift, pltpu.roll(a_ref[...], back, 1), pltpu.roll(b_ref[...], back, 1))
    o_ref[...] = jnp.where(lane < width_ref[j], merged, 0.0).astype(BF16)


def _pack_w_in(w_in):
    k, n_src = w_in.shape
    table = _pack_table()
    last = (n_src - 1) // HEAD_PAD
    grid_spec = pltpu.PrefetchScalarGridSpec(
        num_scalar_prefetch=3,
        grid=(table.shape[0],),
        in_specs=[pl.BlockSpec((k, HEAD_PAD), lambda j, blk, *_: (0, blk[j])),
                  pl.BlockSpec((k, HEAD_PAD), lambda j, blk, *_: (0, jnp.minimum(blk[j] + 1, last)))],
        out_specs=pl.BlockSpec((k, HEAD_PAD), lambda j, *_: (0, j)),
    )
    return pl.pallas_call(
        _repack_kernel,
        grid_spec=grid_spec,
        out_shape=jax.ShapeDtypeStruct((k, N_COLS), BF16),
        compiler_params=_params("arbitrary"),
        name="repack_w_in",
    )(jnp.asarray(table[:, 0] // HEAD_PAD), jnp.asarray(table[:, 0] % HEAD_PAD), jnp.asarray(table[:, 1]),
      w_in, w_in)


def _slc_from_cmp(n_blk, n_chunk):
    cs = np.arange(n_chunk)[None, :] * CMP_STRIDE
    ss = np.arange(n_blk)[:, None] * SLC_LEN
    ov = np.clip(np.minimum(cs + CMP_LEN, ss + SLC_LEN) - np.maximum(cs, ss), 0, None)
    return jnp.asarray(ov / CMP_LEN, dtype=BF16)


def _token_mixer_block(x2, pos_col, mod, g_pre, g_post, w_in, cmp_pos_k, cmp_k_w1, cmp_k_w2,
                       cmp_pos_v, cmp_v_w1, cmp_v_w2, pool_w, pool_scale, w_pool_up, w_nsa_up, w_out):
    T = x2.shape[0]
    half = jnp.arange(ROT_HALF, dtype=F32)
    inv = jnp.power(jnp.float32(ROPE_THETA), -half * 2.0 / (2 * ROT_HALF))
    invf = jnp.concatenate([inv, inv, jnp.zeros((HEAD_PAD - 2 * ROT_HALF,), F32)])[None, :]

    proj = _proj(x2, g_pre, mod, _pack_w_in(w_in))
    qt, ks_r, kw_r, vs_t, vw_t, gate_t = _prep(proj, pos_col, invf)

    n_chunk = T // CMP_STRIDE
    pos_cmp = jnp.pad(pos_col[CMP_LEN - 1::CMP_STRIDE], ((0, 1), (0, 0)))

    def pad_lanes(a, d):
        return jnp.pad(a, [(0, 0)] * (a.ndim - 1) + [(0, HEAD_PAD - d)])

    def w1_pack(w1, d):
        w1 = w1.reshape(CMP_LEN, d, w1.shape[1])
        return jnp.pad(w1, ((0, 0), (0, HEAD_PAD - d), (0, 0))).astype(BF16)

    k_cmp = _compress(proj, C_KC, pad_lanes(cmp_pos_k, D_QK), w1_pack(cmp_k_w1, D_QK),
                      pad_lanes(cmp_k_w2, D_QK).astype(BF16), pos_cmp, invf, True)
    v_cmp = _compress(proj, C_VC, pad_lanes(cmp_pos_v, D_V), w1_pack(cmp_v_w1, D_V),
                      pad_lanes(cmp_v_w2, D_V).T.astype(BF16), pos_cmp, invf, False)

    n_blk = max(T // SLC_LEN, MASK_ROWS)
    o = _attention(qt, k_cmp, v_cmp, ks_r, vs_t, kw_r, vw_t, gate_t, _slc_from_cmp(n_blk, n_chunk))

    w_nsa_up_p = jnp.pad(w_nsa_up.reshape(N_HEADS, D_V, D_MODEL),
                         ((0, 0), (0, HEAD_PAD - D_V), (0, 0))).reshape(N_HEADS * HEAD_PAD, D_MODEL)
    y = _mix(proj, o, pool_w.astype(BF16), pool_scale, w_pool_up.astype(BF16), w_nsa_up_p.astype(BF16))
    return _outproj(y, w_out.astype(BF16), x2, g_post, mod)


ROUTE_TM = 256
EXPERTS_PER_GROUP = N_EXPERTS // N_EXPERT_GROUPS


def _split3(a):
    hi = a.astype(BF16)
    r1 = a - hi.astype(F32)
    mid = r1.astype(BF16)
    lo = (r1 - mid.astype(F32)).astype(BF16)
    return hi, mid, lo


def _dot_f32(a, b):
    a3, b3 = _split3(a), _split3(b)
    out = None
    for i, j in ((2, 0), (1, 1), (0, 2), (1, 0), (0, 1), (0, 0)):
        term = jnp.dot(a3[i], b3[j], preferred_element_type=F32)
        out = term if out is None else out + term
    return out


def _first_index_of(mask, lane, n):
    return jnp.min(jnp.where(mask, lane, float(n)), axis=-1, keepdims=True)


def _route_kernel(x_ref, g_ref, sc_ref, sh_ref, wr_ref, rb_ref,
                  hf_ref, hb_ref, sel_ref, wd_ref, rank_ref, cnt_ref, carry_ref):
    i = pl.program_id(0)
    tm = x_ref.shape[0]

    @pl.when(i == 0)
    def _():
        carry_ref[...] = jnp.zeros_like(carry_ref)

    h = _modulated_norm(x_ref[...], g_ref[...], sc_ref[...], sh_ref[...])
    hf_ref[...] = h
    hb_ref[...] = h.astype(BF16)
    logits = _dot_f32(h, wr_ref[...])
    aff = jax.nn.sigmoid(logits)
    biased = aff + rb_ref[...]
    grp = lax.broadcasted_iota(jnp.int32, (1, N_EXPERTS), 1) // EXPERTS_PER_GROUP
    lane = lax.broadcasted_iota(jnp.int32, (1, N_EXPERTS), 1).astype(F32)

    gs = []
    for gi in range(N_EXPERT_GROUPS):
        v = jnp.where(grp == gi, biased, REMOVED)
        m1 = jnp.max(v, axis=-1, keepdims=True)
        i1 = _first_index_of(v == m1, lane, N_EXPERTS)
        m2 = jnp.max(jnp.where(lane == i1, REMOVED, v), axis=-1, keepdims=True)
        gs.append(m1 + m2)
    keep = jnp.zeros((tm, N_EXPERTS), jnp.bool_)
    for gi in range(N_EXPERT_GROUPS):
        ahead = jnp.zeros((tm, 1), jnp.int32)
        for gj in range(N_EXPERT_GROUPS):
            if gj < gi:
                ahead += (gs[gj] >= gs[gi]).astype(jnp.int32)
            elif gj > gi:
                ahead += (gs[gj] > gs[gi]).astype(jnp.int32)
        keep = keep | ((grp == gi) & (ahead < TOPK_GROUPS))
    masked = jnp.where(keep, biased, NEG_INF)
    sel = jnp.zeros((tm, N_EXPERTS), F32)
    for _ in range(TOP_K):
        best = jnp.max(masked, axis=-1, keepdims=True)
        hit = lane == _first_index_of(masked == best, lane, N_EXPERTS)
        sel = jnp.where(hit, 1.0, sel)
        masked = jnp.where(hit, REMOVED, masked)
    w = aff * sel
    wd_ref[...] = w / jnp.sum(w, axis=-1, keepdims=True) * ROUTED_SCALE
    sel_ref[...] = sel

    r_i = lax.broadcasted_iota(jnp.int32, (tm, tm), 0)
    c_i = lax.broadcasted_iota(jnp.int32, (tm, tm), 1)
    before = jnp.where(c_i < r_i, 1.0, 0.0).astype(BF16)
    carry = carry_ref[...]
    rank_ref[...] = jnp.dot(before, sel.astype(BF16), preferred_element_type=F32) + carry
    carry = carry + jnp.sum(sel, axis=0, keepdims=True)
    carry_ref[...] = carry
    cnt_ref[...] = carry


def _route(x1, g_pre, mod, w_router, router_bias):
    T = x1.shape[0]
    tm = min(ROUTE_TM, T)
    row = pl.BlockSpec((tm, D_MODEL), lambda i: (i, 0))
    erow = pl.BlockSpec((tm, N_EXPERTS), lambda i: (i, 0))
    e_shape = jax.ShapeDtypeStruct((T, N_EXPERTS), F32)
    return pl.pallas_call(
        _route_kernel,
        grid=(T // tm,),
        in_specs=[row, _const_spec((1, D_MODEL)), _mod_spec(4), _mod_spec(3),
                  _const_spec((D_MODEL, N_EXPERTS)), _const_spec((1, N_EXPERTS))],
        out_specs=[row, row, erow, erow, erow, _const_spec((1, N_EXPERTS))],
        out_shape=[jax.ShapeDtypeStruct((T, D_MODEL), F32), jax.ShapeDtypeStruct((T, D_MODEL), BF16),
                   e_shape, e_shape, e_shape, jax.ShapeDtypeStruct((1, N_EXPERTS), F32)],
        scratch_shapes=[pltpu.VMEM((1, N_EXPERTS), F32)],
        compiler_params=_params("arbitrary"),
        name="moe_route",
    )(x1, g_pre, mod, mod, w_router, router_bias)


def _dest_kernel(sel_ref, wd_ref, rank_ref, ps_ref, dest_ref, w8_ref):
    tm = sel_ref.shape[0]
    slot = ps_ref[...] + rank_ref[...]
    wd = wd_ref[...]
    lane = lax.broadcasted_iota(jnp.int32, (1, N_EXPERTS), 1).astype(F32)
    lane_k = lax.broadcasted_iota(jnp.int32, (1, TOP_K), 1)
    dest = jnp.zeros((tm, TOP_K), F32)
    w8 = jnp.zeros((tm, TOP_K), F32)
    rem = sel_ref[...]
    for k in range(TOP_K):
        hit = lane == _first_index_of(rem > 0.5, lane, N_EXPERTS)
        dest = jnp.where(lane_k == k, jnp.sum(jnp.where(hit, slot, 0.0), axis=-1, keepdims=True), dest)
        w8 = jnp.where(lane_k == k, jnp.sum(jnp.where(hit, wd, 0.0), axis=-1, keepdims=True), w8)
        rem = jnp.where(hit, 0.0, rem)
    dest_ref[...] = dest.astype(jnp.int32)
    w8_ref[...] = w8


def _dest(sel, wd, rank, pstart):
    T = sel.shape[0]
    tm = min(ROUTE_TM, T)
    erow = pl.BlockSpec((tm, N_EXPERTS), lambda i: (i, 0))
    krow = pl.BlockSpec((tm, TOP_K), lambda i: (i, 0))
    return pl.pallas_call(
        _dest_kernel,
        grid=(T // tm,),
        in_specs=[erow, erow, erow, _const_spec((1, N_EXPERTS))],
        out_specs=[krow, krow],
        out_shape=[jax.ShapeDtypeStruct((T, TOP_K), jnp.int32), jax.ShapeDtypeStruct((T, TOP_K), F32)],
        compiler_params=_params("arbitrary"),
        name="moe_dest",
    )(sel, wd, rank, pstart)


DISPATCH_TM = 128


def _dispatch_kernel(dest_ref, h_ref, xs_hbm, sem):
    tm = h_ref.shape[0]

    def body(t, carry):
        for k in range(TOP_K):
            d = dest_ref[t * TOP_K + k]
            pltpu.make_async_copy(h_ref.at[pl.ds(t, 1), :], xs_hbm.at[pl.ds(d, 1), :], sem).start()
        return carry

    lax.fori_loop(0, tm, body, 0)
    for k in range(TOP_K):
        pltpu.make_async_copy(h_ref, xs_hbm.at[pl.ds(0, tm), :], sem).wait()


def _dispatch(dest_flat, hf, n_slots):
    T = hf.shape[0]
    tm = min(DISPATCH_TM, T)
    return pl.pallas_call(
        _dispatch_kernel,
        grid=(T // tm,),
        in_specs=[pl.BlockSpec((tm * TOP_K,), lambda i: (i,), memory_space=pltpu.SMEM),
                  pl.BlockSpec((tm, D_MODEL), lambda i: (i, 0))],
        out_specs=pl.BlockSpec(memory_space=pl.ANY),
        out_shape=jax.ShapeDtypeStruct((n_slots, D_MODEL), F32),
        scratch_shapes=[pltpu.SemaphoreType.DMA(())],
        compiler_params=pltpu.CompilerParams(dimension_semantics=("arbitrary",), vmem_limit_bytes=VMEM_LIMIT,
                                             has_side_effects=True),
        name="moe_dispatch",
    )(dest_flat, hf)


EXPERT_BLOCK = 256


def _swiglu(x, wg, wu, wd):
    g = jnp.dot(x, wg, preferred_element_type=F32)
    u = jnp.dot(x, wu, preferred_element_type=F32)
    hid = (g * jax.nn.sigmoid(g)) * u
    return jnp.dot(hid.astype(BF16), wd, preferred_element_type=F32)


def _expert_items(counts, n_items):
    ends = jnp.cumsum(counts)
    starts = ends - counts
    first_blk = starts // EXPERT_BLOCK
    last_blk = (ends - 1) // EXPERT_BLOCK
    per_expert = jnp.where(counts > 0, last_blk - first_blk + 1, 0)
    item_end = jnp.cumsum(per_expert)
    n_used = item_end[-1]
    i = jnp.minimum(jnp.arange(n_items, dtype=jnp.int32), jnp.maximum(n_used - 1, 0))
    e = jnp.sum((item_end[None, :] <= i[:, None]).astype(jnp.int32), axis=1)
    e = jnp.minimum(e, N_EXPERTS - 1)
    blk = first_blk[e] + i - (item_end[e] - per_expert[e])
    lo = jnp.clip(starts[e] - blk * EXPERT_BLOCK, 0, EXPERT_BLOCK)
    hi = jnp.clip(ends[e] - blk * EXPERT_BLOCK, 0, EXPERT_BLOCK)
    live = jnp.arange(n_items, dtype=jnp.int32) < n_used
    hi = jnp.where(live, hi, lo)
    new_expert = jnp.concatenate([jnp.ones((1,), jnp.bool_), e[1:] != e[:-1]])
    new_block = jnp.concatenate([jnp.ones((1,), jnp.bool_), blk[1:] != blk[:-1]])
    as_i32 = lambda a: a.astype(jnp.int32)
    ids = jnp.arange(N_EXPERTS, dtype=jnp.int32)
    used = counts > 0
    slot_of = (jnp.cumsum(as_i32(used)) - 1) % 2
    later = (ids[None, :] > ids[:, None]) & used[None, :]
    next_of = jnp.min(jnp.where(later, ids[None, :], N_EXPERTS), axis=1)
    next_of = jnp.where(next_of < N_EXPERTS, next_of, -1)
    return (as_i32(blk), as_i32(e), as_i32(lo), as_i32(hi), as_i32(new_expert & live),
            as_i32(new_block & live), as_i32(live), as_i32(slot_of[e]), as_i32(next_of[e]), starts)


def _expert_kernel(blk_ref, e_ref, lo_ref, hi_ref, newe_ref, newb_ref, live_ref, slot_ref, nexte_ref,
                   x_ref, wg_hbm, wu_hbm, wd_hbm, y_ref, wg32, wu32, wd32, wgb, wub, wdb, sem):
    i = pl.program_id(0)

    def weight_copies(e, slot):
        return (pltpu.make_async_copy(wg_hbm.at[e], wg32.at[slot], sem.at[slot]),
                pltpu.make_async_copy(wu_hbm.at[e], wu32.at[slot], sem.at[slot]),
                pltpu.make_async_copy(wd_hbm.at[e], wd32.at[slot], sem.at[slot]))

    @pl.when(live_ref[i] == 1)
    def _():
        @pl.when(newe_ref[i] == 1)
        def _():
            slot = slot_ref[i]

            @pl.when(i == 0)
            def _():
                for c in weight_copies(e_ref[i], slot):
                    c.start()

            for c in weight_copies(e_ref[i], slot):
                c.wait()
            wgb[...] = wg32[slot].astype(BF16)
            wub[...] = wu32[slot].astype(BF16)
            wdb[...] = wd32[slot].astype(BF16)

            @pl.when(nexte_ref[i] >= 0)
            def _():
                for c in weight_copies(nexte_ref[i], 1 - slot):
                    c.start()

        row = lax.broadcasted_iota(jnp.int32, (EXPERT_BLOCK, 1), 0)
        mine = (row >= lo_ref[i]) & (row < hi_ref[i])
        y = _swiglu(jnp.where(mine, x_ref[...], 0.0).astype(BF16), wgb[...], wub[...], wdb[...])

        @pl.when(newb_ref[i] == 1)
        def _():
            y_ref[...] = y

        @pl.when(newb_ref[i] == 0)
        def _():
            y_ref[...] += y


def _experts(items, xs, w_gate, w_up, w_down):
    n_items = items[0].shape[0]
    by_block = lambda i, blk, *_: (blk[i], 0)
    hbm = pl.BlockSpec(memory_space=pl.ANY)
    grid_spec = pltpu.PrefetchScalarGridSpec(
        num_scalar_prefetch=len(items),
        grid=(n_items,),
        in_specs=[pl.BlockSpec((EXPERT_BLOCK, D_MODEL), by_block), hbm, hbm, hbm],
        out_specs=pl.BlockSpec((EXPERT_BLOCK, D_MODEL), by_block),
        scratch_shapes=[pltpu.VMEM((2, D_MODEL, D_EXPERT), F32), pltpu.VMEM((2, D_MODEL, D_EXPERT), F32),
                        pltpu.VMEM((2, D_EXPERT, D_MODEL), F32),
                        pltpu.VMEM((D_MODEL, D_EXPERT), BF16), pltpu.VMEM((D_MODEL, D_EXPERT), BF16),
                        pltpu.VMEM((D_EXPERT, D_MODEL), BF16), pltpu.SemaphoreType.DMA((2,))],
    )
    return pl.pallas_call(
        _expert_kernel,
        grid_spec=grid_spec,
        out_shape=jax.ShapeDtypeStruct(xs.shape, F32),
        compiler_params=_params("arbitrary"),
        name="moe_experts",
    )(*items, xs, w_gate, w_up, w_down)


COMBINE_TM = 128


def _combine_kernel(dcur_ref, dnext_ref, w8_ref, hb_ref, x_ref, wsg_ref, wsu_ref, wsd_ref, g_ref, gt_ref,
                    y_hbm, o_ref, gbuf, sem):
    i = pl.program_id(0)
    n = pl.num_programs(0)
    tm = x_ref.shape[0]

    rows = TOP_K * tm

    def issue(d_ref, slot):
        base = slot * rows

        def body(t, carry):
            for k in range(TOP_K):
                pltpu.make_async_copy(y_hbm.at[pl.ds(d_ref[t * TOP_K + k], 1), :],
                                      gbuf.at[pl.ds(base + k * tm + t, 1), :], sem.at[slot]).start()
            return carry

        lax.fori_loop(0, tm, body, 0)

    @pl.when(i == 0)
    def _():
        issue(dcur_ref, 0)

    @pl.when(i + 1 < n)
    def _():
        issue(dnext_ref, (i + 1) % 2)

    slot = i % 2
    base = pl.multiple_of(slot * rows, rows)
    shared = _swiglu(hb_ref[...], wsg_ref[...], wsu_ref[...], wsd_ref[...])
    for k in range(TOP_K):
        pltpu.make_async_copy(y_hbm.at[pl.ds(0, tm), :], gbuf.at[pl.ds(base + k * tm, tm), :],
                              sem.at[slot]).wait()
    w8 = w8_ref[...]
    y = shared
    for k in range(TOP_K):
        y = y + w8[:, k:k + 1] * gbuf[pl.ds(base + k * tm, tm), :]
    o_ref[...] = x_ref[...] + gt_ref[...] * _rms(y, g_ref[...])


def _combine(dest_flat, w8, hb, x1, wsg, wsu, wsd, g_post, mod, y_buf):
    T = x1.shape[0]
    tm = min(COMBINE_TM, T)
    n = T // tm
    row = pl.BlockSpec((tm, D_MODEL), lambda i: (i, 0))
    return pl.pallas_call(
        _combine_kernel,
        grid=(n,),
        in_specs=[pl.BlockSpec((tm * TOP_K,), lambda i: (i,), memory_space=pltpu.SMEM),
                  pl.BlockSpec((tm * TOP_K,), lambda i: (jnp.minimum(i + 1, n - 1),), memory_space=pltpu.SMEM),
                  pl.BlockSpec((tm, TOP_K), lambda i: (i, 0)), row, row,
                  _const_spec(wsg.shape), _const_spec(wsu.shape), _const_spec(wsd.shape),
                  _const_spec((1, D_MODEL)), _mod_spec(5),
                  pl.BlockSpec(memory_space=pl.ANY)],
        out_specs=row,
        out_shape=jax.ShapeDtypeStruct((T, D_MODEL), F32),
        scratch_shapes=[pltpu.VMEM((2 * TOP_K * tm, D_MODEL), F32), pltpu.SemaphoreType.DMA((2,))],
        compiler_params=_params("arbitrary"),
        name="moe_combine",
    )(dest_flat, dest_flat, w8, hb, x1, wsg, wsu, wsd, g_post, mod, y_buf)


def _moe_block(x1, mod, g_pre, g_post, w_router, router_bias, w_exp_gate, w_exp_up, w_exp_down,
               w_sh_gate, w_sh_up, w_sh_down):
    T = x1.shape[0]
    hf, hb, sel, wd, rank, counts = _route(x1, g_pre, mod, w_router, router_bias)

    n_slots = T * TOP_K
    *items, starts = _expert_items(counts[0].astype(jnp.int32), n_slots // EXPERT_BLOCK + N_EXPERTS)

    dest, w8 = _dest(sel, wd, rank, starts.astype(F32)[None, :])
    dest_flat = dest.reshape(n_slots)
    xs = _dispatch(dest_flat, hf, n_slots)
    y_buf = _experts(items, xs, w_exp_gate, w_exp_up, w_exp_down)
    return _combine(dest_flat, w8, hb, x1, w_sh_gate.astype(BF16), w_sh_up.astype(BF16),
                    w_sh_down.astype(BF16), g_post, mod, y_buf)


def kernel(x, c, positions, w_ada, b_ada, g_pre_mix, g_post_mix, g_pre_ffn, g_post_ffn, w_in, cmp_pos_k, cmp_k_w1, cmp_k_w2, cmp_pos_v, cmp_v_w1, cmp_v_w2, pool_w, pool_scale, w_pool_up, w_nsa_up, w_out, w_router, router_bias, w_exp_gate, w_exp_up, w_exp_down, w_sh_gate, w_sh_up, w_sh_down):
    B, T, D = x.shape
    assert B == 1 and D == D_MODEL
    x2 = x.reshape(T, D)
    pos_col = positions.reshape(T, 1)
    for l in range(w_ada.shape[0]):
        mod = _ada(c.reshape(D, 1), w_ada[l], b_ada[l][None, :])
        x2 = _token_mixer_block(x2, pos_col, mod, g_pre_mix[l][None, :], g_post_mix[l][None, :], w_in[l],
                                cmp_pos_k[l], cmp_k_w1[l], cmp_k_w2[l], cmp_pos_v[l], cmp_v_w1[l],
                                cmp_v_w2[l], pool_w[l], pool_scale[l][None, :], w_pool_up[l], w_nsa_up[l],
                                w_out[l])
        x2 = _moe_block(x2, mod, g_pre_ffn[l][None, :], g_post_ffn[l][None, :], w_router[l],
                        router_bias[l][None, :], w_exp_gate[l], w_exp_up[l], w_exp_down[l],
                        w_sh_gate[l], w_sh_up[l], w_sh_down[l])
    return x2.reshape(B, T, D)
```

```python
import functools

import numpy as np
import jax
import jax.numpy as jnp
from jax import lax
from jax.experimental import pallas as pl
from jax.experimental.pallas import tpu as pltpu

F32 = jnp.float32
BF16 = jnp.bfloat16

D_MODEL = 2048
POOL_WIDTH = 1024
POOL_WINDOWS = (2, 4, 8, 16)
POOL_GROUP = 256
POOL_HALO = 16
N_HEADS = 16
N_KV_GROUPS = 4
HEADS_PER_GROUP = 4
D_QK = 96
D_V = 64
HEAD_PAD = 128
ROT_HALF = 12
ROPE_THETA = 500000.0
CMP_LEN = 32
CMP_STRIDE = 16
SLC_LEN = 64
N_SELECT = 16
WINDOW = 512
Q_BLOCK = 512
N_EXPERTS = 64
TOP_K = 8
N_EXPERT_GROUPS = 8
TOPK_GROUPS = 4
D_EXPERT = 512
ROUTED_SCALE = 2.5
RMS_EPS = 1e-6
NEG_INF = -1e30
SEL_FORCE = 1e30
REMOVED = -3e38

VMEM_LIMIT = 56 * 1024 * 1024

C_GM = 0
C_Q = 4096
C_U = 6144
C_KC = 7168
C_KS = 7680
C_KW = 8192
C_VC = 8704
C_VS = 9216
C_VW = 9728
C_GN = 10240
N_COLS = 10752


def _params(*sem):
    return pltpu.CompilerParams(dimension_semantics=sem, vmem_limit_bytes=VMEM_LIMIT)


def _const_spec(shape):
    nd = len(shape)
    return pl.BlockSpec(shape, lambda *_: (0,) * nd)


ADA_TN = 1536
ADA_ROWS = 64


def _ada_kernel(c_ref, w_ref, b_ref, o_ref, s_ref):
    c = c_ref[...]
    s_ref[...] = c * jax.nn.sigmoid(c)

    def body(k, acc):
        r = pl.multiple_of(k * ADA_ROWS, ADA_ROWS)
        for j in range(ADA_ROWS // 8):
            acc = acc + w_ref[pl.ds(r + 8 * j, 8), :] * s_ref[pl.ds(r + 8 * j, 8), :]
        return acc

    acc = lax.fori_loop(0, D_MODEL // ADA_ROWS, body, jnp.zeros((8, ADA_TN), F32))
    o_ref[...] = jnp.sum(acc, axis=0, keepdims=True) + b_ref[...]


def _ada(c_col, w_ada, b_ada):
    n = w_ada.shape[1]
    return pl.pallas_call(
        _ada_kernel,
        grid=(n // ADA_TN,),
        in_specs=[_const_spec((D_MODEL, 1)),
                  pl.BlockSpec((D_MODEL, ADA_TN), lambda j: (0, j)),
                  pl.BlockSpec((1, ADA_TN), lambda j: (0, j))],
        out_specs=pl.BlockSpec((1, ADA_TN), lambda j: (0, j)),
        out_shape=jax.ShapeDtypeStruct((1, n), F32),
        scratch_shapes=[pltpu.VMEM((D_MODEL, 1), F32)],
        compiler_params=_params("arbitrary"),
        name="ada",
    )(c_col, w_ada, b_ada)


def _mod_spec(k):
    return pl.BlockSpec((1, D_MODEL), lambda *_: (0, k))


PROJ_TM = 512
PROJ_TN = 1536


def _modulated_norm(x, g, sc, sh):
    y = x * lax.rsqrt(jnp.mean(x * x, axis=-1, keepdims=True) + RMS_EPS)
    return (y * g) * (1.0 + sc) + sh


def _proj_kernel(x_ref, g_ref, sc_ref, sh_ref, w_ref, o_ref, h_ref):
    @pl.when(pl.program_id(1) == 0)
    def _():
        h_ref[...] = _modulated_norm(x_ref[...], g_ref[...], sc_ref[...], sh_ref[...]).astype(BF16)

    o_ref[...] = jnp.dot(h_ref[...], w_ref[...], preferred_element_type=F32)


def _proj(x2, g_pre, mod, w_in_p):
    T = x2.shape[0]
    tm = min(PROJ_TM, T)
    return pl.pallas_call(
        _proj_kernel,
        grid=(T // tm, N_COLS // PROJ_TN),
        in_specs=[pl.BlockSpec((tm, D_MODEL), lambda i, j: (i, 0)),
                  _const_spec((1, D_MODEL)), _mod_spec(1), _mod_spec(0),
                  pl.BlockSpec((D_MODEL, PROJ_TN), lambda i, j: (0, j))],
        out_specs=pl.BlockSpec((tm, PROJ_TN), lambda i, j: (i, j)),
        out_shape=jax.ShapeDtypeStruct((T, N_COLS), F32),
        scratch_shapes=[pltpu.VMEM((tm, D_MODEL), BF16)],
        compiler_params=_params("arbitrary", "arbitrary"),
        name="in_proj",
    )(x2, g_pre, mod, mod, w_in_p)


def _rope_tables(pos_col, invf):
    ang = pos_col.astype(F32) * invf
    cos = jnp.cos(ang)
    sin = jnp.sin(ang)
    lane = lax.broadcasted_iota(jnp.int32, ang.shape, 1)
    sin_lo = jnp.where(lane < ROT_HALF, -sin, 0.0)
    sin_hi = jnp.where((lane >= ROT_HALF) & (lane < 2 * ROT_HALF), sin, 0.0)
    return cos, sin_lo, sin_hi


def _rope_head(x, tabs):
    cos, sin_lo, sin_hi = tabs
    return (x * cos + pltpu.roll(x, HEAD_PAD - ROT_HALF, 1) * sin_lo
            + pltpu.roll(x, ROT_HALF, 1) * sin_hi)


PREP_TM = 256


LOG2E = 1.4426950408889634
Q_SCALE = (D_QK ** -0.5) * LOG2E
MASK_LANE = D_QK
MASK_ROWS = 16
ONES_ROW = D_V
V_ROWS = 80
MASK_BIG = 1e30


def _with_ones_row(vt):
    row = lax.broadcasted_iota(jnp.int32, (HEAD_PAD, 1), 0)
    return jnp.where(row == ONES_ROW, 1.0, vt)


def _prep_kernel(pos_ref, invf_ref, q_ref, ks_ref, kw_ref, vs_ref, vw_ref, gn_ref,
                 qt_ref, kso_ref, kwo_ref, vst_ref, vwt_ref, gt_ref):
    tm = q_ref.shape[0]
    tabs = _rope_tables(pos_ref[...], invf_ref[...])
    for h in range(N_HEADS):
        sl = slice(h * HEAD_PAD, (h + 1) * HEAD_PAD)
        qt_ref[sl, :] = (_rope_head(q_ref[:, sl], tabs) * Q_SCALE).T.astype(BF16)
    row = pl.program_id(0) * tm + lax.broadcasted_iota(jnp.int32, (tm, 1), 0)
    lane = lax.broadcasted_iota(jnp.int32, (1, HEAD_PAD), 1)
    block_tag = jnp.where(lane == MASK_LANE + ((row >> 6) & (MASK_ROWS - 1)), 1.0, 0.0)
    for g in range(N_KV_GROUPS):
        sl = slice(g * HEAD_PAD, (g + 1) * HEAD_PAD)
        kso_ref[:, sl] = (_rope_head(ks_ref[:, sl], tabs) + block_tag).astype(BF16)
        kwo_ref[:, sl] = _rope_head(kw_ref[:, sl], tabs).astype(BF16)
        vst_ref[sl, :] = _with_ones_row(vs_ref[:, sl].T).astype(BF16)
        vwt_ref[sl, :] = _with_ones_row(vw_ref[:, sl].T).astype(BF16)
        gt_ref[sl, :] = jax.nn.sigmoid(gn_ref[:, sl]).T


def _prep(proj, pos_col, invf):
    T = proj.shape[0]
    tm = min(PREP_TM, T)
    kv = N_KV_GROUPS * HEAD_PAD
    qw = N_HEADS * HEAD_PAD

    def col(width, off):
        return pl.BlockSpec((tm, width), lambda i: (i, off // width))

    rows = lambda width: pl.BlockSpec((tm, width), lambda i: (i, 0))
    cols = lambda height: pl.BlockSpec((height, tm), lambda i: (0, i))
    return pl.pallas_call(
        _prep_kernel,
        grid=(T // tm,),
        in_specs=[pl.BlockSpec((tm, 1), lambda i: (i, 0)), _const_spec((1, HEAD_PAD)),
                  col(qw, C_Q), col(kv, C_KS), col(kv, C_KW), col(kv, C_VS), col(kv, C_VW), col(kv, C_GN)],
        out_specs=[cols(qw), rows(kv), rows(kv), cols(kv), cols(kv), cols(kv)],
        out_shape=[jax.ShapeDtypeStruct((qw, T), BF16), jax.ShapeDtypeStruct((T, kv), BF16),
                   jax.ShapeDtypeStruct((T, kv), BF16), jax.ShapeDtypeStruct((kv, T), BF16),
                   jax.ShapeDtypeStruct((kv, T), BF16), jax.ShapeDtypeStruct((kv, T), F32)],
        compiler_params=_params("arbitrary"),
        name="nsa_prep",
    )(pos_col, invf, proj, proj, proj, proj, proj, proj)


def _cmp_kernel(x_ref, pe_ref, w1_ref, w2_ref, pos_ref, invf_ref, o_ref, *, is_key):
    nchunk = pos_ref.shape[0]
    hid_dim = w1_ref.shape[2]
    first = jnp.zeros((nchunk, hid_dim), F32)
    second = jnp.zeros((nchunk, hid_dim), F32)
    for l in range(CMP_STRIDE):
        rows = x_ref[pl.ds(l, nchunk, stride=CMP_STRIDE), :]
        first += jnp.dot((rows + pe_ref[l:l + 1, :]).astype(BF16), w1_ref[l],
                         preferred_element_type=F32)
        second += jnp.dot((rows + pe_ref[CMP_STRIDE + l:CMP_STRIDE + l + 1, :]).astype(BF16),
                          w1_ref[CMP_STRIDE + l], preferred_element_type=F32)
    second = jnp.concatenate([second[1:], jnp.zeros((1, hid_dim), F32)], axis=0)
    pre = first + second
    hid = pre * jax.nn.sigmoid(pre)
    if is_key:
        out = jnp.dot(hid.astype(BF16), w2_ref[...], preferred_element_type=F32)
        o_ref[0] = _rope_head(out, _rope_tables(pos_ref[...], invf_ref[...])).astype(BF16)
    else:
        out_t = lax.dot_general(w2_ref[...], hid.astype(BF16), (((1,), (1,)), ((), ())),
                                preferred_element_type=F32)
        o_ref[0] = _with_ones_row(out_t).astype(BF16)


def _compress(proj, col_off, pe_p, w1_p, w2_p, pos_cmp, invf, is_key):
    T = proj.shape[0]
    nchunk = T // CMP_STRIDE
    hid = w1_p.shape[2]
    out_block = (1, nchunk, HEAD_PAD) if is_key else (1, HEAD_PAD, nchunk)
    return pl.pallas_call(
        functools.partial(_cmp_kernel, is_key=is_key),
        grid=(N_KV_GROUPS,),
        in_specs=[pl.BlockSpec((T, HEAD_PAD), lambda g: (0, col_off // HEAD_PAD + g)),
                  _const_spec((CMP_LEN, HEAD_PAD)), _const_spec((CMP_LEN, HEAD_PAD, hid)),
                  _const_spec(w2_p.shape), _const_spec((nchunk, 1)), _const_spec((1, HEAD_PAD))],
        out_specs=pl.BlockSpec(out_block, lambda g: (g, 0, 0)),
        out_shape=jax.ShapeDtypeStruct((N_KV_GROUPS,) + out_block[1:], BF16),
        compiler_params=_params("arbitrary"),
        name="compress",
    )(proj, pe_p, w1_p, w2_p, pos_cmp, invf)


SLC_TK = 1024
WIN_SUB = 256
WIN_KEYS = WINDOW + WIN_SUB
R = HEADS_PER_GROUP


def _lanes4(a):
    return jnp.concatenate([a] * R, axis=1)


def _attn_kernel(qt_ref, kc_ref, vct_ref, ks_ref, vst_ref, kw_ref, vwt_ref, gate_ref, c2st_ref, o_ref,
                 qa_ref, neg_ref, acc_ref, m_ref, *, n_sel):
    ci = pl.program_id(1)
    s = ci * Q_BLOCK
    qt = jnp.concatenate([qt_ref[r * HEAD_PAD:(r + 1) * HEAD_PAD, :] for r in range(R)], axis=1)
    tok = s + lax.broadcasted_iota(jnp.int32, (1, Q_BLOCK), 1)
    tok4 = _lanes4(tok)

    n_cmp = kc_ref.shape[1]
    cmp_end = lax.broadcasted_iota(jnp.int32, (n_cmp, 1), 0) * CMP_STRIDE + (CMP_LEN - 1)
    sc = jnp.dot(kc_ref[0], qt, preferred_element_type=F32)
    sc = sc + _lanes4(jnp.where(cmp_end <= tok, 0.0, NEG_INF))
    p = jnp.exp2(sc - jnp.max(sc, axis=0, keepdims=True))
    inv = jnp.where(tok4 >= CMP_LEN - 1, 1.0 / jnp.sum(p, axis=0, keepdims=True), 0.0)
    p = p * inv
    o_cmp = jnp.dot(vct_ref[0, 0:V_ROWS, :], p.astype(BF16), preferred_element_type=F32)

    p_sum = p[:, 0:Q_BLOCK]
    for r in range(1, R):
        p_sum = p_sum + p[:, r * Q_BLOCK:(r + 1) * Q_BLOCK]
    p_hi = p_sum.astype(BF16)
    p_lo = (p_sum - p_hi.astype(F32)).astype(BF16)
    imp = (jnp.dot(c2st_ref[...], p_hi, preferred_element_type=F32)
           + jnp.dot(c2st_ref[...], p_lo, preferred_element_type=F32))

    n_blk = c2st_ref.shape[0]
    jj = lax.broadcasted_iota(jnp.int32, (n_blk, 1), 0)
    cur = tok >> 6
    forced = (jj == 0) | (jj == cur) | (jj == cur - 1)
    score = jnp.where(jj <= cur, jnp.where(forced, SEL_FORCE, imp), NEG_INF)
    chosen = jnp.zeros((n_blk, Q_BLOCK), jnp.bool_)
    for _ in range(n_sel):
        best = jnp.max(score, axis=0, keepdims=True)
        hit = jj == jnp.min(jnp.where(score == best, jj, n_blk), axis=0, keepdims=True)
        chosen = chosen | (hit & (best > 0.5 * NEG_INF))
        score = jnp.where(hit, REMOVED, score)
    neg_ref[...] = _lanes4(jnp.where(chosen, 0.0, -MASK_BIG)).astype(BF16)

    qa_ref[...] = qt
    acc_ref[...] = jnp.zeros_like(acc_ref)
    m_ref[...] = jnp.full_like(m_ref, NEG_INF)

    def tile(jt, causal):
        k0 = pl.multiple_of(jt * SLC_TK, SLC_TK)
        blk0 = pl.multiple_of((k0 // (SLC_LEN * MASK_ROWS)) * MASK_ROWS, MASK_ROWS)
        qa_ref[MASK_LANE:MASK_LANE + MASK_ROWS, :] = neg_ref[pl.ds(blk0, MASK_ROWS), :]
        st = jnp.dot(ks_ref[pl.ds(k0, SLC_TK), :], qa_ref[...], preferred_element_type=F32)
        if causal:
            kpos = k0 + lax.broadcasted_iota(jnp.int32, (SLC_TK, 1), 0)
            st = st + _lanes4(jnp.where(kpos <= tok, 0.0, NEG_INF))
        m_old = m_ref[...]
        m_new = jnp.maximum(m_old, jnp.max(st, axis=0, keepdims=True))
        pt = jnp.exp2(st - m_new).astype(BF16)
        acc_ref[...] = (jnp.exp2(m_old - m_new) * acc_ref[...]
                        + jnp.dot(vst_ref[0:V_ROWS, pl.ds(k0, SLC_TK)], pt, preferred_element_type=F32))
        m_ref[...] = m_new

    def full_tile(jt, carry):
        tile(jt, False)
        return carry

    n_full = s // SLC_TK
    lax.fori_loop(0, n_full, full_tile, 0)
    tile(n_full, True)
    acc = acc_ref[...]
    o_slc = acc * (1.0 / acc[ONES_ROW:ONES_ROW + 1, :])

    win_parts = []
    for b in range(Q_BLOCK // WIN_SUB):
        sb = s + b * WIN_SUB
        tok_b = sb + lax.broadcasted_iota(jnp.int32, (1, WIN_SUB), 1)
        qt_b = jnp.concatenate([qt[:, r * Q_BLOCK + b * WIN_SUB:r * Q_BLOCK + (b + 1) * WIN_SUB]
                                for r in range(R)], axis=1)
        w0 = pl.multiple_of(jnp.maximum(sb - WINDOW, 0), WIN_SUB)
        wpos = w0 + lax.broadcasted_iota(jnp.int32, (WIN_KEYS, 1), 0)
        sw = jnp.dot(kw_ref[pl.ds(w0, WIN_KEYS), :], qt_b, preferred_element_type=F32)
        sw = sw + _lanes4(jnp.where((wpos <= tok_b) & (wpos > tok_b - WINDOW), 0.0, NEG_INF))
        pw = jnp.exp2(sw - jnp.max(sw, axis=0, keepdims=True)).astype(BF16)
        ow = jnp.dot(vwt_ref[0:V_ROWS, pl.ds(w0, WIN_KEYS)], pw, preferred_element_type=F32)
        win_parts.append(ow * (1.0 / ow[ONES_ROW:ONES_ROW + 1, :]))

    gate = gate_ref[...]
    for r in range(R):
        ls = slice(r * Q_BLOCK, (r + 1) * Q_BLOCK)
        o_win = jnp.concatenate([part[:, r * WIN_SUB:(r + 1) * WIN_SUB] for part in win_parts], axis=1)
        o = (gate[3 * r:3 * r + 1, :] * o_cmp[:, ls] + gate[3 * r + 1:3 * r + 2, :] * o_slc[:, ls]
             + gate[3 * r + 2:3 * r + 3, :] * o_win)
        o = jnp.concatenate([o, jnp.zeros((HEAD_PAD - V_ROWS, Q_BLOCK), F32)], axis=0)
        o_ref[:, r * HEAD_PAD:(r + 1) * HEAD_PAD] = o.T.astype(BF16)


def _attention(qt, k_cmp, v_cmp_t, ks_r, vs_t, kw_r, vw_t, gate_t, c2st):
    T = ks_r.shape[0]
    n_cmp = k_cmp.shape[1]
    n_blk = c2st.shape[0]
    n_sel = min(N_SELECT, T // SLC_LEN)
    gw = R * HEAD_PAD
    rows = lambda g, ci: (0, g)
    cols = lambda g, ci: (g, 0)
    return pl.pallas_call(
        functools.partial(_attn_kernel, n_sel=n_sel),
        grid=(N_KV_GROUPS, T // Q_BLOCK),
        in_specs=[pl.BlockSpec((gw, Q_BLOCK), lambda g, ci: (g, ci)),
                  pl.BlockSpec((1, n_cmp, HEAD_PAD), lambda g, ci: (g, 0, 0)),
                  pl.BlockSpec((1, HEAD_PAD, n_cmp), lambda g, ci: (g, 0, 0)),
                  pl.BlockSpec((T, HEAD_PAD), rows), pl.BlockSpec((HEAD_PAD, T), cols),
                  pl.BlockSpec((T, HEAD_PAD), rows), pl.BlockSpec((HEAD_PAD, T), cols),
                  pl.BlockSpec((HEAD_PAD, Q_BLOCK), lambda g, ci: (g, ci)),
                  _const_spec((n_blk, n_cmp))],
        out_specs=pl.BlockSpec((Q_BLOCK, gw), lambda g, ci: (ci, g)),
        out_shape=jax.ShapeDtypeStruct((T, N_HEADS * HEAD_PAD), BF16),
        scratch_shapes=[pltpu.VMEM((HEAD_PAD, R * Q_BLOCK), BF16), pltpu.VMEM((n_blk, R * Q_BLOCK), BF16),
                        pltpu.VMEM((V_ROWS, R * Q_BLOCK), F32), pltpu.VMEM((1, R * Q_BLOCK), F32)],
        compiler_params=_params("arbitrary", "arbitrary"),
        name="nsa_attention",
    )(qt, k_cmp, v_cmp_t, ks_r, vs_t, kw_r, vw_t, gate_t, c2st)


MIX_TM = 256


def _mix_kernel(u_ref, up_ref, gm_ref, o_ref, pw_ref, ps_ref, wpu_ref, wnu_ref, y_ref):
    i = pl.program_id(0)
    tm = u_ref.shape[0]
    u = u_ref[...]
    prev = up_ref[...] * (i > 0).astype(F32)
    ext = jnp.concatenate([prev, u], axis=0)
    t = i * tm + lax.broadcasted_iota(jnp.int32, (tm, 1), 0)
    parts = []
    for g, win in enumerate(POOL_WINDOWS):
        cs = slice(g * POOL_GROUP, (g + 1) * POOL_GROUP)
        run = ext[:, cs]
        k = 1
        while k < win:
            run = run[k:] + run[:-k]
            k *= 2
        lo = POOL_HALO - (win - 1)
        total = run[lo:lo + tm]
        count = jnp.minimum(t + 1, win).astype(F32)
        d = total / count - u[:, cs]
        parts.append(jnp.dot(d.astype(BF16), pw_ref[g], preferred_element_type=F32) * ps_ref[:, cs])
    a = jnp.concatenate(parts, axis=1).astype(BF16)
    ya = jnp.dot(a, wpu_ref[...], preferred_element_type=F32)
    yb = jnp.dot(o_ref[...], wnu_ref[...], preferred_element_type=F32)
    gm = jax.nn.sigmoid(gm_ref[...])
    y_ref[...] = (gm[:, :D_MODEL] * ya + gm[:, D_MODEL:] * yb).astype(BF16)


def _mix(proj, o, pool_w_b, pool_scale, w_pool_up_b, w_nsa_up_p):
    T = proj.shape[0]
    tm = min(MIX_TM, T)
    halo_blocks = tm // POOL_HALO
    return pl.pallas_call(
        _mix_kernel,
        grid=(T // tm,),
        in_specs=[pl.BlockSpec((tm, POOL_WIDTH), lambda i: (i, C_U // POOL_WIDTH)),
                  pl.BlockSpec((POOL_HALO, POOL_WIDTH),
                               lambda i: (jnp.maximum(i * halo_blocks - 1, 0), C_U // POOL_WIDTH)),
                  pl.BlockSpec((tm, 2 * D_MODEL), lambda i: (i, 0)),
                  pl.BlockSpec((tm, N_HEADS * HEAD_PAD), lambda i: (i, 0)),
                  _const_spec(pool_w_b.shape), _const_spec((1, POOL_WIDTH)),
                  _const_spec(w_pool_up_b.shape), _const_spec(w_nsa_up_p.shape)],
        out_specs=pl.BlockSpec((tm, D_MODEL), lambda i: (i, 0)),
        out_shape=jax.ShapeDtypeStruct((T, D_MODEL), BF16),
        compiler_params=_params("arbitrary"),
        name="pool_merge",
    )(proj, proj, proj, o, pool_w_b, pool_scale, w_pool_up_b, w_nsa_up_p)


OUT_TM = 256


def _rms(y, g):
    return y * lax.rsqrt(jnp.mean(y * y, axis=-1, keepdims=True) + RMS_EPS) * g


def _outproj_kernel(y_ref, w_ref, x_ref, g_ref, gt_ref, o_ref):
    z = jnp.dot(y_ref[...], w_ref[...], preferred_element_type=F32)
    o_ref[...] = x_ref[...] + gt_ref[...] * _rms(z, g_ref[...])


def _outproj(y, w_out_b, x2, g_post, mod):
    T = y.shape[0]
    tm = min(OUT_TM, T)
    row = pl.BlockSpec((tm, D_MODEL), lambda i: (i, 0))
    return pl.pallas_call(
        _outproj_kernel,
        grid=(T // tm,),
        in_specs=[row, _const_spec((D_MODEL, D_MODEL)), row, _const_spec((1, D_MODEL)), _mod_spec(2)],
        out_specs=row,
        out_shape=jax.ShapeDtypeStruct((T, D_MODEL), F32),
        compiler_params=_params("arbitrary"),
        name="out_proj",
    )(y, w_out_b, x2, g_post, mod)


def _pack_table():
    sizes = (POOL_WIDTH, N_HEADS * D_QK, N_KV_GROUPS * D_QK, N_KV_GROUPS * D_V, N_KV_GROUPS * D_QK,
             N_KV_GROUPS * D_V, N_KV_GROUPS * D_QK, N_KV_GROUPS * D_V, 3 * N_HEADS, 2 * D_MODEL)
    u, q, kc, vc, ks, vs, kw, vw, gn, gm = [int(o) for o in np.cumsum((0,) + sizes)[:-1]]
    gates = 3 * HEADS_PER_GROUP
    segments = [(gm, HEAD_PAD, 2 * D_MODEL // HEAD_PAD), (q, D_QK, N_HEADS), (u, HEAD_PAD, POOL_WIDTH // HEAD_PAD),
                (kc, D_QK, N_KV_GROUPS), (ks, D_QK, N_KV_GROUPS), (kw, D_QK, N_KV_GROUPS),
                (vc, D_V, N_KV_GROUPS), (vs, D_V, N_KV_GROUPS), (vw, D_V, N_KV_GROUPS),
                (gn, gates, N_KV_GROUPS)]
    tiles = [(start + i * width, width) for start, width, count in segments for i in range(count)]
    assert len(tiles) * HEAD_PAD == N_COLS
    return np.asarray(tiles, np.int32)


def _repack_kernel(blk_ref, shift_ref, width_ref, a_ref, b_ref, o_ref):
    j = pl.program_id(0)
    shift = shift_ref[j]
    lane = lax.broadcasted_iota(jnp.int32, (1, HEAD_PAD), 1)
    back = (HEAD_PAD - shift) & (HEAD_PAD - 1)
    merged = jnp.where(lane < HEAD_PAD - shift, pltpu.roll(a_ref[...], back, 1), pltpu.roll(b_ref[...], back, 1))
    o_ref[...] = jnp.where(lane < width_ref[j], merged, 0.0).astype(BF16)


def _pack_w_in(w_in):
    k, n_src = w_in.shape
    table = _pack_table()
    last = (n_src - 1) // HEAD_PAD
    grid_spec = pltpu.PrefetchScalarGridSpec(
        num_scalar_prefetch=3,
        grid=(table.shape[0],),
        in_specs=[pl.BlockSpec((k, HEAD_PAD), lambda j, blk, *_: (0, blk[j])),
                  pl.BlockSpec((k, HEAD_PAD), lambda j, blk, *_: (0, jnp.minimum(blk[j] + 1, last)))],
        out_specs=pl.BlockSpec((k, HEAD_PAD), lambda j, *_: (0, j)),
    )
    return pl.pallas_call(
        _repack_kernel,
        grid_spec=grid_spec,
        out_shape=jax.ShapeDtypeStruct((k, N_COLS), BF16),
        compiler_params=_params("arbitrary"),
        name="repack_w_in",
    )(jnp.asarray(table[:, 0] // HEAD_PAD), jnp.asarray(table[:, 0] % HEAD_PAD), jnp.asarray(table[:, 1]),
      w_in, w_in)


def _slc_from_cmp(n_blk, n_chunk):
    cs = np.arange(n_chunk)[None, :] * CMP_STRIDE
    ss = np.arange(n_blk)[:, None] * SLC_LEN
    ov = np.clip(np.minimum(cs + CMP_LEN, ss + SLC_LEN) - np.maximum(cs, ss), 0, None)
    return jnp.asarray(ov / CMP_LEN, dtype=BF16)


def _token_mixer_block(x2, pos_col, mod, g_pre, g_post, w_in, cmp_pos_k, cmp_k_w1, cmp_k_w2,
                       cmp_pos_v, cmp_v_w1, cmp_v_w2, pool_w, pool_scale, w_pool_up, w_nsa_up, w_out):
    T = x2.shape[0]
    half = jnp.arange(ROT_HALF, dtype=F32)
    inv = jnp.power(jnp.float32(ROPE_THETA), -half * 2.0 / (2 * ROT_HALF))
    invf = jnp.concatenate([inv, inv, jnp.zeros((HEAD_PAD - 2 * ROT_HALF,), F32)])[None, :]

    proj = _proj(x2, g_pre, mod, _pack_w_in(w_in))
    qt, ks_r, kw_r, vs_t, vw_t, gate_t = _prep(proj, pos_col, invf)

    n_chunk = T // CMP_STRIDE
    pos_cmp = jnp.pad(pos_col[CMP_LEN - 1::CMP_STRIDE], ((0, 1), (0, 0)))

    def pad_lanes(a, d):
        return jnp.pad(a, [(0, 0)] * (a.ndim - 1) + [(0, HEAD_PAD - d)])

    def w1_pack(w1, d):
        w1 = w1.reshape(CMP_LEN, d, w1.shape[1])
        return jnp.pad(w1, ((0, 0), (0, HEAD_PAD - d), (0, 0))).astype(BF16)

    k_cmp = _compress(proj, C_KC, pad_lanes(cmp_pos_k, D_QK), w1_pack(cmp_k_w1, D_QK),
                      pad_lanes(cmp_k_w2, D_QK).astype(BF16), pos_cmp, invf, True)
    v_cmp = _compress(proj, C_VC, pad_lanes(cmp_pos_v, D_V), w1_pack(cmp_v_w1, D_V),
                      pad_lanes(cmp_v_w2, D_V).T.astype(BF16), pos_cmp, invf, False)

    n_blk = max(T // SLC_LEN, MASK_ROWS)
    o = _attention(qt, k_cmp, v_cmp, ks_r, vs_t, kw_r, vw_t, gate_t, _slc_from_cmp(n_blk, n_chunk))

    w_nsa_up_p = jnp.pad(w_nsa_up.reshape(N_HEADS, D_V, D_MODEL),
                         ((0, 0), (0, HEAD_PAD - D_V), (0, 0))).reshape(N_HEADS * HEAD_PAD, D_MODEL)
    y = _mix(proj, o, pool_w.astype(BF16), pool_scale, w_pool_up.astype(BF16), w_nsa_up_p.astype(BF16))
    return _outproj(y, w_out.astype(BF16), x2, g_post, mod)


ROUTE_TM = 256
EXPERTS_PER_GROUP = N_EXPERTS // N_EXPERT_GROUPS


def _split3(a):
    hi = a.astype(BF16)
    r1 = a - hi.astype(F32)
    mid = r1.astype(BF16)
    lo = (r1 - mid.astype(F32)).astype(BF16)
    return hi, mid, lo


def _dot_f32(a, b):
    a3, b3 = _split3(a), _split3(b)
    out = None
    for i, j in ((2, 0), (1, 1), (0, 2), (1, 0), (0, 1), (0, 0)):
        term = jnp.dot(a3[i], b3[j], preferred_element_type=F32)
        out = term if out is None else out + term
    return out


def _first_index_of(mask, lane, n):
    return jnp.min(jnp.where(mask, lane, float(n)), axis=-1, keepdims=True)


def _route_kernel(x_ref, g_ref, sc_ref, sh_ref, wr_ref, rb_ref,
                  hf_ref, hb_ref, sel_ref, wd_ref, rank_ref, cnt_ref, carry_ref):
    i = pl.program_id(0)
    tm = x_ref.shape[0]

    @pl.when(i == 0)
    def _():
        carry_ref[...] = jnp.zeros_like(carry_ref)

    h = _modulated_norm(x_ref[...], g_ref[...], sc_ref[...], sh_ref[...])
    hf_ref[...] = h
    hb_ref[...] = h.astype(BF16)
    logits = _dot_f32(h, wr_ref[...])
    aff = jax.nn.sigmoid(logits)
    biased = aff + rb_ref[...]
    grp = lax.broadcasted_iota(jnp.int32, (1, N_EXPERTS), 1) // EXPERTS_PER_GROUP
    lane = lax.broadcasted_iota(jnp.int32, (1, N_EXPERTS), 1).astype(F32)

    gs = []
    for gi in range(N_EXPERT_GROUPS):
        v = jnp.where(grp == gi, biased, REMOVED)
        m1 = jnp.max(v, axis=-1, keepdims=True)
        i1 = _first_index_of(v == m1, lane, N_EXPERTS)
        m2 = jnp.max(jnp.where(lane == i1, REMOVED, v), axis=-1, keepdims=True)
        gs.append(m1 + m2)
    keep = jnp.zeros((tm, N_EXPERTS), jnp.bool_)
    for gi in range(N_EXPERT_GROUPS):
        ahead = jnp.zeros((tm, 1), jnp.int32)
        for gj in range(N_EXPERT_GROUPS):
            if gj < gi:
                ahead += (gs[gj] >= gs[gi]).astype(jnp.int32)
            elif gj > gi:
                ahead += (gs[gj] > gs[gi]).astype(jnp.int32)
        keep = keep | ((grp == gi) & (ahead < TOPK_GROUPS))
    masked = jnp.where(keep, biased, NEG_INF)
    sel = jnp.zeros((tm, N_EXPERTS), F32)
    for _ in range(TOP_K):
        best = jnp.max(masked, axis=-1, keepdims=True)
        hit = lane == _first_index_of(masked == best, lane, N_EXPERTS)
        sel = jnp.where(hit, 1.0, sel)
        masked = jnp.where(hit, REMOVED, masked)
    w = aff * sel
    wd_ref[...] = w / jnp.sum(w, axis=-1, keepdims=True) * ROUTED_SCALE
    sel_ref[...] = sel

    r_i = lax.broadcasted_iota(jnp.int32, (tm, tm), 0)
    c_i = lax.broadcasted_iota(jnp.int32, (tm, tm), 1)
    before = jnp.where(c_i < r_i, 1.0, 0.0).astype(BF16)
    carry = carry_ref[...]
    rank_ref[...] = jnp.dot(before, sel.astype(BF16), preferred_element_type=F32) + carry
    carry = carry + jnp.sum(sel, axis=0, keepdims=True)
    carry_ref[...] = carry
    cnt_ref[...] = carry


def _route(x1, g_pre, mod, w_router, router_bias):
    T = x1.shape[0]
    tm = min(ROUTE_TM, T)
    row = pl.BlockSpec((tm, D_MODEL), lambda i: (i, 0))
    erow = pl.BlockSpec((tm, N_EXPERTS), lambda i: (i, 0))
    e_shape = jax.ShapeDtypeStruct((T, N_EXPERTS), F32)
    return pl.pallas_call(
        _route_kernel,
        grid=(T // tm,),
        in_specs=[row, _const_spec((1, D_MODEL)), _mod_spec(4), _mod_spec(3),
                  _const_spec((D_MODEL, N_EXPERTS)), _const_spec((1, N_EXPERTS))],
        out_specs=[row, row, erow, erow, erow, _const_spec((1, N_EXPERTS))],
        out_shape=[jax.ShapeDtypeStruct((T, D_MODEL), F32), jax.ShapeDtypeStruct((T, D_MODEL), BF16),
                   e_shape, e_shape, e_shape, jax.ShapeDtypeStruct((1, N_EXPERTS), F32)],
        scratch_shapes=[pltpu.VMEM((1, N_EXPERTS), F32)],
        compiler_params=_params("arbitrary"),
        name="moe_route",
    )(x1, g_pre, mod, mod, w_router, router_bias)


def _dest_kernel(sel_ref, wd_ref, rank_ref, ps_ref, dest_ref, w8_ref):
    tm = sel_ref.shape[0]
    slot = ps_ref[...] + rank_ref[...]
    wd = wd_ref[...]
    lane = lax.broadcasted_iota(jnp.int32, (1, N_EXPERTS), 1).astype(F32)
    lane_k = lax.broadcasted_iota(jnp.int32, (1, TOP_K), 1)
    lane_p = lax.broadcasted_iota(jnp.int32, (1, HEAD_PAD), 1)
    dest = jnp.zeros((tm, HEAD_PAD), F32)
    w8 = jnp.zeros((tm, TOP_K), F32)
    rem = sel_ref[...]
    for k in range(TOP_K):
        hit = lane == _first_index_of(rem > 0.5, lane, N_EXPERTS)
        dest = jnp.where(lane_p == k, jnp.sum(jnp.where(hit, slot, 0.0), axis=-1, keepdims=True), dest)
        w8 = jnp.where(lane_k == k, jnp.sum(jnp.where(hit, wd, 0.0), axis=-1, keepdims=True), w8)
        rem = jnp.where(hit, 0.0, rem)
    dest_ref[...] = dest.T[0:TOP_K, :].astype(jnp.int32)
    w8_ref[...] = w8


def _dest(sel, wd, rank, pstart):
    T = sel.shape[0]
    tm = min(ROUTE_TM, T)
    erow = pl.BlockSpec((tm, N_EXPERTS), lambda i: (i, 0))
    krow = pl.BlockSpec((tm, TOP_K), lambda i: (i, 0))
    return pl.pallas_call(
        _dest_kernel,
        grid=(T // tm,),
        in_specs=[erow, erow, erow, _const_spec((1, N_EXPERTS))],
        out_specs=[pl.BlockSpec((TOP_K, tm), lambda i: (0, i)), krow],
        out_shape=[jax.ShapeDtypeStruct((TOP_K, T), jnp.int32), jax.ShapeDtypeStruct((T, TOP_K), F32)],
        compiler_params=_params("arbitrary"),
        name="moe_dest",
    )(sel, wd, rank, pstart)


DISPATCH_TM = 128


def _dispatch_kernel(dest_ref, h_ref, xs_hbm, sem):
    tm = h_ref.shape[0]

    def body(t, carry):
        for k in range(TOP_K):
            pltpu.make_async_copy(h_ref.at[pl.ds(t, 1), :], xs_hbm.at[pl.ds(dest_ref[k, t], 1), :], sem).start()
        return carry

    lax.fori_loop(0, tm, body, 0)
    for k in range(TOP_K):
        pltpu.make_async_copy(h_ref, xs_hbm.at[pl.ds(0, tm), :], sem).wait()


def _dispatch(dest_t, hf, n_slots):
    T = hf.shape[0]
    tm = min(DISPATCH_TM, T)
    return pl.pallas_call(
        _dispatch_kernel,
        grid=(T // tm,),
        in_specs=[pl.BlockSpec((TOP_K, tm), lambda i: (0, i), memory_space=pltpu.SMEM),
                  pl.BlockSpec((tm, D_MODEL), lambda i: (i, 0))],
        out_specs=pl.BlockSpec(memory_space=pl.ANY),
        out_shape=jax.ShapeDtypeStruct((n_slots, D_MODEL), F32),
        scratch_shapes=[pltpu.SemaphoreType.DMA(())],
        compiler_params=pltpu.CompilerParams(dimension_semantics=("arbitrary",), vmem_limit_bytes=VMEM_LIMIT,
                                             has_side_effects=True),
        name="moe_dispatch",
    )(dest_t, hf)


EXPERT_BLOCK = 256


def _swiglu(x, wg, wu, wd):
    g = jnp.dot(x, wg, preferred_element_type=F32)
    u = jnp.dot(x, wu, preferred_element_type=F32)
    hid = (g * jax.nn.sigmoid(g)) * u
    return jnp.dot(hid.astype(BF16), wd, preferred_element_type=F32)


def _expert_items(counts, n_items):
    ends = jnp.cumsum(counts)
    starts = ends - counts
    first_blk = starts // EXPERT_BLOCK
    last_blk = (ends - 1) // EXPERT_BLOCK
    per_expert = jnp.where(counts > 0, last_blk - first_blk + 1, 0)
    item_end = jnp.cumsum(per_expert)
    n_used = item_end[-1]
    i = jnp.minimum(jnp.arange(n_items, dtype=jnp.int32), jnp.maximum(n_used - 1, 0))
    e = jnp.sum((item_end[None, :] <= i[:, None]).astype(jnp.int32), axis=1)
    e = jnp.minimum(e, N_EXPERTS - 1)
    ids = jnp.arange(N_EXPERTS, dtype=jnp.int32)
    owner = e[:, None] == ids[None, :]
    of_item = lambda per_expert_value: jnp.sum(jnp.where(owner, per_expert_value[None, :], 0), axis=1)
    blk = of_item(first_blk) + i - of_item(item_end - per_expert)
    lo = jnp.clip(of_item(starts) - blk * EXPERT_BLOCK, 0, EXPERT_BLOCK)
    hi = jnp.clip(of_item(ends) - blk * EXPERT_BLOCK, 0, EXPERT_BLOCK)
    live = jnp.arange(n_items, dtype=jnp.int32) < n_used
    hi = jnp.where(live, hi, lo)
    new_expert = jnp.concatenate([jnp.ones((1,), jnp.bool_), e[1:] != e[:-1]])
    new_block = jnp.concatenate([jnp.ones((1,), jnp.bool_), blk[1:] != blk[:-1]])
    as_i32 = lambda a: a.astype(jnp.int32)
    used = counts > 0
    slot_of = (jnp.cumsum(as_i32(used)) - 1) % 2
    later = (ids[None, :] > ids[:, None]) & used[None, :]
    next_of = jnp.min(jnp.where(later, ids[None, :], N_EXPERTS), axis=1)
    next_of = jnp.where(next_of < N_EXPERTS, next_of, -1)
    return (as_i32(blk), as_i32(e), as_i32(lo), as_i32(hi), as_i32(new_expert & live),
            as_i32(new_block & live), as_i32(live), as_i32(of_item(slot_of)), as_i32(of_item(next_of)), starts)


def _expert_kernel(blk_ref, e_ref, lo_ref, hi_ref, newe_ref, newb_ref, live_ref, slot_ref, nexte_ref,
                   x_ref, wg_hbm, wu_hbm, wd_hbm, y_ref, wg32, wu32, wd32, wgb, wub, wdb, sem):
    i = pl.program_id(0)

    def weight_copies(e, slot):
        return (pltpu.make_async_copy(wg_hbm.at[e], wg32.at[slot], sem.at[slot]),
                pltpu.make_async_copy(wu_hbm.at[e], wu32.at[slot], sem.at[slot]),
                pltpu.make_async_copy(wd_hbm.at[e], wd32.at[slot], sem.at[slot]))

    @pl.when(live_ref[i] == 1)
    def _():
        @pl.when(newe_ref[i] == 1)
        def _():
            slot = slot_ref[i]

            @pl.when(i == 0)
            def _():
                for c in weight_copies(e_ref[i], slot):
                    c.start()

            for c in weight_copies(e_ref[i], slot):
                c.wait()
            wgb[...] = wg32[slot].astype(BF16)
            wub[...] = wu32[slot].astype(BF16)
            wdb[...] = wd32[slot].astype(BF16)

            @pl.when(nexte_ref[i] >= 0)
            def _():
                for c in weight_copies(nexte_ref[i], 1 - slot):
                    c.start()

        row = lax.broadcasted_iota(jnp.int32, (EXPERT_BLOCK, 1), 0)
        mine = (row >= lo_ref[i]) & (row < hi_ref[i])
        y = _swiglu(jnp.where(mine, x_ref[...], 0.0).astype(BF16), wgb[...], wub[...], wdb[...])

        @pl.when(newb_ref[i] == 1)
        def _():
            y_ref[...] = y

        @pl.when(newb_ref[i] == 0)
        def _():
            y_ref[...] += y


def _experts(items, xs, w_gate, w_up, w_down):
    n_items = items[0].shape[0]
    by_block = lambda i, blk, *_: (blk[i], 0)
    hbm = pl.BlockSpec(memory_space=pl.ANY)
    grid_spec = pltpu.PrefetchScalarGridSpec(
        num_scalar_prefetch=len(items),
        grid=(n_items,),
        in_specs=[pl.BlockSpec((EXPERT_BLOCK, D_MODEL), by_block), hbm, hbm, hbm],
        out_specs=pl.BlockSpec((EXPERT_BLOCK, D_MODEL), by_block),
        scratch_shapes=[pltpu.VMEM((2, D_MODEL, D_EXPERT), F32), pltpu.VMEM((2, D_MODEL, D_EXPERT), F32),
                        pltpu.VMEM((2, D_EXPERT, D_MODEL), F32),
                        pltpu.VMEM((D_MODEL, D_EXPERT), BF16), pltpu.VMEM((D_MODEL, D_EXPERT), BF16),
                        pltpu.VMEM((D_EXPERT, D_MODEL), BF16), pltpu.SemaphoreType.DMA((2,))],
    )
    return pl.pallas_call(
        _expert_kernel,
        grid_spec=grid_spec,
        out_shape=jax.ShapeDtypeStruct(xs.shape, F32),
        compiler_params=_params("arbitrary"),
        name="moe_experts",
    )(*items, xs, w_gate, w_up, w_down)


COMBINE_TM = 128


def _combine_kernel(dcur_ref, dnext_ref, w8_ref, hb_ref, x_ref, wsg_ref, wsu_ref, wsd_ref, g_ref, gt_ref,
                    y_hbm, o_ref, gbuf, sem):
    i = pl.program_id(0)
    n = pl.num_programs(0)
    tm = x_ref.shape[0]

    rows = TOP_K * tm

    def issue(d_ref, slot):
        base = slot * rows

        def body(t, carry):
            for k in range(TOP_K):
                pltpu.make_async_copy(y_hbm.at[pl.ds(d_ref[k, t], 1), :],
                                      gbuf.at[pl.ds(base + k * tm + t, 1), :], sem.at[slot]).start()
            return carry

        lax.fori_loop(0, tm, body, 0)

    @pl.when(i == 0)
    def _():
        issue(dcur_ref, 0)

    @pl.when(i + 1 < n)
    def _():
        issue(dnext_ref, (i + 1) % 2)

    slot = i % 2
    base = pl.multiple_of(slot * rows, rows)
    shared = _swiglu(hb_ref[...], wsg_ref[...], wsu_ref[...], wsd_ref[...])
    for k in range(TOP_K):
        pltpu.make_async_copy(y_hbm.at[pl.ds(0, tm), :], gbuf.at[pl.ds(base + k * tm, tm), :],
                              sem.at[slot]).wait()
    w8 = w8_ref[...]
    y = shared
    for k in range(TOP_K):
        y = y + w8[:, k:k + 1] * gbuf[pl.ds(base + k * tm, tm), :]
    o_ref[...] = x_ref[...] + gt_ref[...] * _rms(y, g_ref[...])


def _combine(dest_t, w8, hb, x1, wsg, wsu, wsd, g_post, mod, y_buf):
    T = x1.shape[0]
    tm = min(COMBINE_TM, T)
    n = T // tm
    row = pl.BlockSpec((tm, D_MODEL), lambda i: (i, 0))
    return pl.pallas_call(
        _combine_kernel,
        grid=(n,),
        in_specs=[pl.BlockSpec((TOP_K, tm), lambda i: (0, i), memory_space=pltpu.SMEM),
                  pl.BlockSpec((TOP_K, tm), lambda i: (0, jnp.minimum(i + 1, n - 1)), memory_space=pltpu.SMEM),
                  pl.BlockSpec((tm, TOP_K), lambda i: (i, 0)), row, row,
                  _const_spec(wsg.shape), _const_spec(wsu.shape), _const_spec(wsd.shape),
                  _const_spec((1, D_MODEL)), _mod_spec(5),
                  pl.BlockSpec(memory_space=pl.ANY)],
        out_specs=row,
        out_shape=jax.ShapeDtypeStruct((T, D_MODEL), F32),
        scratch_shapes=[pltpu.VMEM((2 * TOP_K * tm, D_MODEL), F32), pltpu.SemaphoreType.DMA((2,))],
        compiler_params=_params("arbitrary"),
        name="moe_combine",
    )(dest_t, dest_t, w8, hb, x1, wsg, wsu, wsd, g_post, mod, y_buf)


def _moe_block(x1, mod, g_pre, g_post, w_router, router_bias, w_exp_gate, w_exp_up, w_exp_down,
               w_sh_gate, w_sh_up, w_sh_down):
    T = x1.shape[0]
    hf, hb, sel, wd, rank, counts = _route(x1, g_pre, mod, w_router, router_bias)

    n_slots = T * TOP_K
    *items, starts = _expert_items(counts[0].astype(jnp.int32), n_slots // EXPERT_BLOCK + N_EXPERTS)

    dest_t, w8 = _dest(sel, wd, rank, starts.astype(F32)[None, :])
    xs = _dispatch(dest_t, hf, n_slots)
    y_buf = _experts(items, xs, w_exp_gate, w_exp_up, w_exp_down)
    return _combine(dest_t, w8, hb, x1, w_sh_gate.astype(BF16), w_sh_up.astype(BF16),
                    w_sh_down.astype(BF16), g_post, mod, y_buf)


def kernel(x, c, positions, w_ada, b_ada, g_pre_mix, g_post_mix, g_pre_ffn, g_post_ffn, w_in, cmp_pos_k, cmp_k_w1, cmp_k_w2, cmp_pos_v, cmp_v_w1, cmp_v_w2, pool_w, pool_scale, w_pool_up, w_nsa_up, w_out, w_router, router_bias, w_exp_gate, w_exp_up, w_exp_down, w_sh_gate, w_sh_up, w_sh_down):
    B, T, D = x.shape
    assert B == 1 and D == D_MODEL
    x2 = x.reshape(T, D)
    pos_col = positions.reshape(T, 1)
    for l in range(w_ada.shape[0]):
        mod = _ada(c.reshape(D, 1), w_ada[l], b_ada[l][None, :])
        x2 = _token_mixer_block(x2, pos_col, mod, g_pre_mix[l][None, :], g_post_mix[l][None, :], w_in[l],
                                cmp_pos_k[l], cmp_k_w1[l], cmp_k_w2[l], cmp_pos_v[l], cmp_v_w1[l],
                                cmp_v_w2[l], pool_w[l], pool_scale[l][None, :], w_pool_up[l], w_nsa_up[l],
                                w_out[l])
        x2 = _moe_block(x2, mod, g_pre_ffn[l][None, :], g_post_ffn[l][None, :], w_router[l],
                        router_bias[l][None, :], w_exp_gate[l], w_exp_up[l], w_exp_down[l],
                        w_sh_gate[l], w_sh_up[l], w_sh_down[l])
    return x2.reshape(B, T, D)
```

```python
import functools

import numpy as np
import jax
import jax.numpy as jnp
from jax import lax
from jax.experimental import pallas as pl
from jax.experimental.pallas import tpu as pltpu

F32 = jnp.float32
BF16 = jnp.bfloat16

D_MODEL = 2048
POOL_WIDTH = 1024
POOL_WINDOWS = (2, 4, 8, 16)
POOL_GROUP = 256
POOL_HALO = 16
N_HEADS = 16
N_KV_GROUPS = 4
HEADS_PER_GROUP = 4
D_QK = 96
D_V = 64
HEAD_PAD = 128
ROT_HALF = 12
ROPE_THETA = 500000.0
CMP_LEN = 32
CMP_STRIDE = 16
SLC_LEN = 64
N_SELECT = 16
WINDOW = 512
Q_BLOCK = 512
N_EXPERTS = 64
TOP_K = 8
N_EXPERT_GROUPS = 8
TOPK_GROUPS = 4
D_EXPERT = 512
ROUTED_SCALE = 2.5
RMS_EPS = 1e-6
NEG_INF = -1e30
SEL_FORCE = 1e30
REMOVED = -3e38

VMEM_LIMIT = 56 * 1024 * 1024

C_GM = 0
C_Q = 4096
C_U = 6144
C_KC = 7168
C_KS = 7680
C_KW = 8192
C_VC = 8704
C_VS = 9216
C_VW = 9728
C_GN = 10240
N_COLS = 10368


def _params(*sem):
    return pltpu.CompilerParams(dimension_semantics=sem, vmem_limit_bytes=VMEM_LIMIT)


def _const_spec(shape):
    nd = len(shape)
    return pl.BlockSpec(shape, lambda *_: (0,) * nd)


ADA_TN = 1536
ADA_ROWS = 64


def _ada_kernel(c_ref, w_ref, b_ref, o_ref, s_ref):
    c = c_ref[...]
    s_ref[...] = c * jax.nn.sigmoid(c)

    def body(k, acc):
        r = pl.multiple_of(k * ADA_ROWS, ADA_ROWS)
        for j in range(ADA_ROWS // 8):
            acc = acc + w_ref[pl.ds(r + 8 * j, 8), :] * s_ref[pl.ds(r + 8 * j, 8), :]
        return acc

    acc = lax.fori_loop(0, D_MODEL // ADA_ROWS, body, jnp.zeros((8, ADA_TN), F32))
    o_ref[...] = jnp.sum(acc, axis=0, keepdims=True) + b_ref[...]


def _ada(c_col, w_ada, b_ada):
    n = w_ada.shape[1]
    return pl.pallas_call(
        _ada_kernel,
        grid=(n // ADA_TN,),
        in_specs=[_const_spec((D_MODEL, 1)),
                  pl.BlockSpec((D_MODEL, ADA_TN), lambda j: (0, j)),
                  pl.BlockSpec((1, ADA_TN), lambda j: (0, j))],
        out_specs=pl.BlockSpec((1, ADA_TN), lambda j: (0, j)),
        out_shape=jax.ShapeDtypeStruct((1, n), F32),
        scratch_shapes=[pltpu.VMEM((D_MODEL, 1), F32)],
        compiler_params=_params("arbitrary"),
        name="ada",
    )(c_col, w_ada, b_ada)


def _mod_spec(k):
    return pl.BlockSpec((1, D_MODEL), lambda *_: (0, k))


PROJ_TM = 512
PROJ_TN = 1152


def _modulated_norm(x, g, sc, sh):
    y = x * lax.rsqrt(jnp.mean(x * x, axis=-1, keepdims=True) + RMS_EPS)
    return (y * g) * (1.0 + sc) + sh


def _hnorm_kernel(x_ref, g_ref, sc_ref, sh_ref, h_ref):
    h_ref[...] = _modulated_norm(x_ref[...], g_ref[...], sc_ref[...], sh_ref[...]).astype(BF16)


def _hnorm(x2, g_pre, mod):
    T = x2.shape[0]
    tm = min(PROJ_TM, T)
    row = pl.BlockSpec((tm, D_MODEL), lambda i: (i, 0))
    return pl.pallas_call(
        _hnorm_kernel,
        grid=(T // tm,),
        in_specs=[row, _const_spec((1, D_MODEL)), _mod_spec(1), _mod_spec(0)],
        out_specs=row,
        out_shape=jax.ShapeDtypeStruct((T, D_MODEL), BF16),
        compiler_params=_params("arbitrary"),
        name="pre_norm",
    )(x2, g_pre, mod, mod)


def _proj_kernel(h_ref, wt_ref, o_ref, wb_ref):
    @pl.when(pl.program_id(1) == 0)
    def _():
        wb_ref[...] = wt_ref[...].astype(BF16)

    o_ref[...] = lax.dot_general(h_ref[...], wb_ref[...], (((1,), (1,)), ((), ())),
                                 preferred_element_type=F32)


def _proj(h, w_in_tp):
    T = h.shape[0]
    tm = min(PROJ_TM, T)
    return pl.pallas_call(
        _proj_kernel,
        grid=(N_COLS // PROJ_TN, T // tm),
        in_specs=[pl.BlockSpec((tm, D_MODEL), lambda j, i: (i, 0)),
                  pl.BlockSpec((PROJ_TN, D_MODEL), lambda j, i: (j, 0))],
        out_specs=pl.BlockSpec((tm, PROJ_TN), lambda j, i: (i, j)),
        out_shape=jax.ShapeDtypeStruct((T, N_COLS), F32),
        scratch_shapes=[pltpu.VMEM((PROJ_TN, D_MODEL), BF16)],
        compiler_params=_params("arbitrary", "arbitrary"),
        name="in_proj",
    )(h, w_in_tp)


def _rope_tables(pos_col, invf):
    ang = pos_col.astype(F32) * invf
    cos = jnp.cos(ang)
    sin = jnp.sin(ang)
    lane = lax.broadcasted_iota(jnp.int32, ang.shape, 1)
    sin_lo = jnp.where(lane < ROT_HALF, -sin, 0.0)
    sin_hi = jnp.where((lane >= ROT_HALF) & (lane < 2 * ROT_HALF), sin, 0.0)
    return cos, sin_lo, sin_hi


def _rope_head(x, tabs):
    cos, sin_lo, sin_hi = tabs
    return (x * cos + pltpu.roll(x, HEAD_PAD - ROT_HALF, 1) * sin_lo
            + pltpu.roll(x, ROT_HALF, 1) * sin_hi)


PREP_TM = 256


LOG2E = 1.4426950408889634
Q_SCALE = (D_QK ** -0.5) * LOG2E
MASK_LANE = D_QK
MASK_ROWS = 16
ONES_ROW = D_V
V_ROWS = 80
MASK_BIG = 1e30


def _with_ones_row(vt):
    row = lax.broadcasted_iota(jnp.int32, (HEAD_PAD, 1), 0)
    return jnp.where(row == ONES_ROW, 1.0, vt)


def _prep_kernel(pos_ref, invf_ref, q_ref, ks_ref, kw_ref, vs_ref, vw_ref, gn_ref,
                 qt_ref, kso_ref, kwo_ref, vst_ref, vwt_ref, gt_ref):
    tm = q_ref.shape[0]
    tabs = _rope_tables(pos_ref[...], invf_ref[...])
    lane = lax.broadcasted_iota(jnp.int32, (1, HEAD_PAD), 1)
    qk = lambda ref, sl: jnp.where(lane < D_QK, ref[:, sl], 0.0)
    val = lambda ref, sl: jnp.where(lane < D_V, ref[:, sl], 0.0)
    for h in range(N_HEADS):
        sl = slice(h * HEAD_PAD, (h + 1) * HEAD_PAD)
        qt_ref[sl, :] = (_rope_head(qk(q_ref, sl), tabs) * Q_SCALE).T.astype(BF16)
    row = pl.program_id(0) * tm + lax.broadcasted_iota(jnp.int32, (tm, 1), 0)
    block_tag = jnp.where(lane == MASK_LANE + ((row >> 6) & (MASK_ROWS - 1)), 1.0, 0.0)
    for g in range(N_KV_GROUPS):
        sl = slice(g * HEAD_PAD, (g + 1) * HEAD_PAD)
        kso_ref[:, sl] = (_rope_head(qk(ks_ref, sl), tabs) + block_tag).astype(BF16)
        kwo_ref[:, sl] = _rope_head(qk(kw_ref, sl), tabs).astype(BF16)
        vst_ref[sl, :] = _with_ones_row(val(vs_ref, sl).T).astype(BF16)
        vwt_ref[sl, :] = _with_ones_row(val(vw_ref, sl).T).astype(BF16)
    gt_ref[...] = jax.nn.sigmoid(gn_ref[...]).T


def _prep(proj, pos_col, invf):
    T = proj.shape[0]
    tm = min(PREP_TM, T)
    kv = N_KV_GROUPS * HEAD_PAD
    qw = N_HEADS * HEAD_PAD

    def col(width, off):
        return pl.BlockSpec((tm, width), lambda i: (i, off // width))

    rows = lambda width: pl.BlockSpec((tm, width), lambda i: (i, 0))
    cols = lambda height: pl.BlockSpec((height, tm), lambda i: (0, i))
    return pl.pallas_call(
        _prep_kernel,
        grid=(T // tm,),
        in_specs=[pl.BlockSpec((tm, 1), lambda i: (i, 0)), _const_spec((1, HEAD_PAD)),
                  col(qw, C_Q), col(kv, C_KS), col(kv, C_KW), col(kv, C_VS), col(kv, C_VW),
                  col(HEAD_PAD, C_GN)],
        out_specs=[cols(qw), rows(kv), rows(kv), cols(kv), cols(kv), cols(HEAD_PAD)],
        out_shape=[jax.ShapeDtypeStruct((qw, T), BF16), jax.ShapeDtypeStruct((T, kv), BF16),
                   jax.ShapeDtypeStruct((T, kv), BF16), jax.ShapeDtypeStruct((kv, T), BF16),
                   jax.ShapeDtypeStruct((kv, T), BF16), jax.ShapeDtypeStruct((HEAD_PAD, T), F32)],
        compiler_params=_params("arbitrary"),
        name="nsa_prep",
    )(pos_col, invf, proj, proj, proj, proj, proj, proj)


def _cmp_kernel(x_ref, pe_ref, w1_ref, w2_ref, pos_ref, invf_ref, o_ref, *, is_key):
    nchunk = pos_ref.shape[0]
    hid_dim = w1_ref.shape[2]
    first = jnp.zeros((nchunk, hid_dim), F32)
    second = jnp.zeros((nchunk, hid_dim), F32)
    for l in range(CMP_STRIDE):
        rows = x_ref[pl.ds(l, nchunk, stride=CMP_STRIDE), :]
        first += jnp.dot((rows + pe_ref[l:l + 1, :]).astype(BF16), w1_ref[l],
                         preferred_element_type=F32)
        second += jnp.dot((rows + pe_ref[CMP_STRIDE + l:CMP_STRIDE + l + 1, :]).astype(BF16),
                          w1_ref[CMP_STRIDE + l], preferred_element_type=F32)
    second = jnp.concatenate([second[1:], jnp.zeros((1, hid_dim), F32)], axis=0)
    pre = first + second
    hid = pre * jax.nn.sigmoid(pre)
    if is_key:
        out = jnp.dot(hid.astype(BF16), w2_ref[...], preferred_element_type=F32)
        o_ref[0] = _rope_head(out, _rope_tables(pos_ref[...], invf_ref[...])).astype(BF16)
    else:
        out_t = lax.dot_general(w2_ref[...], hid.astype(BF16), (((1,), (1,)), ((), ())),
                                preferred_element_type=F32)
        o_ref[0] = _with_ones_row(out_t).astype(BF16)


def _compress(proj, col_off, pe_p, w1_p, w2_p, pos_cmp, invf, is_key):
    T = proj.shape[0]
    nchunk = T // CMP_STRIDE
    hid = w1_p.shape[2]
    out_block = (1, nchunk, HEAD_PAD) if is_key else (1, HEAD_PAD, nchunk)
    return pl.pallas_call(
        functools.partial(_cmp_kernel, is_key=is_key),
        grid=(N_KV_GROUPS,),
        in_specs=[pl.BlockSpec((T, HEAD_PAD), lambda g: (0, col_off // HEAD_PAD + g)),
                  _const_spec((CMP_LEN, HEAD_PAD)), _const_spec((CMP_LEN, HEAD_PAD, hid)),
                  _const_spec(w2_p.shape), _const_spec((nchunk, 1)), _const_spec((1, HEAD_PAD))],
        out_specs=pl.BlockSpec(out_block, lambda g: (g, 0, 0)),
        out_shape=jax.ShapeDtypeStruct((N_KV_GROUPS,) + out_block[1:], BF16),
        compiler_params=_params("arbitrary"),
        name="compress",
    )(proj, pe_p, w1_p, w2_p, pos_cmp, invf)


SLC_TK = 1024
WIN_SUB = 256
WIN_KEYS = WINDOW + WIN_SUB
R = HEADS_PER_GROUP


def _lanes4(a):
    return jnp.concatenate([a] * R, axis=1)


def _attn_kernel(qt_ref, kc_ref, vct_ref, ks_ref, vst_ref, kw_ref, vwt_ref, gate_ref, c2st_ref, o_ref,
                 qa_ref, neg_ref, acc_ref, m_ref, *, n_sel):
    ci = pl.program_id(1)
    s = ci * Q_BLOCK
    qt = jnp.concatenate([qt_ref[r * HEAD_PAD:(r + 1) * HEAD_PAD, :] for r in range(R)], axis=1)
    tok = s + lax.broadcasted_iota(jnp.int32, (1, Q_BLOCK), 1)
    tok4 = _lanes4(tok)

    n_cmp = kc_ref.shape[1]
    cmp_end = lax.broadcasted_iota(jnp.int32, (n_cmp, 1), 0) * CMP_STRIDE + (CMP_LEN - 1)
    sc = jnp.dot(kc_ref[0], qt, preferred_element_type=F32)
    sc = sc + _lanes4(jnp.where(cmp_end <= tok, 0.0, NEG_INF))
    p = jnp.exp2(sc - jnp.max(sc, axis=0, keepdims=True))
    inv = jnp.where(tok4 >= CMP_LEN - 1, 1.0 / jnp.sum(p, axis=0, keepdims=True), 0.0)
    p = p * inv
    o_cmp = jnp.dot(vct_ref[0, 0:V_ROWS, :], p.astype(BF16), preferred_element_type=F32)

    p_sum = p[:, 0:Q_BLOCK]
    for r in range(1, R):
        p_sum = p_sum + p[:, r * Q_BLOCK:(r + 1) * Q_BLOCK]
    p_hi = p_sum.astype(BF16)
    p_lo = (p_sum - p_hi.astype(F32)).astype(BF16)
    imp = (jnp.dot(c2st_ref[...], p_hi, preferred_element_type=F32)
           + jnp.dot(c2st_ref[...], p_lo, preferred_element_type=F32))

    n_blk = c2st_ref.shape[0]
    jj = lax.broadcasted_iota(jnp.int32, (n_blk, 1), 0)
    cur = tok >> 6
    forced = (jj == 0) | (jj == cur) | (jj == cur - 1)
    score = jnp.where(jj <= cur, jnp.where(forced, SEL_FORCE, imp), NEG_INF)
    chosen = jnp.zeros((n_blk, Q_BLOCK), jnp.bool_)
    for _ in range(n_sel):
        best = jnp.max(score, axis=0, keepdims=True)
        hit = jj == jnp.min(jnp.where(score == best, jj, n_blk), axis=0, keepdims=True)
        chosen = chosen | (hit & (best > 0.5 * NEG_INF))
        score = jnp.where(hit, REMOVED, score)
    neg_ref[...] = _lanes4(jnp.where(chosen, 0.0, -MASK_BIG)).astype(BF16)

    qa_ref[...] = qt
    acc_ref[...] = jnp.zeros_like(acc_ref)
    m_ref[...] = jnp.full_like(m_ref, NEG_INF)

    def tile(jt, causal):
        k0 = pl.multiple_of(jt * SLC_TK, SLC_TK)
        blk0 = pl.multiple_of((k0 // (SLC_LEN * MASK_ROWS)) * MASK_ROWS, MASK_ROWS)
        qa_ref[MASK_LANE:MASK_LANE + MASK_ROWS, :] = neg_ref[pl.ds(blk0, MASK_ROWS), :]
        st = jnp.dot(ks_ref[pl.ds(k0, SLC_TK), :], qa_ref[...], preferred_element_type=F32)
        if causal:
            kpos = k0 + lax.broadcasted_iota(jnp.int32, (SLC_TK, 1), 0)
            st = st + _lanes4(jnp.where(kpos <= tok, 0.0, NEG_INF))
        m_old = m_ref[...]
        m_new = jnp.maximum(m_old, jnp.max(st, axis=0, keepdims=True))
        pt = jnp.exp2(st - m_new).astype(BF16)
        acc_ref[...] = (jnp.exp2(m_old - m_new) * acc_ref[...]
                        + jnp.dot(vst_ref[0:V_ROWS, pl.ds(k0, SLC_TK)], pt, preferred_element_type=F32))
        m_ref[...] = m_new

    def full_tile(jt, carry):
        tile(jt, False)
        return carry

    n_full = s // SLC_TK
    lax.fori_loop(0, n_full, full_tile, 0)
    tile(n_full, True)
    acc = acc_ref[...]
    o_slc = acc * (1.0 / acc[ONES_ROW:ONES_ROW + 1, :])

    win_parts = []
    for b in range(Q_BLOCK // WIN_SUB):
        sb = s + b * WIN_SUB
        tok_b = sb + lax.broadcasted_iota(jnp.int32, (1, WIN_SUB), 1)
        qt_b = jnp.concatenate([qt[:, r * Q_BLOCK + b * WIN_SUB:r * Q_BLOCK + (b + 1) * WIN_SUB]
                                for r in range(R)], axis=1)
        w0 = pl.multiple_of(jnp.maximum(sb - WINDOW, 0), WIN_SUB)
        wpos = w0 + lax.broadcasted_iota(jnp.int32, (WIN_KEYS, 1), 0)
        sw = jnp.dot(kw_ref[pl.ds(w0, WIN_KEYS), :], qt_b, preferred_element_type=F32)
        sw = sw + _lanes4(jnp.where((wpos <= tok_b) & (wpos > tok_b - WINDOW), 0.0, NEG_INF))
        pw = jnp.exp2(sw - jnp.max(sw, axis=0, keepdims=True)).astype(BF16)
        ow = jnp.dot(vwt_ref[0:V_ROWS, pl.ds(w0, WIN_KEYS)], pw, preferred_element_type=F32)
        win_parts.append(ow * (1.0 / ow[ONES_ROW:ONES_ROW + 1, :]))

    for r in range(R):
        ls = slice(r * Q_BLOCK, (r + 1) * Q_BLOCK)
        o_win = jnp.concatenate([part[:, r * WIN_SUB:(r + 1) * WIN_SUB] for part in win_parts], axis=1)
        g0 = 3 * (pl.program_id(0) * R + r)
        o = (gate_ref[pl.ds(g0, 1), :] * o_cmp[:, ls] + gate_ref[pl.ds(g0 + 1, 1), :] * o_slc[:, ls]
             + gate_ref[pl.ds(g0 + 2, 1), :] * o_win)
        o = jnp.concatenate([o, jnp.zeros((HEAD_PAD - V_ROWS, Q_BLOCK), F32)], axis=0)
        o_ref[:, r * HEAD_PAD:(r + 1) * HEAD_PAD] = o.T.astype(BF16)


def _attention(qt, k_cmp, v_cmp_t, ks_r, vs_t, kw_r, vw_t, gate_t, c2st):
    T = ks_r.shape[0]
    n_cmp = k_cmp.shape[1]
    n_blk = c2st.shape[0]
    n_sel = min(N_SELECT, T // SLC_LEN)
    gw = R * HEAD_PAD
    rows = lambda g, ci: (0, g)
    cols = lambda g, ci: (g, 0)
    return pl.pallas_call(
        functools.partial(_attn_kernel, n_sel=n_sel),
        grid=(N_KV_GROUPS, T // Q_BLOCK),
        in_specs=[pl.BlockSpec((gw, Q_BLOCK), lambda g, ci: (g, ci)),
                  pl.BlockSpec((1, n_cmp, HEAD_PAD), lambda g, ci: (g, 0, 0)),
                  pl.BlockSpec((1, HEAD_PAD, n_cmp), lambda g, ci: (g, 0, 0)),
                  pl.BlockSpec((T, HEAD_PAD), rows), pl.BlockSpec((HEAD_PAD, T), cols),
                  pl.BlockSpec((T, HEAD_PAD), rows), pl.BlockSpec((HEAD_PAD, T), cols),
                  pl.BlockSpec((HEAD_PAD, Q_BLOCK), lambda g, ci: (0, ci)),
                  _const_spec((n_blk, n_cmp))],
        out_specs=pl.BlockSpec((Q_BLOCK, gw), lambda g, ci: (ci, g)),
        out_shape=jax.ShapeDtypeStruct((T, N_HEADS * HEAD_PAD), BF16),
        scratch_shapes=[pltpu.VMEM((HEAD_PAD, R * Q_BLOCK), BF16), pltpu.VMEM((n_blk, R * Q_BLOCK), BF16),
                        pltpu.VMEM((V_ROWS, R * Q_BLOCK), F32), pltpu.VMEM((1, R * Q_BLOCK), F32)],
        compiler_params=_params("arbitrary", "arbitrary"),
        name="nsa_attention",
    )(qt, k_cmp, v_cmp_t, ks_r, vs_t, kw_r, vw_t, gate_t, c2st)


MIX_TM = 256


def _mix_kernel(u_ref, up_ref, gm_ref, o_ref, pw_ref, ps_ref, wpu_ref, wnu_ref, y_ref):
    i = pl.program_id(0)
    tm = u_ref.shape[0]
    u = u_ref[...]
    prev = up_ref[...] * (i > 0).astype(F32)
    ext = jnp.concatenate([prev, u], axis=0)
    t = i * tm + lax.broadcasted_iota(jnp.int32, (tm, 1), 0)
    parts = []
    for g, win in enumerate(POOL_WINDOWS):
        cs = slice(g * POOL_GROUP, (g + 1) * POOL_GROUP)
        run = ext[:, cs]
        k = 1
        while k < win:
            run = run[k:] + run[:-k]
            k *= 2
        lo = POOL_HALO - (win - 1)
        total = run[lo:lo + tm]
        count = jnp.minimum(t + 1, win).astype(F32)
        d = total / count - u[:, cs]
        parts.append(jnp.dot(d.astype(BF16), pw_ref[g], preferred_element_type=F32) * ps_ref[:, cs])
    a = jnp.concatenate(parts, axis=1).astype(BF16)
    ya = jnp.dot(a, wpu_ref[...], preferred_element_type=F32)
    yb = jnp.dot(o_ref[...], wnu_ref[...], preferred_element_type=F32)
    gm = jax.nn.sigmoid(gm_ref[...])
    y_ref[...] = (gm[:, :D_MODEL] * ya + gm[:, D_MODEL:] * yb).astype(BF16)


def _mix(proj, o, pool_w_b, pool_scale, w_pool_up_b, w_nsa_up_p):
    T = proj.shape[0]
    tm = min(MIX_TM, T)
    halo_blocks = tm // POOL_HALO
    return pl.pallas_call(
        _mix_kernel,
        grid=(T // tm,),
        in_specs=[pl.BlockSpec((tm, POOL_WIDTH), lambda i: (i, C_U // POOL_WIDTH)),
                  pl.BlockSpec((POOL_HALO, POOL_WIDTH),
                               lambda i: (jnp.maximum(i * halo_blocks - 1, 0), C_U // POOL_WIDTH)),
                  pl.BlockSpec((tm, 2 * D_MODEL), lambda i: (i, 0)),
                  pl.BlockSpec((tm, N_HEADS * HEAD_PAD), lambda i: (i, 0)),
                  _const_spec(pool_w_b.shape), _const_spec((1, POOL_WIDTH)),
                  _const_spec(w_pool_up_b.shape), _const_spec(w_nsa_up_p.shape)],
        out_specs=pl.BlockSpec((tm, D_MODEL), lambda i: (i, 0)),
        out_shape=jax.ShapeDtypeStruct((T, D_MODEL), BF16),
        compiler_params=_params("arbitrary"),
        name="pool_merge",
    )(proj, proj, proj, o, pool_w_b, pool_scale, w_pool_up_b, w_nsa_up_p)


OUT_TM = 256


def _rms(y, g):
    return y * lax.rsqrt(jnp.mean(y * y, axis=-1, keepdims=True) + RMS_EPS) * g


def _outproj_kernel(y_ref, w_ref, x_ref, g_ref, gt_ref, o_ref):
    z = jnp.dot(y_ref[...], w_ref[...], preferred_element_type=F32)
    o_ref[...] = x_ref[...] + gt_ref[...] * _rms(z, g_ref[...])


def _outproj(y, w_out_b, x2, g_post, mod):
    T = y.shape[0]
    tm = min(OUT_TM, T)
    row = pl.BlockSpec((tm, D_MODEL), lambda i: (i, 0))
    return pl.pallas_call(
        _outproj_kernel,
        grid=(T // tm,),
        in_specs=[row, _const_spec((D_MODEL, D_MODEL)), row, _const_spec((1, D_MODEL)), _mod_spec(2)],
        out_specs=row,
        out_shape=jax.ShapeDtypeStruct((T, D_MODEL), F32),
        compiler_params=_params("arbitrary"),
        name="out_proj",
    )(y, w_out_b, x2, g_post, mod)


def _pack_table():
    sizes = (POOL_WIDTH, N_HEADS * D_QK, N_KV_GROUPS * D_QK, N_KV_GROUPS * D_V, N_KV_GROUPS * D_QK,
             N_KV_GROUPS * D_V, N_KV_GROUPS * D_QK, N_KV_GROUPS * D_V, 3 * N_HEADS, 2 * D_MODEL)
    u, q, kc, vc, ks, vs, kw, vw, gn, gm = [int(o) for o in np.cumsum((0,) + sizes)[:-1]]
    segments = [(gm, HEAD_PAD, 2 * D_MODEL // HEAD_PAD), (q, D_QK, N_HEADS), (u, HEAD_PAD, POOL_WIDTH // HEAD_PAD),
                (kc, D_QK, N_KV_GROUPS), (ks, D_QK, N_KV_GROUPS), (kw, D_QK, N_KV_GROUPS),
                (vc, D_V, N_KV_GROUPS), (vs, D_V, N_KV_GROUPS), (vw, D_V, N_KV_GROUPS), (gn, HEAD_PAD, 1)]
    starts = [start + i * width for start, width, count in segments for i in range(count)]
    assert len(starts) * HEAD_PAD == N_COLS and max(starts) + HEAD_PAD <= sum(sizes)
    return starts


def _repack_kernel(wt_hbm, o_hbm, sem):
    copies = [pltpu.make_async_copy(wt_hbm.at[pl.ds(src, HEAD_PAD), :],
                                    o_hbm.at[pl.ds(j * HEAD_PAD, HEAD_PAD), :], sem)
              for j, src in enumerate(_pack_table())]
    for c in copies:
        c.start()
    for c in copies:
        c.wait()


def _pack_w_in_t(w_in_t):
    hbm = pl.BlockSpec(memory_space=pl.ANY)
    return pl.pallas_call(
        _repack_kernel,
        in_specs=[hbm],
        out_specs=hbm,
        out_shape=jax.ShapeDtypeStruct((N_COLS, w_in_t.shape[1]), F32),
        scratch_shapes=[pltpu.SemaphoreType.DMA(())],
        name="repack_w_in",
    )(w_in_t)


def _slc_from_cmp(n_blk, n_chunk):
    cs = np.arange(n_chunk)[None, :] * CMP_STRIDE
    ss = np.arange(n_blk)[:, None] * SLC_LEN
    ov = np.clip(np.minimum(cs + CMP_LEN, ss + SLC_LEN) - np.maximum(cs, ss), 0, None)
    return jnp.asarray(ov / CMP_LEN, dtype=BF16)


def _token_mixer_block(x2, pos_col, mod, g_pre, g_post, w_in, cmp_pos_k, cmp_k_w1, cmp_k_w2,
                       cmp_pos_v, cmp_v_w1, cmp_v_w2, pool_w, pool_scale, w_pool_up, w_nsa_up, w_out):
    T = x2.shape[0]
    half = jnp.arange(ROT_HALF, dtype=F32)
    inv = jnp.power(jnp.float32(ROPE_THETA), -half * 2.0 / (2 * ROT_HALF))
    invf = jnp.concatenate([inv, inv, jnp.zeros((HEAD_PAD - 2 * ROT_HALF,), F32)])[None, :]

    proj = _proj(_hnorm(x2, g_pre, mod), _pack_w_in_t(w_in.T))
    qt, ks_r, kw_r, vs_t, vw_t, gate_t = _prep(proj, pos_col, invf)

    n_chunk = T // CMP_STRIDE
    pos_cmp = jnp.pad(pos_col[CMP_LEN - 1::CMP_STRIDE], ((0, 1), (0, 0)))

    def pad_lanes(a, d):
        return jnp.pad(a, [(0, 0)] * (a.ndim - 1) + [(0, HEAD_PAD - d)])

    def w1_pack(w1, d):
        w1 = w1.reshape(CMP_LEN, d, w1.shape[1])
        return jnp.pad(w1, ((0, 0), (0, HEAD_PAD - d), (0, 0))).astype(BF16)

    k_cmp = _compress(proj, C_KC, pad_lanes(cmp_pos_k, D_QK), w1_pack(cmp_k_w1, D_QK),
                      pad_lanes(cmp_k_w2, D_QK).astype(BF16), pos_cmp, invf, True)
    v_cmp = _compress(proj, C_VC, pad_lanes(cmp_pos_v, D_V), w1_pack(cmp_v_w1, D_V),
                      pad_lanes(cmp_v_w2, D_V).T.astype(BF16), pos_cmp, invf, False)

    n_blk = max(T // SLC_LEN, MASK_ROWS)
    o = _attention(qt, k_cmp, v_cmp, ks_r, vs_t, kw_r, vw_t, gate_t, _slc_from_cmp(n_blk, n_chunk))

    w_nsa_up_p = jnp.pad(w_nsa_up.reshape(N_HEADS, D_V, D_MODEL),
                         ((0, 0), (0, HEAD_PAD - D_V), (0, 0))).reshape(N_HEADS * HEAD_PAD, D_MODEL)
    y = _mix(proj, o, pool_w.astype(BF16), pool_scale, w_pool_up.astype(BF16), w_nsa_up_p.astype(BF16))
    return _outproj(y, w_out.astype(BF16), x2, g_post, mod)


ROUTE_TM = 256
EXPERTS_PER_GROUP = N_EXPERTS // N_EXPERT_GROUPS


def _split3(a):
    hi = a.astype(BF16)
    r1 = a - hi.astype(F32)
    mid = r1.astype(BF16)
    lo = (r1 - mid.astype(F32)).astype(BF16)
    return hi, mid, lo


def _dot_f32(a, b):
    a3, b3 = _split3(a), _split3(b)
    out = None
    for i, j in ((2, 0), (1, 1), (0, 2), (1, 0), (0, 1), (0, 0)):
        term = jnp.dot(a3[i], b3[j], preferred_element_type=F32)
        out = term if out is None else out + term
    return out


def _first_index_of(mask, lane, n):
    return jnp.min(jnp.where(mask, lane, float(n)), axis=-1, keepdims=True)


def _route_kernel(x_ref, g_ref, sc_ref, sh_ref, wr_ref, rb_ref,
                  hf_ref, hb_ref, sel_ref, wd_ref, rank_ref, cnt_ref, carry_ref):
    i = pl.program_id(0)
    tm = x_ref.shape[0]

    @pl.when(i == 0)
    def _():
        carry_ref[...] = jnp.zeros_like(carry_ref)

    h = _modulated_norm(x_ref[...], g_ref[...], sc_ref[...], sh_ref[...])
    hf_ref[...] = h
    hb_ref[...] = h.astype(BF16)
    logits = _dot_f32(h, wr_ref[...])
    aff = jax.nn.sigmoid(logits)
    biased = aff + rb_ref[...]
    grp = lax.broadcasted_iota(jnp.int32, (1, N_EXPERTS), 1) // EXPERTS_PER_GROUP
    lane = lax.broadcasted_iota(jnp.int32, (1, N_EXPERTS), 1).astype(F32)

    gs = []
    for gi in range(N_EXPERT_GROUPS):
        v = jnp.where(grp == gi, biased, REMOVED)
        m1 = jnp.max(v, axis=-1, keepdims=True)
        i1 = _first_index_of(v == m1, lane, N_EXPERTS)
        m2 = jnp.max(jnp.where(lane == i1, REMOVED, v), axis=-1, keepdims=True)
        gs.append(m1 + m2)
    keep = jnp.zeros((tm, N_EXPERTS), jnp.bool_)
    for gi in range(N_EXPERT_GROUPS):
        ahead = jnp.zeros((tm, 1), jnp.int32)
        for gj in range(N_EXPERT_GROUPS):
            if gj < gi:
                ahead += (gs[gj] >= gs[gi]).astype(jnp.int32)
            elif gj > gi:
                ahead += (gs[gj] > gs[gi]).astype(jnp.int32)
        keep = keep | ((grp == gi) & (ahead < TOPK_GROUPS))
    masked = jnp.where(keep, biased, NEG_INF)
    sel = jnp.zeros((tm, N_EXPERTS), F32)
    for _ in range(TOP_K):
        best = jnp.max(masked, axis=-1, keepdims=True)
        hit = lane == _first_index_of(masked == best, lane, N_EXPERTS)
        sel = jnp.where(hit, 1.0, sel)
        masked = jnp.where(hit, REMOVED, masked)
    w = aff * sel
    wd_ref[...] = w / jnp.sum(w, axis=-1, keepdims=True) * ROUTED_SCALE
    sel_ref[...] = sel

    r_i = lax.broadcasted_iota(jnp.int32, (tm, tm), 0)
    c_i = lax.broadcasted_iota(jnp.int32, (tm, tm), 1)
    before = jnp.where(c_i < r_i, 1.0, 0.0).astype(BF16)
    carry = carry_ref[...]
    rank_ref[...] = jnp.dot(before, sel.astype(BF16), preferred_element_type=F32) + carry
    carry = carry + jnp.sum(sel, axis=0, keepdims=True)
    carry_ref[...] = carry
    cnt_ref[...] = carry


def _route(x1, g_pre, mod, w_router, router_bias):
    T = x1.shape[0]
    tm = min(ROUTE_TM, T)
    row = pl.BlockSpec((tm, D_MODEL), lambda i: (i, 0))
    erow = pl.BlockSpec((tm, N_EXPERTS), lambda i: (i, 0))
    e_shape = jax.ShapeDtypeStruct((T, N_EXPERTS), F32)
    return pl.pallas_call(
        _route_kernel,
        grid=(T // tm,),
        in_specs=[row, _const_spec((1, D_MODEL)), _mod_spec(4), _mod_spec(3),
                  _const_spec((D_MODEL, N_EXPERTS)), _const_spec((1, N_EXPERTS))],
        out_specs=[row, row, erow, erow, erow, _const_spec((1, N_EXPERTS))],
        out_shape=[jax.ShapeDtypeStruct((T, D_MODEL), F32), jax.ShapeDtypeStruct((T, D_MODEL), BF16),
                   e_shape, e_shape, e_shape, jax.ShapeDtypeStruct((1, N_EXPERTS), F32)],
        scratch_shapes=[pltpu.VMEM((1, N_EXPERTS), F32)],
        compiler_params=_params("arbitrary"),
        name="moe_route",
    )(x1, g_pre, mod, mod, w_router, router_bias)


def _dest_kernel(sel_ref, wd_ref, rank_ref, ps_ref, dest_ref, w8_ref):
    tm = sel_ref.shape[0]
    slot = ps_ref[...] + rank_ref[...]
    wd = wd_ref[...]
    lane = lax.broadcasted_iota(jnp.int32, (1, N_EXPERTS), 1).astype(F32)
    lane_k = lax.broadcasted_iota(jnp.int32, (1, TOP_K), 1)
    lane_p = lax.broadcasted_iota(jnp.int32, (1, HEAD_PAD), 1)
    dest = jnp.zeros((tm, HEAD_PAD), F32)
    w8 = jnp.zeros((tm, TOP_K), F32)
    rem = sel_ref[...]
    for k in range(TOP_K):
        hit = lane == _first_index_of(rem > 0.5, lane, N_EXPERTS)
        dest = jnp.where(lane_p == k, jnp.sum(jnp.where(hit, slot, 0.0), axis=-1, keepdims=True), dest)
        w8 = jnp.where(lane_k == k, jnp.sum(jnp.where(hit, wd, 0.0), axis=-1, keepdims=True), w8)
        rem = jnp.where(hit, 0.0, rem)
    dest_ref[...] = dest.T[0:TOP_K, :].astype(jnp.int32)
    w8_ref[...] = w8


def _dest(sel, wd, rank, pstart):
    T = sel.shape[0]
    tm = min(ROUTE_TM, T)
    erow = pl.BlockSpec((tm, N_EXPERTS), lambda i: (i, 0))
    krow = pl.BlockSpec((tm, TOP_K), lambda i: (i, 0))
    return pl.pallas_call(
        _dest_kernel,
        grid=(T // tm,),
        in_specs=[erow, erow, erow, _const_spec((1, N_EXPERTS))],
        out_specs=[pl.BlockSpec((TOP_K, tm), lambda i: (0, i)), krow],
        out_shape=[jax.ShapeDtypeStruct((TOP_K, T), jnp.int32), jax.ShapeDtypeStruct((T, TOP_K), F32)],
        compiler_params=_params("arbitrary"),
        name="moe_dest",
    )(sel, wd, rank, pstart)


DISPATCH_TM = 128


def _dispatch_kernel(dest_ref, h_ref, xs_hbm, sem):
    tm = h_ref.shape[0]

    def body(t, carry):
        for k in range(TOP_K):
            pltpu.make_async_copy(h_ref.at[pl.ds(t, 1), :], xs_hbm.at[pl.ds(dest_ref[k, t], 1), :], sem).start()
        return carry

    lax.fori_loop(0, tm, body, 0)
    for k in range(TOP_K):
        pltpu.make_async_copy(h_ref, xs_hbm.at[pl.ds(0, tm), :], sem).wait()


def _dispatch(dest_t, hf, n_slots):
    T = hf.shape[0]
    tm = min(DISPATCH_TM, T)
    return pl.pallas_call(
        _dispatch_kernel,
        grid=(T // tm,),
        in_specs=[pl.BlockSpec((TOP_K, tm), lambda i: (0, i), memory_space=pltpu.SMEM),
                  pl.BlockSpec((tm, D_MODEL), lambda i: (i, 0))],
        out_specs=pl.BlockSpec(memory_space=pl.ANY),
        out_shape=jax.ShapeDtypeStruct((n_slots, D_MODEL), F32),
        scratch_shapes=[pltpu.SemaphoreType.DMA(())],
        compiler_params=pltpu.CompilerParams(dimension_semantics=("arbitrary",), vmem_limit_bytes=VMEM_LIMIT,
                                             has_side_effects=True),
        name="moe_dispatch",
    )(dest_t, hf)


EXPERT_BLOCK = 256


def _swiglu(x, wg, wu, wd):
    g = jnp.dot(x, wg, preferred_element_type=F32)
    u = jnp.dot(x, wu, preferred_element_type=F32)
    hid = (g * jax.nn.sigmoid(g)) * u
    return jnp.dot(hid.astype(BF16), wd, preferred_element_type=F32)


def _expert_items(counts, n_items):
    ends = jnp.cumsum(counts)
    starts = ends - counts
    first_blk = starts // EXPERT_BLOCK
    last_blk = (ends - 1) // EXPERT_BLOCK
    per_expert = jnp.where(counts > 0, last_blk - first_blk + 1, 0)
    item_end = jnp.cumsum(per_expert)
    n_used = item_end[-1]
    i = jnp.minimum(jnp.arange(n_items, dtype=jnp.int32), jnp.maximum(n_used - 1, 0))
    e = jnp.sum((item_end[None, :] <= i[:, None]).astype(jnp.int32), axis=1)
    e = jnp.minimum(e, N_EXPERTS - 1)
    ids = jnp.arange(N_EXPERTS, dtype=jnp.int32)
    owner = e[:, None] == ids[None, :]
    of_item = lambda per_expert_value: jnp.sum(jnp.where(owner, per_expert_value[None, :], 0), axis=1)
    blk = of_item(first_blk) + i - of_item(item_end - per_expert)
    lo = jnp.clip(of_item(starts) - blk * EXPERT_BLOCK, 0, EXPERT_BLOCK)
    hi = jnp.clip(of_item(ends) - blk * EXPERT_BLOCK, 0, EXPERT_BLOCK)
    live = jnp.arange(n_items, dtype=jnp.int32) < n_used
    hi = jnp.where(live, hi, lo)
    new_expert = jnp.concatenate([jnp.ones((1,), jnp.bool_), e[1:] != e[:-1]])
    new_block = jnp.concatenate([jnp.ones((1,), jnp.bool_), blk[1:] != blk[:-1]])
    as_i32 = lambda a: a.astype(jnp.int32)
    used = counts > 0
    slot_of = (jnp.cumsum(as_i32(used)) - 1) % 2
    later = (ids[None, :] > ids[:, None]) & used[None, :]
    next_of = jnp.min(jnp.where(later, ids[None, :], N_EXPERTS), axis=1)
    next_of = jnp.where(next_of < N_EXPERTS, next_of, -1)
    return (as_i32(blk), as_i32(e), as_i32(lo), as_i32(hi), as_i32(new_expert & live),
            as_i32(new_block & live), as_i32(live), as_i32(of_item(slot_of)), as_i32(of_item(next_of)), starts)


def _expert_kernel(blk_ref, e_ref, lo_ref, hi_ref, newe_ref, newb_ref, live_ref, slot_ref, nexte_ref,
                   x_ref, wg_hbm, wu_hbm, wd_hbm, y_ref, wg32, wu32, wd32, wgb, wub, wdb, sem):
    i = pl.program_id(0)

    def weight_copies(e, slot):
        return (pltpu.make_async_copy(wg_hbm.at[e], wg32.at[slot], sem.at[slot]),
                pltpu.make_async_copy(wu_hbm.at[e], wu32.at[slot], sem.at[slot]),
                pltpu.make_async_copy(wd_hbm.at[e], wd32.at[slot], sem.at[slot]))

    @pl.when(live_ref[i] == 1)
    def _():
        @pl.when(newe_ref[i] == 1)
        def _():
            slot = slot_ref[i]

            @pl.when(i == 0)
            def _():
                for c in weight_copies(e_ref[i], slot):
                    c.start()

            for c in weight_copies(e_ref[i], slot):
                c.wait()
            wgb[...] = wg32[slot].astype(BF16)
            wub[...] = wu32[slot].astype(BF16)
            wdb[...] = wd32[slot].astype(BF16)

            @pl.when(nexte_ref[i] >= 0)
            def _():
                for c in weight_copies(nexte_ref[i], 1 - slot):
                    c.start()

        row = lax.broadcasted_iota(jnp.int32, (EXPERT_BLOCK, 1), 0)
        mine = (row >= lo_ref[i]) & (row < hi_ref[i])
        y = _swiglu(jnp.where(mine, x_ref[...], 0.0).astype(BF16), wgb[...], wub[...], wdb[...])

        @pl.when(newb_ref[i] == 1)
        def _():
            y_ref[...] = y

        @pl.when(newb_ref[i] == 0)
        def _():
            y_ref[...] += y


def _experts(items, xs, w_gate, w_up, w_down):
    n_items = items[0].shape[0]
    by_block = lambda i, blk, *_: (blk[i], 0)
    hbm = pl.BlockSpec(memory_space=pl.ANY)
    grid_spec = pltpu.PrefetchScalarGridSpec(
        num_scalar_prefetch=len(items),
        grid=(n_items,),
        in_specs=[pl.BlockSpec((EXPERT_BLOCK, D_MODEL), by_block), hbm, hbm, hbm],
        out_specs=pl.BlockSpec((EXPERT_BLOCK, D_MODEL), by_block),
        scratch_shapes=[pltpu.VMEM((2, D_MODEL, D_EXPERT), F32), pltpu.VMEM((2, D_MODEL, D_EXPERT), F32),
                        pltpu.VMEM((2, D_EXPERT, D_MODEL), F32),
                        pltpu.VMEM((D_MODEL, D_EXPERT), BF16), pltpu.VMEM((D_MODEL, D_EXPERT), BF16),
                        pltpu.VMEM((D_EXPERT, D_MODEL), BF16), pltpu.SemaphoreType.DMA((2,))],
    )
    return pl.pallas_call(
        _expert_kernel,
        grid_spec=grid_spec,
        out_shape=jax.ShapeDtypeStruct(xs.shape, F32),
        compiler_params=_params("arbitrary"),
        name="moe_experts",
    )(*items, xs, w_gate, w_up, w_down)


COMBINE_TM = 128


def _combine_kernel(dcur_ref, dnext_ref, w8_ref, hb_ref, x_ref, wsg_ref, wsu_ref, wsd_ref, g_ref, gt_ref,
                    y_hbm, o_ref, gbuf, sem):
    i = pl.program_id(0)
    n = pl.num_programs(0)
    tm = x_ref.shape[0]

    rows = TOP_K * tm

    def issue(d_ref, slot):
        base = slot * rows

        def body(t, carry):
            for k in range(TOP_K):
                pltpu.make_async_copy(y_hbm.at[pl.ds(d_ref[k, t], 1), :],
                                      gbuf.at[pl.ds(base + k * tm + t, 1), :], sem.at[slot]).start()
            return carry

        lax.fori_loop(0, tm, body, 0)

    @pl.when(i == 0)
    def _():
        issue(dcur_ref, 0)

    @pl.when(i + 1 < n)
    def _():
        issue(dnext_ref, (i + 1) % 2)

    slot = i % 2
    base = pl.multiple_of(slot * rows, rows)
    shared = _swiglu(hb_ref[...], wsg_ref[...], wsu_ref[...], wsd_ref[...])
    for k in range(TOP_K):
        pltpu.make_async_copy(y_hbm.at[pl.ds(0, tm), :], gbuf.at[pl.ds(base + k * tm, tm), :],
                              sem.at[slot]).wait()
    w8 = w8_ref[...]
    y = shared
    for k in range(TOP_K):
        y = y + w8[:, k:k + 1] * gbuf[pl.ds(base + k * tm, tm), :]
    o_ref[...] = x_ref[...] + gt_ref[...] * _rms(y, g_ref[...])


def _combine(dest_t, w8, hb, x1, wsg, wsu, wsd, g_post, mod, y_buf):
    T = x1.shape[0]
    tm = min(COMBINE_TM, T)
    n = T // tm
    row = pl.BlockSpec((tm, D_MODEL), lambda i: (i, 0))
    return pl.pallas_call(
        _combine_kernel,
        grid=(n,),
        in_specs=[pl.BlockSpec((TOP_K, tm), lambda i: (0, i), memory_space=pltpu.SMEM),
                  pl.BlockSpec((TOP_K, tm), lambda i: (0, jnp.minimum(i + 1, n - 1)), memory_space=pltpu.SMEM),
                  pl.BlockSpec((tm, TOP_K), lambda i: (i, 0)), row, row,
                  _const_spec(wsg.shape), _const_spec(wsu.shape), _const_spec(wsd.shape),
                  _const_spec((1, D_MODEL)), _mod_spec(5),
                  pl.BlockSpec(memory_space=pl.ANY)],
        out_specs=row,
        out_shape=jax.ShapeDtypeStruct((T, D_MODEL), F32),
        scratch_shapes=[pltpu.VMEM((2 * TOP_K * tm, D_MODEL), F32), pltpu.SemaphoreType.DMA((2,))],
        compiler_params=_params("arbitrary"),
        name="moe_combine",
    )(dest_t, dest_t, w8, hb, x1, wsg, wsu, wsd, g_post, mod, y_buf)


def _moe_block(x1, mod, g_pre, g_post, w_router, router_bias, w_exp_gate, w_exp_up, w_exp_down,
               w_sh_gate, w_sh_up, w_sh_down):
    T = x1.shape[0]
    hf, hb, sel, wd, rank, counts = _route(x1, g_pre, mod, w_router, router_bias)

    n_slots = T * TOP_K
    *items, starts = _expert_items(counts[0].astype(jnp.int32), n_slots // EXPERT_BLOCK + N_EXPERTS)

    dest_t, w8 = _dest(sel, wd, rank, starts.astype(F32)[None, :])
    xs = _dispatch(dest_t, hf, n_slots)
    y_buf = _experts(items, xs, w_exp_gate, w_exp_up, w_exp_down)
    return _combine(dest_t, w8, hb, x1, w_sh_gate.astype(BF16), w_sh_up.astype(BF16),
                    w_sh_down.astype(BF16), g_post, mod, y_buf)


def kernel(x, c, positions, w_ada, b_ada, g_pre_mix, g_post_mix, g_pre_ffn, g_post_ffn, w_in, cmp_pos_k, cmp_k_w1, cmp_k_w2, cmp_pos_v, cmp_v_w1, cmp_v_w2, pool_w, pool_scale, w_pool_up, w_nsa_up, w_out, w_router, router_bias, w_exp_gate, w_exp_up, w_exp_down, w_sh_gate, w_sh_up, w_sh_down):
    B, T, D = x.shape
    assert B == 1 and D == D_MODEL
    x2 = x.reshape(T, D)
    pos_col = positions.reshape(T, 1)
    for l in range(w_ada.shape[0]):
        mod = _ada(c.reshape(D, 1), w_ada[l], b_ada[l][None, :])
        x2 = _token_mixer_block(x2, pos_col, mod, g_pre_mix[l][None, :], g_post_mix[l][None, :], w_in[l],
                                cmp_pos_k[l], cmp_k_w1[l], cmp_k_w2[l], cmp_pos_v[l], cmp_v_w1[l],
                                cmp_v_w2[l], pool_w[l], pool_scale[l][None, :], w_pool_up[l], w_nsa_up[l],
                                w_out[l])
        x2 = _moe_block(x2, mod, g_pre_ffn[l][None, :], g_post_ffn[l][None, :], w_router[l],
                        router_bias[l][None, :], w_exp_gate[l], w_exp_up[l], w_exp_down[l],
                        w_sh_gate[l], w_sh_up[l], w_sh_down[l])
    return x2.reshape(B, T, D)
```

```python
import functools

import numpy as np
import jax
import jax.numpy as jnp
from jax import lax
from jax.experimental import pallas as pl
from jax.experimental.pallas import tpu as pltpu

F32 = jnp.float32
BF16 = jnp.bfloat16

D_MODEL = 2048
POOL_WIDTH = 1024
POOL_WINDOWS = (2, 4, 8, 16)
POOL_GROUP = 256
POOL_HALO = 16
N_HEADS = 16
N_KV_GROUPS = 4
HEADS_PER_GROUP = 4
D_QK = 96
D_V = 64
HEAD_PAD = 128
ROT_HALF = 12
ROPE_THETA = 500000.0
CMP_LEN = 32
CMP_STRIDE = 16
SLC_LEN = 64
N_SELECT = 16
WINDOW = 512
Q_BLOCK = 512
N_EXPERTS = 64
TOP_K = 8
N_EXPERT_GROUPS = 8
TOPK_GROUPS = 4
D_EXPERT = 512
ROUTED_SCALE = 2.5
RMS_EPS = 1e-6
NEG_INF = -1e30
SEL_FORCE = 1e30
REMOVED = -3e38

VMEM_LIMIT = 56 * 1024 * 1024

C_GM = 0
C_Q = 4096
C_U = 6144
C_KC = 7168
C_KS = 7680
C_KW = 8192
C_VC = 8704
C_VS = 9216
C_VW = 9728
C_GN = 10240
N_COLS = 10368


def _params(*sem):
    return pltpu.CompilerParams(dimension_semantics=sem, vmem_limit_bytes=VMEM_LIMIT)


def _const_spec(shape):
    nd = len(shape)
    return pl.BlockSpec(shape, lambda *_: (0,) * nd)


ADA_TN = 1536
ADA_ROWS = 64


def _ada_kernel(c_ref, w_ref, b_ref, o_ref, s_ref):
    c = c_ref[...]
    s_ref[...] = c * jax.nn.sigmoid(c)

    def body(k, acc):
        r = pl.multiple_of(k * ADA_ROWS, ADA_ROWS)
        for j in range(ADA_ROWS // 8):
            acc = acc + w_ref[pl.ds(r + 8 * j, 8), :] * s_ref[pl.ds(r + 8 * j, 8), :]
        return acc

    acc = lax.fori_loop(0, D_MODEL // ADA_ROWS, body, jnp.zeros((8, ADA_TN), F32))
    o_ref[...] = jnp.sum(acc, axis=0, keepdims=True) + b_ref[...]


def _ada(c_col, w_ada, b_ada):
    n = w_ada.shape[1]
    return pl.pallas_call(
        _ada_kernel,
        grid=(n // ADA_TN,),
        in_specs=[_const_spec((D_MODEL, 1)),
                  pl.BlockSpec((D_MODEL, ADA_TN), lambda j: (0, j)),
                  pl.BlockSpec((1, ADA_TN), lambda j: (0, j))],
        out_specs=pl.BlockSpec((1, ADA_TN), lambda j: (0, j)),
        out_shape=jax.ShapeDtypeStruct((1, n), F32),
        scratch_shapes=[pltpu.VMEM((D_MODEL, 1), F32)],
        compiler_params=_params("arbitrary"),
        name="ada",
    )(c_col, w_ada, b_ada)


def _mod_spec(k):
    return pl.BlockSpec((1, D_MODEL), lambda *_: (0, k))


PROJ_TM = 512
PROJ_TN = 1152


def _modulated_norm(x, g, sc, sh):
    y = x * lax.rsqrt(jnp.mean(x * x, axis=-1, keepdims=True) + RMS_EPS)
    return (y * g) * (1.0 + sc) + sh


def _proj_kernel(x_ref, g_ref, sc_ref, sh_ref, w_ref, o_ref, h_ref):
    @pl.when(pl.program_id(1) == 0)
    def _():
        h_ref[...] = _modulated_norm(x_ref[...], g_ref[...], sc_ref[...], sh_ref[...]).astype(BF16)

    o_ref[...] = jnp.dot(h_ref[...], w_ref[...], preferred_element_type=F32)


def _proj(x2, g_pre, mod, w_in_p):
    T = x2.shape[0]
    tm = min(PROJ_TM, T)
    return pl.pallas_call(
        _proj_kernel,
        grid=(T // tm, N_COLS // PROJ_TN),
        in_specs=[pl.BlockSpec((tm, D_MODEL), lambda i, j: (i, 0)),
                  _const_spec((1, D_MODEL)), _mod_spec(1), _mod_spec(0),
                  pl.BlockSpec((D_MODEL, PROJ_TN), lambda i, j: (0, j))],
        out_specs=pl.BlockSpec((tm, PROJ_TN), lambda i, j: (i, j)),
        out_shape=jax.ShapeDtypeStruct((T, N_COLS), F32),
        scratch_shapes=[pltpu.VMEM((tm, D_MODEL), BF16)],
        compiler_params=_params("arbitrary", "arbitrary"),
        name="in_proj",
    )(x2, g_pre, mod, mod, w_in_p)


def _rope_tables(pos_col, invf):
    ang = pos_col.astype(F32) * invf
    cos = jnp.cos(ang)
    sin = jnp.sin(ang)
    lane = lax.broadcasted_iota(jnp.int32, ang.shape, 1)
    sin_lo = jnp.where(lane < ROT_HALF, -sin, 0.0)
    sin_hi = jnp.where((lane >= ROT_HALF) & (lane < 2 * ROT_HALF), sin, 0.0)
    return cos, sin_lo, sin_hi


def _rope_head(x, tabs):
    cos, sin_lo, sin_hi = tabs
    return (x * cos + pltpu.roll(x, HEAD_PAD - ROT_HALF, 1) * sin_lo
            + pltpu.roll(x, ROT_HALF, 1) * sin_hi)


PREP_TM = 256


LOG2E = 1.4426950408889634
Q_SCALE = (D_QK ** -0.5) * LOG2E
MASK_LANE = D_QK
MASK_ROWS = 16
ONES_ROW = D_V
V_ROWS = 80
MASK_BIG = 1e30


def _with_ones_row(vt):
    row = lax.broadcasted_iota(jnp.int32, (HEAD_PAD, 1), 0)
    return jnp.where(row == ONES_ROW, 1.0, vt)


def _prep_kernel(pos_ref, invf_ref, q_ref, ks_ref, kw_ref, vs_ref, vw_ref, gn_ref,
                 qt_ref, kso_ref, kwo_ref, vst_ref, vwt_ref, gt_ref):
    tm = q_ref.shape[0]
    tabs = _rope_tables(pos_ref[...], invf_ref[...])
    for h in range(N_HEADS):
        sl = slice(h * HEAD_PAD, (h + 1) * HEAD_PAD)
        qt_ref[sl, :] = (_rope_head(q_ref[:, sl], tabs) * Q_SCALE).T.astype(BF16)
    row = pl.program_id(0) * tm + lax.broadcasted_iota(jnp.int32, (tm, 1), 0)
    lane = lax.broadcasted_iota(jnp.int32, (1, HEAD_PAD), 1)
    block_tag = jnp.where(lane == MASK_LANE + ((row >> 6) & (MASK_ROWS - 1)), 1.0, 0.0)
    for g in range(N_KV_GROUPS):
        sl = slice(g * HEAD_PAD, (g + 1) * HEAD_PAD)
        kso_ref[:, sl] = (_rope_head(ks_ref[:, sl], tabs) + block_tag).astype(BF16)
        kwo_ref[:, sl] = _rope_head(kw_ref[:, sl], tabs).astype(BF16)
        vst_ref[sl, :] = _with_ones_row(vs_ref[:, sl].T).astype(BF16)
        vwt_ref[sl, :] = _with_ones_row(vw_ref[:, sl].T).astype(BF16)
    gt_ref[...] = jax.nn.sigmoid(gn_ref[...]).T


def _prep(proj, pos_col, invf):
    T = proj.shape[0]
    tm = min(PREP_TM, T)
    kv = N_KV_GROUPS * HEAD_PAD
    qw = N_HEADS * HEAD_PAD

    def col(width, off):
        return pl.BlockSpec((tm, width), lambda i: (i, off // width))

    rows = lambda width: pl.BlockSpec((tm, width), lambda i: (i, 0))
    cols = lambda height: pl.BlockSpec((height, tm), lambda i: (0, i))
    return pl.pallas_call(
        _prep_kernel,
        grid=(T // tm,),
        in_specs=[pl.BlockSpec((tm, 1), lambda i: (i, 0)), _const_spec((1, HEAD_PAD)),
                  col(qw, C_Q), col(kv, C_KS), col(kv, C_KW), col(kv, C_VS), col(kv, C_VW),
                  col(HEAD_PAD, C_GN)],
        out_specs=[cols(qw), rows(kv), rows(kv), cols(kv), cols(kv), cols(HEAD_PAD)],
        out_shape=[jax.ShapeDtypeStruct((qw, T), BF16), jax.ShapeDtypeStruct((T, kv), BF16),
                   jax.ShapeDtypeStruct((T, kv), BF16), jax.ShapeDtypeStruct((kv, T), BF16),
                   jax.ShapeDtypeStruct((kv, T), BF16), jax.ShapeDtypeStruct((HEAD_PAD, T), F32)],
        compiler_params=_params("arbitrary"),
        name="nsa_prep",
    )(pos_col, invf, proj, proj, proj, proj, proj, proj)


def _cmp_kernel(x_ref, pe_ref, w1_ref, w2_ref, pos_ref, invf_ref, o_ref, *, is_key):
    nchunk = pos_ref.shape[0]
    hid_dim = w1_ref.shape[2]
    first = jnp.zeros((nchunk, hid_dim), F32)
    second = jnp.zeros((nchunk, hid_dim), F32)
    for l in range(CMP_STRIDE):
        rows = x_ref[pl.ds(l, nchunk, stride=CMP_STRIDE), :]
        first += jnp.dot((rows + pe_ref[l:l + 1, :]).astype(BF16), w1_ref[l],
                         preferred_element_type=F32)
        second += jnp.dot((rows + pe_ref[CMP_STRIDE + l:CMP_STRIDE + l + 1, :]).astype(BF16),
                          w1_ref[CMP_STRIDE + l], preferred_element_type=F32)
    second = jnp.concatenate([second[1:], jnp.zeros((1, hid_dim), F32)], axis=0)
    pre = first + second
    hid = pre * jax.nn.sigmoid(pre)
    if is_key:
        out = jnp.dot(hid.astype(BF16), w2_ref[...], preferred_element_type=F32)
        o_ref[0] = _rope_head(out, _rope_tables(pos_ref[...], invf_ref[...])).astype(BF16)
    else:
        out_t = lax.dot_general(w2_ref[...], hid.astype(BF16), (((1,), (1,)), ((), ())),
                                preferred_element_type=F32)
        o_ref[0] = _with_ones_row(out_t).astype(BF16)


def _compress(proj, col_off, pe_p, w1_p, w2_p, pos_cmp, invf, is_key):
    T = proj.shape[0]
    nchunk = T // CMP_STRIDE
    hid = w1_p.shape[2]
    out_block = (1, nchunk, HEAD_PAD) if is_key else (1, HEAD_PAD, nchunk)
    return pl.pallas_call(
        functools.partial(_cmp_kernel, is_key=is_key),
        grid=(N_KV_GROUPS,),
        in_specs=[pl.BlockSpec((T, HEAD_PAD), lambda g: (0, col_off // HEAD_PAD + g)),
                  _const_spec((CMP_LEN, HEAD_PAD)), _const_spec((CMP_LEN, HEAD_PAD, hid)),
                  _const_spec(w2_p.shape), _const_spec((nchunk, 1)), _const_spec((1, HEAD_PAD))],
        out_specs=pl.BlockSpec(out_block, lambda g: (g, 0, 0)),
        out_shape=jax.ShapeDtypeStruct((N_KV_GROUPS,) + out_block[1:], BF16),
        compiler_params=_params("arbitrary"),
        name="compress",
    )(proj, pe_p, w1_p, w2_p, pos_cmp, invf)


SLC_TK = 1024
WIN_SUB = 256
WIN_KEYS = WINDOW + WIN_SUB
R = HEADS_PER_GROUP


def _lanes4(a):
    return jnp.concatenate([a] * R, axis=1)


def _attn_kernel(qt_ref, kc_ref, vct_ref, ks_ref, vst_ref, kw_ref, vwt_ref, gate_ref, c2st_ref, o_ref,
                 qa_ref, neg_ref, acc_ref, m_ref, s_ref, cmax_ref, *, n_sel):
    ci = pl.program_id(1)
    s = ci * Q_BLOCK
    qt = jnp.concatenate([qt_ref[r * HEAD_PAD:(r + 1) * HEAD_PAD, :] for r in range(R)], axis=1)
    tok = s + lax.broadcasted_iota(jnp.int32, (1, Q_BLOCK), 1)
    tok4 = _lanes4(tok)

    n_cmp = kc_ref.shape[1]
    cmp_end = lax.broadcasted_iota(jnp.int32, (n_cmp, 1), 0) * CMP_STRIDE + (CMP_LEN - 1)
    sc = jnp.dot(kc_ref[0], qt, preferred_element_type=F32)
    sc = sc + _lanes4(jnp.where(cmp_end <= tok, 0.0, NEG_INF))
    p = jnp.exp2(sc - jnp.max(sc, axis=0, keepdims=True))
    inv = jnp.where(tok4 >= CMP_LEN - 1, 1.0 / jnp.sum(p, axis=0, keepdims=True), 0.0)
    p = p * inv
    o_cmp = jnp.dot(vct_ref[0, 0:V_ROWS, :], p.astype(BF16), preferred_element_type=F32)

    p_sum = p[:, 0:Q_BLOCK]
    for r in range(1, R):
        p_sum = p_sum + p[:, r * Q_BLOCK:(r + 1) * Q_BLOCK]
    p_hi = p_sum.astype(BF16)
    p_lo = (p_sum - p_hi.astype(F32)).astype(BF16)
    imp = (jnp.dot(c2st_ref[...], p_hi, preferred_element_type=F32)
           + jnp.dot(c2st_ref[...], p_lo, preferred_element_type=F32))

    n_blk = c2st_ref.shape[0]
    jj = lax.broadcasted_iota(jnp.int32, (n_blk, 1), 0)
    cur = tok >> 6
    forced = (jj == 0) | (jj == cur) | (jj == cur - 1)
    score = jnp.where(jj <= cur, jnp.where(forced, SEL_FORCE, imp), NEG_INF)
    chosen = jnp.zeros((n_blk, Q_BLOCK), jnp.bool_)
    for _ in range(n_sel):
        best = jnp.max(score, axis=0, keepdims=True)
        hit = jj == jnp.min(jnp.where(score == best, jj, n_blk), axis=0, keepdims=True)
        chosen = chosen | (hit & (best > 0.5 * NEG_INF))
        score = jnp.where(hit, REMOVED, score)
    neg_ref[...] = _lanes4(jnp.where(chosen, 0.0, -MASK_BIG)).astype(BF16)

    qa_ref[...] = qt
    acc_ref[...] = jnp.zeros_like(acc_ref)
    m_ref[...] = jnp.full_like(m_ref, NEG_INF)

    def scores_head(jt, buf, r):
        ls = slice(r * Q_BLOCK, (r + 1) * Q_BLOCK)
        k0 = pl.multiple_of(jt * SLC_TK, SLC_TK)
        if r == 0:
            blk0 = pl.multiple_of((k0 // (SLC_LEN * MASK_ROWS)) * MASK_ROWS, MASK_ROWS)
            qa_ref[MASK_LANE:MASK_LANE + MASK_ROWS, :] = neg_ref[pl.ds(blk0, MASK_ROWS), :]
        st = jnp.dot(ks_ref[pl.ds(k0, SLC_TK), :], qa_ref[:, ls], preferred_element_type=F32)
        s_ref[buf, :, ls] = st
        cmax_ref[buf, :, ls] = jnp.max(st, axis=0, keepdims=True)

    def softmax_head(jt, buf, r, bias=None):
        ls = slice(r * Q_BLOCK, (r + 1) * Q_BLOCK)
        k0 = pl.multiple_of(jt * SLC_TK, SLC_TK)
        m_old = m_ref[:, ls]
        if bias is None:
            st = s_ref[buf, :, ls]
            m_new = jnp.maximum(m_old, cmax_ref[buf, :, ls])
        else:
            st = s_ref[buf, :, ls] + bias
            m_new = jnp.maximum(m_old, jnp.max(st, axis=0, keepdims=True))
        pt = jnp.exp2(st - m_new).astype(BF16)
        acc_ref[:, ls] = (jnp.exp2(m_old - m_new) * acc_ref[:, ls]
                          + jnp.dot(vst_ref[0:V_ROWS, pl.ds(k0, SLC_TK)], pt, preferred_element_type=F32))
        m_ref[:, ls] = m_new

    n_full = s // SLC_TK
    for r in range(R):
        scores_head(0, 0, r)

    def pair(i, carry):
        for nxt, cur in ((1, 0), (0, 1)):
            for r in range(R):
                scores_head(2 * i + cur + 1, nxt, r)
                softmax_head(2 * i + cur, cur, r)
        return carry

    lax.fori_loop(0, n_full // 2, pair, 0)

    def last_tile(buf):
        kpos = n_full * SLC_TK + lax.broadcasted_iota(jnp.int32, (SLC_TK, 1), 0)
        bias = jnp.where(kpos <= tok, 0.0, NEG_INF)
        for r in range(R):
            softmax_head(n_full, buf, r, bias)

    @pl.when(n_full % 2 == 1)
    def _():
        for r in range(R):
            scores_head(n_full, 1, r)
            softmax_head(n_full - 1, 0, r)
        last_tile(1)

    @pl.when(n_full % 2 == 0)
    def _():
        last_tile(0)

    acc = acc_ref[...]
    o_slc = acc * (1.0 / acc[ONES_ROW:ONES_ROW + 1, :])

    win_parts = []
    for b in range(Q_BLOCK // WIN_SUB):
        sb = s + b * WIN_SUB
        tok_b = sb + lax.broadcasted_iota(jnp.int32, (1, WIN_SUB), 1)
        qt_b = jnp.concatenate([qt[:, r * Q_BLOCK + b * WIN_SUB:r * Q_BLOCK + (b + 1) * WIN_SUB]
                                for r in range(R)], axis=1)
        w0 = pl.multiple_of(jnp.maximum(sb - WINDOW, 0), WIN_SUB)
        wpos = w0 + lax.broadcasted_iota(jnp.int32, (WIN_KEYS, 1), 0)
        sw = jnp.dot(kw_ref[pl.ds(w0, WIN_KEYS), :], qt_b, preferred_element_type=F32)
        sw = sw + _lanes4(jnp.where((wpos <= tok_b) & (wpos > tok_b - WINDOW), 0.0, NEG_INF))
        pw = jnp.exp2(sw - jnp.max(sw, axis=0, keepdims=True)).astype(BF16)
        ow = jnp.dot(vwt_ref[0:V_ROWS, pl.ds(w0, WIN_KEYS)], pw, preferred_element_type=F32)
        win_parts.append(ow * (1.0 / ow[ONES_ROW:ONES_ROW + 1, :]))

    for r in range(R):
        ls = slice(r * Q_BLOCK, (r + 1) * Q_BLOCK)
        o_win = jnp.concatenate([part[:, r * WIN_SUB:(r + 1) * WIN_SUB] for part in win_parts], axis=1)
        g0 = 3 * (pl.program_id(0) * R + r)
        o = (gate_ref[pl.ds(g0, 1), :] * o_cmp[:, ls] + gate_ref[pl.ds(g0 + 1, 1), :] * o_slc[:, ls]
             + gate_ref[pl.ds(g0 + 2, 1), :] * o_win)
        o = jnp.concatenate([o, jnp.zeros((HEAD_PAD - V_ROWS, Q_BLOCK), F32)], axis=0)
        o_ref[:, r * HEAD_PAD:(r + 1) * HEAD_PAD] = o.T.astype(BF16)


def _attention(qt, k_cmp, v_cmp_t, ks_r, vs_t, kw_r, vw_t, gate_t, c2st):
    T = ks_r.shape[0]
    n_cmp = k_cmp.shape[1]
    n_blk = c2st.shape[0]
    n_sel = min(N_SELECT, T // SLC_LEN)
    gw = R * HEAD_PAD
    rows = lambda g, ci: (0, g)
    cols = lambda g, ci: (g, 0)
    return pl.pallas_call(
        functools.partial(_attn_kernel, n_sel=n_sel),
        grid=(N_KV_GROUPS, T // Q_BLOCK),
        in_specs=[pl.BlockSpec((gw, Q_BLOCK), lambda g, ci: (g, ci)),
                  pl.BlockSpec((1, n_cmp, HEAD_PAD), lambda g, ci: (g, 0, 0)),
                  pl.BlockSpec((1, HEAD_PAD, n_cmp), lambda g, ci: (g, 0, 0)),
                  pl.BlockSpec((T, HEAD_PAD), rows), pl.BlockSpec((HEAD_PAD, T), cols),
                  pl.BlockSpec((T, HEAD_PAD), rows), pl.BlockSpec((HEAD_PAD, T), cols),
                  pl.BlockSpec((HEAD_PAD, Q_BLOCK), lambda g, ci: (0, ci)),
                  _const_spec((n_blk, n_cmp))],
        out_specs=pl.BlockSpec((Q_BLOCK, gw), lambda g, ci: (ci, g)),
        out_shape=jax.ShapeDtypeStruct((T, N_HEADS * HEAD_PAD), BF16),
        scratch_shapes=[pltpu.VMEM((HEAD_PAD, R * Q_BLOCK), BF16), pltpu.VMEM((n_blk, R * Q_BLOCK), BF16),
                        pltpu.VMEM((V_ROWS, R * Q_BLOCK), F32), pltpu.VMEM((1, R * Q_BLOCK), F32),
                        pltpu.VMEM((2, SLC_TK, R * Q_BLOCK), F32), pltpu.VMEM((2, 1, R * Q_BLOCK), F32)],
        compiler_params=_params("arbitrary", "arbitrary"),
        name="nsa_attention",
    )(qt, k_cmp, v_cmp_t, ks_r, vs_t, kw_r, vw_t, gate_t, c2st)


MIX_TM = 256


def _mix_kernel(u_ref, up_ref, gm_ref, o_ref, pw_ref, ps_ref, wpu_ref, wnu_ref, y_ref):
    i = pl.program_id(0)
    tm = u_ref.shape[0]
    u = u_ref[...]
    prev = up_ref[...] * (i > 0).astype(F32)
    ext = jnp.concatenate([prev, u], axis=0)
    t = i * tm + lax.broadcasted_iota(jnp.int32, (tm, 1), 0)
    parts = []
    for g, win in enumerate(POOL_WINDOWS):
        cs = slice(g * POOL_GROUP, (g + 1) * POOL_GROUP)
        run = ext[:, cs]
        k = 1
        while k < win:
            run = run[k:] + run[:-k]
            k *= 2
        lo = POOL_HALO - (win - 1)
        total = run[lo:lo + tm]
        count = jnp.minimum(t + 1, win).astype(F32)
        d = total / count - u[:, cs]
        parts.append(jnp.dot(d.astype(BF16), pw_ref[g], preferred_element_type=F32) * ps_ref[:, cs])
    a = jnp.concatenate(parts, axis=1).astype(BF16)
    ya = jnp.dot(a, wpu_ref[...], preferred_element_type=F32)
    yb = jnp.dot(o_ref[...], wnu_ref[...], preferred_element_type=F32)
    gm = jax.nn.sigmoid(gm_ref[...])
    y_ref[...] = (gm[:, :D_MODEL] * ya + gm[:, D_MODEL:] * yb).astype(BF16)


def _mix(proj, o, pool_w_b, pool_scale, w_pool_up_b, w_nsa_up_p):
    T = proj.shape[0]
    tm = min(MIX_TM, T)
    halo_blocks = tm // POOL_HALO
    return pl.pallas_call(
        _mix_kernel,
        grid=(T // tm,),
        in_specs=[pl.BlockSpec((tm, POOL_WIDTH), lambda i: (i, C_U // POOL_WIDTH)),
                  pl.BlockSpec((POOL_HALO, POOL_WIDTH),
                               lambda i: (jnp.maximum(i * halo_blocks - 1, 0), C_U // POOL_WIDTH)),
                  pl.BlockSpec((tm, 2 * D_MODEL), lambda i: (i, 0)),
                  pl.BlockSpec((tm, N_HEADS * HEAD_PAD), lambda i: (i, 0)),
                  _const_spec(pool_w_b.shape), _const_spec((1, POOL_WIDTH)),
                  _const_spec(w_pool_up_b.shape), _const_spec(w_nsa_up_p.shape)],
        out_specs=pl.BlockSpec((tm, D_MODEL), lambda i: (i, 0)),
        out_shape=jax.ShapeDtypeStruct((T, D_MODEL), BF16),
        compiler_params=_params("arbitrary"),
        name="pool_merge",
    )(proj, proj, proj, o, pool_w_b, pool_scale, w_pool_up_b, w_nsa_up_p)


OUT_TM = 256


def _rms(y, g):
    return y * lax.rsqrt(jnp.mean(y * y, axis=-1, keepdims=True) + RMS_EPS) * g


def _outproj_kernel(y_ref, w_ref, x_ref, g_ref, gt_ref, o_ref):
    z = jnp.dot(y_ref[...], w_ref[...], preferred_element_type=F32)
    o_ref[...] = x_ref[...] + gt_ref[...] * _rms(z, g_ref[...])


def _outproj(y, w_out_b, x2, g_post, mod):
    T = y.shape[0]
    tm = min(OUT_TM, T)
    row = pl.BlockSpec((tm, D_MODEL), lambda i: (i, 0))
    return pl.pallas_call(
        _outproj_kernel,
        grid=(T // tm,),
        in_specs=[row, _const_spec((D_MODEL, D_MODEL)), row, _const_spec((1, D_MODEL)), _mod_spec(2)],
        out_specs=row,
        out_shape=jax.ShapeDtypeStruct((T, D_MODEL), F32),
        compiler_params=_params("arbitrary"),
        name="out_proj",
    )(y, w_out_b, x2, g_post, mod)


def _pack_table():
    sizes = (POOL_WIDTH, N_HEADS * D_QK, N_KV_GROUPS * D_QK, N_KV_GROUPS * D_V, N_KV_GROUPS * D_QK,
             N_KV_GROUPS * D_V, N_KV_GROUPS * D_QK, N_KV_GROUPS * D_V, 3 * N_HEADS, 2 * D_MODEL)
    u, q, kc, vc, ks, vs, kw, vw, gn, gm = [int(o) for o in np.cumsum((0,) + sizes)[:-1]]
    segments = [(gm, HEAD_PAD, 2 * D_MODEL // HEAD_PAD), (q, D_QK, N_HEADS), (u, HEAD_PAD, POOL_WIDTH // HEAD_PAD),
                (kc, D_QK, N_KV_GROUPS), (ks, D_QK, N_KV_GROUPS), (kw, D_QK, N_KV_GROUPS),
                (vc, D_V, N_KV_GROUPS), (vs, D_V, N_KV_GROUPS), (vw, D_V, N_KV_GROUPS),
                (gn, 3 * N_HEADS, 1)]
    tiles = [(start + i * width, width) for start, width, count in segments for i in range(count)]
    assert len(tiles) * HEAD_PAD == N_COLS
    return np.asarray(tiles, np.int32)


def _repack_kernel(blk_ref, shift_ref, width_ref, a_ref, b_ref, o_ref):
    j = pl.program_id(0)
    shift = shift_ref[j]
    lane = lax.broadcasted_iota(jnp.int32, (1, HEAD_PAD), 1)
    back = (HEAD_PAD - shift) & (HEAD_PAD - 1)
    merged = jnp.where(lane < HEAD_PAD - shift, pltpu.roll(a_ref[...], back, 1), pltpu.roll(b_ref[...], back, 1))
    o_ref[...] = jnp.where(lane < width_ref[j], merged, 0.0).astype(BF16)


def _pack_w_in(w_in):
    k, n_src = w_in.shape
    table = _pack_table()
    last = (n_src - 1) // HEAD_PAD
    grid_spec = pltpu.PrefetchScalarGridSpec(
        num_scalar_prefetch=3,
        grid=(table.shape[0],),
        in_specs=[pl.BlockSpec((k, HEAD_PAD), lambda j, blk, *_: (0, blk[j])),
                  pl.BlockSpec((k, HEAD_PAD), lambda j, blk, *_: (0, jnp.minimum(blk[j] + 1, last)))],
        out_specs=pl.BlockSpec((k, HEAD_PAD), lambda j, *_: (0, j)),
    )
    return pl.pallas_call(
        _repack_kernel,
        grid_spec=grid_spec,
        out_shape=jax.ShapeDtypeStruct((k, N_COLS), BF16),
        compiler_params=_params("arbitrary"),
        name="repack_w_in",
    )(jnp.asarray(table[:, 0] // HEAD_PAD), jnp.asarray(table[:, 0] % HEAD_PAD), jnp.asarray(table[:, 1]),
      w_in, w_in)


def _slc_from_cmp(n_blk, n_chunk):
    cs = np.arange(n_chunk)[None, :] * CMP_STRIDE
    ss = np.arange(n_blk)[:, None] * SLC_LEN
    ov = np.clip(np.minimum(cs + CMP_LEN, ss + SLC_LEN) - np.maximum(cs, ss), 0, None)
    return jnp.asarray(ov / CMP_LEN, dtype=BF16)


def _token_mixer_block(x2, pos_col, mod, g_pre, g_post, w_in, cmp_pos_k, cmp_k_w1, cmp_k_w2,
                       cmp_pos_v, cmp_v_w1, cmp_v_w2, pool_w, pool_scale, w_pool_up, w_nsa_up, w_out):
    T = x2.shape[0]
    half = jnp.arange(ROT_HALF, dtype=F32)
    inv = jnp.power(jnp.float32(ROPE_THETA), -half * 2.0 / (2 * ROT_HALF))
    invf = jnp.concatenate([inv, inv, jnp.zeros((HEAD_PAD - 2 * ROT_HALF,), F32)])[None, :]

    proj = _proj(x2, g_pre, mod, _pack_w_in(w_in))
    qt, ks_r, kw_r, vs_t, vw_t, gate_t = _prep(proj, pos_col, invf)

    n_chunk = T // CMP_STRIDE
    pos_cmp = jnp.pad(pos_col[CMP_LEN - 1::CMP_STRIDE], ((0, 1), (0, 0)))

    def pad_lanes(a, d):
        return jnp.pad(a, [(0, 0)] * (a.ndim - 1) + [(0, HEAD_PAD - d)])

    def w1_pack(w1, d):
        w1 = w1.reshape(CMP_LEN, d, w1.shape[1])
        return jnp.pad(w1, ((0, 0), (0, HEAD_PAD - d), (0, 0))).astype(BF16)

    k_cmp = _compress(proj, C_KC, pad_lanes(cmp_pos_k, D_QK), w1_pack(cmp_k_w1, D_QK),
                      pad_lanes(cmp_k_w2, D_QK).astype(BF16), pos_cmp, invf, True)
    v_cmp = _compress(proj, C_VC, pad_lanes(cmp_pos_v, D_V), w1_pack(cmp_v_w1, D_V),
                      pad_lanes(cmp_v_w2, D_V).T.astype(BF16), pos_cmp, invf, False)

    n_blk = max(T // SLC_LEN, MASK_ROWS)
    o = _attention(qt, k_cmp, v_cmp, ks_r, vs_t, kw_r, vw_t, gate_t, _slc_from_cmp(n_blk, n_chunk))

    w_nsa_up_p = jnp.pad(w_nsa_up.reshape(N_HEADS, D_V, D_MODEL),
                         ((0, 0), (0, HEAD_PAD - D_V), (0, 0))).reshape(N_HEADS * HEAD_PAD, D_MODEL)
    y = _mix(proj, o, pool_w.astype(BF16), pool_scale, w_pool_up.astype(BF16), w_nsa_up_p.astype(BF16))
    return _outproj(y, w_out.astype(BF16), x2, g_post, mod)


ROUTE_TM = 256
EXPERTS_PER_GROUP = N_EXPERTS // N_EXPERT_GROUPS


def _split3(a):
    hi = a.astype(BF16)
    r1 = a - hi.astype(F32)
    mid = r1.astype(BF16)
    lo = (r1 - mid.astype(F32)).astype(BF16)
    return hi, mid, lo


def _dot_f32(a, b):
    a3, b3 = _split3(a), _split3(b)
    out = None
    for i, j in ((2, 0), (1, 1), (0, 2), (1, 0), (0, 1), (0, 0)):
        term = jnp.dot(a3[i], b3[j], preferred_element_type=F32)
        out = term if out is None else out + term
    return out


def _first_index_of(mask, lane, n):
    return jnp.min(jnp.where(mask, lane, float(n)), axis=-1, keepdims=True)


def _route_kernel(x_ref, g_ref, sc_ref, sh_ref, wr_ref, rb_ref,
                  hf_ref, hb_ref, sel_ref, wd_ref, rank_ref, cnt_ref, carry_ref):
    i = pl.program_id(0)
    tm = x_ref.shape[0]

    @pl.when(i == 0)
    def _():
        carry_ref[...] = jnp.zeros_like(carry_ref)

    h = _modulated_norm(x_ref[...], g_ref[...], sc_ref[...], sh_ref[...])
    hf_ref[...] = h
    hb_ref[...] = h.astype(BF16)
    logits = _dot_f32(h, wr_ref[...])
    aff = jax.nn.sigmoid(logits)
    biased = aff + rb_ref[...]
    grp = lax.broadcasted_iota(jnp.int32, (1, N_EXPERTS), 1) // EXPERTS_PER_GROUP
    lane = lax.broadcasted_iota(jnp.int32, (1, N_EXPERTS), 1).astype(F32)

    gs = []
    for gi in range(N_EXPERT_GROUPS):
        v = jnp.where(grp == gi, biased, REMOVED)
        m1 = jnp.max(v, axis=-1, keepdims=True)
        i1 = _first_index_of(v == m1, lane, N_EXPERTS)
        m2 = jnp.max(jnp.where(lane == i1, REMOVED, v), axis=-1, keepdims=True)
        gs.append(m1 + m2)
    keep = jnp.zeros((tm, N_EXPERTS), jnp.bool_)
    for gi in range(N_EXPERT_GROUPS):
        ahead = jnp.zeros((tm, 1), jnp.int32)
        for gj in range(N_EXPERT_GROUPS):
            if gj < gi:
                ahead += (gs[gj] >= gs[gi]).astype(jnp.int32)
            elif gj > gi:
                ahead += (gs[gj] > gs[gi]).astype(jnp.int32)
        keep = keep | ((grp == gi) & (ahead < TOPK_GROUPS))
    masked = jnp.where(keep, biased, NEG_INF)
    sel = jnp.zeros((tm, N_EXPERTS), F32)
    for _ in range(TOP_K):
        best = jnp.max(masked, axis=-1, keepdims=True)
        hit = lane == _first_index_of(masked == best, lane, N_EXPERTS)
        sel = jnp.where(hit, 1.0, sel)
        masked = jnp.where(hit, REMOVED, masked)
    w = aff * sel
    wd_ref[...] = w / jnp.sum(w, axis=-1, keepdims=True) * ROUTED_SCALE
    sel_ref[...] = sel

    r_i = lax.broadcasted_iota(jnp.int32, (tm, tm), 0)
    c_i = lax.broadcasted_iota(jnp.int32, (tm, tm), 1)
    before = jnp.where(c_i < r_i, 1.0, 0.0).astype(BF16)
    carry = carry_ref[...]
    rank_ref[...] = jnp.dot(before, sel.astype(BF16), preferred_element_type=F32) + carry
    carry = carry + jnp.sum(sel, axis=0, keepdims=True)
    carry_ref[...] = carry
    cnt_ref[...] = carry


def _route(x1, g_pre, mod, w_router, router_bias):
    T = x1.shape[0]
    tm = min(ROUTE_TM, T)
    row = pl.BlockSpec((tm, D_MODEL), lambda i: (i, 0))
    erow = pl.BlockSpec((tm, N_EXPERTS), lambda i: (i, 0))
    e_shape = jax.ShapeDtypeStruct((T, N_EXPERTS), F32)
    return pl.pallas_call(
        _route_kernel,
        grid=(T // tm,),
        in_specs=[row, _const_spec((1, D_MODEL)), _mod_spec(4), _mod_spec(3),
                  _const_spec((D_MODEL, N_EXPERTS)), _const_spec((1, N_EXPERTS))],
        out_specs=[row, row, erow, erow, erow, _const_spec((1, N_EXPERTS))],
        out_shape=[jax.ShapeDtypeStruct((T, D_MODEL), F32), jax.ShapeDtypeStruct((T, D_MODEL), BF16),
                   e_shape, e_shape, e_shape, jax.ShapeDtypeStruct((1, N_EXPERTS), F32)],
        scratch_shapes=[pltpu.VMEM((1, N_EXPERTS), F32)],
        compiler_params=_params("arbitrary"),
        name="moe_route",
    )(x1, g_pre, mod, mod, w_router, router_bias)


def _dest_kernel(sel_ref, wd_ref, rank_ref, ps_ref, dest_ref, w8_ref):
    tm = sel_ref.shape[0]
    slot = ps_ref[...] + rank_ref[...]
    wd = wd_ref[...]
    lane = lax.broadcasted_iota(jnp.int32, (1, N_EXPERTS), 1).astype(F32)
    lane_k = lax.broadcasted_iota(jnp.int32, (1, TOP_K), 1)
    lane_p = lax.broadcasted_iota(jnp.int32, (1, HEAD_PAD), 1)
    dest = jnp.zeros((tm, HEAD_PAD), F32)
    w8 = jnp.zeros((tm, TOP_K), F32)
    rem = sel_ref[...]
    for k in range(TOP_K):
        hit = lane == _first_index_of(rem > 0.5, lane, N_EXPERTS)
        dest = jnp.where(lane_p == k, jnp.sum(jnp.where(hit, slot, 0.0), axis=-1, keepdims=True), dest)
        w8 = jnp.where(lane_k == k, jnp.sum(jnp.where(hit, wd, 0.0), axis=-1, keepdims=True), w8)
        rem = jnp.where(hit, 0.0, rem)
    dest_ref[...] = dest.T[0:TOP_K, :].astype(jnp.int32)
    w8_ref[...] = w8


def _dest(sel, wd, rank, pstart):
    T = sel.shape[0]
    tm = min(ROUTE_TM, T)
    erow = pl.BlockSpec((tm, N_EXPERTS), lambda i: (i, 0))
    krow = pl.BlockSpec((tm, TOP_K), lambda i: (i, 0))
    return pl.pallas_call(
        _dest_kernel,
        grid=(T // tm,),
        in_specs=[erow, erow, erow, _const_spec((1, N_EXPERTS))],
        out_specs=[pl.BlockSpec((TOP_K, tm), lambda i: (0, i)), krow],
        out_shape=[jax.ShapeDtypeStruct((TOP_K, T), jnp.int32), jax.ShapeDtypeStruct((T, TOP_K), F32)],
        compiler_params=_params("arbitrary"),
        name="moe_dest",
    )(sel, wd, rank, pstart)


DISPATCH_TM = 128


def _dispatch_kernel(dest_ref, h_ref, xs_hbm, sem):
    tm = h_ref.shape[0]

    def body(t, carry):
        for k in range(TOP_K):
            pltpu.make_async_copy(h_ref.at[pl.ds(t, 1), :], xs_hbm.at[pl.ds(dest_ref[k, t], 1), :], sem).start()
        return carry

    lax.fori_loop(0, tm, body, 0)
    for k in range(TOP_K):
        pltpu.make_async_copy(h_ref, xs_hbm.at[pl.ds(0, tm), :], sem).wait()


def _dispatch(dest_t, hf, n_slots):
    T = hf.shape[0]
    tm = min(DISPATCH_TM, T)
    return pl.pallas_call(
        _dispatch_kernel,
        grid=(T // tm,),
        in_specs=[pl.BlockSpec((TOP_K, tm), lambda i: (0, i), memory_space=pltpu.SMEM),
                  pl.BlockSpec((tm, D_MODEL), lambda i: (i, 0))],
        out_specs=pl.BlockSpec(memory_space=pl.ANY),
        out_shape=jax.ShapeDtypeStruct((n_slots, D_MODEL), F32),
        scratch_shapes=[pltpu.SemaphoreType.DMA(())],
        compiler_params=pltpu.CompilerParams(dimension_semantics=("arbitrary",), vmem_limit_bytes=VMEM_LIMIT,
                                             has_side_effects=True),
        name="moe_dispatch",
    )(dest_t, hf)


EXPERT_BLOCK = 256


def _swiglu(x, wg, wu, wd):
    g = jnp.dot(x, wg, preferred_element_type=F32)
    u = jnp.dot(x, wu, preferred_element_type=F32)
    hid = (g * jax.nn.sigmoid(g)) * u
    return jnp.dot(hid.astype(BF16), wd, preferred_element_type=F32)


def _expert_items(counts, n_items):
    ends = jnp.cumsum(counts)
    starts = ends - counts
    first_blk = starts // EXPERT_BLOCK
    last_blk = (ends - 1) // EXPERT_BLOCK
    per_expert = jnp.where(counts > 0, last_blk - first_blk + 1, 0)
    item_end = jnp.cumsum(per_expert)
    n_used = item_end[-1]
    i = jnp.minimum(jnp.arange(n_items, dtype=jnp.int32), jnp.maximum(n_used - 1, 0))
    e = jnp.sum((item_end[None, :] <= i[:, None]).astype(jnp.int32), axis=1)
    e = jnp.minimum(e, N_EXPERTS - 1)
    ids = jnp.arange(N_EXPERTS, dtype=jnp.int32)
    owner = e[:, None] == ids[None, :]
    of_item = lambda per_expert_value: jnp.sum(jnp.where(owner, per_expert_value[None, :], 0), axis=1)
    blk = of_item(first_blk) + i - of_item(item_end - per_expert)
    lo = jnp.clip(of_item(starts) - blk * EXPERT_BLOCK, 0, EXPERT_BLOCK)
    hi = jnp.clip(of_item(ends) - blk * EXPERT_BLOCK, 0, EXPERT_BLOCK)
    live = jnp.arange(n_items, dtype=jnp.int32) < n_used
    hi = jnp.where(live, hi, lo)
    new_expert = jnp.concatenate([jnp.ones((1,), jnp.bool_), e[1:] != e[:-1]])
    new_block = jnp.concatenate([jnp.ones((1,), jnp.bool_), blk[1:] != blk[:-1]])
    as_i32 = lambda a: a.astype(jnp.int32)
    used = counts > 0
    slot_of = (jnp.cumsum(as_i32(used)) - 1) % 2
    later = (ids[None, :] > ids[:, None]) & used[None, :]
    next_of = jnp.min(jnp.where(later, ids[None, :], N_EXPERTS), axis=1)
    next_of = jnp.where(next_of < N_EXPERTS, next_of, -1)
    return (as_i32(blk), as_i32(e), as_i32(lo), as_i32(hi), as_i32(new_expert & live),
            as_i32(new_block & live), as_i32(live), as_i32(of_item(slot_of)), as_i32(of_item(next_of)), starts)


def _expert_kernel(blk_ref, e_ref, lo_ref, hi_ref, newe_ref, newb_ref, live_ref, slot_ref, nexte_ref,
                   x_ref, wg_hbm, wu_hbm, wd_hbm, y_ref, wg32, wu32, wd32, wgb, wub, wdb, sem):
    i = pl.program_id(0)

    def weight_copies(e, slot):
        return (pltpu.make_async_copy(wg_hbm.at[e], wg32.at[slot], sem.at[slot]),
                pltpu.make_async_copy(wu_hbm.at[e], wu32.at[slot], sem.at[slot]),
                pltpu.make_async_copy(wd_hbm.at[e], wd32.at[slot], sem.at[slot]))

    @pl.when(live_ref[i] == 1)
    def _():
        @pl.when(newe_ref[i] == 1)
        def _():
            slot = slot_ref[i]

            @pl.when(i == 0)
            def _():
                for c in weight_copies(e_ref[i], slot):
                    c.start()

            for c in weight_copies(e_ref[i], slot):
                c.wait()
            wgb[...] = wg32[slot].astype(BF16)
            wub[...] = wu32[slot].astype(BF16)
            wdb[...] = wd32[slot].astype(BF16)

            @pl.when(nexte_ref[i] >= 0)
            def _():
                for c in weight_copies(nexte_ref[i], 1 - slot):
                    c.start()

        row = lax.broadcasted_iota(jnp.int32, (EXPERT_BLOCK, 1), 0)
        mine = (row >= lo_ref[i]) & (row < hi_ref[i])
        y = _swiglu(jnp.where(mine, x_ref[...], 0.0).astype(BF16), wgb[...], wub[...], wdb[...])

        @pl.when(newb_ref[i] == 1)
        def _():
            y_ref[...] = y

        @pl.when(newb_ref[i] == 0)
        def _():
            y_ref[...] += y


def _experts(items, xs, w_gate, w_up, w_down):
    n_items = items[0].shape[0]
    by_block = lambda i, blk, *_: (blk[i], 0)
    hbm = pl.BlockSpec(memory_space=pl.ANY)
    grid_spec = pltpu.PrefetchScalarGridSpec(
        num_scalar_prefetch=len(items),
        grid=(n_items,),
        in_specs=[pl.BlockSpec((EXPERT_BLOCK, D_MODEL), by_block), hbm, hbm, hbm],
        out_specs=pl.BlockSpec((EXPERT_BLOCK, D_MODEL), by_block),
        scratch_shapes=[pltpu.VMEM((2, D_MODEL, D_EXPERT), F32), pltpu.VMEM((2, D_MODEL, D_EXPERT), F32),
                        pltpu.VMEM((2, D_EXPERT, D_MODEL), F32),
                        pltpu.VMEM((D_MODEL, D_EXPERT), BF16), pltpu.VMEM((D_MODEL, D_EXPERT), BF16),
                        pltpu.VMEM((D_EXPERT, D_MODEL), BF16), pltpu.SemaphoreType.DMA((2,))],
    )
    return pl.pallas_call(
        _expert_kernel,
        grid_spec=grid_spec,
        out_shape=jax.ShapeDtypeStruct(xs.shape, F32),
        compiler_params=_params("arbitrary"),
        name="moe_experts",
    )(*items, xs, w_gate, w_up, w_down)


COMBINE_TM = 128


def _combine_kernel(dcur_ref, dnext_ref, w8_ref, hb_ref, x_ref, wsg_ref, wsu_ref, wsd_ref, g_ref, gt_ref,
                    y_hbm, o_ref, gbuf, sem):
    i = pl.program_id(0)
    n = pl.num_programs(0)
    tm = x_ref.shape[0]

    rows = TOP_K * tm

    def issue(d_ref, slot):
        base = slot * rows

        def body(t, carry):
            for k in range(TOP_K):
                pltpu.make_async_copy(y_hbm.at[pl.ds(d_ref[k, t], 1), :],
                                      gbuf.at[pl.ds(base + k * tm + t, 1), :], sem.at[slot]).start()
            return carry

        lax.fori_loop(0, tm, body, 0)

    @pl.when(i == 0)
    def _():
        issue(dcur_ref, 0)

    @pl.when(i + 1 < n)
    def _():
        issue(dnext_ref, (i + 1) % 2)

    slot = i % 2
    base = pl.multiple_of(slot * rows, rows)
    shared = _swiglu(hb_ref[...], wsg_ref[...], wsu_ref[...], wsd_ref[...])
    for k in range(TOP_K):
        pltpu.make_async_copy(y_hbm.at[pl.ds(0, tm), :], gbuf.at[pl.ds(base + k * tm, tm), :],
                              sem.at[slot]).wait()
    w8 = w8_ref[...]
    y = shared
    for k in range(TOP_K):
        y = y + w8[:, k:k + 1] * gbuf[pl.ds(base + k * tm, tm), :]
    o_ref[...] = x_ref[...] + gt_ref[...] * _rms(y, g_ref[...])


def _combine(dest_t, w8, hb, x1, wsg, wsu, wsd, g_post, mod, y_buf):
    T = x1.shape[0]
    tm = min(COMBINE_TM, T)
    n = T // tm
    row = pl.BlockSpec((tm, D_MODEL), lambda i: (i, 0))
    return pl.pallas_call(
        _combine_kernel,
        grid=(n,),
        in_specs=[pl.BlockSpec((TOP_K, tm), lambda i: (0, i), memory_space=pltpu.SMEM),
                  pl.BlockSpec((TOP_K, tm), lambda i: (0, jnp.minimum(i + 1, n - 1)), memory_space=pltpu.SMEM),
                  pl.BlockSpec((tm, TOP_K), lambda i: (i, 0)), row, row,
                  _const_spec(wsg.shape), _const_spec(wsu.shape), _const_spec(wsd.shape),
                  _const_spec((1, D_MODEL)), _mod_spec(5),
                  pl.BlockSpec(memory_space=pl.ANY)],
        out_specs=row,
        out_shape=jax.ShapeDtypeStruct((T, D_MODEL), F32),
        scratch_shapes=[pltpu.VMEM((2 * TOP_K * tm, D_MODEL), F32), pltpu.SemaphoreType.DMA((2,))],
        compiler_params=_params("arbitrary"),
        name="moe_combine",
    )(dest_t, dest_t, w8, hb, x1, wsg, wsu, wsd, g_post, mod, y_buf)


def _moe_block(x1, mod, g_pre, g_post, w_router, router_bias, w_exp_gate, w_exp_up, w_exp_down,
               w_sh_gate, w_sh_up, w_sh_down):
    T = x1.shape[0]
    hf, hb, sel, wd, rank, counts = _route(x1, g_pre, mod, w_router, router_bias)

    n_slots = T * TOP_K
    *items, starts = _expert_items(counts[0].astype(jnp.int32), n_slots // EXPERT_BLOCK + N_EXPERTS)

    dest_t, w8 = _dest(sel, wd, rank, starts.astype(F32)[None, :])
    xs = _dispatch(dest_t, hf, n_slots)
    y_buf = _experts(items, xs, w_exp_gate, w_exp_up, w_exp_down)
    return _combine(dest_t, w8, hb, x1, w_sh_gate.astype(BF16), w_sh_up.astype(BF16),
                    w_sh_down.astype(BF16), g_post, mod, y_buf)


def kernel(x, c, positions, w_ada, b_ada, g_pre_mix, g_post_mix, g_pre_ffn, g_post_ffn, w_in, cmp_pos_k, cmp_k_w1, cmp_k_w2, cmp_pos_v, cmp_v_w1, cmp_v_w2, pool_w, pool_scale, w_pool_up, w_nsa_up, w_out, w_router, router_bias, w_exp_gate, w_exp_up, w_exp_down, w_sh_gate, w_sh_up, w_sh_down):
    B, T, D = x.shape
    assert B == 1 and D == D_MODEL
    x2 = x.reshape(T, D)
    pos_col = positions.reshape(T, 1)
    for l in range(w_ada.shape[0]):
        mod = _ada(c.reshape(D, 1), w_ada[l], b_ada[l][None, :])
        x2 = _token_mixer_block(x2, pos_col, mod, g_pre_mix[l][None, :], g_post_mix[l][None, :], w_in[l],
                                cmp_pos_k[l], cmp_k_w1[l], cmp_k_w2[l], cmp_pos_v[l], cmp_v_w1[l],
                                cmp_v_w2[l], pool_w[l], pool_scale[l][None, :], w_pool_up[l], w_nsa_up[l],
                                w_out[l])
        x2 = _moe_block(x2, mod, g_pre_ffn[l][None, :], g_post_ffn[l][None, :], w_router[l],
                        router_bias[l][None, :], w_exp_gate[l], w_exp_up[l], w_exp_down[l],
                        w_sh_gate[l], w_sh_up[l], w_sh_down[l])
    return x2.reshape(B, T, D)
```

```python
import functools

import numpy as np
import jax
import jax.numpy as jnp
from jax import lax
from jax.experimental import pallas as pl
from jax.experimental.pallas import tpu as pltpu

F32 = jnp.float32
BF16 = jnp.bfloat16

D_MODEL = 2048
POOL_WIDTH = 1024
POOL_WINDOWS = (2, 4, 8, 16)
POOL_GROUP = 256
POOL_HALO = 16
N_HEADS = 16
N_KV_GROUPS = 4
HEADS_PER_GROUP = 4
D_QK = 96
D_V = 64
HEAD_PAD = 128
ROT_HALF = 12
ROPE_THETA = 500000.0
CMP_LEN = 32
CMP_STRIDE = 16
SLC_LEN = 64
N_SELECT = 16
WINDOW = 512
Q_BLOCK = 512
N_EXPERTS = 64
TOP_K = 8
N_EXPERT_GROUPS = 8
TOPK_GROUPS = 4
D_EXPERT = 512
ROUTED_SCALE = 2.5
RMS_EPS = 1e-6
NEG_INF = -1e30
SEL_FORCE = 1e30
REMOVED = -3e38

VMEM_LIMIT = 56 * 1024 * 1024

C_GM = 0
C_Q = 4096
C_U = 6144
C_KC = 7168
C_KS = 7680
C_KW = 8192
C_VC = 8704
C_VS = 9216
C_VW = 9728
C_GN = 10240
N_COLS = 10752


def _params(*sem):
    return pltpu.CompilerParams(dimension_semantics=sem, vmem_limit_bytes=VMEM_LIMIT)


def _const_spec(shape):
    nd = len(shape)
    return pl.BlockSpec(shape, lambda *_: (0,) * nd)


ADA_TN = 1536
ADA_ROWS = 64


def _ada_kernel(c_ref, w_ref, b_ref, o_ref, s_ref):
    c = c_ref[...]
    s_ref[...] = c * jax.nn.sigmoid(c)

    def body(k, acc):
        r = pl.multiple_of(k * ADA_ROWS, ADA_ROWS)
        for j in range(ADA_ROWS // 8):
            acc = acc + w_ref[pl.ds(r + 8 * j, 8), :] * s_ref[pl.ds(r + 8 * j, 8), :]
        return acc

    acc = lax.fori_loop(0, D_MODEL // ADA_ROWS, body, jnp.zeros((8, ADA_TN), F32))
    o_ref[...] = jnp.sum(acc, axis=0, keepdims=True) + b_ref[...]


def _ada(c_col, w_ada, b_ada):
    n = w_ada.shape[1]
    return pl.pallas_call(
        _ada_kernel,
        grid=(n // ADA_TN,),
        in_specs=[_const_spec((D_MODEL, 1)),
                  pl.BlockSpec((D_MODEL, ADA_TN), lambda j: (0, j)),
                  pl.BlockSpec((1, ADA_TN), lambda j: (0, j))],
        out_specs=pl.BlockSpec((1, ADA_TN), lambda j: (0, j)),
        out_shape=jax.ShapeDtypeStruct((1, n), F32),
        scratch_shapes=[pltpu.VMEM((D_MODEL, 1), F32)],
        compiler_params=_params("arbitrary"),
        name="ada",
    )(c_col, w_ada, b_ada)


def _mod_spec(k):
    return pl.BlockSpec((1, D_MODEL), lambda *_: (0, k))


PROJ_TM = 512
PROJ_TN = 1536


def _modulated_norm(x, g, sc, sh):
    y = x * lax.rsqrt(jnp.mean(x * x, axis=-1, keepdims=True) + RMS_EPS)
    return (y * g) * (1.0 + sc) + sh


def _proj_kernel(x_ref, g_ref, sc_ref, sh_ref, w_ref, o_ref, h_ref):
    @pl.when(pl.program_id(1) == 0)
    def _():
        h_ref[...] = _modulated_norm(x_ref[...], g_ref[...], sc_ref[...], sh_ref[...]).astype(BF16)

    o_ref[...] = jnp.dot(h_ref[...], w_ref[...], preferred_element_type=F32)


def _proj(x2, g_pre, mod, w_in_p):
    T = x2.shape[0]
    tm = min(PROJ_TM, T)
    return pl.pallas_call(
        _proj_kernel,
        grid=(T // tm, N_COLS // PROJ_TN),
        in_specs=[pl.BlockSpec((tm, D_MODEL), lambda i, j: (i, 0)),
                  _const_spec((1, D_MODEL)), _mod_spec(1), _mod_spec(0),
                  pl.BlockSpec((D_MODEL, PROJ_TN), lambda i, j: (0, j))],
        out_specs=pl.BlockSpec((tm, PROJ_TN), lambda i, j: (i, j)),
        out_shape=jax.ShapeDtypeStruct((T, N_COLS), F32),
        scratch_shapes=[pltpu.VMEM((tm, D_MODEL), BF16)],
        compiler_params=_params("arbitrary", "arbitrary"),
        name="in_proj",
    )(x2, g_pre, mod, mod, w_in_p)


def _rope_tables(pos_col, invf):
    ang = pos_col.astype(F32) * invf
    cos = jnp.cos(ang)
    sin = jnp.sin(ang)
    lane = lax.broadcasted_iota(jnp.int32, ang.shape, 1)
    sin_lo = jnp.where(lane < ROT_HALF, -sin, 0.0)
    sin_hi = jnp.where((lane >= ROT_HALF) & (lane < 2 * ROT_HALF), sin, 0.0)
    return cos, sin_lo, sin_hi


def _rope_head(x, tabs):
    cos, sin_lo, sin_hi = tabs
    return (x * cos + pltpu.roll(x, HEAD_PAD - ROT_HALF, 1) * sin_lo
            + pltpu.roll(x, ROT_HALF, 1) * sin_hi)


PREP_TM = 256


LOG2E = 1.4426950408889634
Q_SCALE = (D_QK ** -0.5) * LOG2E
MASK_LANE = D_QK
MASK_ROWS = 16
ONES_ROW = D_V
V_ROWS = 80
MASK_BIG = 1e30


def _with_ones_row(vt):
    row = lax.broadcasted_iota(jnp.int32, (HEAD_PAD, 1), 0)
    return jnp.where(row == ONES_ROW, 1.0, vt)


def _prep_kernel(pos_ref, invf_ref, q_ref, ks_ref, kw_ref, vs_ref, vw_ref, gn_ref,
                 qt_ref, kso_ref, kwo_ref, vst_ref, vwt_ref, gt_ref):
    tm = q_ref.shape[0]
    tabs = _rope_tables(pos_ref[...], invf_ref[...])
    for h in range(N_HEADS):
        sl = slice(h * HEAD_PAD, (h + 1) * HEAD_PAD)
        qt_ref[sl, :] = (_rope_head(q_ref[:, sl], tabs) * Q_SCALE).T.astype(BF16)
    row = pl.program_id(0) * tm + lax.broadcasted_iota(jnp.int32, (tm, 1), 0)
    lane = lax.broadcasted_iota(jnp.int32, (1, HEAD_PAD), 1)
    block_tag = jnp.where(lane == MASK_LANE + ((row >> 6) & (MASK_ROWS - 1)), 1.0, 0.0)
    for g in range(N_KV_GROUPS):
        sl = slice(g * HEAD_PAD, (g + 1) * HEAD_PAD)
        kso_ref[:, sl] = (_rope_head(ks_ref[:, sl], tabs) + block_tag).astype(BF16)
        kwo_ref[:, sl] = _rope_head(kw_ref[:, sl], tabs).astype(BF16)
        vst_ref[sl, :] = _with_ones_row(vs_ref[:, sl].T).astype(BF16)
        vwt_ref[sl, :] = _with_ones_row(vw_ref[:, sl].T).astype(BF16)
    gt_ref[...] = jax.nn.sigmoid(gn_ref[...]).T


def _prep(proj, pos_col, invf):
    T = proj.shape[0]
    tm = min(PREP_TM, T)
    kv = N_KV_GROUPS * HEAD_PAD
    qw = N_HEADS * HEAD_PAD

    def col(width, off):
        return pl.BlockSpec((tm, width), lambda i: (i, off // width))

    rows = lambda width: pl.BlockSpec((tm, width), lambda i: (i, 0))
    cols = lambda height: pl.BlockSpec((height, tm), lambda i: (0, i))
    return pl.pallas_call(
        _prep_kernel,
        grid=(T // tm,),
        in_specs=[pl.BlockSpec((tm, 1), lambda i: (i, 0)), _const_spec((1, HEAD_PAD)),
                  col(qw, C_Q), col(kv, C_KS), col(kv, C_KW), col(kv, C_VS), col(kv, C_VW),
                  col(HEAD_PAD, C_GN)],
        out_specs=[cols(qw), rows(kv), rows(kv), cols(kv), cols(kv), cols(HEAD_PAD)],
        out_shape=[jax.ShapeDtypeStruct((qw, T), BF16), jax.ShapeDtypeStruct((T, kv), BF16),
                   jax.ShapeDtypeStruct((T, kv), BF16), jax.ShapeDtypeStruct((kv, T), BF16),
                   jax.ShapeDtypeStruct((kv, T), BF16), jax.ShapeDtypeStruct((HEAD_PAD, T), F32)],
        compiler_params=_params("arbitrary"),
        name="nsa_prep",
    )(pos_col, invf, proj, proj, proj, proj, proj, proj)


def _cmp_kernel(x_ref, pe_ref, w1_ref, w2_ref, pos_ref, invf_ref, o_ref, *, is_key):
    nchunk = pos_ref.shape[0]
    hid_dim = w1_ref.shape[2]
    first = jnp.zeros((nchunk, hid_dim), F32)
    second = jnp.zeros((nchunk, hid_dim), F32)
    for l in range(CMP_STRIDE):
        rows = x_ref[pl.ds(l, nchunk, stride=CMP_STRIDE), :]
        first += jnp.dot((rows + pe_ref[l:l + 1, :]).astype(BF16), w1_ref[l],
                         preferred_element_type=F32)
        second += jnp.dot((rows + pe_ref[CMP_STRIDE + l:CMP_STRIDE + l + 1, :]).astype(BF16),
                          w1_ref[CMP_STRIDE + l], preferred_element_type=F32)
    second = jnp.concatenate([second[1:], jnp.zeros((1, hid_dim), F32)], axis=0)
    pre = first + second
    hid = pre * jax.nn.sigmoid(pre)
    if is_key:
        out = jnp.dot(hid.astype(BF16), w2_ref[...], preferred_element_type=F32)
        o_ref[0] = _rope_head(out, _rope_tables(pos_ref[...], invf_ref[...])).astype(BF16)
    else:
        out_t = lax.dot_general(w2_ref[...], hid.astype(BF16), (((1,), (1,)), ((), ())),
                                preferred_element_type=F32)
        o_ref[0] = _with_ones_row(out_t).astype(BF16)


def _compress(proj, col_off, pe_p, w1_p, w2_p, pos_cmp, invf, is_key):
    T = proj.shape[0]
    nchunk = T // CMP_STRIDE
    hid = w1_p.shape[2]
    out_block = (1, nchunk, HEAD_PAD) if is_key else (1, HEAD_PAD, nchunk)
    return pl.pallas_call(
        functools.partial(_cmp_kernel, is_key=is_key),
        grid=(N_KV_GROUPS,),
        in_specs=[pl.BlockSpec((T, HEAD_PAD), lambda g: (0, col_off // HEAD_PAD + g)),
                  _const_spec((CMP_LEN, HEAD_PAD)), _const_spec((CMP_LEN, HEAD_PAD, hid)),
                  _const_spec(w2_p.shape), _const_spec((nchunk, 1)), _const_spec((1, HEAD_PAD))],
        out_specs=pl.BlockSpec(out_block, lambda g: (g, 0, 0)),
        out_shape=jax.ShapeDtypeStruct((N_KV_GROUPS,) + out_block[1:], BF16),
        compiler_params=_params("arbitrary"),
        name="compress",
    )(proj, pe_p, w1_p, w2_p, pos_cmp, invf)


SLC_TK = 1024
WIN_SUB = 256
WIN_KEYS = WINDOW + WIN_SUB
R = HEADS_PER_GROUP


def _lanes4(a):
    return jnp.concatenate([a] * R, axis=1)


def _attn_kernel(qt_ref, kc_ref, vct_ref, ks_ref, vst_ref, kw_ref, vwt_ref, gate_ref, c2st_ref, o_ref,
                 qa_ref, neg_ref, acc_ref, m_ref, s_ref, cmax_ref, *, n_sel):
    ci = pl.program_id(1)
    s = ci * Q_BLOCK
    qt = jnp.concatenate([qt_ref[r * HEAD_PAD:(r + 1) * HEAD_PAD, :] for r in range(R)], axis=1)
    tok = s + lax.broadcasted_iota(jnp.int32, (1, Q_BLOCK), 1)

    n_cmp = kc_ref.shape[1]
    cmp_end = lax.broadcasted_iota(jnp.int32, (n_cmp, 1), 0) * CMP_STRIDE + (CMP_LEN - 1)
    cmp_bias = jnp.where(cmp_end <= tok, 0.0, NEG_INF)
    cmp_live = tok >= CMP_LEN - 1

    def cmp_scores(r):
        return jnp.dot(kc_ref[0], qt[:, r * Q_BLOCK:(r + 1) * Q_BLOCK], preferred_element_type=F32) + cmp_bias

    o_cmp_heads = []
    p_sum = None
    sc_next = cmp_scores(0)
    for r in range(R):
        sc = sc_next
        if r + 1 < R:
            sc_next = cmp_scores(r + 1)
        p = jnp.exp2(sc - jnp.max(sc, axis=0, keepdims=True))
        p = p * jnp.where(cmp_live, 1.0 / jnp.sum(p, axis=0, keepdims=True), 0.0)
        o_cmp_heads.append(jnp.dot(vct_ref[0, 0:V_ROWS, :], p.astype(BF16), preferred_element_type=F32))
        p_sum = p if p_sum is None else p_sum + p
    o_cmp = jnp.concatenate(o_cmp_heads, axis=1)
    p_hi = p_sum.astype(BF16)
    p_lo = (p_sum - p_hi.astype(F32)).astype(BF16)
    imp = (jnp.dot(c2st_ref[...], p_hi, preferred_element_type=F32)
           + jnp.dot(c2st_ref[...], p_lo, preferred_element_type=F32))

    n_blk = c2st_ref.shape[0]
    jj = lax.broadcasted_iota(jnp.int32, (n_blk, 1), 0)
    cur = tok >> 6
    forced = (jj == 0) | (jj == cur) | (jj == cur - 1)
    score = jnp.where(jj <= cur, jnp.where(forced, SEL_FORCE, imp), NEG_INF)
    chosen = jnp.zeros((n_blk, Q_BLOCK), jnp.bool_)
    for _ in range(n_sel):
        best = jnp.max(score, axis=0, keepdims=True)
        hit = jj == jnp.min(jnp.where(score == best, jj, n_blk), axis=0, keepdims=True)
        chosen = chosen | (hit & (best > 0.5 * NEG_INF))
        score = jnp.where(hit, REMOVED, score)
    neg_ref[...] = _lanes4(jnp.where(chosen, 0.0, -MASK_BIG)).astype(BF16)

    qa_ref[...] = qt
    acc_ref[...] = jnp.zeros_like(acc_ref)
    m_ref[...] = jnp.full_like(m_ref, NEG_INF)

    def scores_head(jt, buf, r):
        ls = slice(r * Q_BLOCK, (r + 1) * Q_BLOCK)
        k0 = pl.multiple_of(jt * SLC_TK, SLC_TK)
        if r == 0:
            blk0 = pl.multiple_of((k0 // (SLC_LEN * MASK_ROWS)) * MASK_ROWS, MASK_ROWS)
            qa_ref[MASK_LANE:MASK_LANE + MASK_ROWS, :] = neg_ref[pl.ds(blk0, MASK_ROWS), :]
        st = jnp.dot(ks_ref[pl.ds(k0, SLC_TK), :], qa_ref[:, ls], preferred_element_type=F32)
        s_ref[buf, :, ls] = st
        cmax_ref[buf, :, ls] = jnp.max(st, axis=0, keepdims=True)

    def softmax_head(jt, buf, r, bias=None):
        ls = slice(r * Q_BLOCK, (r + 1) * Q_BLOCK)
        k0 = pl.multiple_of(jt * SLC_TK, SLC_TK)
        m_old = m_ref[:, ls]
        if bias is None:
            st = s_ref[buf, :, ls]
            m_new = jnp.maximum(m_old, cmax_ref[buf, :, ls])
        else:
            st = s_ref[buf, :, ls] + bias
            m_new = jnp.maximum(m_old, jnp.max(st, axis=0, keepdims=True))
        pt = jnp.exp2(st - m_new).astype(BF16)
        acc_ref[:, ls] = (jnp.exp2(m_old - m_new) * acc_ref[:, ls]
                          + jnp.dot(vst_ref[0:V_ROWS, pl.ds(k0, SLC_TK)], pt, preferred_element_type=F32))
        m_ref[:, ls] = m_new

    n_full = s // SLC_TK
    for r in range(R):
        scores_head(0, 0, r)

    def pair(i, carry):
        for nxt, cur in ((1, 0), (0, 1)):
            for r in range(R):
                scores_head(2 * i + cur + 1, nxt, r)
                softmax_head(2 * i + cur, cur, r)
        return carry

    lax.fori_loop(0, n_full // 2, pair, 0)

    def last_tile(buf):
        kpos = n_full * SLC_TK + lax.broadcasted_iota(jnp.int32, (SLC_TK, 1), 0)
        bias = jnp.where(kpos <= tok, 0.0, NEG_INF)
        for r in range(R):
            softmax_head(n_full, buf, r, bias)

    @pl.when(n_full % 2 == 1)
    def _():
        for r in range(R):
            scores_head(n_full, 1, r)
            softmax_head(n_full - 1, 0, r)
        last_tile(1)

    @pl.when(n_full % 2 == 0)
    def _():
        last_tile(0)

    acc = acc_ref[...]
    o_slc = acc * (1.0 / acc[ONES_ROW:ONES_ROW + 1, :])

    def win_scores(b):
        sb = s + b * WIN_SUB
        tok_b = sb + lax.broadcasted_iota(jnp.int32, (1, WIN_SUB), 1)
        qt_b = jnp.concatenate([qt[:, r * Q_BLOCK + b * WIN_SUB:r * Q_BLOCK + (b + 1) * WIN_SUB]
                                for r in range(R)], axis=1)
        w0 = pl.multiple_of(jnp.maximum(sb - WINDOW, 0), WIN_SUB)
        wpos = w0 + lax.broadcasted_iota(jnp.int32, (WIN_KEYS, 1), 0)
        sw = jnp.dot(kw_ref[pl.ds(w0, WIN_KEYS), :], qt_b, preferred_element_type=F32)
        return w0, sw + _lanes4(jnp.where((wpos <= tok_b) & (wpos > tok_b - WINDOW), 0.0, NEG_INF))

    n_win = Q_BLOCK // WIN_SUB
    win_parts = []
    nxt = win_scores(0)
    for b in range(n_win):
        w0, sw = nxt
        if b + 1 < n_win:
            nxt = win_scores(b + 1)
        pw = jnp.exp2(sw - jnp.max(sw, axis=0, keepdims=True)).astype(BF16)
        ow = jnp.dot(vwt_ref[0:V_ROWS, pl.ds(w0, WIN_KEYS)], pw, preferred_element_type=F32)
        win_parts.append(ow * (1.0 / ow[ONES_ROW:ONES_ROW + 1, :]))

    heads = []
    for r in range(R):
        ls = slice(r * Q_BLOCK, (r + 1) * Q_BLOCK)
        o_win = jnp.concatenate([part[:, r * WIN_SUB:(r + 1) * WIN_SUB] for part in win_parts], axis=1)
        g0 = 3 * (pl.program_id(0) * R + r)
        o = (gate_ref[pl.ds(g0, 1), :] * o_cmp[:, ls] + gate_ref[pl.ds(g0 + 1, 1), :] * o_slc[:, ls]
             + gate_ref[pl.ds(g0 + 2, 1), :] * o_win)
        heads.append(o[0:D_V, :])
    o_ref[...] = jnp.concatenate(heads, axis=0).T.astype(BF16)


def _attention(qt, k_cmp, v_cmp_t, ks_r, vs_t, kw_r, vw_t, gate_t, c2st):
    T = ks_r.shape[0]
    n_cmp = k_cmp.shape[1]
    n_blk = c2st.shape[0]
    n_sel = min(N_SELECT, T // SLC_LEN)
    gw = R * HEAD_PAD
    rows = lambda g, ci: (0, g)
    cols = lambda g, ci: (g, 0)
    return pl.pallas_call(
        functools.partial(_attn_kernel, n_sel=n_sel),
        grid=(N_KV_GROUPS, T // Q_BLOCK),
        in_specs=[pl.BlockSpec((gw, Q_BLOCK), lambda g, ci: (g, ci)),
                  pl.BlockSpec((1, n_cmp, HEAD_PAD), lambda g, ci: (g, 0, 0)),
                  pl.BlockSpec((1, HEAD_PAD, n_cmp), lambda g, ci: (g, 0, 0)),
                  pl.BlockSpec((T, HEAD_PAD), rows), pl.BlockSpec((HEAD_PAD, T), cols),
                  pl.BlockSpec((T, HEAD_PAD), rows), pl.BlockSpec((HEAD_PAD, T), cols),
                  pl.BlockSpec((HEAD_PAD, Q_BLOCK), lambda g, ci: (0, ci)),
                  _const_spec((n_blk, n_cmp))],
        out_specs=pl.BlockSpec((Q_BLOCK, R * D_V), lambda g, ci: (ci, g)),
        out_shape=jax.ShapeDtypeStruct((T, N_HEADS * D_V), BF16),
        scratch_shapes=[pltpu.VMEM((HEAD_PAD, R * Q_BLOCK), BF16), pltpu.VMEM((n_blk, R * Q_BLOCK), BF16),
                        pltpu.VMEM((V_ROWS, R * Q_BLOCK), F32), pltpu.VMEM((1, R * Q_BLOCK), F32),
                        pltpu.VMEM((2, SLC_TK, R * Q_BLOCK), F32), pltpu.VMEM((2, 1, R * Q_BLOCK), F32)],
        compiler_params=_params("arbitrary", "arbitrary"),
        name="nsa_attention",
    )(qt, k_cmp, v_cmp_t, ks_r, vs_t, kw_r, vw_t, gate_t, c2st)


MIX_TM = 256


def _mix_kernel(u_ref, up_ref, gm_ref, o_ref, pw_ref, ps_ref, wpu_ref, wnu_ref, y_ref):
    i = pl.program_id(0)
    tm = u_ref.shape[0]
    u = u_ref[...]
    prev = up_ref[...] * (i > 0).astype(F32)
    ext = jnp.concatenate([prev, u], axis=0)
    t = i * tm + lax.broadcasted_iota(jnp.int32, (tm, 1), 0)
    parts = []
    for g, win in enumerate(POOL_WINDOWS):
        cs = slice(g * POOL_GROUP, (g + 1) * POOL_GROUP)
        run = ext[:, cs]
        k = 1
        while k < win:
            run = run[k:] + run[:-k]
            k *= 2
        lo = POOL_HALO - (win - 1)
        total = run[lo:lo + tm]
        count = jnp.minimum(t + 1, win).astype(F32)
        d = total / count - u[:, cs]
        parts.append(jnp.dot(d.astype(BF16), pw_ref[g], preferred_element_type=F32) * ps_ref[:, cs])
    a = jnp.concatenate(parts, axis=1).astype(BF16)
    ya = jnp.dot(a, wpu_ref[...], preferred_element_type=F32)
    yb = jnp.dot(o_ref[...], wnu_ref[...], preferred_element_type=F32)
    gm = jax.nn.sigmoid(gm_ref[...])
    y_ref[...] = (gm[:, :D_MODEL] * ya + gm[:, D_MODEL:] * yb).astype(BF16)


def _mix(proj, o, pool_w_b, pool_scale, w_pool_up_b, w_nsa_up_p):
    T = proj.shape[0]
    tm = min(MIX_TM, T)
    halo_blocks = tm // POOL_HALO
    return pl.pallas_call(
        _mix_kernel,
        grid=(T // tm,),
        in_specs=[pl.BlockSpec((tm, POOL_WIDTH), lambda i: (i, C_U // POOL_WIDTH)),
                  pl.BlockSpec((POOL_HALO, POOL_WIDTH),
                               lambda i: (jnp.maximum(i * halo_blocks - 1, 0), C_U // POOL_WIDTH)),
                  pl.BlockSpec((tm, 2 * D_MODEL), lambda i: (i, 0)),
                  pl.BlockSpec((tm, N_HEADS * D_V), lambda i: (i, 0)),
                  _const_spec(pool_w_b.shape), _const_spec((1, POOL_WIDTH)),
                  _const_spec(w_pool_up_b.shape), _const_spec(w_nsa_up_p.shape)],
        out_specs=pl.BlockSpec((tm, D_MODEL), lambda i: (i, 0)),
        out_shape=jax.ShapeDtypeStruct((T, D_MODEL), BF16),
        compiler_params=_params("arbitrary"),
        name="pool_merge",
    )(proj, proj, proj, o, pool_w_b, pool_scale, w_pool_up_b, w_nsa_up_p)


OUT_TM = 256


def _rms(y, g):
    return y * lax.rsqrt(jnp.mean(y * y, axis=-1, keepdims=True) + RMS_EPS) * g


def _outproj_kernel(y_ref, w_ref, x_ref, g_ref, gt_ref, o_ref):
    z = jnp.dot(y_ref[...], w_ref[...], preferred_element_type=F32)
    o_ref[...] = x_ref[...] + gt_ref[...] * _rms(z, g_ref[...])


def _outproj(y, w_out_b, x2, g_post, mod):
    T = y.shape[0]
    tm = min(OUT_TM, T)
    row = pl.BlockSpec((tm, D_MODEL), lambda i: (i, 0))
    return pl.pallas_call(
        _outproj_kernel,
        grid=(T // tm,),
        in_specs=[row, _const_spec((D_MODEL, D_MODEL)), row, _const_spec((1, D_MODEL)), _mod_spec(2)],
        out_specs=row,
        out_shape=jax.ShapeDtypeStruct((T, D_MODEL), F32),
        compiler_params=_params("arbitrary"),
        name="out_proj",
    )(y, w_out_b, x2, g_post, mod)


def _pack_table():
    sizes = (POOL_WIDTH, N_HEADS * D_QK, N_KV_GROUPS * D_QK, N_KV_GROUPS * D_V, N_KV_GROUPS * D_QK,
             N_KV_GROUPS * D_V, N_KV_GROUPS * D_QK, N_KV_GROUPS * D_V, 3 * N_HEADS, 2 * D_MODEL)
    u, q, kc, vc, ks, vs, kw, vw, gn, gm = [int(o) for o in np.cumsum((0,) + sizes)[:-1]]
    segments = [(gm, HEAD_PAD, 2 * D_MODEL // HEAD_PAD), (q, D_QK, N_HEADS), (u, HEAD_PAD, POOL_WIDTH // HEAD_PAD),
                (kc, D_QK, N_KV_GROUPS), (ks, D_QK, N_KV_GROUPS), (kw, D_QK, N_KV_GROUPS),
                (vc, D_V, N_KV_GROUPS), (vs, D_V, N_KV_GROUPS), (vw, D_V, N_KV_GROUPS),
                (gn, 3 * N_HEADS, 1)]
    tiles = [(start + i * width, width) for start, width, count in segments for i in range(count)]
    tiles += [(0, 0)] * (N_COLS // HEAD_PAD - len(tiles))
    assert len(tiles) * HEAD_PAD == N_COLS
    return np.asarray(tiles, np.int32)


def _repack_kernel(blk_ref, shift_ref, width_ref, a_ref, b_ref, o_ref):
    j = pl.program_id(0)
    shift = shift_ref[j]
    lane = lax.broadcasted_iota(jnp.int32, (1, HEAD_PAD), 1)
    back = (HEAD_PAD - shift) & (HEAD_PAD - 1)
    merged = jnp.where(lane < HEAD_PAD - shift, pltpu.roll(a_ref[...], back, 1), pltpu.roll(b_ref[...], back, 1))
    o_ref[...] = jnp.where(lane < width_ref[j], merged, 0.0).astype(BF16)


def _pack_w_in(w_in):
    k, n_src = w_in.shape
    table = _pack_table()
    last = (n_src - 1) // HEAD_PAD
    grid_spec = pltpu.PrefetchScalarGridSpec(
        num_scalar_prefetch=3,
        grid=(table.shape[0],),
        in_specs=[pl.BlockSpec((k, HEAD_PAD), lambda j, blk, *_: (0, blk[j])),
                  pl.BlockSpec((k, HEAD_PAD), lambda j, blk, *_: (0, jnp.minimum(blk[j] + 1, last)))],
        out_specs=pl.BlockSpec((k, HEAD_PAD), lambda j, *_: (0, j)),
    )
    return pl.pallas_call(
        _repack_kernel,
        grid_spec=grid_spec,
        out_shape=jax.ShapeDtypeStruct((k, N_COLS), BF16),
        compiler_params=_params("arbitrary"),
        name="repack_w_in",
    )(jnp.asarray(table[:, 0] // HEAD_PAD), jnp.asarray(table[:, 0] % HEAD_PAD), jnp.asarray(table[:, 1]),
      w_in, w_in)


def _slc_from_cmp(n_blk, n_chunk):
    cs = np.arange(n_chunk)[None, :] * CMP_STRIDE
    ss = np.arange(n_blk)[:, None] * SLC_LEN
    ov = np.clip(np.minimum(cs + CMP_LEN, ss + SLC_LEN) - np.maximum(cs, ss), 0, None)
    return jnp.asarray(ov / CMP_LEN, dtype=BF16)


def _token_mixer_block(x2, pos_col, mod, g_pre, g_post, w_in, cmp_pos_k, cmp_k_w1, cmp_k_w2,
                       cmp_pos_v, cmp_v_w1, cmp_v_w2, pool_w, pool_scale, w_pool_up, w_nsa_up, w_out):
    T = x2.shape[0]
    half = jnp.arange(ROT_HALF, dtype=F32)
    inv = jnp.power(jnp.float32(ROPE_THETA), -half * 2.0 / (2 * ROT_HALF))
    invf = jnp.concatenate([inv, inv, jnp.zeros((HEAD_PAD - 2 * ROT_HALF,), F32)])[None, :]

    proj = _proj(x2, g_pre, mod, _pack_w_in(w_in))
    qt, ks_r, kw_r, vs_t, vw_t, gate_t = _prep(proj, pos_col, invf)

    n_chunk = T // CMP_STRIDE
    pos_cmp = jnp.pad(pos_col[CMP_LEN - 1::CMP_STRIDE], ((0, 1), (0, 0)))

    def pad_lanes(a, d):
        return jnp.pad(a, [(0, 0)] * (a.ndim - 1) + [(0, HEAD_PAD - d)])

    def w1_pack(w1, d):
        w1 = w1.reshape(CMP_LEN, d, w1.shape[1])
        return jnp.pad(w1, ((0, 0), (0, HEAD_PAD - d), (0, 0))).astype(BF16)

    k_cmp = _compress(proj, C_KC, pad_lanes(cmp_pos_k, D_QK), w1_pack(cmp_k_w1, D_QK),
                      pad_lanes(cmp_k_w2, D_QK).astype(BF16), pos_cmp, invf, True)
    v_cmp = _compress(proj, C_VC, pad_lanes(cmp_pos_v, D_V), w1_pack(cmp_v_w1, D_V),
                      pad_lanes(cmp_v_w2, D_V).T.astype(BF16), pos_cmp, invf, False)

    n_blk = max(T // SLC_LEN, MASK_ROWS)
    o = _attention(qt, k_cmp, v_cmp, ks_r, vs_t, kw_r, vw_t, gate_t, _slc_from_cmp(n_blk, n_chunk))

    y = _mix(proj, o, pool_w.astype(BF16), pool_scale, w_pool_up.astype(BF16), w_nsa_up.astype(BF16))
    return _outproj(y, w_out.astype(BF16), x2, g_post, mod)


ROUTE_TM = 256
EXPERTS_PER_GROUP = N_EXPERTS // N_EXPERT_GROUPS


def _split3(a):
    hi = a.astype(BF16)
    r1 = a - hi.astype(F32)
    mid = r1.astype(BF16)
    lo = (r1 - mid.astype(F32)).astype(BF16)
    return hi, mid, lo


def _dot_f32(a, b):
    a3, b3 = _split3(a), _split3(b)
    out = None
    for i, j in ((2, 0), (1, 1), (0, 2), (1, 0), (0, 1), (0, 0)):
        term = jnp.dot(a3[i], b3[j], preferred_element_type=F32)
        out = term if out is None else out + term
    return out


def _first_index_of(mask, lane, n):
    return jnp.min(jnp.where(mask, lane, float(n)), axis=-1, keepdims=True)


def _route_kernel(x_ref, g_ref, sc_ref, sh_ref, wr_ref, rb_ref,
                  hf_ref, hb_ref, sel_ref, wd_ref, rank_ref, cnt_ref, carry_ref):
    i = pl.program_id(0)
    tm = x_ref.shape[0]

    @pl.when(i == 0)
    def _():
        carry_ref[...] = jnp.zeros_like(carry_ref)

    h = _modulated_norm(x_ref[...], g_ref[...], sc_ref[...], sh_ref[...])
    hf_ref[...] = h
    hb_ref[...] = h.astype(BF16)
    logits = _dot_f32(h, wr_ref[...])
    aff = jax.nn.sigmoid(logits)
    biased = aff + rb_ref[...]
    grp = lax.broadcasted_iota(jnp.int32, (1, N_EXPERTS), 1) // EXPERTS_PER_GROUP
    lane = lax.broadcasted_iota(jnp.int32, (1, N_EXPERTS), 1).astype(F32)

    gs = []
    for gi in range(N_EXPERT_GROUPS):
        v = jnp.where(grp == gi, biased, REMOVED)
        m1 = jnp.max(v, axis=-1, keepdims=True)
        i1 = _first_index_of(v == m1, lane, N_EXPERTS)
        m2 = jnp.max(jnp.where(lane == i1, REMOVED, v), axis=-1, keepdims=True)
        gs.append(m1 + m2)
    keep = jnp.zeros((tm, N_EXPERTS), jnp.bool_)
    for gi in range(N_EXPERT_GROUPS):
        ahead = jnp.zeros((tm, 1), jnp.int32)
        for gj in range(N_EXPERT_GROUPS):
            if gj < gi:
                ahead += (gs[gj] >= gs[gi]).astype(jnp.int32)
            elif gj > gi:
                ahead += (gs[gj] > gs[gi]).astype(jnp.int32)
        keep = keep | ((grp == gi) & (ahead < TOPK_GROUPS))
    masked = jnp.where(keep, biased, NEG_INF)
    sel = jnp.zeros((tm, N_EXPERTS), F32)
    for _ in range(TOP_K):
        best = jnp.max(masked, axis=-1, keepdims=True)
        hit = lane == _first_index_of(masked == best, lane, N_EXPERTS)
        sel = jnp.where(hit, 1.0, sel)
        masked = jnp.where(hit, REMOVED, masked)
    w = aff * sel
    wd_ref[...] = w / jnp.sum(w, axis=-1, keepdims=True) * ROUTED_SCALE
    sel_ref[...] = sel

    r_i = lax.broadcasted_iota(jnp.int32, (tm, tm), 0)
    c_i = lax.broadcasted_iota(jnp.int32, (tm, tm), 1)
    before = jnp.where(c_i < r_i, 1.0, 0.0).astype(BF16)
    carry = carry_ref[...]
    rank_ref[...] = jnp.dot(before, sel.astype(BF16), preferred_element_type=F32) + carry
    carry = carry + jnp.sum(sel, axis=0, keepdims=True)
    carry_ref[...] = carry
    cnt_ref[...] = carry


def _route(x1, g_pre, mod, w_router, router_bias):
    T = x1.shape[0]
    tm = min(ROUTE_TM, T)
    row = pl.BlockSpec((tm, D_MODEL), lambda i: (i, 0))
    erow = pl.BlockSpec((tm, N_EXPERTS), lambda i: (i, 0))
    e_shape = jax.ShapeDtypeStruct((T, N_EXPERTS), F32)
    return pl.pallas_call(
        _route_kernel,
        grid=(T // tm,),
        in_specs=[row, _const_spec((1, D_MODEL)), _mod_spec(4), _mod_spec(3),
                  _const_spec((D_MODEL, N_EXPERTS)), _const_spec((1, N_EXPERTS))],
        out_specs=[row, row, erow, erow, erow, _const_spec((1, N_EXPERTS))],
        out_shape=[jax.ShapeDtypeStruct((T, D_MODEL), F32), jax.ShapeDtypeStruct((T, D_MODEL), BF16),
                   e_shape, e_shape, e_shape, jax.ShapeDtypeStruct((1, N_EXPERTS), F32)],
        scratch_shapes=[pltpu.VMEM((1, N_EXPERTS), F32)],
        compiler_params=_params("arbitrary"),
        name="moe_route",
    )(x1, g_pre, mod, mod, w_router, router_bias)


def _dest_kernel(sel_ref, wd_ref, rank_ref, ps_ref, dest_ref, w8_ref):
    tm = sel_ref.shape[0]
    slot = ps_ref[...] + rank_ref[...]
    wd = wd_ref[...]
    lane = lax.broadcasted_iota(jnp.int32, (1, N_EXPERTS), 1).astype(F32)
    lane_k = lax.broadcasted_iota(jnp.int32, (1, TOP_K), 1)
    lane_p = lax.broadcasted_iota(jnp.int32, (1, HEAD_PAD), 1)
    dest = jnp.zeros((tm, HEAD_PAD), F32)
    w8 = jnp.zeros((tm, TOP_K), F32)
    rem = sel_ref[...]
    for k in range(TOP_K):
        hit = lane == _first_index_of(rem > 0.5, lane, N_EXPERTS)
        dest = jnp.where(lane_p == k, jnp.sum(jnp.where(hit, slot, 0.0), axis=-1, keepdims=True), dest)
        w8 = jnp.where(lane_k == k, jnp.sum(jnp.where(hit, wd, 0.0), axis=-1, keepdims=True), w8)
        rem = jnp.where(hit, 0.0, rem)
    dest_ref[...] = dest.T[0:TOP_K, :].astype(jnp.int32)
    w8_ref[...] = w8


def _dest(sel, wd, rank, pstart):
    T = sel.shape[0]
    tm = min(ROUTE_TM, T)
    erow = pl.BlockSpec((tm, N_EXPERTS), lambda i: (i, 0))
    krow = pl.BlockSpec((tm, TOP_K), lambda i: (i, 0))
    return pl.pallas_call(
        _dest_kernel,
        grid=(T // tm,),
        in_specs=[erow, erow, erow, _const_spec((1, N_EXPERTS))],
        out_specs=[pl.BlockSpec((TOP_K, tm), lambda i: (0, i)), krow],
        out_shape=[jax.ShapeDtypeStruct((TOP_K, T), jnp.int32), jax.ShapeDtypeStruct((T, TOP_K), F32)],
        compiler_params=_params("arbitrary"),
        name="moe_dest",
    )(sel, wd, rank, pstart)


DISPATCH_TM = 128


def _dispatch_kernel(dest_ref, h_ref, xs_hbm, sem):
    tm = h_ref.shape[0]

    def body(t, carry):
        for k in range(TOP_K):
            pltpu.make_async_copy(h_ref.at[pl.ds(t, 1), :], xs_hbm.at[pl.ds(dest_ref[k, t], 1), :], sem).start()
        return carry

    lax.fori_loop(0, tm, body, 0)
    for k in range(TOP_K):
        pltpu.make_async_copy(h_ref, xs_hbm.at[pl.ds(0, tm), :], sem).wait()


def _dispatch(dest_t, hf, n_slots):
    T = hf.shape[0]
    tm = min(DISPATCH_TM, T)
    return pl.pallas_call(
        _dispatch_kernel,
        grid=(T // tm,),
        in_specs=[pl.BlockSpec((TOP_K, tm), lambda i: (0, i), memory_space=pltpu.SMEM),
                  pl.BlockSpec((tm, D_MODEL), lambda i: (i, 0))],
        out_specs=pl.BlockSpec(memory_space=pl.ANY),
        out_shape=jax.ShapeDtypeStruct((n_slots, D_MODEL), F32),
        scratch_shapes=[pltpu.SemaphoreType.DMA(())],
        compiler_params=pltpu.CompilerParams(dimension_semantics=("arbitrary",), vmem_limit_bytes=VMEM_LIMIT,
                                             has_side_effects=True),
        name="moe_dispatch",
    )(dest_t, hf)


EXPERT_BLOCK = 256


def _swiglu(x, wg, wu, wd):
    g = jnp.dot(x, wg, preferred_element_type=F32)
    u = jnp.dot(x, wu, preferred_element_type=F32)
    hid = (g * jax.nn.sigmoid(g)) * u
    return jnp.dot(hid.astype(BF16), wd, preferred_element_type=F32)


def _expert_items(counts, n_items):
    ends = jnp.cumsum(counts)
    starts = ends - counts
    first_blk = starts // EXPERT_BLOCK
    last_blk = (ends - 1) // EXPERT_BLOCK
    per_expert = jnp.where(counts > 0, last_blk - first_blk + 1, 0)
    item_end = jnp.cumsum(per_expert)
    n_used = item_end[-1]
    i = jnp.minimum(jnp.arange(n_items, dtype=jnp.int32), jnp.maximum(n_used - 1, 0))
    e = jnp.sum((item_end[None, :] <= i[:, None]).astype(jnp.int32), axis=1)
    e = jnp.minimum(e, N_EXPERTS - 1)
    ids = jnp.arange(N_EXPERTS, dtype=jnp.int32)
    owner = e[:, None] == ids[None, :]
    of_item = lambda per_expert_value: jnp.sum(jnp.where(owner, per_expert_value[None, :], 0), axis=1)
    blk = of_item(first_blk) + i - of_item(item_end - per_expert)
    lo = jnp.clip(of_item(starts) - blk * EXPERT_BLOCK, 0, EXPERT_BLOCK)
    hi = jnp.clip(of_item(ends) - blk * EXPERT_BLOCK, 0, EXPERT_BLOCK)
    live = jnp.arange(n_items, dtype=jnp.int32) < n_used
    hi = jnp.where(live, hi, lo)
    new_expert = jnp.concatenate([jnp.ones((1,), jnp.bool_), e[1:] != e[:-1]])
    new_block = jnp.concatenate([jnp.ones((1,), jnp.bool_), blk[1:] != blk[:-1]])
    as_i32 = lambda a: a.astype(jnp.int32)
    used = counts > 0
    slot_of = (jnp.cumsum(as_i32(used)) - 1) % 2
    later = (ids[None, :] > ids[:, None]) & used[None, :]
    next_of = jnp.min(jnp.where(later, ids[None, :], N_EXPERTS), axis=1)
    next_of = jnp.where(next_of < N_EXPERTS, next_of, -1)
    return (as_i32(blk), as_i32(e), as_i32(lo), as_i32(hi), as_i32(new_expert & live),
            as_i32(new_block & live), as_i32(live), as_i32(of_item(slot_of)), as_i32(of_item(next_of)), starts)


def _expert_kernel(blk_ref, e_ref, lo_ref, hi_ref, newe_ref, newb_ref, live_ref, slot_ref, nexte_ref,
                   x_ref, wg_hbm, wu_hbm, wd_hbm, y_ref, wg32, wu32, wd32, wgb, wub, wdb, sem):
    i = pl.program_id(0)

    def weight_copies(e, slot):
        return (pltpu.make_async_copy(wg_hbm.at[e], wg32.at[slot], sem.at[slot]),
                pltpu.make_async_copy(wu_hbm.at[e], wu32.at[slot], sem.at[slot]),
                pltpu.make_async_copy(wd_hbm.at[e], wd32.at[slot], sem.at[slot]))

    @pl.when(live_ref[i] == 1)
    def _():
        @pl.when(newe_ref[i] == 1)
        def _():
            slot = slot_ref[i]

            @pl.when(i == 0)
            def _():
                for c in weight_copies(e_ref[i], slot):
                    c.start()

            for c in weight_copies(e_ref[i], slot):
                c.wait()
            wgb[...] = wg32[slot].astype(BF16)
            wub[...] = wu32[slot].astype(BF16)
            wdb[...] = wd32[slot].astype(BF16)

            @pl.when(nexte_ref[i] >= 0)
            def _():
                for c in weight_copies(nexte_ref[i], 1 - slot):
                    c.start()

        row = lax.broadcasted_iota(jnp.int32, (EXPERT_BLOCK, 1), 0)
        mine = (row >= lo_ref[i]) & (row < hi_ref[i])
        y = _swiglu(jnp.where(mine, x_ref[...], 0.0).astype(BF16), wgb[...], wub[...], wdb[...])

        @pl.when(newb_ref[i] == 1)
        def _():
            y_ref[...] = y

        @pl.when(newb_ref[i] == 0)
        def _():
            y_ref[...] += y


def _experts(items, xs, w_gate, w_up, w_down):
    n_items = items[0].shape[0]
    by_block = lambda i, blk, *_: (blk[i], 0)
    hbm = pl.BlockSpec(memory_space=pl.ANY)
    grid_spec = pltpu.PrefetchScalarGridSpec(
        num_scalar_prefetch=len(items),
        grid=(n_items,),
        in_specs=[pl.BlockSpec((EXPERT_BLOCK, D_MODEL), by_block), hbm, hbm, hbm],
        out_specs=pl.BlockSpec((EXPERT_BLOCK, D_MODEL), by_block),
        scratch_shapes=[pltpu.VMEM((2, D_MODEL, D_EXPERT), F32), pltpu.VMEM((2, D_MODEL, D_EXPERT), F32),
                        pltpu.VMEM((2, D_EXPERT, D_MODEL), F32),
                        pltpu.VMEM((D_MODEL, D_EXPERT), BF16), pltpu.VMEM((D_MODEL, D_EXPERT), BF16),
                        pltpu.VMEM((D_EXPERT, D_MODEL), BF16), pltpu.SemaphoreType.DMA((2,))],
    )
    return pl.pallas_call(
        _expert_kernel,
        grid_spec=grid_spec,
        out_shape=jax.ShapeDtypeStruct(xs.shape, F32),
        compiler_params=_params("arbitrary"),
        name="moe_experts",
    )(*items, xs, w_gate, w_up, w_down)


COMBINE_TM = 128


def _combine_kernel(dcur_ref, dnext_ref, w8_ref, hb_ref, x_ref, wsg_ref, wsu_ref, wsd_ref, g_ref, gt_ref,
                    y_hbm, o_ref, gbuf, sem):
    i = pl.program_id(0)
    n = pl.num_programs(0)
    tm = x_ref.shape[0]

    rows = TOP_K * tm

    def issue(d_ref, slot):
        base = slot * rows

        def body(t, carry):
            for k in range(TOP_K):
                pltpu.make_async_copy(y_hbm.at[pl.ds(d_ref[k, t], 1), :],
                                      gbuf.at[pl.ds(base + k * tm + t, 1), :], sem.at[slot]).start()
            return carry

        lax.fori_loop(0, tm, body, 0)

    @pl.when(i == 0)
    def _():
        issue(dcur_ref, 0)

    @pl.when(i + 1 < n)
    def _():
        issue(dnext_ref, (i + 1) % 2)

    slot = i % 2
    base = pl.multiple_of(slot * rows, rows)
    shared = _swiglu(hb_ref[...], wsg_ref[...], wsu_ref[...], wsd_ref[...])
    for k in range(TOP_K):
        pltpu.make_async_copy(y_hbm.at[pl.ds(0, tm), :], gbuf.at[pl.ds(base + k * tm, tm), :],
                              sem.at[slot]).wait()
    w8 = w8_ref[...]
    y = shared
    for k in range(TOP_K):
        y = y + w8[:, k:k + 1] * gbuf[pl.ds(base + k * tm, tm), :]
    o_ref[...] = x_ref[...] + gt_ref[...] * _rms(y, g_ref[...])


def _combine(dest_t, w8, hb, x1, wsg, wsu, wsd, g_post, mod, y_buf):
    T = x1.shape[0]
    tm = min(COMBINE_TM, T)
    n = T // tm
    row = pl.BlockSpec((tm, D_MODEL), lambda i: (i, 0))
    return pl.pallas_call(
        _combine_kernel,
        grid=(n,),
        in_specs=[pl.BlockSpec((TOP_K, tm), lambda i: (0, i), memory_space=pltpu.SMEM),
                  pl.BlockSpec((TOP_K, tm), lambda i: (0, jnp.minimum(i + 1, n - 1)), memory_space=pltpu.SMEM),
                  pl.BlockSpec((tm, TOP_K), lambda i: (i, 0)), row, row,
                  _const_spec(wsg.shape), _const_spec(wsu.shape), _const_spec(wsd.shape),
                  _const_spec((1, D_MODEL)), _mod_spec(5),
                  pl.BlockSpec(memory_space=pl.ANY)],
        out_specs=row,
        out_shape=jax.ShapeDtypeStruct((T, D_MODEL), F32),
        scratch_shapes=[pltpu.VMEM((2 * TOP_K * tm, D_MODEL), F32), pltpu.SemaphoreType.DMA((2,))],
        compiler_params=_params("arbitrary"),
        name="moe_combine",
    )(dest_t, dest_t, w8, hb, x1, wsg, wsu, wsd, g_post, mod, y_buf)


def _moe_block(x1, mod, g_pre, g_post, w_router, router_bias, w_exp_gate, w_exp_up, w_exp_down,
               w_sh_gate, w_sh_up, w_sh_down):
    T = x1.shape[0]
    hf, hb, sel, wd, rank, counts = _route(x1, g_pre, mod, w_router, router_bias)

    n_slots = T * TOP_K
    *items, starts = _expert_items(counts[0].astype(jnp.int32), n_slots // EXPERT_BLOCK + N_EXPERTS)

    dest_t, w8 = _dest(sel, wd, rank, starts.astype(F32)[None, :])
    xs = _dispatch(dest_t, hf, n_slots)
    y_buf = _experts(items, xs, w_exp_gate, w_exp_up, w_exp_down)
    return _combine(dest_t, w8, hb, x1, w_sh_gate.astype(BF16), w_sh_up.astype(BF16),
                    w_sh_down.astype(BF16), g_post, mod, y_buf)


def kernel(x, c, positions, w_ada, b_ada, g_pre_mix, g_post_mix, g_pre_ffn, g_post_ffn, w_in, cmp_pos_k, cmp_k_w1, cmp_k_w2, cmp_pos_v, cmp_v_w1, cmp_v_w2, pool_w, pool_scale, w_pool_up, w_nsa_up, w_out, w_router, router_bias, w_exp_gate, w_exp_up, w_exp_down, w_sh_gate, w_sh_up, w_sh_down):
    B, T, D = x.shape
    assert B == 1 and D == D_MODEL
    x2 = x.reshape(T, D)
    pos_col = positions.reshape(T, 1)
    for l in range(w_ada.shape[0]):
        mod = _ada(c.reshape(D, 1), w_ada[l], b_ada[l][None, :])
        x2 = _token_mixer_block(x2, pos_col, mod, g_pre_mix[l][None, :], g_post_mix[l][None, :], w_in[l],
                                cmp_pos_k[l], cmp_k_w1[l], cmp_k_w2[l], cmp_pos_v[l], cmp_v_w1[l],
                                cmp_v_w2[l], pool_w[l], pool_scale[l][None, :], w_pool_up[l], w_nsa_up[l],
                                w_out[l])
        x2 = _moe_block(x2, mod, g_pre_ffn[l][None, :], g_post_ffn[l][None, :], w_router[l],
                        router_bias[l][None, :], w_exp_gate[l], w_exp_up[l], w_exp_down[l],
                        w_sh_gate[l], w_sh_up[l], w_sh_down[l])
    return x2.reshape(B, T, D)
```

```python
import functools

import numpy as np
import jax
import jax.numpy as jnp
from jax import lax
from jax.experimental import pallas as pl
from jax.experimental.pallas import tpu as pltpu

F32 = jnp.float32
BF16 = jnp.bfloat16

D_MODEL = 2048
POOL_WIDTH = 1024
POOL_WINDOWS = (2, 4, 8, 16)
POOL_GROUP = 256
POOL_HALO = 16
N_HEADS = 16
N_KV_GROUPS = 4
HEADS_PER_GROUP = 4
D_QK = 96
D_V = 64
HEAD_PAD = 128
ROT_HALF = 12
ROPE_THETA = 500000.0
CMP_LEN = 32
CMP_STRIDE = 16
SLC_LEN = 64
N_SELECT = 16
WINDOW = 512
Q_BLOCK = 512
N_EXPERTS = 64
TOP_K = 8
N_EXPERT_GROUPS = 8
TOPK_GROUPS = 4
D_EXPERT = 512
ROUTED_SCALE = 2.5
RMS_EPS = 1e-6
NEG_INF = -1e30
SEL_FORCE = 1e30
REMOVED = -3e38

VMEM_LIMIT = 56 * 1024 * 1024

C_GM = 0
C_Q = 4096
C_U = 6144
C_KC = 7168
C_KS = 7680
C_KW = 8192
C_VC = 8704
C_VS = 9216
C_VW = 9728
C_GN = 10240
N_COLS = 10752


def _params(*sem):
    return pltpu.CompilerParams(dimension_semantics=sem, vmem_limit_bytes=VMEM_LIMIT)


def _const_spec(shape):
    nd = len(shape)
    return pl.BlockSpec(shape, lambda *_: (0,) * nd)


ADA_TN = 1536
ADA_ROWS = 64


def _ada_kernel(c_ref, w_ref, b_ref, o_ref, s_ref):
    c = c_ref[...]
    s_ref[...] = c * jax.nn.sigmoid(c)

    def body(k, acc):
        r = pl.multiple_of(k * ADA_ROWS, ADA_ROWS)
        for j in range(ADA_ROWS // 8):
            acc = acc + w_ref[pl.ds(r + 8 * j, 8), :] * s_ref[pl.ds(r + 8 * j, 8), :]
        return acc

    acc = lax.fori_loop(0, D_MODEL // ADA_ROWS, body, jnp.zeros((8, ADA_TN), F32))
    o_ref[...] = jnp.sum(acc, axis=0, keepdims=True) + b_ref[...]


def _ada(c_col, w_ada, b_ada):
    n = w_ada.shape[1]
    return pl.pallas_call(
        _ada_kernel,
        grid=(n // ADA_TN,),
        in_specs=[_const_spec((D_MODEL, 1)),
                  pl.BlockSpec((D_MODEL, ADA_TN), lambda j: (0, j)),
                  pl.BlockSpec((1, ADA_TN), lambda j: (0, j))],
        out_specs=pl.BlockSpec((1, ADA_TN), lambda j: (0, j)),
        out_shape=jax.ShapeDtypeStruct((1, n), F32),
        scratch_shapes=[pltpu.VMEM((D_MODEL, 1), F32)],
        compiler_params=_params("arbitrary"),
        name="ada",
    )(c_col, w_ada, b_ada)


def _mod_spec(k):
    return pl.BlockSpec((1, D_MODEL), lambda *_: (0, k))


PROJ_TM = 512
PROJ_TN = 1536


def _modulated_norm(x, g, sc, sh):
    y = x * lax.rsqrt(jnp.mean(x * x, axis=-1, keepdims=True) + RMS_EPS)
    return (y * g) * (1.0 + sc) + sh


def _proj_kernel(x_ref, g_ref, sc_ref, sh_ref, w_ref, o_ref, h_ref):
    @pl.when(pl.program_id(1) == 0)
    def _():
        h_ref[...] = _modulated_norm(x_ref[...], g_ref[...], sc_ref[...], sh_ref[...]).astype(BF16)

    o_ref[...] = jnp.dot(h_ref[...], w_ref[...], preferred_element_type=F32)


def _proj(x2, g_pre, mod, w_in_p):
    T = x2.shape[0]
    tm = min(PROJ_TM, T)
    return pl.pallas_call(
        _proj_kernel,
        grid=(T // tm, N_COLS // PROJ_TN),
        in_specs=[pl.BlockSpec((tm, D_MODEL), lambda i, j: (i, 0)),
                  _const_spec((1, D_MODEL)), _mod_spec(1), _mod_spec(0),
                  pl.BlockSpec((D_MODEL, PROJ_TN), lambda i, j: (0, j))],
        out_specs=pl.BlockSpec((tm, PROJ_TN), lambda i, j: (i, j)),
        out_shape=jax.ShapeDtypeStruct((T, N_COLS), F32),
        scratch_shapes=[pltpu.VMEM((tm, D_MODEL), BF16)],
        compiler_params=_params("arbitrary", "arbitrary"),
        name="in_proj",
    )(x2, g_pre, mod, mod, w_in_p)


def _rope_tables(pos_col, invf):
    ang = pos_col.astype(F32) * invf
    cos = jnp.cos(ang)
    sin = jnp.sin(ang)
    lane = lax.broadcasted_iota(jnp.int32, ang.shape, 1)
    sin_lo = jnp.where(lane < ROT_HALF, -sin, 0.0)
    sin_hi = jnp.where((lane >= ROT_HALF) & (lane < 2 * ROT_HALF), sin, 0.0)
    return cos, sin_lo, sin_hi


def _rope_head(x, tabs):
    cos, sin_lo, sin_hi = tabs
    return (x * cos + pltpu.roll(x, HEAD_PAD - ROT_HALF, 1) * sin_lo
            + pltpu.roll(x, ROT_HALF, 1) * sin_hi)


PREP_TM = 256


LOG2E = 1.4426950408889634
Q_SCALE = (D_QK ** -0.5) * LOG2E
MASK_LANE = D_QK
MASK_ROWS = 16
ONES_ROW = D_V
V_ROWS = 80
MASK_BIG = 1e30


def _with_ones_row(vt):
    row = lax.broadcasted_iota(jnp.int32, (HEAD_PAD, 1), 0)
    return jnp.where(row == ONES_ROW, 1.0, vt)


def _prep_kernel(pos_ref, invf_ref, q_ref, ks_ref, kw_ref, vs_ref, vw_ref, gn_ref,
                 qt_ref, kso_ref, kwo_ref, vst_ref, vwt_ref, gt_ref):
    tm = q_ref.shape[0]
    tabs = _rope_tables(pos_ref[...], invf_ref[...])
    for h in range(N_HEADS):
        sl = slice(h * HEAD_PAD, (h + 1) * HEAD_PAD)
        qt_ref[sl, :] = (_rope_head(q_ref[:, sl], tabs) * Q_SCALE).T.astype(BF16)
    row = pl.program_id(0) * tm + lax.broadcasted_iota(jnp.int32, (tm, 1), 0)
    lane = lax.broadcasted_iota(jnp.int32, (1, HEAD_PAD), 1)
    block_tag = jnp.where(lane == MASK_LANE + ((row >> 6) & (MASK_ROWS - 1)), 1.0, 0.0)
    for g in range(N_KV_GROUPS):
        sl = slice(g * HEAD_PAD, (g + 1) * HEAD_PAD)
        kso_ref[:, sl] = (_rope_head(ks_ref[:, sl], tabs) + block_tag).astype(BF16)
        kwo_ref[:, sl] = _rope_head(kw_ref[:, sl], tabs).astype(BF16)
        vst_ref[sl, :] = _with_ones_row(vs_ref[:, sl].T).astype(BF16)
        vwt_ref[sl, :] = _with_ones_row(vw_ref[:, sl].T).astype(BF16)
    gt_ref[...] = jax.nn.sigmoid(gn_ref[...]).T


def _prep(proj, pos_col, invf):
    T = proj.shape[0]
    tm = min(PREP_TM, T)
    kv = N_KV_GROUPS * HEAD_PAD
    qw = N_HEADS * HEAD_PAD

    def col(width, off):
        return pl.BlockSpec((tm, width), lambda i: (i, off // width))

    rows = lambda width: pl.BlockSpec((tm, width), lambda i: (i, 0))
    cols = lambda height: pl.BlockSpec((height, tm), lambda i: (0, i))
    return pl.pallas_call(
        _prep_kernel,
        grid=(T // tm,),
        in_specs=[pl.BlockSpec((tm, 1), lambda i: (i, 0)), _const_spec((1, HEAD_PAD)),
                  col(qw, C_Q), col(kv, C_KS), col(kv, C_KW), col(kv, C_VS), col(kv, C_VW),
                  col(HEAD_PAD, C_GN)],
        out_specs=[cols(qw), rows(kv), rows(kv), cols(kv), cols(kv), cols(HEAD_PAD)],
        out_shape=[jax.ShapeDtypeStruct((qw, T), BF16), jax.ShapeDtypeStruct((T, kv), BF16),
                   jax.ShapeDtypeStruct((T, kv), BF16), jax.ShapeDtypeStruct((kv, T), BF16),
                   jax.ShapeDtypeStruct((kv, T), BF16), jax.ShapeDtypeStruct((HEAD_PAD, T), F32)],
        compiler_params=_params("arbitrary"),
        name="nsa_prep",
    )(pos_col, invf, proj, proj, proj, proj, proj, proj)


def _cmp_kernel(x_ref, pe_ref, w1_ref, w2_ref, pos_ref, invf_ref, o_ref, *, is_key):
    nchunk = pos_ref.shape[0]
    hid_dim = w1_ref.shape[2]
    first = jnp.zeros((nchunk, hid_dim), F32)
    second = jnp.zeros((nchunk, hid_dim), F32)
    for l in range(CMP_STRIDE):
        rows = x_ref[pl.ds(l, nchunk, stride=CMP_STRIDE), :]
        first += jnp.dot((rows + pe_ref[l:l + 1, :]).astype(BF16), w1_ref[l],
                         preferred_element_type=F32)
        second += jnp.dot((rows + pe_ref[CMP_STRIDE + l:CMP_STRIDE + l + 1, :]).astype(BF16),
                          w1_ref[CMP_STRIDE + l], preferred_element_type=F32)
    second = jnp.concatenate([second[1:], jnp.zeros((1, hid_dim), F32)], axis=0)
    pre = first + second
    hid = pre * jax.nn.sigmoid(pre)
    if is_key:
        out = jnp.dot(hid.astype(BF16), w2_ref[...], preferred_element_type=F32)
        o_ref[0] = _rope_head(out, _rope_tables(pos_ref[...], invf_ref[...])).astype(BF16)
    else:
        out_t = lax.dot_general(w2_ref[...], hid.astype(BF16), (((1,), (1,)), ((), ())),
                                preferred_element_type=F32)
        o_ref[0] = _with_ones_row(out_t).astype(BF16)


def _compress(proj, col_off, pe_p, w1_p, w2_p, pos_cmp, invf, is_key):
    T = proj.shape[0]
    nchunk = T // CMP_STRIDE
    hid = w1_p.shape[2]
    out_block = (1, nchunk, HEAD_PAD) if is_key else (1, HEAD_PAD, nchunk)
    return pl.pallas_call(
        functools.partial(_cmp_kernel, is_key=is_key),
        grid=(N_KV_GROUPS,),
        in_specs=[pl.BlockSpec((T, HEAD_PAD), lambda g: (0, col_off // HEAD_PAD + g)),
                  _const_spec((CMP_LEN, HEAD_PAD)), _const_spec((CMP_LEN, HEAD_PAD, hid)),
                  _const_spec(w2_p.shape), _const_spec((nchunk, 1)), _const_spec((1, HEAD_PAD))],
        out_specs=pl.BlockSpec(out_block, lambda g: (g, 0, 0)),
        out_shape=jax.ShapeDtypeStruct((N_KV_GROUPS,) + out_block[1:], BF16),
        compiler_params=_params("arbitrary"),
        name="compress",
    )(proj, pe_p, w1_p, w2_p, pos_cmp, invf)


SLC_TK = 1024
WIN_SUB = 256
WIN_KEYS = WINDOW + WIN_SUB
R = HEADS_PER_GROUP


def _lanes4(a):
    return jnp.concatenate([a] * R, axis=1)


def _attn_kernel(qt_ref, kc_ref, vct_ref, ks_ref, vst_ref, kw_ref, vwt_ref, gate_ref, c2st_ref, o_ref,
                 qa_ref, neg_ref, acc_ref, m_ref, s_ref, cmax_ref, *, n_sel):
    ci = pl.program_id(1)
    s = ci * Q_BLOCK
    qt = jnp.concatenate([qt_ref[r * HEAD_PAD:(r + 1) * HEAD_PAD, :] for r in range(R)], axis=1)
    tok = s + lax.broadcasted_iota(jnp.int32, (1, Q_BLOCK), 1)

    n_cmp = kc_ref.shape[1]
    cmp_end = lax.broadcasted_iota(jnp.int32, (n_cmp, 1), 0) * CMP_STRIDE + (CMP_LEN - 1)
    cmp_bias = jnp.where(cmp_end <= tok, 0.0, NEG_INF)
    cmp_live = tok >= CMP_LEN - 1

    def cmp_scores(r):
        return jnp.dot(kc_ref[0], qt[:, r * Q_BLOCK:(r + 1) * Q_BLOCK], preferred_element_type=F32) + cmp_bias

    o_cmp_heads = []
    p_sum = None
    sc_next = cmp_scores(0)
    for r in range(R):
        sc = sc_next
        if r + 1 < R:
            sc_next = cmp_scores(r + 1)
        p = jnp.exp2(sc - jnp.max(sc, axis=0, keepdims=True))
        p = p * jnp.where(cmp_live, 1.0 / jnp.sum(p, axis=0, keepdims=True), 0.0)
        o_cmp_heads.append(jnp.dot(vct_ref[0, 0:V_ROWS, :], p.astype(BF16), preferred_element_type=F32))
        p_sum = p if p_sum is None else p_sum + p
    o_cmp = jnp.concatenate(o_cmp_heads, axis=1)
    p_hi = p_sum.astype(BF16)
    p_lo = (p_sum - p_hi.astype(F32)).astype(BF16)
    imp = (jnp.dot(c2st_ref[...], p_hi, preferred_element_type=F32)
           + jnp.dot(c2st_ref[...], p_lo, preferred_element_type=F32))

    n_blk = c2st_ref.shape[0]
    jj = lax.broadcasted_iota(jnp.int32, (n_blk, 1), 0)
    cur = tok >> 6
    forced = (jj == 0) | (jj == cur) | (jj == cur - 1)
    score = jnp.where(jj <= cur, jnp.where(forced, SEL_FORCE, imp), NEG_INF)
    chosen = jnp.zeros((n_blk, Q_BLOCK), jnp.bool_)
    for _ in range(n_sel):
        best = jnp.max(score, axis=0, keepdims=True)
        hit = jj == jnp.min(jnp.where(score == best, jj, n_blk), axis=0, keepdims=True)
        chosen = chosen | (hit & (best > 0.5 * NEG_INF))
        score = jnp.where(hit, REMOVED, score)
    neg_ref[...] = _lanes4(jnp.where(chosen, 0.0, -MASK_BIG)).astype(BF16)

    qa_ref[...] = qt
    acc_ref[...] = jnp.zeros_like(acc_ref)
    m_ref[...] = jnp.full_like(m_ref, NEG_INF)

    def scores_head(jt, buf, r):
        ls = slice(r * Q_BLOCK, (r + 1) * Q_BLOCK)
        k0 = pl.multiple_of(jt * SLC_TK, SLC_TK)
        if r == 0:
            blk0 = pl.multiple_of((k0 // (SLC_LEN * MASK_ROWS)) * MASK_ROWS, MASK_ROWS)
            qa_ref[MASK_LANE:MASK_LANE + MASK_ROWS, :] = neg_ref[pl.ds(blk0, MASK_ROWS), :]
        st = jnp.dot(ks_ref[pl.ds(k0, SLC_TK), :], qa_ref[:, ls], preferred_element_type=F32)
        s_ref[buf, :, ls] = st
        cmax_ref[buf, :, ls] = jnp.max(st, axis=0, keepdims=True)

    def softmax_head(jt, buf, r, bias=None):
        ls = slice(r * Q_BLOCK, (r + 1) * Q_BLOCK)
        k0 = pl.multiple_of(jt * SLC_TK, SLC_TK)
        m_old = m_ref[:, ls]
        if bias is None:
            st = s_ref[buf, :, ls]
            m_new = jnp.maximum(m_old, cmax_ref[buf, :, ls])
        else:
            st = s_ref[buf, :, ls] + bias
            m_new = jnp.maximum(m_old, jnp.max(st, axis=0, keepdims=True))
        pt = jnp.exp2(st - m_new).astype(BF16)
        acc_ref[:, ls] = (jnp.exp2(m_old - m_new) * acc_ref[:, ls]
                          + jnp.dot(vst_ref[0:V_ROWS, pl.ds(k0, SLC_TK)], pt, preferred_element_type=F32))
        m_ref[:, ls] = m_new

    n_full = s // SLC_TK
    for r in range(R):
        scores_head(0, 0, r)

    def pair(i, carry):
        for nxt, cur in ((1, 0), (0, 1)):
            for r in range(R):
                scores_head(2 * i + cur + 1, nxt, r)
                softmax_head(2 * i + cur, cur, r)
        return carry

    lax.fori_loop(0, n_full // 2, pair, 0)

    def last_tile(buf):
        kpos = n_full * SLC_TK + lax.broadcasted_iota(jnp.int32, (SLC_TK, 1), 0)
        bias = jnp.where(kpos <= tok, 0.0, NEG_INF)
        for r in range(R):
            softmax_head(n_full, buf, r, bias)

    @pl.when(n_full % 2 == 1)
    def _():
        for r in range(R):
            scores_head(n_full, 1, r)
            softmax_head(n_full - 1, 0, r)
        last_tile(1)

    @pl.when(n_full % 2 == 0)
    def _():
        last_tile(0)

    acc = acc_ref[...]
    o_slc = acc * (1.0 / acc[ONES_ROW:ONES_ROW + 1, :])

    def win_scores(b):
        sb = s + b * WIN_SUB
        tok_b = sb + lax.broadcasted_iota(jnp.int32, (1, WIN_SUB), 1)
        qt_b = jnp.concatenate([qt[:, r * Q_BLOCK + b * WIN_SUB:r * Q_BLOCK + (b + 1) * WIN_SUB]
                                for r in range(R)], axis=1)
        w0 = pl.multiple_of(jnp.maximum(sb - WINDOW, 0), WIN_SUB)
        wpos = w0 + lax.broadcasted_iota(jnp.int32, (WIN_KEYS, 1), 0)
        sw = jnp.dot(kw_ref[pl.ds(w0, WIN_KEYS), :], qt_b, preferred_element_type=F32)
        return w0, sw + _lanes4(jnp.where((wpos <= tok_b) & (wpos > tok_b - WINDOW), 0.0, NEG_INF))

    n_win = Q_BLOCK // WIN_SUB
    win_parts = []
    nxt = win_scores(0)
    for b in range(n_win):
        w0, sw = nxt
        if b + 1 < n_win:
            nxt = win_scores(b + 1)
        pw = jnp.exp2(sw - jnp.max(sw, axis=0, keepdims=True)).astype(BF16)
        ow = jnp.dot(vwt_ref[0:V_ROWS, pl.ds(w0, WIN_KEYS)], pw, preferred_element_type=F32)
        win_parts.append(ow * (1.0 / ow[ONES_ROW:ONES_ROW + 1, :]))

    heads = []
    for r in range(R):
        ls = slice(r * Q_BLOCK, (r + 1) * Q_BLOCK)
        o_win = jnp.concatenate([part[:, r * WIN_SUB:(r + 1) * WIN_SUB] for part in win_parts], axis=1)
        g0 = 3 * (pl.program_id(0) * R + r)
        o = (gate_ref[pl.ds(g0, 1), :] * o_cmp[:, ls] + gate_ref[pl.ds(g0 + 1, 1), :] * o_slc[:, ls]
             + gate_ref[pl.ds(g0 + 2, 1), :] * o_win)
        heads.append(o[0:D_V, :])
    o_ref[...] = jnp.concatenate(heads, axis=0).T.astype(BF16)


def _attention(qt, k_cmp, v_cmp_t, ks_r, vs_t, kw_r, vw_t, gate_t, c2st):
    T = ks_r.shape[0]
    n_cmp = k_cmp.shape[1]
    n_blk = c2st.shape[0]
    n_sel = min(N_SELECT, T // SLC_LEN)
    gw = R * HEAD_PAD
    rows = lambda g, ci: (0, g)
    cols = lambda g, ci: (g, 0)
    return pl.pallas_call(
        functools.partial(_attn_kernel, n_sel=n_sel),
        grid=(N_KV_GROUPS, T // Q_BLOCK),
        in_specs=[pl.BlockSpec((gw, Q_BLOCK), lambda g, ci: (g, ci)),
                  pl.BlockSpec((1, n_cmp, HEAD_PAD), lambda g, ci: (g, 0, 0)),
                  pl.BlockSpec((1, HEAD_PAD, n_cmp), lambda g, ci: (g, 0, 0)),
                  pl.BlockSpec((T, HEAD_PAD), rows), pl.BlockSpec((HEAD_PAD, T), cols),
                  pl.BlockSpec((T, HEAD_PAD), rows), pl.BlockSpec((HEAD_PAD, T), cols),
                  pl.BlockSpec((HEAD_PAD, Q_BLOCK), lambda g, ci: (0, ci)),
                  _const_spec((n_blk, n_cmp))],
        out_specs=pl.BlockSpec((Q_BLOCK, R * D_V), lambda g, ci: (ci, g)),
        out_shape=jax.ShapeDtypeStruct((T, N_HEADS * D_V), BF16),
        scratch_shapes=[pltpu.VMEM((HEAD_PAD, R * Q_BLOCK), BF16), pltpu.VMEM((n_blk, R * Q_BLOCK), BF16),
                        pltpu.VMEM((V_ROWS, R * Q_BLOCK), F32), pltpu.VMEM((1, R * Q_BLOCK), F32),
                        pltpu.VMEM((2, SLC_TK, R * Q_BLOCK), F32), pltpu.VMEM((2, 1, R * Q_BLOCK), F32)],
        compiler_params=_params("arbitrary", "arbitrary"),
        name="nsa_attention",
    )(qt, k_cmp, v_cmp_t, ks_r, vs_t, kw_r, vw_t, gate_t, c2st)


MIX_TM = 256


def _mix_kernel(u_ref, up_ref, gm_ref, o_ref, pw_ref, ps_ref, wpu_ref, wnu_ref, y_ref):
    i = pl.program_id(0)
    tm = u_ref.shape[0]
    u = u_ref[...]
    prev = up_ref[...] * (i > 0).astype(F32)
    ext = jnp.concatenate([prev, u], axis=0)
    t = i * tm + lax.broadcasted_iota(jnp.int32, (tm, 1), 0)
    parts = []
    for g, win in enumerate(POOL_WINDOWS):
        cs = slice(g * POOL_GROUP, (g + 1) * POOL_GROUP)
        run = ext[:, cs]
        k = 1
        while k < win:
            run = run[k:] + run[:-k]
            k *= 2
        lo = POOL_HALO - (win - 1)
        total = run[lo:lo + tm]
        count = jnp.minimum(t + 1, win).astype(F32)
        d = total / count - u[:, cs]
        parts.append(jnp.dot(d.astype(BF16), pw_ref[g], preferred_element_type=F32) * ps_ref[:, cs])
    a = jnp.concatenate(parts, axis=1).astype(BF16)
    ya = jnp.dot(a, wpu_ref[...], preferred_element_type=F32)
    yb = jnp.dot(o_ref[...], wnu_ref[...], preferred_element_type=F32)
    gm = jax.nn.sigmoid(gm_ref[...])
    y_ref[...] = (gm[:, :D_MODEL] * ya + gm[:, D_MODEL:] * yb).astype(BF16)


def _mix(proj, o, pool_w_b, pool_scale, w_pool_up_b, w_nsa_up_p):
    T = proj.shape[0]
    tm = min(MIX_TM, T)
    halo_blocks = tm // POOL_HALO
    return pl.pallas_call(
        _mix_kernel,
        grid=(T // tm,),
        in_specs=[pl.BlockSpec((tm, POOL_WIDTH), lambda i: (i, C_U // POOL_WIDTH)),
                  pl.BlockSpec((POOL_HALO, POOL_WIDTH),
                               lambda i: (jnp.maximum(i * halo_blocks - 1, 0), C_U // POOL_WIDTH)),
                  pl.BlockSpec((tm, 2 * D_MODEL), lambda i: (i, 0)),
                  pl.BlockSpec((tm, N_HEADS * D_V), lambda i: (i, 0)),
                  _const_spec(pool_w_b.shape), _const_spec((1, POOL_WIDTH)),
                  _const_spec(w_pool_up_b.shape), _const_spec(w_nsa_up_p.shape)],
        out_specs=pl.BlockSpec((tm, D_MODEL), lambda i: (i, 0)),
        out_shape=jax.ShapeDtypeStruct((T, D_MODEL), BF16),
        compiler_params=_params("arbitrary"),
        name="pool_merge",
    )(proj, proj, proj, o, pool_w_b, pool_scale, w_pool_up_b, w_nsa_up_p)


OUT_TM = 256


def _rms(y, g):
    return y * lax.rsqrt(jnp.mean(y * y, axis=-1, keepdims=True) + RMS_EPS) * g


def _outproj_kernel(y_ref, w_ref, x_ref, g_ref, gt_ref, o_ref):
    z = jnp.dot(y_ref[...], w_ref[...], preferred_element_type=F32)
    o_ref[...] = x_ref[...] + gt_ref[...] * _rms(z, g_ref[...])


def _outproj(y, w_out_b, x2, g_post, mod):
    T = y.shape[0]
    tm = min(OUT_TM, T)
    row = pl.BlockSpec((tm, D_MODEL), lambda i: (i, 0))
    return pl.pallas_call(
        _outproj_kernel,
        grid=(T // tm,),
        in_specs=[row, _const_spec((D_MODEL, D_MODEL)), row, _const_spec((1, D_MODEL)), _mod_spec(2)],
        out_specs=row,
        out_shape=jax.ShapeDtypeStruct((T, D_MODEL), F32),
        compiler_params=_params("arbitrary"),
        name="out_proj",
    )(y, w_out_b, x2, g_post, mod)


def _pack_table():
    sizes = (POOL_WIDTH, N_HEADS * D_QK, N_KV_GROUPS * D_QK, N_KV_GROUPS * D_V, N_KV_GROUPS * D_QK,
             N_KV_GROUPS * D_V, N_KV_GROUPS * D_QK, N_KV_GROUPS * D_V, 3 * N_HEADS, 2 * D_MODEL)
    u, q, kc, vc, ks, vs, kw, vw, gn, gm = [int(o) for o in np.cumsum((0,) + sizes)[:-1]]
    segments = [(gm, HEAD_PAD, 2 * D_MODEL // HEAD_PAD), (q, D_QK, N_HEADS), (u, HEAD_PAD, POOL_WIDTH // HEAD_PAD),
                (kc, D_QK, N_KV_GROUPS), (ks, D_QK, N_KV_GROUPS), (kw, D_QK, N_KV_GROUPS),
                (vc, D_V, N_KV_GROUPS), (vs, D_V, N_KV_GROUPS), (vw, D_V, N_KV_GROUPS),
                (gn, 3 * N_HEADS, 1)]
    tiles = [(start + i * width, width) for start, width, count in segments for i in range(count)]
    tiles += [(0, 0)] * (N_COLS // HEAD_PAD - len(tiles))
    assert len(tiles) * HEAD_PAD == N_COLS
    return np.asarray(tiles, np.int32)


def _repack_kernel(blk_ref, shift_ref, width_ref, a_ref, b_ref, o_ref):
    j = pl.program_id(0)
    shift = shift_ref[j]
    lane = lax.broadcasted_iota(jnp.int32, (1, HEAD_PAD), 1)
    back = (HEAD_PAD - shift) & (HEAD_PAD - 1)
    merged = jnp.where(lane < HEAD_PAD - shift, pltpu.roll(a_ref[...], back, 1), pltpu.roll(b_ref[...], back, 1))
    o_ref[...] = jnp.where(lane < width_ref[j], merged, 0.0).astype(BF16)


def _pack_w_in(w_in):
    k, n_src = w_in.shape
    table = _pack_table()
    last = (n_src - 1) // HEAD_PAD
    grid_spec = pltpu.PrefetchScalarGridSpec(
        num_scalar_prefetch=3,
        grid=(table.shape[0],),
        in_specs=[pl.BlockSpec((k, HEAD_PAD), lambda j, blk, *_: (0, blk[j])),
                  pl.BlockSpec((k, HEAD_PAD), lambda j, blk, *_: (0, jnp.minimum(blk[j] + 1, last)))],
        out_specs=pl.BlockSpec((k, HEAD_PAD), lambda j, *_: (0, j)),
    )
    return pl.pallas_call(
        _repack_kernel,
        grid_spec=grid_spec,
        out_shape=jax.ShapeDtypeStruct((k, N_COLS), BF16),
        compiler_params=_params("arbitrary"),
        name="repack_w_in",
    )(jnp.asarray(table[:, 0] // HEAD_PAD), jnp.asarray(table[:, 0] % HEAD_PAD), jnp.asarray(table[:, 1]),
      w_in, w_in)


def _slc_from_cmp(n_blk, n_chunk):
    cs = np.arange(n_chunk)[None, :] * CMP_STRIDE
    ss = np.arange(n_blk)[:, None] * SLC_LEN
    ov = np.clip(np.minimum(cs + CMP_LEN, ss + SLC_LEN) - np.maximum(cs, ss), 0, None)
    return jnp.asarray(ov / CMP_LEN, dtype=BF16)


def _token_mixer_block(x2, pos_col, mod, g_pre, g_post, w_in, cmp_pos_k, cmp_k_w1, cmp_k_w2,
                       cmp_pos_v, cmp_v_w1, cmp_v_w2, pool_w, pool_scale, w_pool_up, w_nsa_up, w_out):
    T = x2.shape[0]
    half = jnp.arange(ROT_HALF, dtype=F32)
    inv = jnp.power(jnp.float32(ROPE_THETA), -half * 2.0 / (2 * ROT_HALF))
    invf = jnp.concatenate([inv, inv, jnp.zeros((HEAD_PAD - 2 * ROT_HALF,), F32)])[None, :]

    proj = _proj(x2, g_pre, mod, _pack_w_in(w_in))
    qt, ks_r, kw_r, vs_t, vw_t, gate_t = _prep(proj, pos_col, invf)

    n_chunk = T // CMP_STRIDE
    pos_cmp = jnp.pad(pos_col[CMP_LEN - 1::CMP_STRIDE], ((0, 1), (0, 0)))

    def pad_lanes(a, d):
        return jnp.pad(a, [(0, 0)] * (a.ndim - 1) + [(0, HEAD_PAD - d)])

    def w1_pack(w1, d):
        w1 = w1.reshape(CMP_LEN, d, w1.shape[1])
        return jnp.pad(w1, ((0, 0), (0, HEAD_PAD - d), (0, 0))).astype(BF16)

    k_cmp = _compress(proj, C_KC, pad_lanes(cmp_pos_k, D_QK), w1_pack(cmp_k_w1, D_QK),
                      pad_lanes(cmp_k_w2, D_QK).astype(BF16), pos_cmp, invf, True)
    v_cmp = _compress(proj, C_VC, pad_lanes(cmp_pos_v, D_V), w1_pack(cmp_v_w1, D_V),
                      pad_lanes(cmp_v_w2, D_V).T.astype(BF16), pos_cmp, invf, False)

    n_blk = max(T // SLC_LEN, MASK_ROWS)
    o = _attention(qt, k_cmp, v_cmp, ks_r, vs_t, kw_r, vw_t, gate_t, _slc_from_cmp(n_blk, n_chunk))

    y = _mix(proj, o, pool_w.astype(BF16), pool_scale, w_pool_up.astype(BF16), w_nsa_up.astype(BF16))
    return _outproj(y, w_out.astype(BF16), x2, g_post, mod)


ROUTE_TM = 256
EXPERTS_PER_GROUP = N_EXPERTS // N_EXPERT_GROUPS


def _split3(a):
    hi = a.astype(BF16)
    r1 = a - hi.astype(F32)
    mid = r1.astype(BF16)
    lo = (r1 - mid.astype(F32)).astype(BF16)
    return hi, mid, lo


def _dot_f32_nt(a, b):
    a3, b3 = _split3(a), _split3(b)
    out = None
    for i, j in ((2, 0), (1, 1), (0, 2), (1, 0), (0, 1), (0, 0)):
        term = lax.dot_general(a3[i], b3[j], (((1,), (1,)), ((), ())), preferred_element_type=F32)
        out = term if out is None else out + term
    return out


def _first_row_of(mask, row, n):
    return jnp.min(jnp.where(mask, row, n), axis=0, keepdims=True)


def _route_kernel(x_ref, g_ref, sc_ref, sh_ref, wr_ref, rb_ref,
                  hf_ref, hb_ref, sel_ref, wd_ref, rank_ref, cnt_ref, carry_ref):
    i = pl.program_id(0)
    tm = x_ref.shape[0]

    @pl.when(i == 0)
    def _():
        carry_ref[...] = jnp.zeros_like(carry_ref)

    h = _modulated_norm(x_ref[...], g_ref[...], sc_ref[...], sh_ref[...])
    hf_ref[...] = h
    hb_ref[...] = h.astype(BF16)
    logits = _dot_f32_nt(wr_ref[...], h)
    aff = jax.nn.sigmoid(logits)
    biased = aff + rb_ref[...]
    row = lax.broadcasted_iota(jnp.int32, (N_EXPERTS, 1), 0)
    row_g = lax.broadcasted_iota(jnp.int32, (EXPERTS_PER_GROUP, 1), 0)

    gs = []
    for gi in range(N_EXPERT_GROUPS):
        v = biased[gi * EXPERTS_PER_GROUP:(gi + 1) * EXPERTS_PER_GROUP, :]
        m1 = jnp.max(v, axis=0, keepdims=True)
        i1 = _first_row_of(v == m1, row_g, EXPERTS_PER_GROUP)
        m2 = jnp.max(jnp.where(row_g == i1, REMOVED, v), axis=0, keepdims=True)
        gs.append(m1 + m2)
    keep = []
    for gi in range(N_EXPERT_GROUPS):
        ahead = jnp.zeros((1, tm), jnp.int32)
        for gj in range(N_EXPERT_GROUPS):
            if gj < gi:
                ahead += (gs[gj] >= gs[gi]).astype(jnp.int32)
            elif gj > gi:
                ahead += (gs[gj] > gs[gi]).astype(jnp.int32)
        keep.append(jnp.broadcast_to(ahead < TOPK_GROUPS, (EXPERTS_PER_GROUP, tm)))
    masked = jnp.where(jnp.concatenate(keep, axis=0), biased, NEG_INF)
    sel = jnp.zeros((N_EXPERTS, tm), F32)
    for _ in range(TOP_K):
        best = jnp.max(masked, axis=0, keepdims=True)
        hit = row == _first_row_of(masked == best, row, N_EXPERTS)
        sel = jnp.where(hit, 1.0, sel)
        masked = jnp.where(hit, REMOVED, masked)
    w = aff * sel
    wd_ref[...] = w / jnp.sum(w, axis=0, keepdims=True) * ROUTED_SCALE
    sel_ref[...] = sel

    r_i = lax.broadcasted_iota(jnp.int32, (tm, tm), 0)
    c_i = lax.broadcasted_iota(jnp.int32, (tm, tm), 1)
    before = jnp.where(r_i < c_i, 1.0, 0.0).astype(BF16)
    carry = carry_ref[...]
    rank_ref[...] = jnp.dot(sel.astype(BF16), before, preferred_element_type=F32) + carry
    carry = carry + jnp.sum(sel, axis=1, keepdims=True)
    carry_ref[...] = carry
    cnt_ref[...] = carry


def _route(x1, g_pre, mod, w_router, router_bias):
    T = x1.shape[0]
    tm = min(ROUTE_TM, T)
    row = pl.BlockSpec((tm, D_MODEL), lambda i: (i, 0))
    ecol = pl.BlockSpec((N_EXPERTS, tm), lambda i: (0, i))
    e_shape = jax.ShapeDtypeStruct((N_EXPERTS, T), F32)
    return pl.pallas_call(
        _route_kernel,
        grid=(T // tm,),
        in_specs=[row, _const_spec((1, D_MODEL)), _mod_spec(4), _mod_spec(3),
                  _const_spec((N_EXPERTS, D_MODEL)), _const_spec((N_EXPERTS, 1))],
        out_specs=[row, row, ecol, ecol, ecol, _const_spec((N_EXPERTS, 1))],
        out_shape=[jax.ShapeDtypeStruct((T, D_MODEL), F32), jax.ShapeDtypeStruct((T, D_MODEL), BF16),
                   e_shape, e_shape, e_shape, jax.ShapeDtypeStruct((N_EXPERTS, 1), F32)],
        scratch_shapes=[pltpu.VMEM((N_EXPERTS, 1), F32)],
        compiler_params=_params("arbitrary"),
        name="moe_route",
    )(x1, g_pre, mod, mod, w_router.T, router_bias.reshape(N_EXPERTS, 1))


def _dest_kernel(sel_ref, wd_ref, rank_ref, ps_ref, dest_ref, w8_ref):
    tm = sel_ref.shape[1]
    slot = ps_ref[...] + rank_ref[...]
    wd = wd_ref[...]
    row = lax.broadcasted_iota(jnp.int32, (N_EXPERTS, 1), 0)
    dests, weights = [], []
    rem = sel_ref[...]
    for k in range(TOP_K):
        hit = row == _first_row_of(rem > 0.5, row, N_EXPERTS)
        dests.append(jnp.sum(jnp.where(hit, slot, 0.0), axis=0, keepdims=True))
        weights.append(jnp.sum(jnp.where(hit, wd, 0.0), axis=0, keepdims=True))
        rem = jnp.where(hit, 0.0, rem)
    dest_ref[...] = jnp.concatenate(dests, axis=0).astype(jnp.int32)
    w_pad = jnp.concatenate(weights + [jnp.zeros((HEAD_PAD - TOP_K, tm), F32)], axis=0)
    w8_ref[...] = w_pad.T[:, 0:TOP_K]


def _dest(sel_t, wd_t, rank_t, starts_col):
    T = sel_t.shape[1]
    tm = min(ROUTE_TM, T)
    ecol = pl.BlockSpec((N_EXPERTS, tm), lambda i: (0, i))
    return pl.pallas_call(
        _dest_kernel,
        grid=(T // tm,),
        in_specs=[ecol, ecol, ecol, _const_spec((N_EXPERTS, 1))],
        out_specs=[pl.BlockSpec((TOP_K, tm), lambda i: (0, i)), pl.BlockSpec((tm, TOP_K), lambda i: (i, 0))],
        out_shape=[jax.ShapeDtypeStruct((TOP_K, T), jnp.int32), jax.ShapeDtypeStruct((T, TOP_K), F32)],
        compiler_params=_params("arbitrary"),
        name="moe_dest",
    )(sel_t, wd_t, rank_t, starts_col)


DISPATCH_TM = 128


def _dispatch_kernel(dest_ref, h_ref, xs_hbm, sem):
    tm = h_ref.shape[0]

    def body(t, carry):
        for k in range(TOP_K):
            pltpu.make_async_copy(h_ref.at[pl.ds(t, 1), :], xs_hbm.at[pl.ds(dest_ref[k, t], 1), :], sem).start()
        return carry

    lax.fori_loop(0, tm, body, 0)
    for k in range(TOP_K):
        pltpu.make_async_copy(h_ref, xs_hbm.at[pl.ds(0, tm), :], sem).wait()


def _dispatch(dest_t, hf, n_slots):
    T = hf.shape[0]
    tm = min(DISPATCH_TM, T)
    return pl.pallas_call(
        _dispatch_kernel,
        grid=(T // tm,),
        in_specs=[pl.BlockSpec((TOP_K, tm), lambda i: (0, i), memory_space=pltpu.SMEM),
                  pl.BlockSpec((tm, D_MODEL), lambda i: (i, 0))],
        out_specs=pl.BlockSpec(memory_space=pl.ANY),
        out_shape=jax.ShapeDtypeStruct((n_slots, D_MODEL), F32),
        scratch_shapes=[pltpu.SemaphoreType.DMA(())],
        compiler_params=pltpu.CompilerParams(dimension_semantics=("arbitrary",), vmem_limit_bytes=VMEM_LIMIT,
                                             has_side_effects=True),
        name="moe_dispatch",
    )(dest_t, hf)


EXPERT_BLOCK = 256


def _swiglu(x, wg, wu, wd):
    g = jnp.dot(x, wg, preferred_element_type=F32)
    u = jnp.dot(x, wu, preferred_element_type=F32)
    hid = (g * jax.nn.sigmoid(g)) * u
    return jnp.dot(hid.astype(BF16), wd, preferred_element_type=F32)


def _expert_items(counts, n_items):
    ends = jnp.cumsum(counts)
    starts = ends - counts
    first_blk = starts // EXPERT_BLOCK
    last_blk = (ends - 1) // EXPERT_BLOCK
    per_expert = jnp.where(counts > 0, last_blk - first_blk + 1, 0)
    item_end = jnp.cumsum(per_expert)
    n_used = item_end[-1]
    i = jnp.minimum(jnp.arange(n_items, dtype=jnp.int32), jnp.maximum(n_used - 1, 0))
    e = jnp.sum((item_end[None, :] <= i[:, None]).astype(jnp.int32), axis=1)
    e = jnp.minimum(e, N_EXPERTS - 1)
    ids = jnp.arange(N_EXPERTS, dtype=jnp.int32)
    owner = e[:, None] == ids[None, :]
    of_item = lambda per_expert_value: jnp.sum(jnp.where(owner, per_expert_value[None, :], 0), axis=1)
    blk = of_item(first_blk) + i - of_item(item_end - per_expert)
    lo = jnp.clip(of_item(starts) - blk * EXPERT_BLOCK, 0, EXPERT_BLOCK)
    hi = jnp.clip(of_item(ends) - blk * EXPERT_BLOCK, 0, EXPERT_BLOCK)
    live = jnp.arange(n_items, dtype=jnp.int32) < n_used
    hi = jnp.where(live, hi, lo)
    new_expert = jnp.concatenate([jnp.ones((1,), jnp.bool_), e[1:] != e[:-1]])
    new_block = jnp.concatenate([jnp.ones((1,), jnp.bool_), blk[1:] != blk[:-1]])
    as_i32 = lambda a: a.astype(jnp.int32)
    used = counts > 0
    slot_of = (jnp.cumsum(as_i32(used)) - 1) % 2
    later = (ids[None, :] > ids[:, None]) & used[None, :]
    next_of = jnp.min(jnp.where(later, ids[None, :], N_EXPERTS), axis=1)
    next_of = jnp.where(next_of < N_EXPERTS, next_of, -1)
    return (as_i32(blk), as_i32(e), as_i32(lo), as_i32(hi), as_i32(new_expert & live),
            as_i32(new_block & live), as_i32(live), as_i32(of_item(slot_of)), as_i32(of_item(next_of)), starts)


def _expert_kernel(blk_ref, e_ref, lo_ref, hi_ref, newe_ref, newb_ref, live_ref, slot_ref, nexte_ref,
                   x_ref, wg_hbm, wu_hbm, wd_hbm, y_ref, wg32, wu32, wd32, wgb, wub, wdb, sem):
    i = pl.program_id(0)

    def weight_copies(e, slot):
        return (pltpu.make_async_copy(wg_hbm.at[e], wg32.at[slot], sem.at[slot]),
                pltpu.make_async_copy(wu_hbm.at[e], wu32.at[slot], sem.at[slot]),
                pltpu.make_async_copy(wd_hbm.at[e], wd32.at[slot], sem.at[slot]))

    @pl.when(live_ref[i] == 1)
    def _():
        @pl.when(newe_ref[i] == 1)
        def _():
            slot = slot_ref[i]

            @pl.when(i == 0)
            def _():
                for c in weight_copies(e_ref[i], slot):
                    c.start()

            for c in weight_copies(e_ref[i], slot):
                c.wait()
            wgb[...] = wg32[slot].astype(BF16)
            wub[...] = wu32[slot].astype(BF16)
            wdb[...] = wd32[slot].astype(BF16)

            @pl.when(nexte_ref[i] >= 0)
            def _():
                for c in weight_copies(nexte_ref[i], 1 - slot):
                    c.start()

        row = lax.broadcasted_iota(jnp.int32, (EXPERT_BLOCK, 1), 0)
        mine = (row >= lo_ref[i]) & (row < hi_ref[i])
        y = _swiglu(jnp.where(mine, x_ref[...], 0.0).astype(BF16), wgb[...], wub[...], wdb[...])

        @pl.when(newb_ref[i] == 1)
        def _():
            y_ref[...] = y

        @pl.when(newb_ref[i] == 0)
        def _():
            y_ref[...] += y


def _experts(items, xs, w_gate, w_up, w_down):
    n_items = items[0].shape[0]
    by_block = lambda i, blk, *_: (blk[i], 0)
    hbm = pl.BlockSpec(memory_space=pl.ANY)
    grid_spec = pltpu.PrefetchScalarGridSpec(
        num_scalar_prefetch=len(items),
        grid=(n_items,),
        in_specs=[pl.BlockSpec((EXPERT_BLOCK, D_MODEL), by_block), hbm, hbm, hbm],
        out_specs=pl.BlockSpec((EXPERT_BLOCK, D_MODEL), by_block),
        scratch_shapes=[pltpu.VMEM((2, D_MODEL, D_EXPERT), F32), pltpu.VMEM((2, D_MODEL, D_EXPERT), F32),
                        pltpu.VMEM((2, D_EXPERT, D_MODEL), F32),
                        pltpu.VMEM((D_MODEL, D_EXPERT), BF16), pltpu.VMEM((D_MODEL, D_EXPERT), BF16),
                        pltpu.VMEM((D_EXPERT, D_MODEL), BF16), pltpu.SemaphoreType.DMA((2,))],
    )
    return pl.pallas_call(
        _expert_kernel,
        grid_spec=grid_spec,
        out_shape=jax.ShapeDtypeStruct(xs.shape, F32),
        compiler_params=_params("arbitrary"),
        name="moe_experts",
    )(*items, xs, w_gate, w_up, w_down)


COMBINE_TM = 128


def _combine_kernel(dcur_ref, dnext_ref, w8_ref, hb_ref, x_ref, wsg_ref, wsu_ref, wsd_ref, g_ref, gt_ref,
                    y_hbm, o_ref, gbuf, sem):
    i = pl.program_id(0)
    n = pl.num_programs(0)
    tm = x_ref.shape[0]

    rows = TOP_K * tm

    def issue(d_ref, slot):
        base = slot * rows

        def body(t, carry):
            for k in range(TOP_K):
                pltpu.make_async_copy(y_hbm.at[pl.ds(d_ref[k, t], 1), :],
                                      gbuf.at[pl.ds(base + k * tm + t, 1), :], sem.at[slot]).start()
            return carry

        lax.fori_loop(0, tm, body, 0)

    @pl.when(i == 0)
    def _():
        issue(dcur_ref, 0)

    @pl.when(i + 1 < n)
    def _():
        issue(dnext_ref, (i + 1) % 2)

    slot = i % 2
    base = pl.multiple_of(slot * rows, rows)
    shared = _swiglu(hb_ref[...], wsg_ref[...], wsu_ref[...], wsd_ref[...])
    for k in range(TOP_K):
        pltpu.make_async_copy(y_hbm.at[pl.ds(0, tm), :], gbuf.at[pl.ds(base + k * tm, tm), :],
                              sem.at[slot]).wait()
    w8 = w8_ref[...]
    y = shared
    for k in range(TOP_K):
        y = y + w8[:, k:k + 1] * gbuf[pl.ds(base + k * tm, tm), :]
    o_ref[...] = x_ref[...] + gt_ref[...] * _rms(y, g_ref[...])


def _combine(dest_t, w8, hb, x1, wsg, wsu, wsd, g_post, mod, y_buf):
    T = x1.shape[0]
    tm = min(COMBINE_TM, T)
    n = T // tm
    row = pl.BlockSpec((tm, D_MODEL), lambda i: (i, 0))
    return pl.pallas_call(
        _combine_kernel,
        grid=(n,),
        in_specs=[pl.BlockSpec((TOP_K, tm), lambda i: (0, i), memory_space=pltpu.SMEM),
                  pl.BlockSpec((TOP_K, tm), lambda i: (0, jnp.minimum(i + 1, n - 1)), memory_space=pltpu.SMEM),
                  pl.BlockSpec((tm, TOP_K), lambda i: (i, 0)), row, row,
                  _const_spec(wsg.shape), _const_spec(wsu.shape), _const_spec(wsd.shape),
                  _const_spec((1, D_MODEL)), _mod_spec(5),
                  pl.BlockSpec(memory_space=pl.ANY)],
        out_specs=row,
        out_shape=jax.ShapeDtypeStruct((T, D_MODEL), F32),
        scratch_shapes=[pltpu.VMEM((2 * TOP_K * tm, D_MODEL), F32), pltpu.SemaphoreType.DMA((2,))],
        compiler_params=_params("arbitrary"),
        name="moe_combine",
    )(dest_t, dest_t, w8, hb, x1, wsg, wsu, wsd, g_post, mod, y_buf)


def _moe_block(x1, mod, g_pre, g_post, w_router, router_bias, w_exp_gate, w_exp_up, w_exp_down,
               w_sh_gate, w_sh_up, w_sh_down):
    T = x1.shape[0]
    hf, hb, sel_t, wd_t, rank_t, counts = _route(x1, g_pre, mod, w_router, router_bias)

    n_slots = T * TOP_K
    *items, starts = _expert_items(counts[:, 0].astype(jnp.int32), n_slots // EXPERT_BLOCK + N_EXPERTS)

    dest_t, w8 = _dest(sel_t, wd_t, rank_t, starts.astype(F32)[:, None])
    xs = _dispatch(dest_t, hf, n_slots)
    y_buf = _experts(items, xs, w_exp_gate, w_exp_up, w_exp_down)
    return _combine(dest_t, w8, hb, x1, w_sh_gate.astype(BF16), w_sh_up.astype(BF16),
                    w_sh_down.astype(BF16), g_post, mod, y_buf)


def kernel(x, c, positions, w_ada, b_ada, g_pre_mix, g_post_mix, g_pre_ffn, g_post_ffn, w_in, cmp_pos_k, cmp_k_w1, cmp_k_w2, cmp_pos_v, cmp_v_w1, cmp_v_w2, pool_w, pool_scale, w_pool_up, w_nsa_up, w_out, w_router, router_bias, w_exp_gate, w_exp_up, w_exp_down, w_sh_gate, w_sh_up, w_sh_down):
    B, T, D = x.shape
    assert B == 1 and D == D_MODEL
    x2 = x.reshape(T, D)
    pos_col = positions.reshape(T, 1)
    for l in range(w_ada.shape[0]):
        mod = _ada(c.reshape(D, 1), w_ada[l], b_ada[l][None, :])
        x2 = _token_mixer_block(x2, pos_col, mod, g_pre_mix[l][None, :], g_post_mix[l][None, :], w_in[l],
                                cmp_pos_k[l], cmp_k_w1[l], cmp_k_w2[l], cmp_pos_v[l], cmp_v_w1[l],
                                cmp_v_w2[l], pool_w[l], pool_scale[l][None, :], w_pool_up[l], w_nsa_up[l],
                                w_out[l])
        x2 = _moe_block(x2, mod, g_pre_ffn[l][None, :], g_post_ffn[l][None, :], w_router[l],
                        router_bias[l][None, :], w_exp_gate[l], w_exp_up[l], w_exp_down[l],
                        w_sh_gate[l], w_sh_up[l], w_sh_down[l])
    return x2.reshape(B, T, D)
```

```python
import functools

import numpy as np
import jax
import jax.numpy as jnp
from jax import lax
from jax.experimental import pallas as pl
from jax.experimental.pallas import tpu as pltpu

F32 = jnp.float32
BF16 = jnp.bfloat16

D_MODEL = 2048
POOL_WIDTH = 1024
POOL_WINDOWS = (2, 4, 8, 16)
POOL_GROUP = 256
POOL_HALO = 16
N_HEADS = 16
N_KV_GROUPS = 4
HEADS_PER_GROUP = 4
D_QK = 96
D_V = 64
HEAD_PAD = 128
ROT_HALF = 12
ROPE_THETA = 500000.0
CMP_LEN = 32
CMP_STRIDE = 16
SLC_LEN = 64
N_SELECT = 16
WINDOW = 512
Q_BLOCK = 512
N_EXPERTS = 64
TOP_K = 8
N_EXPERT_GROUPS = 8
TOPK_GROUPS = 4
D_EXPERT = 512
ROUTED_SCALE = 2.5
RMS_EPS = 1e-6
NEG_INF = -1e30
SEL_FORCE = 1e30
REMOVED = -3e38

VMEM_LIMIT = 56 * 1024 * 1024

C_GM = 0
C_Q = 4096
C_U = 6144
C_KC = 7168
C_KS = 7680
C_KW = 8192
C_VC = 8704
C_VS = 9216
C_VW = 9728
C_GN = 10240
N_COLS = 10752


def _params(*sem):
    return pltpu.CompilerParams(dimension_semantics=sem, vmem_limit_bytes=VMEM_LIMIT)


def _const_spec(shape):
    nd = len(shape)
    return pl.BlockSpec(shape, lambda *_: (0,) * nd)


ADA_TN = 1536
ADA_ROWS = 64


def _ada_kernel(c_ref, w_ref, b_ref, o_ref, s_ref):
    c = c_ref[...]
    s_ref[...] = c * jax.nn.sigmoid(c)

    def body(k, acc):
        r = pl.multiple_of(k * ADA_ROWS, ADA_ROWS)
        for j in range(ADA_ROWS // 8):
            acc = acc + w_ref[pl.ds(r + 8 * j, 8), :] * s_ref[pl.ds(r + 8 * j, 8), :]
        return acc

    acc = lax.fori_loop(0, D_MODEL // ADA_ROWS, body, jnp.zeros((8, ADA_TN), F32))
    o_ref[...] = jnp.sum(acc, axis=0, keepdims=True) + b_ref[...]


def _ada(c_col, w_ada, b_ada):
    n = w_ada.shape[1]
    return pl.pallas_call(
        _ada_kernel,
        grid=(n // ADA_TN,),
        in_specs=[_const_spec((D_MODEL, 1)),
                  pl.BlockSpec((D_MODEL, ADA_TN), lambda j: (0, j)),
                  pl.BlockSpec((1, ADA_TN), lambda j: (0, j))],
        out_specs=pl.BlockSpec((1, ADA_TN), lambda j: (0, j)),
        out_shape=jax.ShapeDtypeStruct((1, n), F32),
        scratch_shapes=[pltpu.VMEM((D_MODEL, 1), F32)],
        compiler_params=_params("arbitrary"),
        name="ada",
    )(c_col, w_ada, b_ada)


def _mod_spec(k):
    return pl.BlockSpec((1, D_MODEL), lambda *_: (0, k))


PROJ_TM = 512
PROJ_TN = 1536


def _modulated_norm(x, g, sc, sh):
    y = x * lax.rsqrt(jnp.mean(x * x, axis=-1, keepdims=True) + RMS_EPS)
    return (y * g) * (1.0 + sc) + sh


def _proj_kernel(x_ref, g_ref, sc_ref, sh_ref, w_ref, o_ref, h_ref):
    @pl.when(pl.program_id(1) == 0)
    def _():
        h_ref[...] = _modulated_norm(x_ref[...], g_ref[...], sc_ref[...], sh_ref[...]).astype(BF16)

    o_ref[...] = jnp.dot(h_ref[...], w_ref[...], preferred_element_type=F32)


def _proj(x2, g_pre, mod, w_in_p):
    T = x2.shape[0]
    tm = min(PROJ_TM, T)
    return pl.pallas_call(
        _proj_kernel,
        grid=(T // tm, N_COLS // PROJ_TN),
        in_specs=[pl.BlockSpec((tm, D_MODEL), lambda i, j: (i, 0)),
                  _const_spec((1, D_MODEL)), _mod_spec(1), _mod_spec(0),
                  pl.BlockSpec((D_MODEL, PROJ_TN), lambda i, j: (0, j))],
        out_specs=pl.BlockSpec((tm, PROJ_TN), lambda i, j: (i, j)),
        out_shape=jax.ShapeDtypeStruct((T, N_COLS), F32),
        scratch_shapes=[pltpu.VMEM((tm, D_MODEL), BF16)],
        compiler_params=_params("arbitrary", "arbitrary"),
        name="in_proj",
    )(x2, g_pre, mod, mod, w_in_p)


def _rope_tables(pos_col, invf):
    ang = pos_col.astype(F32) * invf
    cos = jnp.cos(ang)
    sin = jnp.sin(ang)
    lane = lax.broadcasted_iota(jnp.int32, ang.shape, 1)
    sin_lo = jnp.where(lane < ROT_HALF, -sin, 0.0)
    sin_hi = jnp.where((lane >= ROT_HALF) & (lane < 2 * ROT_HALF), sin, 0.0)
    return cos, sin_lo, sin_hi


def _rope_head(x, tabs):
    cos, sin_lo, sin_hi = tabs
    return (x * cos + pltpu.roll(x, HEAD_PAD - ROT_HALF, 1) * sin_lo
            + pltpu.roll(x, ROT_HALF, 1) * sin_hi)


PREP_TM = 256


LOG2E = 1.4426950408889634
Q_SCALE = (D_QK ** -0.5) * LOG2E
MASK_LANE = D_QK
MASK_ROWS = 16
ONES_ROW = D_V
V_ROWS = 80
MASK_BIG = 1e30


def _with_ones_row(vt):
    row = lax.broadcasted_iota(jnp.int32, (HEAD_PAD, 1), 0)
    return jnp.where(row == ONES_ROW, 1.0, vt)


def _prep_kernel(pos_ref, invf_ref, q_ref, ks_ref, kw_ref, vs_ref, vw_ref, gn_ref,
                 qt_ref, kso_ref, kwo_ref, vst_ref, vwt_ref, gt_ref):
    tm = q_ref.shape[0]
    tabs = _rope_tables(pos_ref[...], invf_ref[...])
    for h in range(N_HEADS):
        sl = slice(h * HEAD_PAD, (h + 1) * HEAD_PAD)
        qt_ref[sl, :] = (_rope_head(q_ref[:, sl], tabs) * Q_SCALE).T.astype(BF16)
    row = pl.program_id(0) * tm + lax.broadcasted_iota(jnp.int32, (tm, 1), 0)
    lane = lax.broadcasted_iota(jnp.int32, (1, HEAD_PAD), 1)
    block_tag = jnp.where(lane == MASK_LANE + ((row >> 6) & (MASK_ROWS - 1)), 1.0, 0.0)
    for g in range(N_KV_GROUPS):
        sl = slice(g * HEAD_PAD, (g + 1) * HEAD_PAD)
        kso_ref[:, sl] = (_rope_head(ks_ref[:, sl], tabs) + block_tag).astype(BF16)
        kwo_ref[:, sl] = _rope_head(kw_ref[:, sl], tabs).astype(BF16)
        vst_ref[sl, :] = _with_ones_row(vs_ref[:, sl].T).astype(BF16)
        vwt_ref[sl, :] = _with_ones_row(vw_ref[:, sl].T).astype(BF16)
    gt_ref[...] = jax.nn.sigmoid(gn_ref[...]).T


def _prep(proj, pos_col, invf):
    T = proj.shape[0]
    tm = min(PREP_TM, T)
    kv = N_KV_GROUPS * HEAD_PAD
    qw = N_HEADS * HEAD_PAD

    def col(width, off):
        return pl.BlockSpec((tm, width), lambda i: (i, off // width))

    rows = lambda width: pl.BlockSpec((tm, width), lambda i: (i, 0))
    cols = lambda height: pl.BlockSpec((height, tm), lambda i: (0, i))
    return pl.pallas_call(
        _prep_kernel,
        grid=(T // tm,),
        in_specs=[pl.BlockSpec((tm, 1), lambda i: (i, 0)), _const_spec((1, HEAD_PAD)),
                  col(qw, C_Q), col(kv, C_KS), col(kv, C_KW), col(kv, C_VS), col(kv, C_VW),
                  col(HEAD_PAD, C_GN)],
        out_specs=[cols(qw), rows(kv), rows(kv), cols(kv), cols(kv), cols(HEAD_PAD)],
        out_shape=[jax.ShapeDtypeStruct((qw, T), BF16), jax.ShapeDtypeStruct((T, kv), BF16),
                   jax.ShapeDtypeStruct((T, kv), BF16), jax.ShapeDtypeStruct((kv, T), BF16),
                   jax.ShapeDtypeStruct((kv, T), BF16), jax.ShapeDtypeStruct((HEAD_PAD, T), F32)],
        compiler_params=_params("arbitrary"),
        name="nsa_prep",
    )(pos_col, invf, proj, proj, proj, proj, proj, proj)


def _cmp_kernel(x_ref, pe_ref, w1_ref, w2_ref, pos_ref, invf_ref, o_ref, *, is_key):
    nchunk = pos_ref.shape[0]
    hid_dim = w1_ref.shape[2]
    first = jnp.zeros((nchunk, hid_dim), F32)
    second = jnp.zeros((nchunk, hid_dim), F32)
    for l in range(CMP_STRIDE):
        rows = x_ref[pl.ds(l, nchunk, stride=CMP_STRIDE), :]
        first += jnp.dot((rows + pe_ref[l:l + 1, :]).astype(BF16), w1_ref[l],
                         preferred_element_type=F32)
        second += jnp.dot((rows + pe_ref[CMP_STRIDE + l:CMP_STRIDE + l + 1, :]).astype(BF16),
                          w1_ref[CMP_STRIDE + l], preferred_element_type=F32)
    second = jnp.concatenate([second[1:], jnp.zeros((1, hid_dim), F32)], axis=0)
    pre = first + second
    hid = pre * jax.nn.sigmoid(pre)
    if is_key:
        out = jnp.dot(hid.astype(BF16), w2_ref[...], preferred_element_type=F32)
        o_ref[0] = _rope_head(out, _rope_tables(pos_ref[...], invf_ref[...])).astype(BF16)
    else:
        out_t = lax.dot_general(w2_ref[...], hid.astype(BF16), (((1,), (1,)), ((), ())),
                                preferred_element_type=F32)
        o_ref[0] = _with_ones_row(out_t).astype(BF16)


def _compress(proj, col_off, pe_p, w1_p, w2_p, pos_cmp, invf, is_key):
    T = proj.shape[0]
    nchunk = T // CMP_STRIDE
    hid = w1_p.shape[2]
    out_block = (1, nchunk, HEAD_PAD) if is_key else (1, HEAD_PAD, nchunk)
    return pl.pallas_call(
        functools.partial(_cmp_kernel, is_key=is_key),
        grid=(N_KV_GROUPS,),
        in_specs=[pl.BlockSpec((T, HEAD_PAD), lambda g: (0, col_off // HEAD_PAD + g)),
                  _const_spec((CMP_LEN, HEAD_PAD)), _const_spec((CMP_LEN, HEAD_PAD, hid)),
                  _const_spec(w2_p.shape), _const_spec((nchunk, 1)), _const_spec((1, HEAD_PAD))],
        out_specs=pl.BlockSpec(out_block, lambda g: (g, 0, 0)),
        out_shape=jax.ShapeDtypeStruct((N_KV_GROUPS,) + out_block[1:], BF16),
        compiler_params=_params("arbitrary"),
        name="compress",
    )(proj, pe_p, w1_p, w2_p, pos_cmp, invf)


SLC_TK = 1024
WIN_SUB = 256
WIN_KEYS = WINDOW + WIN_SUB
R = HEADS_PER_GROUP


def _lanes4(a):
    return jnp.concatenate([a] * R, axis=1)


def _attn_kernel(qt_ref, kc_ref, vct_ref, ks_ref, vst_ref, kw_ref, vwt_ref, gate_ref, c2st_ref, o_ref,
                 qa_ref, neg_ref, acc_ref, m_ref, s_ref, cmax_ref, *, n_sel):
    ci = pl.program_id(1)
    s = ci * Q_BLOCK
    qt = jnp.concatenate([qt_ref[r * HEAD_PAD:(r + 1) * HEAD_PAD, :] for r in range(R)], axis=1)
    tok = s + lax.broadcasted_iota(jnp.int32, (1, Q_BLOCK), 1)

    n_cmp = kc_ref.shape[1]
    cmp_end = lax.broadcasted_iota(jnp.int32, (n_cmp, 1), 0) * CMP_STRIDE + (CMP_LEN - 1)
    cmp_bias = jnp.where(cmp_end <= tok, 0.0, NEG_INF)
    cmp_live = tok >= CMP_LEN - 1

    def cmp_scores(r):
        return jnp.dot(kc_ref[0], qt[:, r * Q_BLOCK:(r + 1) * Q_BLOCK], preferred_element_type=F32) + cmp_bias

    o_cmp_heads = []
    p_sum = None
    sc_next = cmp_scores(0)
    for r in range(R):
        sc = sc_next
        if r + 1 < R:
            sc_next = cmp_scores(r + 1)
        p = jnp.exp2(sc - jnp.max(sc, axis=0, keepdims=True))
        p = p * jnp.where(cmp_live, 1.0 / jnp.sum(p, axis=0, keepdims=True), 0.0)
        o_cmp_heads.append(jnp.dot(vct_ref[0, 0:V_ROWS, :], p.astype(BF16), preferred_element_type=F32))
        p_sum = p if p_sum is None else p_sum + p
    o_cmp = jnp.concatenate(o_cmp_heads, axis=1)
    p_hi = p_sum.astype(BF16)
    p_lo = (p_sum - p_hi.astype(F32)).astype(BF16)
    imp = (jnp.dot(c2st_ref[...], p_hi, preferred_element_type=F32)
           + jnp.dot(c2st_ref[...], p_lo, preferred_element_type=F32))

    n_blk = c2st_ref.shape[0]
    jj = lax.broadcasted_iota(jnp.int32, (n_blk, 1), 0)
    cur = tok >> 6
    forced = (jj == 0) | (jj == cur) | (jj == cur - 1)
    score = jnp.where(jj <= cur, jnp.where(forced, SEL_FORCE, imp), NEG_INF)
    chosen = jnp.zeros((n_blk, Q_BLOCK), jnp.bool_)
    for _ in range(n_sel):
        best = jnp.max(score, axis=0, keepdims=True)
        hit = jj == jnp.min(jnp.where(score == best, jj, n_blk), axis=0, keepdims=True)
        chosen = chosen | (hit & (best > 0.5 * NEG_INF))
        score = jnp.where(hit, REMOVED, score)
    neg_ref[...] = _lanes4(jnp.where(chosen, 0.0, -MASK_BIG)).astype(BF16)

    qa_ref[...] = qt
    acc_ref[...] = jnp.zeros_like(acc_ref)
    m_ref[...] = jnp.full_like(m_ref, NEG_INF)

    def scores_head(jt, buf, r):
        ls = slice(r * Q_BLOCK, (r + 1) * Q_BLOCK)
        k0 = pl.multiple_of(jt * SLC_TK, SLC_TK)
        if r == 0:
            blk0 = pl.multiple_of((k0 // (SLC_LEN * MASK_ROWS)) * MASK_ROWS, MASK_ROWS)
            qa_ref[MASK_LANE:MASK_LANE + MASK_ROWS, :] = neg_ref[pl.ds(blk0, MASK_ROWS), :]
        st = jnp.dot(ks_ref[pl.ds(k0, SLC_TK), :], qa_ref[:, ls], preferred_element_type=F32)
        s_ref[buf, :, ls] = st
        cmax_ref[buf, :, ls] = jnp.max(st, axis=0, keepdims=True)

    def softmax_head(jt, buf, r, bias=None):
        ls = slice(r * Q_BLOCK, (r + 1) * Q_BLOCK)
        k0 = pl.multiple_of(jt * SLC_TK, SLC_TK)
        m_old = m_ref[:, ls]
        if bias is None:
            st = s_ref[buf, :, ls]
            m_new = jnp.maximum(m_old, cmax_ref[buf, :, ls])
        else:
            st = s_ref[buf, :, ls] + bias
            m_new = jnp.maximum(m_old, jnp.max(st, axis=0, keepdims=True))
        pt = jnp.exp2(st - m_new).astype(BF16)
        acc_ref[:, ls] = (jnp.exp2(m_old - m_new) * acc_ref[:, ls]
                          + jnp.dot(vst_ref[0:V_ROWS, pl.ds(k0, SLC_TK)], pt, preferred_element_type=F32))
        m_ref[:, ls] = m_new

    n_full = s // SLC_TK
    for r in range(R):
        scores_head(0, 0, r)

    def pair(i, carry):
        for nxt, cur in ((1, 0), (0, 1)):
            for r in range(R):
                scores_head(2 * i + cur + 1, nxt, r)
                softmax_head(2 * i + cur, cur, r)
        return carry

    lax.fori_loop(0, n_full // 2, pair, 0)

    def last_tile(buf):
        kpos = n_full * SLC_TK + lax.broadcasted_iota(jnp.int32, (SLC_TK, 1), 0)
        bias = jnp.where(kpos <= tok, 0.0, NEG_INF)
        for r in range(R):
            softmax_head(n_full, buf, r, bias)

    @pl.when(n_full % 2 == 1)
    def _():
        for r in range(R):
            scores_head(n_full, 1, r)
            softmax_head(n_full - 1, 0, r)
        last_tile(1)

    @pl.when(n_full % 2 == 0)
    def _():
        last_tile(0)

    acc = acc_ref[...]
    o_slc = acc * (1.0 / acc[ONES_ROW:ONES_ROW + 1, :])

    def win_scores(b):
        sb = s + b * WIN_SUB
        tok_b = sb + lax.broadcasted_iota(jnp.int32, (1, WIN_SUB), 1)
        qt_b = jnp.concatenate([qt[:, r * Q_BLOCK + b * WIN_SUB:r * Q_BLOCK + (b + 1) * WIN_SUB]
                                for r in range(R)], axis=1)
        w0 = pl.multiple_of(jnp.maximum(sb - WINDOW, 0), WIN_SUB)
        wpos = w0 + lax.broadcasted_iota(jnp.int32, (WIN_KEYS, 1), 0)
        sw = jnp.dot(kw_ref[pl.ds(w0, WIN_KEYS), :], qt_b, preferred_element_type=F32)
        return w0, sw + _lanes4(jnp.where((wpos <= tok_b) & (wpos > tok_b - WINDOW), 0.0, NEG_INF))

    n_win = Q_BLOCK // WIN_SUB
    win_parts = []
    nxt = win_scores(0)
    for b in range(n_win):
        w0, sw = nxt
        if b + 1 < n_win:
            nxt = win_scores(b + 1)
        pw = jnp.exp2(sw - jnp.max(sw, axis=0, keepdims=True)).astype(BF16)
        ow = jnp.dot(vwt_ref[0:V_ROWS, pl.ds(w0, WIN_KEYS)], pw, preferred_element_type=F32)
        win_parts.append(ow * (1.0 / ow[ONES_ROW:ONES_ROW + 1, :]))

    heads = []
    for r in range(R):
        ls = slice(r * Q_BLOCK, (r + 1) * Q_BLOCK)
        o_win = jnp.concatenate([part[:, r * WIN_SUB:(r + 1) * WIN_SUB] for part in win_parts], axis=1)
        g0 = 3 * (pl.program_id(0) * R + r)
        o = (gate_ref[pl.ds(g0, 1), :] * o_cmp[:, ls] + gate_ref[pl.ds(g0 + 1, 1), :] * o_slc[:, ls]
             + gate_ref[pl.ds(g0 + 2, 1), :] * o_win)
        heads.append(o[0:D_V, :])
    o_ref[...] = jnp.concatenate(heads, axis=0).T.astype(BF16)


def _attention(qt, k_cmp, v_cmp_t, ks_r, vs_t, kw_r, vw_t, gate_t, c2st):
    T = ks_r.shape[0]
    n_cmp = k_cmp.shape[1]
    n_blk = c2st.shape[0]
    n_sel = min(N_SELECT, T // SLC_LEN)
    gw = R * HEAD_PAD
    rows = lambda g, ci: (0, g)
    cols = lambda g, ci: (g, 0)
    return pl.pallas_call(
        functools.partial(_attn_kernel, n_sel=n_sel),
        grid=(N_KV_GROUPS, T // Q_BLOCK),
        in_specs=[pl.BlockSpec((gw, Q_BLOCK), lambda g, ci: (g, ci)),
                  pl.BlockSpec((1, n_cmp, HEAD_PAD), lambda g, ci: (g, 0, 0)),
                  pl.BlockSpec((1, HEAD_PAD, n_cmp), lambda g, ci: (g, 0, 0)),
                  pl.BlockSpec((T, HEAD_PAD), rows), pl.BlockSpec((HEAD_PAD, T), cols),
                  pl.BlockSpec((T, HEAD_PAD), rows), pl.BlockSpec((HEAD_PAD, T), cols),
                  pl.BlockSpec((HEAD_PAD, Q_BLOCK), lambda g, ci: (0, ci)),
                  _const_spec((n_blk, n_cmp))],
        out_specs=pl.BlockSpec((Q_BLOCK, R * D_V), lambda g, ci: (ci, g)),
        out_shape=jax.ShapeDtypeStruct((T, N_HEADS * D_V), BF16),
        scratch_shapes=[pltpu.VMEM((HEAD_PAD, R * Q_BLOCK), BF16), pltpu.VMEM((n_blk, R * Q_BLOCK), BF16),
                        pltpu.VMEM((V_ROWS, R * Q_BLOCK), F32), pltpu.VMEM((1, R * Q_BLOCK), F32),
                        pltpu.VMEM((2, SLC_TK, R * Q_BLOCK), F32), pltpu.VMEM((2, 1, R * Q_BLOCK), F32)],
        compiler_params=_params("arbitrary", "arbitrary"),
        name="nsa_attention",
    )(qt, k_cmp, v_cmp_t, ks_r, vs_t, kw_r, vw_t, gate_t, c2st)


MIX_TM = 256


def _mix_kernel(u_ref, up_ref, gm_ref, o_ref, pw_ref, ps_ref, wpu_ref, wnu_ref, y_ref):
    i = pl.program_id(0)
    tm = u_ref.shape[0]
    u = u_ref[...]
    prev = up_ref[...] * (i > 0).astype(F32)
    ext = jnp.concatenate([prev, u], axis=0)
    t = i * tm + lax.broadcasted_iota(jnp.int32, (tm, 1), 0)
    parts = []
    for g, win in enumerate(POOL_WINDOWS):
        cs = slice(g * POOL_GROUP, (g + 1) * POOL_GROUP)
        run = ext[:, cs]
        k = 1
        while k < win:
            run = run[k:] + run[:-k]
            k *= 2
        lo = POOL_HALO - (win - 1)
        total = run[lo:lo + tm]
        count = jnp.minimum(t + 1, win).astype(F32)
        d = total / count - u[:, cs]
        parts.append(jnp.dot(d.astype(BF16), pw_ref[g], preferred_element_type=F32) * ps_ref[:, cs])
    a = jnp.concatenate(parts, axis=1).astype(BF16)
    ya = jnp.dot(a, wpu_ref[...], preferred_element_type=F32)
    yb = jnp.dot(o_ref[...], wnu_ref[...], preferred_element_type=F32)
    gm = jax.nn.sigmoid(gm_ref[...])
    y_ref[...] = (gm[:, :D_MODEL] * ya + gm[:, D_MODEL:] * yb).astype(BF16)


def _mix(proj, o, pool_w_b, pool_scale, w_pool_up_b, w_nsa_up_p):
    T = proj.shape[0]
    tm = min(MIX_TM, T)
    halo_blocks = tm // POOL_HALO
    return pl.pallas_call(
        _mix_kernel,
        grid=(T // tm,),
        in_specs=[pl.BlockSpec((tm, POOL_WIDTH), lambda i: (i, C_U // POOL_WIDTH)),
                  pl.BlockSpec((POOL_HALO, POOL_WIDTH),
                               lambda i: (jnp.maximum(i * halo_blocks - 1, 0), C_U // POOL_WIDTH)),
                  pl.BlockSpec((tm, 2 * D_MODEL), lambda i: (i, 0)),
                  pl.BlockSpec((tm, N_HEADS * D_V), lambda i: (i, 0)),
                  _const_spec(pool_w_b.shape), _const_spec((1, POOL_WIDTH)),
                  _const_spec(w_pool_up_b.shape), _const_spec(w_nsa_up_p.shape)],
        out_specs=pl.BlockSpec((tm, D_MODEL), lambda i: (i, 0)),
        out_shape=jax.ShapeDtypeStruct((T, D_MODEL), BF16),
        compiler_params=_params("arbitrary"),
        name="pool_merge",
    )(proj, proj, proj, o, pool_w_b, pool_scale, w_pool_up_b, w_nsa_up_p)


OUT_TM = 256


def _rms(y, g):
    return y * lax.rsqrt(jnp.mean(y * y, axis=-1, keepdims=True) + RMS_EPS) * g


def _outproj_kernel(y_ref, w_ref, x_ref, g_ref, gt_ref, o_ref):
    z = jnp.dot(y_ref[...], w_ref[...], preferred_element_type=F32)
    o_ref[...] = x_ref[...] + gt_ref[...] * _rms(z, g_ref[...])


def _outproj(y, w_out_b, x2, g_post, mod):
    T = y.shape[0]
    tm = min(OUT_TM, T)
    row = pl.BlockSpec((tm, D_MODEL), lambda i: (i, 0))
    return pl.pallas_call(
        _outproj_kernel,
        grid=(T // tm,),
        in_specs=[row, _const_spec((D_MODEL, D_MODEL)), row, _const_spec((1, D_MODEL)), _mod_spec(2)],
        out_specs=row,
        out_shape=jax.ShapeDtypeStruct((T, D_MODEL), F32),
        compiler_params=_params("arbitrary"),
        name="out_proj",
    )(y, w_out_b, x2, g_post, mod)


def _pack_table():
    sizes = (POOL_WIDTH, N_HEADS * D_QK, N_KV_GROUPS * D_QK, N_KV_GROUPS * D_V, N_KV_GROUPS * D_QK,
             N_KV_GROUPS * D_V, N_KV_GROUPS * D_QK, N_KV_GROUPS * D_V, 3 * N_HEADS, 2 * D_MODEL)
    u, q, kc, vc, ks, vs, kw, vw, gn, gm = [int(o) for o in np.cumsum((0,) + sizes)[:-1]]
    segments = [(gm, HEAD_PAD, 2 * D_MODEL // HEAD_PAD), (q, D_QK, N_HEADS), (u, HEAD_PAD, POOL_WIDTH // HEAD_PAD),
                (kc, D_QK, N_KV_GROUPS), (ks, D_QK, N_KV_GROUPS), (kw, D_QK, N_KV_GROUPS),
                (vc, D_V, N_KV_GROUPS), (vs, D_V, N_KV_GROUPS), (vw, D_V, N_KV_GROUPS),
                (gn, 3 * N_HEADS, 1)]
    tiles = [(start + i * width, width) for start, width, count in segments for i in range(count)]
    tiles += [(0, 0)] * (N_COLS // HEAD_PAD - len(tiles))
    assert len(tiles) * HEAD_PAD == N_COLS
    return np.asarray(tiles, np.int32)


def _repack_kernel(blk_ref, shift_ref, width_ref, a_ref, b_ref, o_ref):
    j = pl.program_id(0)
    shift = shift_ref[j]
    lane = lax.broadcasted_iota(jnp.int32, (1, HEAD_PAD), 1)
    back = (HEAD_PAD - shift) & (HEAD_PAD - 1)
    merged = jnp.where(lane < HEAD_PAD - shift, pltpu.roll(a_ref[...], back, 1), pltpu.roll(b_ref[...], back, 1))
    o_ref[...] = jnp.where(lane < width_ref[j], merged, 0.0).astype(BF16)


def _pack_w_in(w_in):
    k, n_src = w_in.shape
    table = _pack_table()
    last = (n_src - 1) // HEAD_PAD
    grid_spec = pltpu.PrefetchScalarGridSpec(
        num_scalar_prefetch=3,
        grid=(table.shape[0],),
        in_specs=[pl.BlockSpec((k, HEAD_PAD), lambda j, blk, *_: (0, blk[j])),
                  pl.BlockSpec((k, HEAD_PAD), lambda j, blk, *_: (0, jnp.minimum(blk[j] + 1, last)))],
        out_specs=pl.BlockSpec((k, HEAD_PAD), lambda j, *_: (0, j)),
    )
    return pl.pallas_call(
        _repack_kernel,
        grid_spec=grid_spec,
        out_shape=jax.ShapeDtypeStruct((k, N_COLS), BF16),
        compiler_params=_params("arbitrary"),
        name="repack_w_in",
    )(jnp.asarray(table[:, 0] // HEAD_PAD), jnp.asarray(table[:, 0] % HEAD_PAD), jnp.asarray(table[:, 1]),
      w_in, w_in)


def _slc_from_cmp(n_blk, n_chunk):
    cs = np.arange(n_chunk)[None, :] * CMP_STRIDE
    ss = np.arange(n_blk)[:, None] * SLC_LEN
    ov = np.clip(np.minimum(cs + CMP_LEN, ss + SLC_LEN) - np.maximum(cs, ss), 0, None)
    return jnp.asarray(ov / CMP_LEN, dtype=BF16)


def _token_mixer_block(x2, pos_col, mod, g_pre, g_post, w_in, cmp_pos_k, cmp_k_w1, cmp_k_w2,
                       cmp_pos_v, cmp_v_w1, cmp_v_w2, pool_w, pool_scale, w_pool_up, w_nsa_up, w_out):
    T = x2.shape[0]
    half = jnp.arange(ROT_HALF, dtype=F32)
    inv = jnp.power(jnp.float32(ROPE_THETA), -half * 2.0 / (2 * ROT_HALF))
    invf = jnp.concatenate([inv, inv, jnp.zeros((HEAD_PAD - 2 * ROT_HALF,), F32)])[None, :]

    proj = _proj(x2, g_pre, mod, _pack_w_in(w_in))
    qt, ks_r, kw_r, vs_t, vw_t, gate_t = _prep(proj, pos_col, invf)

    n_chunk = T // CMP_STRIDE
    pos_cmp = jnp.pad(pos_col[CMP_LEN - 1::CMP_STRIDE], ((0, 1), (0, 0)))

    def pad_lanes(a, d):
        return jnp.pad(a, [(0, 0)] * (a.ndim - 1) + [(0, HEAD_PAD - d)])

    def w1_pack(w1, d):
        w1 = w1.reshape(CMP_LEN, d, w1.shape[1])
        return jnp.pad(w1, ((0, 0), (0, HEAD_PAD - d), (0, 0))).astype(BF16)

    k_cmp = _compress(proj, C_KC, pad_lanes(cmp_pos_k, D_QK), w1_pack(cmp_k_w1, D_QK),
                      pad_lanes(cmp_k_w2, D_QK).astype(BF16), pos_cmp, invf, True)
    v_cmp = _compress(proj, C_VC, pad_lanes(cmp_pos_v, D_V), w1_pack(cmp_v_w1, D_V),
                      pad_lanes(cmp_v_w2, D_V).T.astype(BF16), pos_cmp, invf, False)

    n_blk = max(T // SLC_LEN, MASK_ROWS)
    o = _attention(qt, k_cmp, v_cmp, ks_r, vs_t, kw_r, vw_t, gate_t, _slc_from_cmp(n_blk, n_chunk))

    y = _mix(proj, o, pool_w.astype(BF16), pool_scale, w_pool_up.astype(BF16), w_nsa_up.astype(BF16))
    return _outproj(y, w_out.astype(BF16), x2, g_post, mod)


ROUTE_TM = 256
EXPERTS_PER_GROUP = N_EXPERTS // N_EXPERT_GROUPS


def _split3(a):
    hi = a.astype(BF16)
    r1 = a - hi.astype(F32)
    mid = r1.astype(BF16)
    lo = (r1 - mid.astype(F32)).astype(BF16)
    return hi, mid, lo


def _dot_f32_nt(a, b):
    a3, b3 = _split3(a), _split3(b)
    out = None
    for i, j in ((2, 0), (1, 1), (0, 2), (1, 0), (0, 1), (0, 0)):
        term = lax.dot_general(a3[i], b3[j], (((1,), (1,)), ((), ())), preferred_element_type=F32)
        out = term if out is None else out + term
    return out


def _first_row_of(mask, row, n):
    return jnp.min(jnp.where(mask, row, n), axis=0, keepdims=True)


def _route_kernel(x_ref, g_ref, sc_ref, sh_ref, wr_ref, rb_ref,
                  hf_ref, hb_ref, sel_ref, wd_ref, rank_ref, cnt_ref, carry_ref):
    i = pl.program_id(0)
    tm = x_ref.shape[0]

    @pl.when(i == 0)
    def _():
        carry_ref[...] = jnp.zeros_like(carry_ref)

    h = _modulated_norm(x_ref[...], g_ref[...], sc_ref[...], sh_ref[...])
    hf_ref[...] = h
    hb_ref[...] = h.astype(BF16)
    logits = _dot_f32_nt(wr_ref[...], h)
    aff = jax.nn.sigmoid(logits)
    biased = aff + rb_ref[...]
    row = lax.broadcasted_iota(jnp.int32, (N_EXPERTS, 1), 0)
    row_g = lax.broadcasted_iota(jnp.int32, (EXPERTS_PER_GROUP, 1), 0)

    gs = []
    for gi in range(N_EXPERT_GROUPS):
        v = biased[gi * EXPERTS_PER_GROUP:(gi + 1) * EXPERTS_PER_GROUP, :]
        m1 = jnp.max(v, axis=0, keepdims=True)
        i1 = _first_row_of(v == m1, row_g, EXPERTS_PER_GROUP)
        m2 = jnp.max(jnp.where(row_g == i1, REMOVED, v), axis=0, keepdims=True)
        gs.append(m1 + m2)
    keep = []
    for gi in range(N_EXPERT_GROUPS):
        ahead = jnp.zeros((1, tm), jnp.int32)
        for gj in range(N_EXPERT_GROUPS):
            if gj < gi:
                ahead += (gs[gj] >= gs[gi]).astype(jnp.int32)
            elif gj > gi:
                ahead += (gs[gj] > gs[gi]).astype(jnp.int32)
        keep.append(jnp.broadcast_to(ahead < TOPK_GROUPS, (EXPERTS_PER_GROUP, tm)))
    masked = jnp.where(jnp.concatenate(keep, axis=0), biased, NEG_INF)
    sel = jnp.zeros((N_EXPERTS, tm), F32)
    for _ in range(TOP_K):
        best = jnp.max(masked, axis=0, keepdims=True)
        hit = row == _first_row_of(masked == best, row, N_EXPERTS)
        sel = jnp.where(hit, 1.0, sel)
        masked = jnp.where(hit, REMOVED, masked)
    w = aff * sel
    wd_ref[...] = w / jnp.sum(w, axis=0, keepdims=True) * ROUTED_SCALE
    sel_ref[...] = sel

    r_i = lax.broadcasted_iota(jnp.int32, (tm, tm), 0)
    c_i = lax.broadcasted_iota(jnp.int32, (tm, tm), 1)
    before = jnp.where(r_i < c_i, 1.0, 0.0).astype(BF16)
    carry = carry_ref[...]
    rank_ref[...] = jnp.dot(sel.astype(BF16), before, preferred_element_type=F32) + carry
    carry = carry + jnp.sum(sel, axis=1, keepdims=True)
    carry_ref[...] = carry
    cnt_ref[...] = carry


def _route(x1, g_pre, mod, w_router, router_bias):
    T = x1.shape[0]
    tm = min(ROUTE_TM, T)
    row = pl.BlockSpec((tm, D_MODEL), lambda i: (i, 0))
    ecol = pl.BlockSpec((N_EXPERTS, tm), lambda i: (0, i))
    e_shape = jax.ShapeDtypeStruct((N_EXPERTS, T), F32)
    return pl.pallas_call(
        _route_kernel,
        grid=(T // tm,),
        in_specs=[row, _const_spec((1, D_MODEL)), _mod_spec(4), _mod_spec(3),
                  _const_spec((N_EXPERTS, D_MODEL)), _const_spec((N_EXPERTS, 1))],
        out_specs=[row, row, ecol, ecol, ecol, _const_spec((N_EXPERTS, 1))],
        out_shape=[jax.ShapeDtypeStruct((T, D_MODEL), F32), jax.ShapeDtypeStruct((T, D_MODEL), BF16),
                   e_shape, e_shape, e_shape, jax.ShapeDtypeStruct((N_EXPERTS, 1), F32)],
        scratch_shapes=[pltpu.VMEM((N_EXPERTS, 1), F32)],
        compiler_params=_params("arbitrary"),
        name="moe_route",
    )(x1, g_pre, mod, mod, w_router.T, router_bias.reshape(N_EXPERTS, 1))


def _dest_kernel(sel_ref, wd_ref, rank_ref, ps_ref, dest_ref, w8_ref):
    tm = sel_ref.shape[1]
    slot = ps_ref[...] + rank_ref[...]
    wd = wd_ref[...]
    row = lax.broadcasted_iota(jnp.int32, (N_EXPERTS, 1), 0)
    dests, weights = [], []
    rem = sel_ref[...]
    for k in range(TOP_K):
        hit = row == _first_row_of(rem > 0.5, row, N_EXPERTS)
        dests.append(jnp.sum(jnp.where(hit, slot, 0.0), axis=0, keepdims=True))
        weights.append(jnp.sum(jnp.where(hit, wd, 0.0), axis=0, keepdims=True))
        rem = jnp.where(hit, 0.0, rem)
    dest_ref[...] = jnp.concatenate(dests, axis=0).astype(jnp.int32)
    w_pad = jnp.concatenate(weights + [jnp.zeros((HEAD_PAD - TOP_K, tm), F32)], axis=0)
    w8_ref[...] = w_pad.T[:, 0:TOP_K]


def _dest(sel_t, wd_t, rank_t, starts_col):
    T = sel_t.shape[1]
    tm = min(ROUTE_TM, T)
    ecol = pl.BlockSpec((N_EXPERTS, tm), lambda i: (0, i))
    return pl.pallas_call(
        _dest_kernel,
        grid=(T // tm,),
        in_specs=[ecol, ecol, ecol, _const_spec((N_EXPERTS, 1))],
        out_specs=[pl.BlockSpec((TOP_K, tm), lambda i: (0, i)), pl.BlockSpec((tm, TOP_K), lambda i: (i, 0))],
        out_shape=[jax.ShapeDtypeStruct((TOP_K, T), jnp.int32), jax.ShapeDtypeStruct((T, TOP_K), F32)],
        compiler_params=_params("arbitrary"),
        name="moe_dest",
    )(sel_t, wd_t, rank_t, starts_col)


DISPATCH_TM = 128


def _dispatch_kernel(dest_ref, h_ref, hb_ref, wsg_ref, wsu_ref, wsd_ref, xs_hbm, shared_ref, sem):
    tm = h_ref.shape[0]

    def body(t, carry):
        for k in range(TOP_K):
            pltpu.make_async_copy(h_ref.at[pl.ds(t, 1), :], xs_hbm.at[pl.ds(dest_ref[k, t], 1), :], sem).start()
        return carry

    lax.fori_loop(0, tm, body, 0)
    shared_ref[...] = _swiglu(hb_ref[...], wsg_ref[...], wsu_ref[...], wsd_ref[...])
    for k in range(TOP_K):
        pltpu.make_async_copy(h_ref, xs_hbm.at[pl.ds(0, tm), :], sem).wait()


def _dispatch(dest_t, hf, hb, wsg, wsu, wsd, n_slots):
    T = hf.shape[0]
    tm = min(DISPATCH_TM, T)
    row = pl.BlockSpec((tm, D_MODEL), lambda i: (i, 0))
    return pl.pallas_call(
        _dispatch_kernel,
        grid=(T // tm,),
        in_specs=[pl.BlockSpec((TOP_K, tm), lambda i: (0, i), memory_space=pltpu.SMEM), row, row,
                  _const_spec(wsg.shape), _const_spec(wsu.shape), _const_spec(wsd.shape)],
        out_specs=[pl.BlockSpec(memory_space=pl.ANY), row],
        out_shape=[jax.ShapeDtypeStruct((n_slots, D_MODEL), F32), jax.ShapeDtypeStruct((T, D_MODEL), F32)],
        scratch_shapes=[pltpu.SemaphoreType.DMA(())],
        compiler_params=pltpu.CompilerParams(dimension_semantics=("arbitrary",), vmem_limit_bytes=VMEM_LIMIT,
                                             has_side_effects=True),
        name="moe_dispatch",
    )(dest_t, hf, hb, wsg, wsu, wsd)


EXPERT_BLOCK = 256


def _swiglu(x, wg, wu, wd):
    g = jnp.dot(x, wg, preferred_element_type=F32)
    u = jnp.dot(x, wu, preferred_element_type=F32)
    hid = (g * jax.nn.sigmoid(g)) * u
    return jnp.dot(hid.astype(BF16), wd, preferred_element_type=F32)


def _expert_items(counts, n_items):
    ends = jnp.cumsum(counts)
    starts = ends - counts
    first_blk = starts // EXPERT_BLOCK
    last_blk = (ends - 1) // EXPERT_BLOCK
    per_expert = jnp.where(counts > 0, last_blk - first_blk + 1, 0)
    item_end = jnp.cumsum(per_expert)
    n_used = item_end[-1]
    i = jnp.minimum(jnp.arange(n_items, dtype=jnp.int32), jnp.maximum(n_used - 1, 0))
    e = jnp.sum((item_end[None, :] <= i[:, None]).astype(jnp.int32), axis=1)
    e = jnp.minimum(e, N_EXPERTS - 1)
    ids = jnp.arange(N_EXPERTS, dtype=jnp.int32)
    owner = e[:, None] == ids[None, :]
    of_item = lambda per_expert_value: jnp.sum(jnp.where(owner, per_expert_value[None, :], 0), axis=1)
    blk = of_item(first_blk) + i - of_item(item_end - per_expert)
    lo = jnp.clip(of_item(starts) - blk * EXPERT_BLOCK, 0, EXPERT_BLOCK)
    hi = jnp.clip(of_item(ends) - blk * EXPERT_BLOCK, 0, EXPERT_BLOCK)
    live = jnp.arange(n_items, dtype=jnp.int32) < n_used
    hi = jnp.where(live, hi, lo)
    new_expert = jnp.concatenate([jnp.ones((1,), jnp.bool_), e[1:] != e[:-1]])
    new_block = jnp.concatenate([jnp.ones((1,), jnp.bool_), blk[1:] != blk[:-1]])
    as_i32 = lambda a: a.astype(jnp.int32)
    used = counts > 0
    slot_of = (jnp.cumsum(as_i32(used)) - 1) % 2
    later = (ids[None, :] > ids[:, None]) & used[None, :]
    next_of = jnp.min(jnp.where(later, ids[None, :], N_EXPERTS), axis=1)
    next_of = jnp.where(next_of < N_EXPERTS, next_of, -1)
    return (as_i32(blk), as_i32(e), as_i32(lo), as_i32(hi), as_i32(new_expert & live),
            as_i32(new_block & live), as_i32(live), as_i32(of_item(slot_of)), as_i32(of_item(next_of)), starts)


def _expert_kernel(blk_ref, e_ref, lo_ref, hi_ref, newe_ref, newb_ref, live_ref, slot_ref, nexte_ref,
                   x_ref, wg_hbm, wu_hbm, wd_hbm, y_ref, wg32, wu32, wd32, wgb, wub, wdb, sem):
    i = pl.program_id(0)

    def weight_copies(e, slot):
        return (pltpu.make_async_copy(wg_hbm.at[e], wg32.at[slot], sem.at[slot]),
                pltpu.make_async_copy(wu_hbm.at[e], wu32.at[slot], sem.at[slot]),
                pltpu.make_async_copy(wd_hbm.at[e], wd32.at[slot], sem.at[slot]))

    @pl.when(live_ref[i] == 1)
    def _():
        @pl.when(newe_ref[i] == 1)
        def _():
            slot = slot_ref[i]

            @pl.when(i == 0)
            def _():
                for c in weight_copies(e_ref[i], slot):
                    c.start()

            for c in weight_copies(e_ref[i], slot):
                c.wait()
            wgb[...] = wg32[slot].astype(BF16)
            wub[...] = wu32[slot].astype(BF16)
            wdb[...] = wd32[slot].astype(BF16)

            @pl.when(nexte_ref[i] >= 0)
            def _():
                for c in weight_copies(nexte_ref[i], 1 - slot):
                    c.start()

        row = lax.broadcasted_iota(jnp.int32, (EXPERT_BLOCK, 1), 0)
        mine = (row >= lo_ref[i]) & (row < hi_ref[i])
        y = _swiglu(jnp.where(mine, x_ref[...], 0.0).astype(BF16), wgb[...], wub[...], wdb[...])

        @pl.when(newb_ref[i] == 1)
        def _():
            y_ref[...] = y

        @pl.when(newb_ref[i] == 0)
        def _():
            y_ref[...] += y


def _experts(items, xs, w_gate, w_up, w_down):
    n_items = items[0].shape[0]
    by_block = lambda i, blk, *_: (blk[i], 0)
    hbm = pl.BlockSpec(memory_space=pl.ANY)
    grid_spec = pltpu.PrefetchScalarGridSpec(
        num_scalar_prefetch=len(items),
        grid=(n_items,),
        in_specs=[pl.BlockSpec((EXPERT_BLOCK, D_MODEL), by_block), hbm, hbm, hbm],
        out_specs=pl.BlockSpec((EXPERT_BLOCK, D_MODEL), by_block),
        scratch_shapes=[pltpu.VMEM((2, D_MODEL, D_EXPERT), F32), pltpu.VMEM((2, D_MODEL, D_EXPERT), F32),
                        pltpu.VMEM((2, D_EXPERT, D_MODEL), F32),
                        pltpu.VMEM((D_MODEL, D_EXPERT), BF16), pltpu.VMEM((D_MODEL, D_EXPERT), BF16),
                        pltpu.VMEM((D_EXPERT, D_MODEL), BF16), pltpu.SemaphoreType.DMA((2,))],
    )
    return pl.pallas_call(
        _expert_kernel,
        grid_spec=grid_spec,
        out_shape=jax.ShapeDtypeStruct(xs.shape, F32),
        compiler_params=_params("arbitrary"),
        name="moe_experts",
    )(*items, xs, w_gate, w_up, w_down)


COMBINE_TM = 128


def _combine_kernel(dcur_ref, dnext_ref, w8_ref, shared_ref, x_ref, g_ref, gt_ref, y_hbm, o_ref, gbuf, sem):
    i = pl.program_id(0)
    n = pl.num_programs(0)
    tm = x_ref.shape[0]

    rows = TOP_K * tm

    def issue(d_ref, slot):
        base = slot * rows

        def body(t, carry):
            for k in range(TOP_K):
                pltpu.make_async_copy(y_hbm.at[pl.ds(d_ref[k, t], 1), :],
                                      gbuf.at[pl.ds(base + k * tm + t, 1), :], sem.at[slot]).start()
            return carry

        lax.fori_loop(0, tm, body, 0)

    @pl.when(i == 0)
    def _():
        issue(dcur_ref, 0)

    @pl.when(i + 1 < n)
    def _():
        issue(dnext_ref, (i + 1) % 2)

    slot = i % 2
    base = pl.multiple_of(slot * rows, rows)
    for k in range(TOP_K):
        pltpu.make_async_copy(y_hbm.at[pl.ds(0, tm), :], gbuf.at[pl.ds(base + k * tm, tm), :],
                              sem.at[slot]).wait()
    w8 = w8_ref[...]
    y = shared_ref[...]
    for k in range(TOP_K):
        y = y + w8[:, k:k + 1] * gbuf[pl.ds(base + k * tm, tm), :]
    o_ref[...] = x_ref[...] + gt_ref[...] * _rms(y, g_ref[...])


def _combine(dest_t, w8, shared, x1, g_post, mod, y_buf):
    T = x1.shape[0]
    tm = min(COMBINE_TM, T)
    n = T // tm
    row = pl.BlockSpec((tm, D_MODEL), lambda i: (i, 0))
    return pl.pallas_call(
        _combine_kernel,
        grid=(n,),
        in_specs=[pl.BlockSpec((TOP_K, tm), lambda i: (0, i), memory_space=pltpu.SMEM),
                  pl.BlockSpec((TOP_K, tm), lambda i: (0, jnp.minimum(i + 1, n - 1)), memory_space=pltpu.SMEM),
                  pl.BlockSpec((tm, TOP_K), lambda i: (i, 0)), row, row,
                  _const_spec((1, D_MODEL)), _mod_spec(5),
                  pl.BlockSpec(memory_space=pl.ANY)],
        out_specs=row,
        out_shape=jax.ShapeDtypeStruct((T, D_MODEL), F32),
        scratch_shapes=[pltpu.VMEM((2 * TOP_K * tm, D_MODEL), F32), pltpu.SemaphoreType.DMA((2,))],
        compiler_params=_params("arbitrary"),
        name="moe_combine",
    )(dest_t, dest_t, w8, shared, x1, g_post, mod, y_buf)


def _moe_block(x1, mod, g_pre, g_post, w_router, router_bias, w_exp_gate, w_exp_up, w_exp_down,
               w_sh_gate, w_sh_up, w_sh_down):
    T = x1.shape[0]
    hf, hb, sel_t, wd_t, rank_t, counts = _route(x1, g_pre, mod, w_router, router_bias)

    n_slots = T * TOP_K
    *items, starts = _expert_items(counts[:, 0].astype(jnp.int32), n_slots // EXPERT_BLOCK + N_EXPERTS)

    dest_t, w8 = _dest(sel_t, wd_t, rank_t, starts.astype(F32)[:, None])
    xs, shared = _dispatch(dest_t, hf, hb, w_sh_gate.astype(BF16), w_sh_up.astype(BF16),
                           w_sh_down.astype(BF16), n_slots)
    y_buf = _experts(items, xs, w_exp_gate, w_exp_up, w_exp_down)
    return _combine(dest_t, w8, shared, x1, g_post, mod, y_buf)


def kernel(x, c, positions, w_ada, b_ada, g_pre_mix, g_post_mix, g_pre_ffn, g_post_ffn, w_in, cmp_pos_k, cmp_k_w1, cmp_k_w2, cmp_pos_v, cmp_v_w1, cmp_v_w2, pool_w, pool_scale, w_pool_up, w_nsa_up, w_out, w_router, router_bias, w_exp_gate, w_exp_up, w_exp_down, w_sh_gate, w_sh_up, w_sh_down):
    B, T, D = x.shape
    assert B == 1 and D == D_MODEL
    x2 = x.reshape(T, D)
    pos_col = positions.reshape(T, 1)
    for l in range(w_ada.shape[0]):
        mod = _ada(c.reshape(D, 1), w_ada[l], b_ada[l][None, :])
        x2 = _token_mixer_block(x2, pos_col, mod, g_pre_mix[l][None, :], g_post_mix[l][None, :], w_in[l],
                                cmp_pos_k[l], cmp_k_w1[l], cmp_k_w2[l], cmp_pos_v[l], cmp_v_w1[l],
                                cmp_v_w2[l], pool_w[l], pool_scale[l][None, :], w_pool_up[l], w_nsa_up[l],
                                w_out[l])
        x2 = _moe_block(x2, mod, g_pre_ffn[l][None, :], g_post_ffn[l][None, :], w_router[l],
                        router_bias[l][None, :], w_exp_gate[l], w_exp_up[l], w_exp_down[l],
                        w_sh_gate[l], w_sh_up[l], w_sh_down[l])
    return x2.reshape(B, T, D)
```

```python
import functools

import numpy as np
import jax
import jax.numpy as jnp
from jax import lax
from jax.experimental import pallas as pl
from jax.experimental.pallas import tpu as pltpu

F32 = jnp.float32
BF16 = jnp.bfloat16

D_MODEL = 2048
POOL_WIDTH = 1024
POOL_WINDOWS = (2, 4, 8, 16)
POOL_GROUP = 256
POOL_HALO = 16
N_HEADS = 16
N_KV_GROUPS = 4
HEADS_PER_GROUP = 4
D_QK = 96
D_V = 64
HEAD_PAD = 128
ROT_HALF = 12
ROPE_THETA = 500000.0
CMP_LEN = 32
CMP_STRIDE = 16
SLC_LEN = 64
N_SELECT = 16
WINDOW = 512
Q_BLOCK = 512
N_EXPERTS = 64
TOP_K = 8
N_EXPERT_GROUPS = 8
TOPK_GROUPS = 4
D_EXPERT = 512
ROUTED_SCALE = 2.5
RMS_EPS = 1e-6
NEG_INF = -1e30
SEL_FORCE = 1e30
REMOVED = -3e38

VMEM_LIMIT = 56 * 1024 * 1024

C_GM = 0
C_Q = 4096
C_U = 6144
C_KC = 7168
C_KS = 7680
C_KW = 8192
C_VC = 8704
C_VS = 9216
C_VW = 9728
C_GN = 10240
N_COLS = 10752


def _params(*sem):
    return pltpu.CompilerParams(dimension_semantics=sem, vmem_limit_bytes=VMEM_LIMIT)


def _const_spec(shape):
    nd = len(shape)
    return pl.BlockSpec(shape, lambda *_: (0,) * nd)


ADA_TN = 1536
ADA_ROWS = 64


def _ada_kernel(c_ref, w_ref, b_ref, o_ref, s_ref):
    c = c_ref[...]
    s_ref[...] = c * jax.nn.sigmoid(c)

    def body(k, acc):
        r = pl.multiple_of(k * ADA_ROWS, ADA_ROWS)
        for j in range(ADA_ROWS // 8):
            acc = acc + w_ref[pl.ds(r + 8 * j, 8), :] * s_ref[pl.ds(r + 8 * j, 8), :]
        return acc

    acc = lax.fori_loop(0, D_MODEL // ADA_ROWS, body, jnp.zeros((8, ADA_TN), F32))
    o_ref[...] = jnp.sum(acc, axis=0, keepdims=True) + b_ref[...]


def _ada(c_col, w_ada, b_ada):
    n = w_ada.shape[1]
    return pl.pallas_call(
        _ada_kernel,
        grid=(n // ADA_TN,),
        in_specs=[_const_spec((D_MODEL, 1)),
                  pl.BlockSpec((D_MODEL, ADA_TN), lambda j: (0, j)),
                  pl.BlockSpec((1, ADA_TN), lambda j: (0, j))],
        out_specs=pl.BlockSpec((1, ADA_TN), lambda j: (0, j)),
        out_shape=jax.ShapeDtypeStruct((1, n), F32),
        scratch_shapes=[pltpu.VMEM((D_MODEL, 1), F32)],
        compiler_params=_params("arbitrary"),
        name="ada",
    )(c_col, w_ada, b_ada)


def _mod_spec(k):
    return pl.BlockSpec((1, D_MODEL), lambda *_: (0, k))


PROJ_TM = 512
PROJ_TN = 1536


def _modulated_norm(x, g, sc, sh):
    y = x * lax.rsqrt(jnp.mean(x * x, axis=-1, keepdims=True) + RMS_EPS)
    return (y * g) * (1.0 + sc) + sh


def _proj_kernel(x_ref, g_ref, sc_ref, sh_ref, w_ref, o_ref, h_ref):
    @pl.when(pl.program_id(1) == 0)
    def _():
        h_ref[...] = _modulated_norm(x_ref[...], g_ref[...], sc_ref[...], sh_ref[...]).astype(BF16)

    o_ref[...] = jnp.dot(h_ref[...], w_ref[...], preferred_element_type=F32)


def _proj(x2, g_pre, mod, w_in_p):
    T = x2.shape[0]
    tm = min(PROJ_TM, T)
    return pl.pallas_call(
        _proj_kernel,
        grid=(T // tm, N_COLS // PROJ_TN),
        in_specs=[pl.BlockSpec((tm, D_MODEL), lambda i, j: (i, 0)),
                  _const_spec((1, D_MODEL)), _mod_spec(1), _mod_spec(0),
                  pl.BlockSpec((D_MODEL, PROJ_TN), lambda i, j: (0, j))],
        out_specs=pl.BlockSpec((tm, PROJ_TN), lambda i, j: (i, j)),
        out_shape=jax.ShapeDtypeStruct((T, N_COLS), F32),
        scratch_shapes=[pltpu.VMEM((tm, D_MODEL), BF16)],
        compiler_params=_params("arbitrary", "arbitrary"),
        name="in_proj",
    )(x2, g_pre, mod, mod, w_in_p)


def _rope_tables(pos_col, invf):
    ang = pos_col.astype(F32) * invf
    cos = jnp.cos(ang)
    sin = jnp.sin(ang)
    lane = lax.broadcasted_iota(jnp.int32, ang.shape, 1)
    sin_lo = jnp.where(lane < ROT_HALF, -sin, 0.0)
    sin_hi = jnp.where((lane >= ROT_HALF) & (lane < 2 * ROT_HALF), sin, 0.0)
    return cos, sin_lo, sin_hi


def _rope_head(x, tabs):
    cos, sin_lo, sin_hi = tabs
    return (x * cos + pltpu.roll(x, HEAD_PAD - ROT_HALF, 1) * sin_lo
            + pltpu.roll(x, ROT_HALF, 1) * sin_hi)


PREP_TM = 256


LOG2E = 1.4426950408889634
Q_SCALE = (D_QK ** -0.5) * LOG2E
MASK_LANE = D_QK
MASK_ROWS = 16
ONES_ROW = D_V
V_ROWS = 80
MASK_BIG = 1e30


def _with_ones_row(vt):
    row = lax.broadcasted_iota(jnp.int32, (HEAD_PAD, 1), 0)
    return jnp.where(row == ONES_ROW, 1.0, vt)


def _prep_kernel(pos_ref, invf_ref, q_ref, ks_ref, kw_ref, vs_ref, vw_ref, gn_ref,
                 qt_ref, kso_ref, kwo_ref, vst_ref, vwt_ref, gt_ref):
    tm = q_ref.shape[0]
    tabs = _rope_tables(pos_ref[...], invf_ref[...])
    for h in range(N_HEADS):
        sl = slice(h * HEAD_PAD, (h + 1) * HEAD_PAD)
        qt_ref[sl, :] = (_rope_head(q_ref[:, sl], tabs) * Q_SCALE).T.astype(BF16)
    row = pl.program_id(0) * tm + lax.broadcasted_iota(jnp.int32, (tm, 1), 0)
    lane = lax.broadcasted_iota(jnp.int32, (1, HEAD_PAD), 1)
    block_tag = jnp.where(lane == MASK_LANE + ((row >> 6) & (MASK_ROWS - 1)), 1.0, 0.0)
    for g in range(N_KV_GROUPS):
        sl = slice(g * HEAD_PAD, (g + 1) * HEAD_PAD)
        kso_ref[:, sl] = (_rope_head(ks_ref[:, sl], tabs) + block_tag).astype(BF16)
        kwo_ref[:, sl] = _rope_head(kw_ref[:, sl], tabs).astype(BF16)
        vst_ref[sl, :] = _with_ones_row(vs_ref[:, sl].T).astype(BF16)
        vwt_ref[sl, :] = _with_ones_row(vw_ref[:, sl].T).astype(BF16)
    gt_ref[...] = jax.nn.sigmoid(gn_ref[...]).T


def _prep(proj, pos_col, invf):
    T = proj.shape[0]
    tm = min(PREP_TM, T)
    kv = N_KV_GROUPS * HEAD_PAD
    qw = N_HEADS * HEAD_PAD

    def col(width, off):
        return pl.BlockSpec((tm, width), lambda i: (i, off // width))

    rows = lambda width: pl.BlockSpec((tm, width), lambda i: (i, 0))
    cols = lambda height: pl.BlockSpec((height, tm), lambda i: (0, i))
    return pl.pallas_call(
        _prep_kernel,
        grid=(T // tm,),
        in_specs=[pl.BlockSpec((tm, 1), lambda i: (i, 0)), _const_spec((1, HEAD_PAD)),
                  col(qw, C_Q), col(kv, C_KS), col(kv, C_KW), col(kv, C_VS), col(kv, C_VW),
                  col(HEAD_PAD, C_GN)],
        out_specs=[cols(qw), rows(kv), rows(kv), cols(kv), cols(kv), cols(HEAD_PAD)],
        out_shape=[jax.ShapeDtypeStruct((qw, T), BF16), jax.ShapeDtypeStruct((T, kv), BF16),
                   jax.ShapeDtypeStruct((T, kv), BF16), jax.ShapeDtypeStruct((kv, T), BF16),
                   jax.ShapeDtypeStruct((kv, T), BF16), jax.ShapeDtypeStruct((HEAD_PAD, T), F32)],
        compiler_params=_params("arbitrary"),
        name="nsa_prep",
    )(pos_col, invf, proj, proj, proj, proj, proj, proj)


def _cmp_kernel(x_ref, pe_ref, w1_ref, w2_ref, pos_ref, invf_ref, o_ref, *, is_key):
    nchunk = pos_ref.shape[0]
    hid_dim = w1_ref.shape[2]
    first = jnp.zeros((nchunk, hid_dim), F32)
    second = jnp.zeros((nchunk, hid_dim), F32)
    for l in range(CMP_STRIDE):
        rows = x_ref[pl.ds(l, nchunk, stride=CMP_STRIDE), :]
        first += jnp.dot((rows + pe_ref[l:l + 1, :]).astype(BF16), w1_ref[l],
                         preferred_element_type=F32)
        second += jnp.dot((rows + pe_ref[CMP_STRIDE + l:CMP_STRIDE + l + 1, :]).astype(BF16),
                          w1_ref[CMP_STRIDE + l], preferred_element_type=F32)
    second = jnp.concatenate([second[1:], jnp.zeros((1, hid_dim), F32)], axis=0)
    pre = first + second
    hid = pre * jax.nn.sigmoid(pre)
    if is_key:
        out = jnp.dot(hid.astype(BF16), w2_ref[...], preferred_element_type=F32)
        o_ref[0] = _rope_head(out, _rope_tables(pos_ref[...], invf_ref[...])).astype(BF16)
    else:
        out_t = lax.dot_general(w2_ref[...], hid.astype(BF16), (((1,), (1,)), ((), ())),
                                preferred_element_type=F32)
        o_ref[0] = _with_ones_row(out_t).astype(BF16)


def _compress(proj, col_off, pe_p, w1_p, w2_p, pos_cmp, invf, is_key):
    T = proj.shape[0]
    nchunk = T // CMP_STRIDE
    hid = w1_p.shape[2]
    out_block = (1, nchunk, HEAD_PAD) if is_key else (1, HEAD_PAD, nchunk)
    return pl.pallas_call(
        functools.partial(_cmp_kernel, is_key=is_key),
        grid=(N_KV_GROUPS,),
        in_specs=[pl.BlockSpec((T, HEAD_PAD), lambda g: (0, col_off // HEAD_PAD + g)),
                  _const_spec((CMP_LEN, HEAD_PAD)), _const_spec((CMP_LEN, HEAD_PAD, hid)),
                  _const_spec(w2_p.shape), _const_spec((nchunk, 1)), _const_spec((1, HEAD_PAD))],
        out_specs=pl.BlockSpec(out_block, lambda g: (g, 0, 0)),
        out_shape=jax.ShapeDtypeStruct((N_KV_GROUPS,) + out_block[1:], BF16),
        compiler_params=_params("arbitrary"),
        name="compress",
    )(proj, pe_p, w1_p, w2_p, pos_cmp, invf)


SLC_TK = 1024
WIN_SUB = 128
WIN_KEYS = WINDOW + WIN_SUB
R = HEADS_PER_GROUP


def _lanes4(a):
    return jnp.concatenate([a] * R, axis=1)


def _attn_kernel(qt_ref, kc_ref, vct_ref, ks_ref, vst_ref, kw_ref, vwt_ref, gate_ref, c2st_ref, o_ref,
                 qa_ref, neg_ref, acc_ref, m_ref, s_ref, cmax_ref, *, n_sel):
    ci = pl.program_id(1)
    s = ci * Q_BLOCK
    qt = jnp.concatenate([qt_ref[r * HEAD_PAD:(r + 1) * HEAD_PAD, :] for r in range(R)], axis=1)
    tok = s + lax.broadcasted_iota(jnp.int32, (1, Q_BLOCK), 1)

    n_cmp = kc_ref.shape[1]
    cmp_end = lax.broadcasted_iota(jnp.int32, (n_cmp, 1), 0) * CMP_STRIDE + (CMP_LEN - 1)
    cmp_bias = jnp.where(cmp_end <= tok, 0.0, NEG_INF)
    cmp_live = tok >= CMP_LEN - 1

    def cmp_scores(r):
        return jnp.dot(kc_ref[0], qt[:, r * Q_BLOCK:(r + 1) * Q_BLOCK], preferred_element_type=F32) + cmp_bias

    o_cmp_heads = []
    p_sum = None
    sc_next = cmp_scores(0)
    for r in range(R):
        sc = sc_next
        if r + 1 < R:
            sc_next = cmp_scores(r + 1)
        p = jnp.exp2(sc - jnp.max(sc, axis=0, keepdims=True))
        p = p * jnp.where(cmp_live, 1.0 / jnp.sum(p, axis=0, keepdims=True), 0.0)
        o_cmp_heads.append(jnp.dot(vct_ref[0, 0:V_ROWS, :], p.astype(BF16), preferred_element_type=F32))
        p_sum = p if p_sum is None else p_sum + p
    o_cmp = jnp.concatenate(o_cmp_heads, axis=1)
    p_hi = p_sum.astype(BF16)
    p_lo = (p_sum - p_hi.astype(F32)).astype(BF16)
    imp = (jnp.dot(c2st_ref[...], p_hi, preferred_element_type=F32)
           + jnp.dot(c2st_ref[...], p_lo, preferred_element_type=F32))

    n_blk = c2st_ref.shape[0]
    jj = lax.broadcasted_iota(jnp.int32, (n_blk, 1), 0)
    cur = tok >> 6
    forced = (jj == 0) | (jj == cur) | (jj == cur - 1)
    score = jnp.where(jj <= cur, jnp.where(forced, SEL_FORCE, imp), NEG_INF)
    chosen = jnp.zeros((n_blk, Q_BLOCK), jnp.bool_)
    for _ in range(n_sel):
        best = jnp.max(score, axis=0, keepdims=True)
        hit = jj == jnp.min(jnp.where(score == best, jj, n_blk), axis=0, keepdims=True)
        chosen = chosen | (hit & (best > 0.5 * NEG_INF))
        score = jnp.where(hit, REMOVED, score)
    neg_ref[...] = _lanes4(jnp.where(chosen, 0.0, -MASK_BIG)).astype(BF16)

    qa_ref[...] = qt
    acc_ref[...] = jnp.zeros_like(acc_ref)
    m_ref[...] = jnp.full_like(m_ref, NEG_INF)

    def scores_head(jt, buf, r):
        ls = slice(r * Q_BLOCK, (r + 1) * Q_BLOCK)
        k0 = pl.multiple_of(jt * SLC_TK, SLC_TK)
        if r == 0:
            blk0 = pl.multiple_of((k0 // (SLC_LEN * MASK_ROWS)) * MASK_ROWS, MASK_ROWS)
            qa_ref[MASK_LANE:MASK_LANE + MASK_ROWS, :] = neg_ref[pl.ds(blk0, MASK_ROWS), :]
        st = jnp.dot(ks_ref[pl.ds(k0, SLC_TK), :], qa_ref[:, ls], preferred_element_type=F32)
        s_ref[buf, :, ls] = st
        cmax_ref[buf, :, ls] = jnp.max(st, axis=0, keepdims=True)

    def softmax_head(jt, buf, r, bias=None):
        ls = slice(r * Q_BLOCK, (r + 1) * Q_BLOCK)
        k0 = pl.multiple_of(jt * SLC_TK, SLC_TK)
        m_old = m_ref[:, ls]
        if bias is None:
            st = s_ref[buf, :, ls]
            m_new = jnp.maximum(m_old, cmax_ref[buf, :, ls])
        else:
            st = s_ref[buf, :, ls] + bias
            m_new = jnp.maximum(m_old, jnp.max(st, axis=0, keepdims=True))
        pt = jnp.exp2(st - m_new).astype(BF16)
        acc_ref[:, ls] = (jnp.exp2(m_old - m_new) * acc_ref[:, ls]
                          + jnp.dot(vst_ref[0:V_ROWS, pl.ds(k0, SLC_TK)], pt, preferred_element_type=F32))
        m_ref[:, ls] = m_new

    n_full = s // SLC_TK
    for r in range(R):
        scores_head(0, 0, r)

    def pair(i, carry):
        for nxt, cur in ((1, 0), (0, 1)):
            for r in range(R):
                scores_head(2 * i + cur + 1, nxt, r)
                softmax_head(2 * i + cur, cur, r)
        return carry

    lax.fori_loop(0, n_full // 2, pair, 0)

    def last_tile(buf):
        kpos = n_full * SLC_TK + lax.broadcasted_iota(jnp.int32, (SLC_TK, 1), 0)
        bias = jnp.where(kpos <= tok, 0.0, NEG_INF)
        for r in range(R):
            softmax_head(n_full, buf, r, bias)

    @pl.when(n_full % 2 == 1)
    def _():
        for r in range(R):
            scores_head(n_full, 1, r)
            softmax_head(n_full - 1, 0, r)
        last_tile(1)

    @pl.when(n_full % 2 == 0)
    def _():
        last_tile(0)

    acc = acc_ref[...]
    o_slc = acc * (1.0 / acc[ONES_ROW:ONES_ROW + 1, :])

    def win_scores(b):
        sb = s + b * WIN_SUB
        tok_b = sb + lax.broadcasted_iota(jnp.int32, (1, WIN_SUB), 1)
        qt_b = jnp.concatenate([qt[:, r * Q_BLOCK + b * WIN_SUB:r * Q_BLOCK + (b + 1) * WIN_SUB]
                                for r in range(R)], axis=1)
        w0 = pl.multiple_of(jnp.maximum(sb - WINDOW, 0), WIN_SUB)
        wpos = w0 + lax.broadcasted_iota(jnp.int32, (WIN_KEYS, 1), 0)
        sw = jnp.dot(kw_ref[pl.ds(w0, WIN_KEYS), :], qt_b, preferred_element_type=F32)
        return w0, sw + _lanes4(jnp.where((wpos <= tok_b) & (wpos > tok_b - WINDOW), 0.0, NEG_INF))

    n_win = Q_BLOCK // WIN_SUB
    win_parts = []
    nxt = win_scores(0)
    for b in range(n_win):
        w0, sw = nxt
        if b + 1 < n_win:
            nxt = win_scores(b + 1)
        pw = jnp.exp2(sw - jnp.max(sw, axis=0, keepdims=True)).astype(BF16)
        ow = jnp.dot(vwt_ref[0:V_ROWS, pl.ds(w0, WIN_KEYS)], pw, preferred_element_type=F32)
        win_parts.append(ow * (1.0 / ow[ONES_ROW:ONES_ROW + 1, :]))

    heads = []
    for r in range(R):
        ls = slice(r * Q_BLOCK, (r + 1) * Q_BLOCK)
        o_win = jnp.concatenate([part[:, r * WIN_SUB:(r + 1) * WIN_SUB] for part in win_parts], axis=1)
        g0 = 3 * (pl.program_id(0) * R + r)
        o = (gate_ref[pl.ds(g0, 1), :] * o_cmp[:, ls] + gate_ref[pl.ds(g0 + 1, 1), :] * o_slc[:, ls]
             + gate_ref[pl.ds(g0 + 2, 1), :] * o_win)
        heads.append(o[0:D_V, :])
    o_ref[...] = jnp.concatenate(heads, axis=0).T.astype(BF16)


def _attention(qt, k_cmp, v_cmp_t, ks_r, vs_t, kw_r, vw_t, gate_t, c2st):
    T = ks_r.shape[0]
    n_cmp = k_cmp.shape[1]
    n_blk = c2st.shape[0]
    n_sel = min(N_SELECT, T // SLC_LEN)
    gw = R * HEAD_PAD
    rows = lambda g, ci: (0, g)
    cols = lambda g, ci: (g, 0)
    return pl.pallas_call(
        functools.partial(_attn_kernel, n_sel=n_sel),
        grid=(N_KV_GROUPS, T // Q_BLOCK),
        in_specs=[pl.BlockSpec((gw, Q_BLOCK), lambda g, ci: (g, ci)),
                  pl.BlockSpec((1, n_cmp, HEAD_PAD), lambda g, ci: (g, 0, 0)),
                  pl.BlockSpec((1, HEAD_PAD, n_cmp), lambda g, ci: (g, 0, 0)),
                  pl.BlockSpec((T, HEAD_PAD), rows), pl.BlockSpec((HEAD_PAD, T), cols),
                  pl.BlockSpec((T, HEAD_PAD), rows), pl.BlockSpec((HEAD_PAD, T), cols),
                  pl.BlockSpec((HEAD_PAD, Q_BLOCK), lambda g, ci: (0, ci)),
                  _const_spec((n_blk, n_cmp))],
        out_specs=pl.BlockSpec((Q_BLOCK, R * D_V), lambda g, ci: (ci, g)),
        out_shape=jax.ShapeDtypeStruct((T, N_HEADS * D_V), BF16),
        scratch_shapes=[pltpu.VMEM((HEAD_PAD, R * Q_BLOCK), BF16), pltpu.VMEM((n_blk, R * Q_BLOCK), BF16),
                        pltpu.VMEM((V_ROWS, R * Q_BLOCK), F32), pltpu.VMEM((1, R * Q_BLOCK), F32),
                        pltpu.VMEM((2, SLC_TK, R * Q_BLOCK), F32), pltpu.VMEM((2, 1, R * Q_BLOCK), F32)],
        compiler_params=_params("arbitrary", "arbitrary"),
        name="nsa_attention",
    )(qt, k_cmp, v_cmp_t, ks_r, vs_t, kw_r, vw_t, gate_t, c2st)


MIX_TM = 256


def _mix_kernel(u_ref, up_ref, gm_ref, o_ref, pw_ref, ps_ref, wpu_ref, wnu_ref, y_ref):
    i = pl.program_id(0)
    tm = u_ref.shape[0]
    u = u_ref[...]
    prev = up_ref[...] * (i > 0).astype(F32)
    ext = jnp.concatenate([prev, u], axis=0)
    t = i * tm + lax.broadcasted_iota(jnp.int32, (tm, 1), 0)
    parts = []
    for g, win in enumerate(POOL_WINDOWS):
        cs = slice(g * POOL_GROUP, (g + 1) * POOL_GROUP)
        run = ext[:, cs]
        k = 1
        while k < win:
            run = run[k:] + run[:-k]
            k *= 2
        lo = POOL_HALO - (win - 1)
        total = run[lo:lo + tm]
        count = jnp.minimum(t + 1, win).astype(F32)
        d = total / count - u[:, cs]
        parts.append(jnp.dot(d.astype(BF16), pw_ref[g], preferred_element_type=F32) * ps_ref[:, cs])
    a = jnp.concatenate(parts, axis=1).astype(BF16)
    ya = jnp.dot(a, wpu_ref[...], preferred_element_type=F32)
    yb = jnp.dot(o_ref[...], wnu_ref[...], preferred_element_type=F32)
    gm = jax.nn.sigmoid(gm_ref[...])
    y_ref[...] = (gm[:, :D_MODEL] * ya + gm[:, D_MODEL:] * yb).astype(BF16)


def _mix(proj, o, pool_w_b, pool_scale, w_pool_up_b, w_nsa_up_p):
    T = proj.shape[0]
    tm = min(MIX_TM, T)
    halo_blocks = tm // POOL_HALO
    return pl.pallas_call(
        _mix_kernel,
        grid=(T // tm,),
        in_specs=[pl.BlockSpec((tm, POOL_WIDTH), lambda i: (i, C_U // POOL_WIDTH)),
                  pl.BlockSpec((POOL_HALO, POOL_WIDTH),
                               lambda i: (jnp.maximum(i * halo_blocks - 1, 0), C_U // POOL_WIDTH)),
                  pl.BlockSpec((tm, 2 * D_MODEL), lambda i: (i, 0)),
                  pl.BlockSpec((tm, N_HEADS * D_V), lambda i: (i, 0)),
                  _const_spec(pool_w_b.shape), _const_spec((1, POOL_WIDTH)),
                  _const_spec(w_pool_up_b.shape), _const_spec(w_nsa_up_p.shape)],
        out_specs=pl.BlockSpec((tm, D_MODEL), lambda i: (i, 0)),
        out_shape=jax.ShapeDtypeStruct((T, D_MODEL), BF16),
        compiler_params=_params("arbitrary"),
        name="pool_merge",
    )(proj, proj, proj, o, pool_w_b, pool_scale, w_pool_up_b, w_nsa_up_p)


OUT_TM = 256


def _rms(y, g):
    return y * lax.rsqrt(jnp.mean(y * y, axis=-1, keepdims=True) + RMS_EPS) * g


def _outproj_kernel(y_ref, w_ref, x_ref, g_ref, gt_ref, o_ref):
    z = jnp.dot(y_ref[...], w_ref[...], preferred_element_type=F32)
    o_ref[...] = x_ref[...] + gt_ref[...] * _rms(z, g_ref[...])


def _outproj(y, w_out_b, x2, g_post, mod):
    T = y.shape[0]
    tm = min(OUT_TM, T)
    row = pl.BlockSpec((tm, D_MODEL), lambda i: (i, 0))
    return pl.pallas_call(
        _outproj_kernel,
        grid=(T // tm,),
        in_specs=[row, _const_spec((D_MODEL, D_MODEL)), row, _const_spec((1, D_MODEL)), _mod_spec(2)],
        out_specs=row,
        out_shape=jax.ShapeDtypeStruct((T, D_MODEL), F32),
        compiler_params=_params("arbitrary"),
        name="out_proj",
    )(y, w_out_b, x2, g_post, mod)


def _pack_table():
    sizes = (POOL_WIDTH, N_HEADS * D_QK, N_KV_GROUPS * D_QK, N_KV_GROUPS * D_V, N_KV_GROUPS * D_QK,
             N_KV_GROUPS * D_V, N_KV_GROUPS * D_QK, N_KV_GROUPS * D_V, 3 * N_HEADS, 2 * D_MODEL)
    u, q, kc, vc, ks, vs, kw, vw, gn, gm = [int(o) for o in np.cumsum((0,) + sizes)[:-1]]
    segments = [(gm, HEAD_PAD, 2 * D_MODEL // HEAD_PAD), (q, D_QK, N_HEADS), (u, HEAD_PAD, POOL_WIDTH // HEAD_PAD),
                (kc, D_QK, N_KV_GROUPS), (ks, D_QK, N_KV_GROUPS), (kw, D_QK, N_KV_GROUPS),
                (vc, D_V, N_KV_GROUPS), (vs, D_V, N_KV_GROUPS), (vw, D_V, N_KV_GROUPS),
                (gn, 3 * N_HEADS, 1)]
    tiles = [(start + i * width, width) for start, width, count in segments for i in range(count)]
    tiles += [(0, 0)] * (N_COLS // HEAD_PAD - len(tiles))
    assert len(tiles) * HEAD_PAD == N_COLS
    return np.asarray(tiles, np.int32)


def _repack_kernel(blk_ref, shift_ref, width_ref, a_ref, b_ref, o_ref):
    j = pl.program_id(0)
    shift = shift_ref[j]
    lane = lax.broadcasted_iota(jnp.int32, (1, HEAD_PAD), 1)
    back = (HEAD_PAD - shift) & (HEAD_PAD - 1)
    merged = jnp.where(lane < HEAD_PAD - shift, pltpu.roll(a_ref[...], back, 1), pltpu.roll(b_ref[...], back, 1))
    o_ref[...] = jnp.where(lane < width_ref[j], merged, 0.0).astype(BF16)


def _pack_w_in(w_in):
    k, n_src = w_in.shape
    table = _pack_table()
    last = (n_src - 1) // HEAD_PAD
    grid_spec = pltpu.PrefetchScalarGridSpec(
        num_scalar_prefetch=3,
        grid=(table.shape[0],),
        in_specs=[pl.BlockSpec((k, HEAD_PAD), lambda j, blk, *_: (0, blk[j])),
                  pl.BlockSpec((k, HEAD_PAD), lambda j, blk, *_: (0, jnp.minimum(blk[j] + 1, last)))],
        out_specs=pl.BlockSpec((k, HEAD_PAD), lambda j, *_: (0, j)),
    )
    return pl.pallas_call(
        _repack_kernel,
        grid_spec=grid_spec,
        out_shape=jax.ShapeDtypeStruct((k, N_COLS), BF16),
        compiler_params=_params("arbitrary"),
        name="repack_w_in",
    )(jnp.asarray(table[:, 0] // HEAD_PAD), jnp.asarray(table[:, 0] % HEAD_PAD), jnp.asarray(table[:, 1]),
      w_in, w_in)


def _slc_from_cmp(n_blk, n_chunk):
    cs = np.arange(n_chunk)[None, :] * CMP_STRIDE
    ss = np.arange(n_blk)[:, None] * SLC_LEN
    ov = np.clip(np.minimum(cs + CMP_LEN, ss + SLC_LEN) - np.maximum(cs, ss), 0, None)
    return jnp.asarray(ov / CMP_LEN, dtype=BF16)


def _token_mixer_block(x2, pos_col, mod, g_pre, g_post, w_in, cmp_pos_k, cmp_k_w1, cmp_k_w2,
                       cmp_pos_v, cmp_v_w1, cmp_v_w2, pool_w, pool_scale, w_pool_up, w_nsa_up, w_out):
    T = x2.shape[0]
    half = jnp.arange(ROT_HALF, dtype=F32)
    inv = jnp.power(jnp.float32(ROPE_THETA), -half * 2.0 / (2 * ROT_HALF))
    invf = jnp.concatenate([inv, inv, jnp.zeros((HEAD_PAD - 2 * ROT_HALF,), F32)])[None, :]

    proj = _proj(x2, g_pre, mod, _pack_w_in(w_in))
    qt, ks_r, kw_r, vs_t, vw_t, gate_t = _prep(proj, pos_col, invf)

    n_chunk = T // CMP_STRIDE
    pos_cmp = jnp.pad(pos_col[CMP_LEN - 1::CMP_STRIDE], ((0, 1), (0, 0)))

    def pad_lanes(a, d):
        return jnp.pad(a, [(0, 0)] * (a.ndim - 1) + [(0, HEAD_PAD - d)])

    def w1_pack(w1, d):
        w1 = w1.reshape(CMP_LEN, d, w1.shape[1])
        return jnp.pad(w1, ((0, 0), (0, HEAD_PAD - d), (0, 0))).astype(BF16)

    k_cmp = _compress(proj, C_KC, pad_lanes(cmp_pos_k, D_QK), w1_pack(cmp_k_w1, D_QK),
                      pad_lanes(cmp_k_w2, D_QK).astype(BF16), pos_cmp, invf, True)
    v_cmp = _compress(proj, C_VC, pad_lanes(cmp_pos_v, D_V), w1_pack(cmp_v_w1, D_V),
                      pad_lanes(cmp_v_w2, D_V).T.astype(BF16), pos_cmp, invf, False)

    n_blk = max(T // SLC_LEN, MASK_ROWS)
    o = _attention(qt, k_cmp, v_cmp, ks_r, vs_t, kw_r, vw_t, gate_t, _slc_from_cmp(n_blk, n_chunk))

    y = _mix(proj, o, pool_w.astype(BF16), pool_scale, w_pool_up.astype(BF16), w_nsa_up.astype(BF16))
    return _outproj(y, w_out.astype(BF16), x2, g_post, mod)


ROUTE_TM = 256
EXPERTS_PER_GROUP = N_EXPERTS // N_EXPERT_GROUPS


def _split3(a):
    hi = a.astype(BF16)
    r1 = a - hi.astype(F32)
    mid = r1.astype(BF16)
    lo = (r1 - mid.astype(F32)).astype(BF16)
    return hi, mid, lo


def _dot_f32_nt(a, b):
    a3, b3 = _split3(a), _split3(b)
    out = None
    for i, j in ((2, 0), (1, 1), (0, 2), (1, 0), (0, 1), (0, 0)):
        term = lax.dot_general(a3[i], b3[j], (((1,), (1,)), ((), ())), preferred_element_type=F32)
        out = term if out is None else out + term
    return out


def _first_row_of(mask, row, n):
    return jnp.min(jnp.where(mask, row, n), axis=0, keepdims=True)


def _route_kernel(x_ref, g_ref, sc_ref, sh_ref, wr_ref, rb_ref,
                  hf_ref, hb_ref, sel_ref, wd_ref, rank_ref, cnt_ref, carry_ref):
    i = pl.program_id(0)
    tm = x_ref.shape[0]

    @pl.when(i == 0)
    def _():
        carry_ref[...] = jnp.zeros_like(carry_ref)

    h = _modulated_norm(x_ref[...], g_ref[...], sc_ref[...], sh_ref[...])
    hf_ref[...] = h
    hb_ref[...] = h.astype(BF16)
    logits = _dot_f32_nt(wr_ref[...], h)
    aff = jax.nn.sigmoid(logits)
    biased = aff + rb_ref[...]
    row = lax.broadcasted_iota(jnp.int32, (N_EXPERTS, 1), 0)
    row_g = lax.broadcasted_iota(jnp.int32, (EXPERTS_PER_GROUP, 1), 0)

    gs = []
    for gi in range(N_EXPERT_GROUPS):
        v = biased[gi * EXPERTS_PER_GROUP:(gi + 1) * EXPERTS_PER_GROUP, :]
        m1 = jnp.max(v, axis=0, keepdims=True)
        i1 = _first_row_of(v == m1, row_g, EXPERTS_PER_GROUP)
        m2 = jnp.max(jnp.where(row_g == i1, REMOVED, v), axis=0, keepdims=True)
        gs.append(m1 + m2)
    keep = []
    for gi in range(N_EXPERT_GROUPS):
        ahead = jnp.zeros((1, tm), jnp.int32)
        for gj in range(N_EXPERT_GROUPS):
            if gj < gi:
                ahead += (gs[gj] >= gs[gi]).astype(jnp.int32)
            elif gj > gi:
                ahead += (gs[gj] > gs[gi]).astype(jnp.int32)
        keep.append(jnp.broadcast_to(ahead < TOPK_GROUPS, (EXPERTS_PER_GROUP, tm)))
    masked = jnp.where(jnp.concatenate(keep, axis=0), biased, NEG_INF)
    sel = jnp.zeros((N_EXPERTS, tm), F32)
    for _ in range(TOP_K):
        best = jnp.max(masked, axis=0, keepdims=True)
        hit = row == _first_row_of(masked == best, row, N_EXPERTS)
        sel = jnp.where(hit, 1.0, sel)
        masked = jnp.where(hit, REMOVED, masked)
    w = aff * sel
    wd_ref[...] = w / jnp.sum(w, axis=0, keepdims=True) * ROUTED_SCALE
    sel_ref[...] = sel

    r_i = lax.broadcasted_iota(jnp.int32, (tm, tm), 0)
    c_i = lax.broadcasted_iota(jnp.int32, (tm, tm), 1)
    before = jnp.where(r_i < c_i, 1.0, 0.0).astype(BF16)
    carry = carry_ref[...]
    rank_ref[...] = jnp.dot(sel.astype(BF16), before, preferred_element_type=F32) + carry
    carry = carry + jnp.sum(sel, axis=1, keepdims=True)
    carry_ref[...] = carry
    cnt_ref[...] = carry


def _route(x1, g_pre, mod, w_router, router_bias):
    T = x1.shape[0]
    tm = min(ROUTE_TM, T)
    row = pl.BlockSpec((tm, D_MODEL), lambda i: (i, 0))
    ecol = pl.BlockSpec((N_EXPERTS, tm), lambda i: (0, i))
    e_shape = jax.ShapeDtypeStruct((N_EXPERTS, T), F32)
    return pl.pallas_call(
        _route_kernel,
        grid=(T // tm,),
        in_specs=[row, _const_spec((1, D_MODEL)), _mod_spec(4), _mod_spec(3),
                  _const_spec((N_EXPERTS, D_MODEL)), _const_spec((N_EXPERTS, 1))],
        out_specs=[row, row, ecol, ecol, ecol, _const_spec((N_EXPERTS, 1))],
        out_shape=[jax.ShapeDtypeStruct((T, D_MODEL), F32), jax.ShapeDtypeStruct((T, D_MODEL), BF16),
                   e_shape, e_shape, e_shape, jax.ShapeDtypeStruct((N_EXPERTS, 1), F32)],
        scratch_shapes=[pltpu.VMEM((N_EXPERTS, 1), F32)],
        compiler_params=_params("arbitrary"),
        name="moe_route",
    )(x1, g_pre, mod, mod, w_router.T, router_bias.reshape(N_EXPERTS, 1))


def _dest_kernel(sel_ref, wd_ref, rank_ref, ps_ref, dest_ref, w8_ref):
    tm = sel_ref.shape[1]
    slot = ps_ref[...] + rank_ref[...]
    wd = wd_ref[...]
    row = lax.broadcasted_iota(jnp.int32, (N_EXPERTS, 1), 0)
    dests, weights = [], []
    rem = sel_ref[...]
    for k in range(TOP_K):
        hit = row == _first_row_of(rem > 0.5, row, N_EXPERTS)
        dests.append(jnp.sum(jnp.where(hit, slot, 0.0), axis=0, keepdims=True))
        weights.append(jnp.sum(jnp.where(hit, wd, 0.0), axis=0, keepdims=True))
        rem = jnp.where(hit, 0.0, rem)
    dest_ref[...] = jnp.concatenate(dests, axis=0).astype(jnp.int32)
    w_pad = jnp.concatenate(weights + [jnp.zeros((HEAD_PAD - TOP_K, tm), F32)], axis=0)
    w8_ref[...] = w_pad.T[:, 0:TOP_K]


def _dest(sel_t, wd_t, rank_t, starts_col):
    T = sel_t.shape[1]
    tm = min(ROUTE_TM, T)
    ecol = pl.BlockSpec((N_EXPERTS, tm), lambda i: (0, i))
    return pl.pallas_call(
        _dest_kernel,
        grid=(T // tm,),
        in_specs=[ecol, ecol, ecol, _const_spec((N_EXPERTS, 1))],
        out_specs=[pl.BlockSpec((TOP_K, tm), lambda i: (0, i)), pl.BlockSpec((tm, TOP_K), lambda i: (i, 0))],
        out_shape=[jax.ShapeDtypeStruct((TOP_K, T), jnp.int32), jax.ShapeDtypeStruct((T, TOP_K), F32)],
        compiler_params=_params("arbitrary"),
        name="moe_dest",
    )(sel_t, wd_t, rank_t, starts_col)


DISPATCH_TM = 128


def _dispatch_kernel(dest_ref, h_ref, hb_ref, wsg_ref, wsu_ref, wsd_ref, xs_hbm, shared_ref, sem):
    tm = h_ref.shape[0]

    def body(t, carry):
        for k in range(TOP_K):
            pltpu.make_async_copy(h_ref.at[pl.ds(t, 1), :], xs_hbm.at[pl.ds(dest_ref[k, t], 1), :], sem).start()
        return carry

    lax.fori_loop(0, tm, body, 0)
    shared_ref[...] = _swiglu(hb_ref[...], wsg_ref[...], wsu_ref[...], wsd_ref[...])
    for k in range(TOP_K):
        pltpu.make_async_copy(h_ref, xs_hbm.at[pl.ds(0, tm), :], sem).wait()


def _dispatch(dest_t, hf, hb, wsg, wsu, wsd, n_slots):
    T = hf.shape[0]
    tm = min(DISPATCH_TM, T)
    row = pl.BlockSpec((tm, D_MODEL), lambda i: (i, 0))
    return pl.pallas_call(
        _dispatch_kernel,
        grid=(T // tm,),
        in_specs=[pl.BlockSpec((TOP_K, tm), lambda i: (0, i), memory_space=pltpu.SMEM), row, row,
                  _const_spec(wsg.shape), _const_spec(wsu.shape), _const_spec(wsd.shape)],
        out_specs=[pl.BlockSpec(memory_space=pl.ANY), row],
        out_shape=[jax.ShapeDtypeStruct((n_slots, D_MODEL), F32), jax.ShapeDtypeStruct((T, D_MODEL), F32)],
        scratch_shapes=[pltpu.SemaphoreType.DMA(())],
        compiler_params=pltpu.CompilerParams(dimension_semantics=("arbitrary",), vmem_limit_bytes=VMEM_LIMIT,
                                             has_side_effects=True),
        name="moe_dispatch",
    )(dest_t, hf, hb, wsg, wsu, wsd)


EXPERT_BLOCK = 256


def _swiglu(x, wg, wu, wd):
    g = jnp.dot(x, wg, preferred_element_type=F32)
    u = jnp.dot(x, wu, preferred_element_type=F32)
    hid = (g * jax.nn.sigmoid(g)) * u
    return jnp.dot(hid.astype(BF16), wd, preferred_element_type=F32)


def _expert_items(counts, n_items):
    ends = jnp.cumsum(counts)
    starts = ends - counts
    first_blk = starts // EXPERT_BLOCK
    last_blk = (ends - 1) // EXPERT_BLOCK
    per_expert = jnp.where(counts > 0, last_blk - first_blk + 1, 0)
    item_end = jnp.cumsum(per_expert)
    n_used = item_end[-1]
    i = jnp.minimum(jnp.arange(n_items, dtype=jnp.int32), jnp.maximum(n_used - 1, 0))
    e = jnp.sum((item_end[None, :] <= i[:, None]).astype(jnp.int32), axis=1)
    e = jnp.minimum(e, N_EXPERTS - 1)
    ids = jnp.arange(N_EXPERTS, dtype=jnp.int32)
    owner = e[:, None] == ids[None, :]
    of_item = lambda per_expert_value: jnp.sum(jnp.where(owner, per_expert_value[None, :], 0), axis=1)
    blk = of_item(first_blk) + i - of_item(item_end - per_expert)
    lo = jnp.clip(of_item(starts) - blk * EXPERT_BLOCK, 0, EXPERT_BLOCK)
    hi = jnp.clip(of_item(ends) - blk * EXPERT_BLOCK, 0, EXPERT_BLOCK)
    live = jnp.arange(n_items, dtype=jnp.int32) < n_used
    hi = jnp.where(live, hi, lo)
    new_expert = jnp.concatenate([jnp.ones((1,), jnp.bool_), e[1:] != e[:-1]])
    new_block = jnp.concatenate([jnp.ones((1,), jnp.bool_), blk[1:] != blk[:-1]])
    as_i32 = lambda a: a.astype(jnp.int32)
    used = counts > 0
    slot_of = (jnp.cumsum(as_i32(used)) - 1) % 2
    later = (ids[None, :] > ids[:, None]) & used[None, :]
    next_of = jnp.min(jnp.where(later, ids[None, :], N_EXPERTS), axis=1)
    next_of = jnp.where(next_of < N_EXPERTS, next_of, -1)
    return (as_i32(blk), as_i32(e), as_i32(lo), as_i32(hi), as_i32(new_expert & live),
            as_i32(new_block & live), as_i32(live), as_i32(of_item(slot_of)), as_i32(of_item(next_of)), starts)


def _expert_kernel(blk_ref, e_ref, lo_ref, hi_ref, newe_ref, newb_ref, live_ref, slot_ref, nexte_ref,
                   x_ref, wg_hbm, wu_hbm, wd_hbm, y_ref, wg32, wu32, wd32, wgb, wub, wdb, sem):
    i = pl.program_id(0)

    def weight_copies(e, slot):
        return (pltpu.make_async_copy(wg_hbm.at[e], wg32.at[slot], sem.at[slot]),
                pltpu.make_async_copy(wu_hbm.at[e], wu32.at[slot], sem.at[slot]),
                pltpu.make_async_copy(wd_hbm.at[e], wd32.at[slot], sem.at[slot]))

    @pl.when(live_ref[i] == 1)
    def _():
        @pl.when(newe_ref[i] == 1)
        def _():
            slot = slot_ref[i]

            @pl.when(i == 0)
            def _():
                for c in weight_copies(e_ref[i], slot):
                    c.start()

            for c in weight_copies(e_ref[i], slot):
                c.wait()
            wgb[...] = wg32[slot].astype(BF16)
            wub[...] = wu32[slot].astype(BF16)
            wdb[...] = wd32[slot].astype(BF16)

            @pl.when(nexte_ref[i] >= 0)
            def _():
                for c in weight_copies(nexte_ref[i], 1 - slot):
                    c.start()

        row = lax.broadcasted_iota(jnp.int32, (EXPERT_BLOCK, 1), 0)
        mine = (row >= lo_ref[i]) & (row < hi_ref[i])
        y = _swiglu(jnp.where(mine, x_ref[...], 0.0).astype(BF16), wgb[...], wub[...], wdb[...])

        @pl.when(newb_ref[i] == 1)
        def _():
            y_ref[...] = y

        @pl.when(newb_ref[i] == 0)
        def _():
            y_ref[...] += y


def _experts(items, xs, w_gate, w_up, w_down):
    n_items = items[0].shape[0]
    by_block = lambda i, blk, *_: (blk[i], 0)
    hbm = pl.BlockSpec(memory_space=pl.ANY)
    grid_spec = pltpu.PrefetchScalarGridSpec(
        num_scalar_prefetch=len(items),
        grid=(n_items,),
        in_specs=[pl.BlockSpec((EXPERT_BLOCK, D_MODEL), by_block), hbm, hbm, hbm],
        out_specs=pl.BlockSpec((EXPERT_BLOCK, D_MODEL), by_block),
        scratch_shapes=[pltpu.VMEM((2, D_MODEL, D_EXPERT), F32), pltpu.VMEM((2, D_MODEL, D_EXPERT), F32),
                        pltpu.VMEM((2, D_EXPERT, D_MODEL), F32),
                        pltpu.VMEM((D_MODEL, D_EXPERT), BF16), pltpu.VMEM((D_MODEL, D_EXPERT), BF16),
                        pltpu.VMEM((D_EXPERT, D_MODEL), BF16), pltpu.SemaphoreType.DMA((2,))],
    )
    return pl.pallas_call(
        _expert_kernel,
        grid_spec=grid_spec,
        out_shape=jax.ShapeDtypeStruct(xs.shape, F32),
        compiler_params=_params("arbitrary"),
        name="moe_experts",
    )(*items, xs, w_gate, w_up, w_down)


COMBINE_TM = 128


def _combine_kernel(dcur_ref, dnext_ref, w8_ref, shared_ref, x_ref, g_ref, gt_ref, y_hbm, o_ref, gbuf, sem):
    i = pl.program_id(0)
    n = pl.num_programs(0)
    tm = x_ref.shape[0]

    rows = TOP_K * tm

    def issue(d_ref, slot):
        base = slot * rows

        def body(t, carry):
            for k in range(TOP_K):
                pltpu.make_async_copy(y_hbm.at[pl.ds(d_ref[k, t], 1), :],
                                      gbuf.at[pl.ds(base + k * tm + t, 1), :], sem.at[slot]).start()
            return carry

        lax.fori_loop(0, tm, body, 0)

    @pl.when(i == 0)
    def _():
        issue(dcur_ref, 0)

    @pl.when(i + 1 < n)
    def _():
        issue(dnext_ref, (i + 1) % 2)

    slot = i % 2
    base = pl.multiple_of(slot * rows, rows)
    for k in range(TOP_K):
        pltpu.make_async_copy(y_hbm.at[pl.ds(0, tm), :], gbuf.at[pl.ds(base + k * tm, tm), :],
                              sem.at[slot]).wait()
    w8 = w8_ref[...]
    y = shared_ref[...]
    for k in range(TOP_K):
        y = y + w8[:, k:k + 1] * gbuf[pl.ds(base + k * tm, tm), :]
    o_ref[...] = x_ref[...] + gt_ref[...] * _rms(y, g_ref[...])


def _combine(dest_t, w8, shared, x1, g_post, mod, y_buf):
    T = x1.shape[0]
    tm = min(COMBINE_TM, T)
    n = T // tm
    row = pl.BlockSpec((tm, D_MODEL), lambda i: (i, 0))
    return pl.pallas_call(
        _combine_kernel,
        grid=(n,),
        in_specs=[pl.BlockSpec((TOP_K, tm), lambda i: (0, i), memory_space=pltpu.SMEM),
                  pl.BlockSpec((TOP_K, tm), lambda i: (0, jnp.minimum(i + 1, n - 1)), memory_space=pltpu.SMEM),
                  pl.BlockSpec((tm, TOP_K), lambda i: (i, 0)), row, row,
                  _const_spec((1, D_MODEL)), _mod_spec(5),
                  pl.BlockSpec(memory_space=pl.ANY)],
        out_specs=row,
        out_shape=jax.ShapeDtypeStruct((T, D_MODEL), F32),
        scratch_shapes=[pltpu.VMEM((2 * TOP_K * tm, D_MODEL), F32), pltpu.SemaphoreType.DMA((2,))],
        compiler_params=_params("arbitrary"),
        name="moe_combine",
    )(dest_t, dest_t, w8, shared, x1, g_post, mod, y_buf)


def _moe_block(x1, mod, g_pre, g_post, w_router, router_bias, w_exp_gate, w_exp_up, w_exp_down,
               w_sh_gate, w_sh_up, w_sh_down):
    T = x1.shape[0]
    hf, hb, sel_t, wd_t, rank_t, counts = _route(x1, g_pre, mod, w_router, router_bias)

    n_slots = T * TOP_K
    *items, starts = _expert_items(counts[:, 0].astype(jnp.int32), n_slots // EXPERT_BLOCK + N_EXPERTS)

    dest_t, w8 = _dest(sel_t, wd_t, rank_t, starts.astype(F32)[:, None])
    xs, shared = _dispatch(dest_t, hf, hb, w_sh_gate.astype(BF16), w_sh_up.astype(BF16),
                           w_sh_down.astype(BF16), n_slots)
    y_buf = _experts(items, xs, w_exp_gate, w_exp_up, w_exp_down)
    return _combine(dest_t, w8, shared, x1, g_post, mod, y_buf)


def kernel(x, c, positions, w_ada, b_ada, g_pre_mix, g_post_mix, g_pre_ffn, g_post_ffn, w_in, cmp_pos_k, cmp_k_w1, cmp_k_w2, cmp_pos_v, cmp_v_w1, cmp_v_w2, pool_w, pool_scale, w_pool_up, w_nsa_up, w_out, w_router, router_bias, w_exp_gate, w_exp_up, w_exp_down, w_sh_gate, w_sh_up, w_sh_down):
    B, T, D = x.shape
    assert B == 1 and D == D_MODEL
    x2 = x.reshape(T, D)
    pos_col = positions.reshape(T, 1)
    for l in range(w_ada.shape[0]):
        mod = _ada(c.reshape(D, 1), w_ada[l], b_ada[l][None, :])
        x2 = _token_mixer_block(x2, pos_col, mod, g_pre_mix[l][None, :], g_post_mix[l][None, :], w_in[l],
                                cmp_pos_k[l], cmp_k_w1[l], cmp_k_w2[l], cmp_pos_v[l], cmp_v_w1[l],
                                cmp_v_w2[l], pool_w[l], pool_scale[l][None, :], w_pool_up[l], w_nsa_up[l],
                                w_out[l])
        x2 = _moe_block(x2, mod, g_pre_ffn[l][None, :], g_post_ffn[l][None, :], w_router[l],
                        router_bias[l][None, :], w_exp_gate[l], w_exp_up[l], w_exp_down[l],
                        w_sh_gate[l], w_sh_up[l], w_sh_down[l])
    return x2.reshape(B, T, D)
```

```python
import functools

import numpy as np
import jax
import jax.numpy as jnp
from jax import lax
from jax.experimental import pallas as pl
from jax.experimental.pallas import tpu as pltpu

F32 = jnp.float32
BF16 = jnp.bfloat16

D_MODEL = 2048
POOL_WIDTH = 1024
POOL_WINDOWS = (2, 4, 8, 16)
POOL_GROUP = 256
POOL_HALO = 16
N_HEADS = 16
N_KV_GROUPS = 4
HEADS_PER_GROUP = 4
D_QK = 96
D_V = 64
HEAD_PAD = 128
ROT_HALF = 12
ROPE_THETA = 500000.0
CMP_LEN = 32
CMP_STRIDE = 16
SLC_LEN = 64
N_SELECT = 16
WINDOW = 512
Q_BLOCK = 512
N_EXPERTS = 64
TOP_K = 8
N_EXPERT_GROUPS = 8
TOPK_GROUPS = 4
D_EXPERT = 512
ROUTED_SCALE = 2.5
RMS_EPS = 1e-6
NEG_INF = -1e30
SEL_FORCE = 1e30
REMOVED = -3e38

VMEM_LIMIT = 56 * 1024 * 1024

C_GM = 0
C_Q = 4096
C_U = 6144
C_KC = 7168
C_KS = 7680
C_KW = 8192
C_VC = 8704
C_VS = 9216
C_VW = 9728
C_GN = 10240
N_COLS = 10752


def _params(*sem):
    return pltpu.CompilerParams(dimension_semantics=sem, vmem_limit_bytes=VMEM_LIMIT)


def _const_spec(shape):
    nd = len(shape)
    return pl.BlockSpec(shape, lambda *_: (0,) * nd)


ADA_TN = 1536
ADA_ROWS = 64


def _ada_kernel(c_ref, w_ref, b_ref, o_ref, s_ref):
    c = c_ref[...]
    s_ref[...] = c * jax.nn.sigmoid(c)

    def body(k, acc):
        r = pl.multiple_of(k * ADA_ROWS, ADA_ROWS)
        for j in range(ADA_ROWS // 8):
            acc = acc + w_ref[pl.ds(r + 8 * j, 8), :] * s_ref[pl.ds(r + 8 * j, 8), :]
        return acc

    acc = lax.fori_loop(0, D_MODEL // ADA_ROWS, body, jnp.zeros((8, ADA_TN), F32))
    o_ref[...] = jnp.sum(acc, axis=0, keepdims=True) + b_ref[...]


def _ada(c_col, w_ada, b_ada):
    n = w_ada.shape[1]
    return pl.pallas_call(
        _ada_kernel,
        grid=(n // ADA_TN,),
        in_specs=[_const_spec((D_MODEL, 1)),
                  pl.BlockSpec((D_MODEL, ADA_TN), lambda j: (0, j)),
                  pl.BlockSpec((1, ADA_TN), lambda j: (0, j))],
        out_specs=pl.BlockSpec((1, ADA_TN), lambda j: (0, j)),
        out_shape=jax.ShapeDtypeStruct((1, n), F32),
        scratch_shapes=[pltpu.VMEM((D_MODEL, 1), F32)],
        compiler_params=_params("arbitrary"),
        name="ada",
    )(c_col, w_ada, b_ada)


def _mod_spec(k):
    return pl.BlockSpec((1, D_MODEL), lambda *_: (0, k))


PROJ_TM = 512
PROJ_TN = 1536


def _modulated_norm(x, g, sc, sh):
    y = x * lax.rsqrt(jnp.mean(x * x, axis=-1, keepdims=True) + RMS_EPS)
    return (y * g) * (1.0 + sc) + sh


def _proj_kernel(x_ref, g_ref, sc_ref, sh_ref, w_ref, o_ref, h_ref):
    @pl.when(pl.program_id(1) == 0)
    def _():
        h_ref[...] = _modulated_norm(x_ref[...], g_ref[...], sc_ref[...], sh_ref[...]).astype(BF16)

    o_ref[...] = jnp.dot(h_ref[...], w_ref[...], preferred_element_type=F32)


def _proj(x2, g_pre, mod, w_in_p):
    T = x2.shape[0]
    tm = min(PROJ_TM, T)
    return pl.pallas_call(
        _proj_kernel,
        grid=(T // tm, N_COLS // PROJ_TN),
        in_specs=[pl.BlockSpec((tm, D_MODEL), lambda i, j: (i, 0)),
                  _const_spec((1, D_MODEL)), _mod_spec(1), _mod_spec(0),
                  pl.BlockSpec((D_MODEL, PROJ_TN), lambda i, j: (0, j))],
        out_specs=pl.BlockSpec((tm, PROJ_TN), lambda i, j: (i, j)),
        out_shape=jax.ShapeDtypeStruct((T, N_COLS), F32),
        scratch_shapes=[pltpu.VMEM((tm, D_MODEL), BF16)],
        compiler_params=_params("arbitrary", "arbitrary"),
        name="in_proj",
    )(x2, g_pre, mod, mod, w_in_p)


def _rope_tables(pos_col, invf):
    ang = pos_col.astype(F32) * invf
    cos = jnp.cos(ang)
    sin = jnp.sin(ang)
    lane = lax.broadcasted_iota(jnp.int32, ang.shape, 1)
    sin_lo = jnp.where(lane < ROT_HALF, -sin, 0.0)
    sin_hi = jnp.where((lane >= ROT_HALF) & (lane < 2 * ROT_HALF), sin, 0.0)
    return cos, sin_lo, sin_hi


def _rope_head(x, tabs):
    cos, sin_lo, sin_hi = tabs
    return (x * cos + pltpu.roll(x, HEAD_PAD - ROT_HALF, 1) * sin_lo
            + pltpu.roll(x, ROT_HALF, 1) * sin_hi)


PREP_TM = 256


LOG2E = 1.4426950408889634
Q_SCALE = (D_QK ** -0.5) * LOG2E
MASK_LANE = D_QK
MASK_ROWS = 16
ONES_ROW = D_V
V_ROWS = 80
MASK_BIG = 1e30


def _with_ones_row(vt):
    row = lax.broadcasted_iota(jnp.int32, (HEAD_PAD, 1), 0)
    return jnp.where(row == ONES_ROW, 1.0, vt)


def _prep_kernel(pos_ref, invf_ref, q_ref, ks_ref, kw_ref, vs_ref, vw_ref, gn_ref,
                 qt_ref, kso_ref, kwo_ref, vst_ref, vwt_ref, gt_ref):
    tm = q_ref.shape[0]
    tabs = _rope_tables(pos_ref[...], invf_ref[...])
    for h in range(N_HEADS):
        sl = slice(h * HEAD_PAD, (h + 1) * HEAD_PAD)
        qt_ref[sl, :] = (_rope_head(q_ref[:, sl], tabs) * Q_SCALE).T.astype(BF16)
    row = pl.program_id(0) * tm + lax.broadcasted_iota(jnp.int32, (tm, 1), 0)
    lane = lax.broadcasted_iota(jnp.int32, (1, HEAD_PAD), 1)
    block_tag = jnp.where(lane == MASK_LANE + ((row >> 6) & (MASK_ROWS - 1)), 1.0, 0.0)
    for g in range(N_KV_GROUPS):
        sl = slice(g * HEAD_PAD, (g + 1) * HEAD_PAD)
        kso_ref[:, sl] = (_rope_head(ks_ref[:, sl], tabs) + block_tag).astype(BF16)
        kwo_ref[:, sl] = _rope_head(kw_ref[:, sl], tabs).astype(BF16)
        vst_ref[sl, :] = _with_ones_row(vs_ref[:, sl].T).astype(BF16)
        vwt_ref[sl, :] = _with_ones_row(vw_ref[:, sl].T).astype(BF16)
    gt_ref[...] = jax.nn.sigmoid(gn_ref[...]).T


def _prep(proj, pos_col, invf):
    T = proj.shape[0]
    tm = min(PREP_TM, T)
    kv = N_KV_GROUPS * HEAD_PAD
    qw = N_HEADS * HEAD_PAD

    def col(width, off):
        return pl.BlockSpec((tm, width), lambda i: (i, off // width))

    rows = lambda width: pl.BlockSpec((tm, width), lambda i: (i, 0))
    cols = lambda height: pl.BlockSpec((height, tm), lambda i: (0, i))
    return pl.pallas_call(
        _prep_kernel,
        grid=(T // tm,),
        in_specs=[pl.BlockSpec((tm, 1), lambda i: (i, 0)), _const_spec((1, HEAD_PAD)),
                  col(qw, C_Q), col(kv, C_KS), col(kv, C_KW), col(kv, C_VS), col(kv, C_VW),
                  col(HEAD_PAD, C_GN)],
        out_specs=[cols(qw), rows(kv), rows(kv), cols(kv), cols(kv), cols(HEAD_PAD)],
        out_shape=[jax.ShapeDtypeStruct((qw, T), BF16), jax.ShapeDtypeStruct((T, kv), BF16),
                   jax.ShapeDtypeStruct((T, kv), BF16), jax.ShapeDtypeStruct((kv, T), BF16),
                   jax.ShapeDtypeStruct((kv, T), BF16), jax.ShapeDtypeStruct((HEAD_PAD, T), F32)],
        compiler_params=_params("arbitrary"),
        name="nsa_prep",
    )(pos_col, invf, proj, proj, proj, proj, proj, proj)


def _cmp_kernel(x_ref, pe_ref, w1_ref, w2_ref, pos_ref, invf_ref, o_ref, *, is_key):
    nchunk = pos_ref.shape[0]
    hid_dim = w1_ref.shape[2]
    first = jnp.zeros((nchunk, hid_dim), F32)
    second = jnp.zeros((nchunk, hid_dim), F32)
    for l in range(CMP_STRIDE):
        rows = x_ref[pl.ds(l, nchunk, stride=CMP_STRIDE), :]
        first += jnp.dot((rows + pe_ref[l:l + 1, :]).astype(BF16), w1_ref[l],
                         preferred_element_type=F32)
        second += jnp.dot((rows + pe_ref[CMP_STRIDE + l:CMP_STRIDE + l + 1, :]).astype(BF16),
                          w1_ref[CMP_STRIDE + l], preferred_element_type=F32)
    second = jnp.concatenate([second[1:], jnp.zeros((1, hid_dim), F32)], axis=0)
    pre = first + second
    hid = pre * jax.nn.sigmoid(pre)
    if is_key:
        out = jnp.dot(hid.astype(BF16), w2_ref[...], preferred_element_type=F32)
        o_ref[0] = _rope_head(out, _rope_tables(pos_ref[...], invf_ref[...])).astype(BF16)
    else:
        out_t = lax.dot_general(w2_ref[...], hid.astype(BF16), (((1,), (1,)), ((), ())),
                                preferred_element_type=F32)
        o_ref[0] = _with_ones_row(out_t).astype(BF16)


def _compress(proj, col_off, pe_p, w1_p, w2_p, pos_cmp, invf, is_key):
    T = proj.shape[0]
    nchunk = T // CMP_STRIDE
    hid = w1_p.shape[2]
    out_block = (1, nchunk, HEAD_PAD) if is_key else (1, HEAD_PAD, nchunk)
    return pl.pallas_call(
        functools.partial(_cmp_kernel, is_key=is_key),
        grid=(N_KV_GROUPS,),
        in_specs=[pl.BlockSpec((T, HEAD_PAD), lambda g: (0, col_off // HEAD_PAD + g)),
                  _const_spec((CMP_LEN, HEAD_PAD)), _const_spec((CMP_LEN, HEAD_PAD, hid)),
                  _const_spec(w2_p.shape), _const_spec((nchunk, 1)), _const_spec((1, HEAD_PAD))],
        out_specs=pl.BlockSpec(out_block, lambda g: (g, 0, 0)),
        out_shape=jax.ShapeDtypeStruct((N_KV_GROUPS,) + out_block[1:], BF16),
        compiler_params=_params("arbitrary"),
        name="compress",
    )(proj, pe_p, w1_p, w2_p, pos_cmp, invf)


SLC_TK = 1024
WIN_SUB = 128
WIN_KEYS = WINDOW + WIN_SUB
R = HEADS_PER_GROUP


def _lanes4(a):
    return jnp.concatenate([a] * R, axis=1)


def _attn_kernel(qt_ref, kc_ref, vct_ref, ks_ref, vst_ref, kw_ref, vwt_ref, gate_ref, c2st_ref, o_ref,
                 qa_ref, neg_ref, acc_ref, m_ref, s_ref, cmax_ref, *, n_sel):
    ci = pl.program_id(1)
    s = ci * Q_BLOCK
    qt = jnp.concatenate([qt_ref[r * HEAD_PAD:(r + 1) * HEAD_PAD, :] for r in range(R)], axis=1)
    tok = s + lax.broadcasted_iota(jnp.int32, (1, Q_BLOCK), 1)

    n_cmp = kc_ref.shape[1]
    cmp_end = lax.broadcasted_iota(jnp.int32, (n_cmp, 1), 0) * CMP_STRIDE + (CMP_LEN - 1)
    cmp_bias = jnp.where(cmp_end <= tok, 0.0, NEG_INF)
    cmp_live = tok >= CMP_LEN - 1

    def cmp_scores(r):
        return jnp.dot(kc_ref[0], qt[:, r * Q_BLOCK:(r + 1) * Q_BLOCK], preferred_element_type=F32) + cmp_bias

    o_cmp_heads = []
    p_sum = None
    sc_next = cmp_scores(0)
    for r in range(R):
        sc = sc_next
        if r + 1 < R:
            sc_next = cmp_scores(r + 1)
        p = jnp.exp2(sc - jnp.max(sc, axis=0, keepdims=True))
        p = p * jnp.where(cmp_live, 1.0 / jnp.sum(p, axis=0, keepdims=True), 0.0)
        o_cmp_heads.append(jnp.dot(vct_ref[0, 0:V_ROWS, :], p.astype(BF16), preferred_element_type=F32))
        p_sum = p if p_sum is None else p_sum + p
    o_cmp = jnp.concatenate(o_cmp_heads, axis=1)
    p_hi = p_sum.astype(BF16)
    p_lo = (p_sum - p_hi.astype(F32)).astype(BF16)
    imp = (jnp.dot(c2st_ref[...], p_hi, preferred_element_type=F32)
           + jnp.dot(c2st_ref[...], p_lo, preferred_element_type=F32))

    n_blk = c2st_ref.shape[0]
    jj = lax.broadcasted_iota(jnp.int32, (n_blk, 1), 0)
    cur = tok >> 6
    forced = (jj == 0) | (jj == cur) | (jj == cur - 1)
    score = jnp.where(jj <= cur, jnp.where(forced, SEL_FORCE, imp), NEG_INF)
    chosen = jnp.zeros((n_blk, Q_BLOCK), jnp.bool_)
    for _ in range(n_sel):
        best = jnp.max(score, axis=0, keepdims=True)
        hit = jj == jnp.min(jnp.where(score == best, jj, n_blk), axis=0, keepdims=True)
        chosen = chosen | (hit & (best > 0.5 * NEG_INF))
        score = jnp.where(hit, REMOVED, score)
    neg_ref[...] = _lanes4(jnp.where(chosen, 0.0, -MASK_BIG)).astype(BF16)

    qa_ref[...] = qt
    acc_ref[...] = jnp.zeros_like(acc_ref)
    m_ref[...] = jnp.full_like(m_ref, NEG_INF)

    def scores_head(jt, buf, r):
        ls = slice(r * Q_BLOCK, (r + 1) * Q_BLOCK)
        k0 = pl.multiple_of(jt * SLC_TK, SLC_TK)
        if r == 0:
            blk0 = pl.multiple_of((k0 // (SLC_LEN * MASK_ROWS)) * MASK_ROWS, MASK_ROWS)
            qa_ref[MASK_LANE:MASK_LANE + MASK_ROWS, :] = neg_ref[pl.ds(blk0, MASK_ROWS), :]
        st = jnp.dot(ks_ref[pl.ds(k0, SLC_TK), :], qa_ref[:, ls], preferred_element_type=F32)
        s_ref[buf, :, ls] = st
        cmax_ref[buf, :, ls] = jnp.max(st, axis=0, keepdims=True)

    def softmax_head(jt, buf, r, bias=None):
        ls = slice(r * Q_BLOCK, (r + 1) * Q_BLOCK)
        k0 = pl.multiple_of(jt * SLC_TK, SLC_TK)
        m_old = m_ref[:, ls]
        if bias is None:
            st = s_ref[buf, :, ls]
            m_new = jnp.maximum(m_old, cmax_ref[buf, :, ls])
        else:
            st = s_ref[buf, :, ls] + bias
            m_new = jnp.maximum(m_old, jnp.max(st, axis=0, keepdims=True))
        pt = jnp.exp2(st - m_new).astype(BF16)
        acc_ref[:, ls] = (jnp.exp2(m_old - m_new) * acc_ref[:, ls]
                          + jnp.dot(vst_ref[0:V_ROWS, pl.ds(k0, SLC_TK)], pt, preferred_element_type=F32))
        m_ref[:, ls] = m_new

    n_full = s // SLC_TK
    for r in range(R):
        scores_head(0, 0, r)

    def pair(i, carry):
        for nxt, cur in ((1, 0), (0, 1)):
            for r in range(R):
                scores_head(2 * i + cur + 1, nxt, r)
                softmax_head(2 * i + cur, cur, r)
        return carry

    lax.fori_loop(0, n_full // 2, pair, 0)

    def last_tile(buf):
        kpos = n_full * SLC_TK + lax.broadcasted_iota(jnp.int32, (SLC_TK, 1), 0)
        bias = jnp.where(kpos <= tok, 0.0, NEG_INF)
        for r in range(R):
            softmax_head(n_full, buf, r, bias)

    @pl.when(n_full % 2 == 1)
    def _():
        for r in range(R):
            scores_head(n_full, 1, r)
            softmax_head(n_full - 1, 0, r)
        last_tile(1)

    @pl.when(n_full % 2 == 0)
    def _():
        last_tile(0)

    acc = acc_ref[...]
    o_slc = acc * (1.0 / acc[ONES_ROW:ONES_ROW + 1, :])

    def win_scores(b):
        sb = s + b * WIN_SUB
        tok_b = sb + lax.broadcasted_iota(jnp.int32, (1, WIN_SUB), 1)
        qt_b = jnp.concatenate([qt[:, r * Q_BLOCK + b * WIN_SUB:r * Q_BLOCK + (b + 1) * WIN_SUB]
                                for r in range(R)], axis=1)
        w0 = pl.multiple_of(jnp.maximum(sb - WINDOW, 0), WIN_SUB)
        wpos = w0 + lax.broadcasted_iota(jnp.int32, (WIN_KEYS, 1), 0)
        sw = jnp.dot(kw_ref[pl.ds(w0, WIN_KEYS), :], qt_b, preferred_element_type=F32)
        return w0, sw + _lanes4(jnp.where((wpos <= tok_b) & (wpos > tok_b - WINDOW), 0.0, NEG_INF))

    n_win = Q_BLOCK // WIN_SUB
    win_parts = []
    nxt = win_scores(0)
    for b in range(n_win):
        w0, sw = nxt
        if b + 1 < n_win:
            nxt = win_scores(b + 1)
        pw = jnp.exp2(sw - jnp.max(sw, axis=0, keepdims=True)).astype(BF16)
        ow = jnp.dot(vwt_ref[0:V_ROWS, pl.ds(w0, WIN_KEYS)], pw, preferred_element_type=F32)
        win_parts.append(ow * (1.0 / ow[ONES_ROW:ONES_ROW + 1, :]))

    heads = []
    for r in range(R):
        ls = slice(r * Q_BLOCK, (r + 1) * Q_BLOCK)
        o_win = jnp.concatenate([part[:, r * WIN_SUB:(r + 1) * WIN_SUB] for part in win_parts], axis=1)
        g0 = 3 * (pl.program_id(0) * R + r)
        o = (gate_ref[pl.ds(g0, 1), :] * o_cmp[:, ls] + gate_ref[pl.ds(g0 + 1, 1), :] * o_slc[:, ls]
             + gate_ref[pl.ds(g0 + 2, 1), :] * o_win)
        heads.append(o[0:D_V, :])
    o_ref[...] = jnp.concatenate(heads, axis=0).T.astype(BF16)


def _attention(qt, k_cmp, v_cmp_t, ks_r, vs_t, kw_r, vw_t, gate_t, c2st):
    T = ks_r.shape[0]
    n_cmp = k_cmp.shape[1]
    n_blk = c2st.shape[0]
    n_sel = min(N_SELECT, T // SLC_LEN)
    gw = R * HEAD_PAD
    rows = lambda g, ci: (0, g)
    cols = lambda g, ci: (g, 0)
    return pl.pallas_call(
        functools.partial(_attn_kernel, n_sel=n_sel),
        grid=(N_KV_GROUPS, T // Q_BLOCK),
        in_specs=[pl.BlockSpec((gw, Q_BLOCK), lambda g, ci: (g, ci)),
                  pl.BlockSpec((1, n_cmp, HEAD_PAD), lambda g, ci: (g, 0, 0)),
                  pl.BlockSpec((1, HEAD_PAD, n_cmp), lambda g, ci: (g, 0, 0)),
                  pl.BlockSpec((T, HEAD_PAD), rows), pl.BlockSpec((HEAD_PAD, T), cols),
                  pl.BlockSpec((T, HEAD_PAD), rows), pl.BlockSpec((HEAD_PAD, T), cols),
                  pl.BlockSpec((HEAD_PAD, Q_BLOCK), lambda g, ci: (0, ci)),
                  _const_spec((n_blk, n_cmp))],
        out_specs=pl.BlockSpec((Q_BLOCK, R * D_V), lambda g, ci: (ci, g)),
        out_shape=jax.ShapeDtypeStruct((T, N_HEADS * D_V), BF16),
        scratch_shapes=[pltpu.VMEM((HEAD_PAD, R * Q_BLOCK), BF16), pltpu.VMEM((n_blk, R * Q_BLOCK), BF16),
                        pltpu.VMEM((V_ROWS, R * Q_BLOCK), F32), pltpu.VMEM((1, R * Q_BLOCK), F32),
                        pltpu.VMEM((2, SLC_TK, R * Q_BLOCK), F32), pltpu.VMEM((2, 1, R * Q_BLOCK), F32)],
        compiler_params=_params("arbitrary", "arbitrary"),
        name="nsa_attention",
    )(qt, k_cmp, v_cmp_t, ks_r, vs_t, kw_r, vw_t, gate_t, c2st)


MIX_TM = 256


def _mix_kernel(u_ref, up_ref, gm_ref, o_ref, pw_ref, ps_ref, wpu_ref, wnu_ref, y_ref):
    i = pl.program_id(0)
    tm = u_ref.shape[0]
    u = u_ref[...]
    prev = up_ref[...] * (i > 0).astype(F32)
    ext = jnp.concatenate([prev, u], axis=0)
    t = i * tm + lax.broadcasted_iota(jnp.int32, (tm, 1), 0)
    parts = []
    for g, win in enumerate(POOL_WINDOWS):
        cs = slice(g * POOL_GROUP, (g + 1) * POOL_GROUP)
        run = ext[:, cs]
        k = 1
        while k < win:
            run = run[k:] + run[:-k]
            k *= 2
        lo = POOL_HALO - (win - 1)
        total = run[lo:lo + tm]
        count = jnp.minimum(t + 1, win).astype(F32)
        d = total / count - u[:, cs]
        parts.append(jnp.dot(d.astype(BF16), pw_ref[g], preferred_element_type=F32) * ps_ref[:, cs])
    a = jnp.concatenate(parts, axis=1).astype(BF16)
    ya = jnp.dot(a, wpu_ref[...], preferred_element_type=F32)
    yb = jnp.dot(o_ref[...], wnu_ref[...], preferred_element_type=F32)
    gm = jax.nn.sigmoid(gm_ref[...])
    y_ref[...] = (gm[:, :D_MODEL] * ya + gm[:, D_MODEL:] * yb).astype(BF16)


def _mix(proj, o, pool_w_b, pool_scale, w_pool_up_b, w_nsa_up_p):
    T = proj.shape[0]
    tm = min(MIX_TM, T)
    halo_blocks = tm // POOL_HALO
    return pl.pallas_call(
        _mix_kernel,
        grid=(T // tm,),
        in_specs=[pl.BlockSpec((tm, POOL_WIDTH), lambda i: (i, C_U // POOL_WIDTH)),
                  pl.BlockSpec((POOL_HALO, POOL_WIDTH),
                               lambda i: (jnp.maximum(i * halo_blocks - 1, 0), C_U // POOL_WIDTH)),
                  pl.BlockSpec((tm, 2 * D_MODEL), lambda i: (i, 0)),
                  pl.BlockSpec((tm, N_HEADS * D_V), lambda i: (i, 0)),
                  _const_spec(pool_w_b.shape), _const_spec((1, POOL_WIDTH)),
                  _const_spec(w_pool_up_b.shape), _const_spec(w_nsa_up_p.shape)],
        out_specs=pl.BlockSpec((tm, D_MODEL), lambda i: (i, 0)),
        out_shape=jax.ShapeDtypeStruct((T, D_MODEL), BF16),
        compiler_params=_params("arbitrary"),
        name="pool_merge",
    )(proj, proj, proj, o, pool_w_b, pool_scale, w_pool_up_b, w_nsa_up_p)


OUT_TM = 256


def _rms(y, g):
    return y * lax.rsqrt(jnp.mean(y * y, axis=-1, keepdims=True) + RMS_EPS) * g


def _outproj_kernel(y_ref, w_ref, x_ref, g_ref, gt_ref, o_ref):
    z = jnp.dot(y_ref[...], w_ref[...], preferred_element_type=F32)
    o_ref[...] = x_ref[...] + gt_ref[...] * _rms(z, g_ref[...])


def _outproj(y, w_out_b, x2, g_post, mod):
    T = y.shape[0]
    tm = min(OUT_TM, T)
    row = pl.BlockSpec((tm, D_MODEL), lambda i: (i, 0))
    return pl.pallas_call(
        _outproj_kernel,
        grid=(T // tm,),
        in_specs=[row, _const_spec((D_MODEL, D_MODEL)), row, _const_spec((1, D_MODEL)), _mod_spec(2)],
        out_specs=row,
        out_shape=jax.ShapeDtypeStruct((T, D_MODEL), F32),
        compiler_params=_params("arbitrary"),
        name="out_proj",
    )(y, w_out_b, x2, g_post, mod)


def _pack_table():
    sizes = (POOL_WIDTH, N_HEADS * D_QK, N_KV_GROUPS * D_QK, N_KV_GROUPS * D_V, N_KV_GROUPS * D_QK,
             N_KV_GROUPS * D_V, N_KV_GROUPS * D_QK, N_KV_GROUPS * D_V, 3 * N_HEADS, 2 * D_MODEL)
    u, q, kc, vc, ks, vs, kw, vw, gn, gm = [int(o) for o in np.cumsum((0,) + sizes)[:-1]]
    segments = [(gm, HEAD_PAD, 2 * D_MODEL // HEAD_PAD), (q, D_QK, N_HEADS), (u, HEAD_PAD, POOL_WIDTH // HEAD_PAD),
                (kc, D_QK, N_KV_GROUPS), (ks, D_QK, N_KV_GROUPS), (kw, D_QK, N_KV_GROUPS),
                (vc, D_V, N_KV_GROUPS), (vs, D_V, N_KV_GROUPS), (vw, D_V, N_KV_GROUPS),
                (gn, 3 * N_HEADS, 1)]
    tiles = [(start + i * width, width) for start, width, count in segments for i in range(count)]
    tiles += [(0, 0)] * (N_COLS // HEAD_PAD - len(tiles))
    assert len(tiles) * HEAD_PAD == N_COLS
    return np.asarray(tiles, np.int32)


def _repack_kernel(blk_ref, shift_ref, width_ref, a_ref, b_ref, o_ref):
    j = pl.program_id(0)
    shift = shift_ref[j]
    lane = lax.broadcasted_iota(jnp.int32, (1, HEAD_PAD), 1)
    back = (HEAD_PAD - shift) & (HEAD_PAD - 1)
    merged = jnp.where(lane < HEAD_PAD - shift, pltpu.roll(a_ref[...], back, 1), pltpu.roll(b_ref[...], back, 1))
    o_ref[...] = jnp.where(lane < width_ref[j], merged, 0.0).astype(BF16)


def _pack_w_in(w_in):
    k, n_src = w_in.shape
    table = _pack_table()
    last = (n_src - 1) // HEAD_PAD
    grid_spec = pltpu.PrefetchScalarGridSpec(
        num_scalar_prefetch=3,
        grid=(table.shape[0],),
        in_specs=[pl.BlockSpec((k, HEAD_PAD), lambda j, blk, *_: (0, blk[j])),
                  pl.BlockSpec((k, HEAD_PAD), lambda j, blk, *_: (0, jnp.minimum(blk[j] + 1, last)))],
        out_specs=pl.BlockSpec((k, HEAD_PAD), lambda j, *_: (0, j)),
    )
    return pl.pallas_call(
        _repack_kernel,
        grid_spec=grid_spec,
        out_shape=jax.ShapeDtypeStruct((k, N_COLS), BF16),
        compiler_params=_params("arbitrary"),
        name="repack_w_in",
    )(jnp.asarray(table[:, 0] // HEAD_PAD), jnp.asarray(table[:, 0] % HEAD_PAD), jnp.asarray(table[:, 1]),
      w_in, w_in)


def _slc_from_cmp(n_blk, n_chunk):
    cs = np.arange(n_chunk)[None, :] * CMP_STRIDE
    ss = np.arange(n_blk)[:, None] * SLC_LEN
    ov = np.clip(np.minimum(cs + CMP_LEN, ss + SLC_LEN) - np.maximum(cs, ss), 0, None)
    return jnp.asarray(ov / CMP_LEN, dtype=BF16)


def _token_mixer_block(x2, pos_col, mod, g_pre, g_post, w_in, cmp_pos_k, cmp_k_w1, cmp_k_w2,
                       cmp_pos_v, cmp_v_w1, cmp_v_w2, pool_w, pool_scale, w_pool_up, w_nsa_up, w_out):
    T = x2.shape[0]
    half = jnp.arange(ROT_HALF, dtype=F32)
    inv = jnp.power(jnp.float32(ROPE_THETA), -half * 2.0 / (2 * ROT_HALF))
    invf = jnp.concatenate([inv, inv, jnp.zeros((HEAD_PAD - 2 * ROT_HALF,), F32)])[None, :]

    proj = _proj(x2, g_pre, mod, _pack_w_in(w_in))
    qt, ks_r, kw_r, vs_t, vw_t, gate_t = _prep(proj, pos_col, invf)

    n_chunk = T // CMP_STRIDE
    pos_cmp = jnp.pad(pos_col[CMP_LEN - 1::CMP_STRIDE], ((0, 1), (0, 0)))

    def pad_lanes(a, d):
        return jnp.pad(a, [(0, 0)] * (a.ndim - 1) + [(0, HEAD_PAD - d)])

    def w1_pack(w1, d):
        w1 = w1.reshape(CMP_LEN, d, w1.shape[1])
        return jnp.pad(w1, ((0, 0), (0, HEAD_PAD - d), (0, 0))).astype(BF16)

    k_cmp = _compress(proj, C_KC, pad_lanes(cmp_pos_k, D_QK), w1_pack(cmp_k_w1, D_QK),
                      pad_lanes(cmp_k_w2, D_QK).astype(BF16), pos_cmp, invf, True)
    v_cmp = _compress(proj, C_VC, pad_lanes(cmp_pos_v, D_V), w1_pack(cmp_v_w1, D_V),
                      pad_lanes(cmp_v_w2, D_V).T.astype(BF16), pos_cmp, invf, False)

    n_blk = max(T // SLC_LEN, MASK_ROWS)
    o = _attention(qt, k_cmp, v_cmp, ks_r, vs_t, kw_r, vw_t, gate_t, _slc_from_cmp(n_blk, n_chunk))

    y = _mix(proj, o, pool_w.astype(BF16), pool_scale, w_pool_up.astype(BF16), w_nsa_up.astype(BF16))
    return _outproj(y, w_out.astype(BF16), x2, g_post, mod)


ROUTE_TM = 256
EXPERTS_PER_GROUP = N_EXPERTS // N_EXPERT_GROUPS


def _split3(a):
    hi = a.astype(BF16)
    r1 = a - hi.astype(F32)
    mid = r1.astype(BF16)
    lo = (r1 - mid.astype(F32)).astype(BF16)
    return hi, mid, lo


def _dot_f32_nt(a, b):
    a3, b3 = _split3(a), _split3(b)
    out = None
    for i, j in ((2, 0), (1, 1), (0, 2), (1, 0), (0, 1), (0, 0)):
        term = lax.dot_general(a3[i], b3[j], (((1,), (1,)), ((), ())), preferred_element_type=F32)
        out = term if out is None else out + term
    return out


def _first_row_of(mask, row, n):
    return jnp.min(jnp.where(mask, row, n), axis=0, keepdims=True)


def _route_kernel(x_ref, g_ref, sc_ref, sh_ref, wr_ref, rb_ref,
                  hf_ref, hb_ref, sel_ref, wd_ref, rank_ref, cnt_ref, carry_ref):
    i = pl.program_id(0)
    tm = x_ref.shape[0]

    @pl.when(i == 0)
    def _():
        carry_ref[...] = jnp.zeros_like(carry_ref)

    h = _modulated_norm(x_ref[...], g_ref[...], sc_ref[...], sh_ref[...])
    hf_ref[...] = h
    hb_ref[...] = h.astype(BF16)
    logits = _dot_f32_nt(wr_ref[...], h)
    aff = jax.nn.sigmoid(logits)
    biased = aff + rb_ref[...]
    row = lax.broadcasted_iota(jnp.int32, (N_EXPERTS, 1), 0)
    row_g = lax.broadcasted_iota(jnp.int32, (EXPERTS_PER_GROUP, 1), 0)

    gs = []
    for gi in range(N_EXPERT_GROUPS):
        v = biased[gi * EXPERTS_PER_GROUP:(gi + 1) * EXPERTS_PER_GROUP, :]
        m1 = jnp.max(v, axis=0, keepdims=True)
        i1 = _first_row_of(v == m1, row_g, EXPERTS_PER_GROUP)
        m2 = jnp.max(jnp.where(row_g == i1, REMOVED, v), axis=0, keepdims=True)
        gs.append(m1 + m2)
    keep = []
    for gi in range(N_EXPERT_GROUPS):
        ahead = jnp.zeros((1, tm), jnp.int32)
        for gj in range(N_EXPERT_GROUPS):
            if gj < gi:
                ahead += (gs[gj] >= gs[gi]).astype(jnp.int32)
            elif gj > gi:
                ahead += (gs[gj] > gs[gi]).astype(jnp.int32)
        keep.append(jnp.broadcast_to(ahead < TOPK_GROUPS, (EXPERTS_PER_GROUP, tm)))
    masked = jnp.where(jnp.concatenate(keep, axis=0), biased, NEG_INF)
    sel = jnp.zeros((N_EXPERTS, tm), F32)
    for _ in range(TOP_K):
        best = jnp.max(masked, axis=0, keepdims=True)
        hit = row == _first_row_of(masked == best, row, N_EXPERTS)
        sel = jnp.where(hit, 1.0, sel)
        masked = jnp.where(hit, REMOVED, masked)
    w = aff * sel
    wd_ref[...] = w / jnp.sum(w, axis=0, keepdims=True) * ROUTED_SCALE
    sel_ref[...] = sel

    r_i = lax.broadcasted_iota(jnp.int32, (tm, tm), 0)
    c_i = lax.broadcasted_iota(jnp.int32, (tm, tm), 1)
    before = jnp.where(r_i < c_i, 1.0, 0.0).astype(BF16)
    carry = carry_ref[...]
    rank_ref[...] = jnp.dot(sel.astype(BF16), before, preferred_element_type=F32) + carry
    carry = carry + jnp.sum(sel, axis=1, keepdims=True)
    carry_ref[...] = carry
    cnt_ref[...] = carry


def _route(x1, g_pre, mod, w_router, router_bias):
    T = x1.shape[0]
    tm = min(ROUTE_TM, T)
    row = pl.BlockSpec((tm, D_MODEL), lambda i: (i, 0))
    ecol = pl.BlockSpec((N_EXPERTS, tm), lambda i: (0, i))
    e_shape = jax.ShapeDtypeStruct((N_EXPERTS, T), F32)
    return pl.pallas_call(
        _route_kernel,
        grid=(T // tm,),
        in_specs=[row, _const_spec((1, D_MODEL)), _mod_spec(4), _mod_spec(3),
                  _const_spec((N_EXPERTS, D_MODEL)), _const_spec((N_EXPERTS, 1))],
        out_specs=[row, row, ecol, ecol, ecol, _const_spec((N_EXPERTS, 1))],
        out_shape=[jax.ShapeDtypeStruct((T, D_MODEL), F32), jax.ShapeDtypeStruct((T, D_MODEL), BF16),
                   e_shape, e_shape, e_shape, jax.ShapeDtypeStruct((N_EXPERTS, 1), F32)],
        scratch_shapes=[pltpu.VMEM((N_EXPERTS, 1), F32)],
        compiler_params=_params("arbitrary"),
        name="moe_route",
    )(x1, g_pre, mod, mod, w_router.T, router_bias.reshape(N_EXPERTS, 1))


def _dest_kernel(sel_ref, wd_ref, rank_ref, ps_ref, dest_ref, w8_ref):
    tm = sel_ref.shape[1]
    slot = ps_ref[...] + rank_ref[...]
    wd = wd_ref[...]
    row = lax.broadcasted_iota(jnp.int32, (N_EXPERTS, 1), 0)
    dests, weights = [], []
    rem = sel_ref[...]
    for k in range(TOP_K):
        hit = row == _first_row_of(rem > 0.5, row, N_EXPERTS)
        dests.append(jnp.sum(jnp.where(hit, slot, 0.0), axis=0, keepdims=True))
        weights.append(jnp.sum(jnp.where(hit, wd, 0.0), axis=0, keepdims=True))
        rem = jnp.where(hit, 0.0, rem)
    dest_ref[...] = jnp.concatenate(dests, axis=0).astype(jnp.int32)
    w_pad = jnp.concatenate(weights + [jnp.zeros((HEAD_PAD - TOP_K, tm), F32)], axis=0)
    w8_ref[...] = w_pad.T[:, 0:TOP_K]


def _dest(sel_t, wd_t, rank_t, starts_col):
    T = sel_t.shape[1]
    tm = min(ROUTE_TM, T)
    ecol = pl.BlockSpec((N_EXPERTS, tm), lambda i: (0, i))
    return pl.pallas_call(
        _dest_kernel,
        grid=(T // tm,),
        in_specs=[ecol, ecol, ecol, _const_spec((N_EXPERTS, 1))],
        out_specs=[pl.BlockSpec((TOP_K, tm), lambda i: (0, i)), pl.BlockSpec((tm, TOP_K), lambda i: (i, 0))],
        out_shape=[jax.ShapeDtypeStruct((TOP_K, T), jnp.int32), jax.ShapeDtypeStruct((T, TOP_K), F32)],
        compiler_params=_params("arbitrary"),
        name="moe_dest",
    )(sel_t, wd_t, rank_t, starts_col)


DISPATCH_TM = 256


def _dispatch_kernel(dest_ref, h_ref, hb_ref, wsg_ref, wsu_ref, wsd_ref, xs_hbm, shared_ref, sem):
    tm = h_ref.shape[0]

    def body(t, carry):
        for k in range(TOP_K):
            pltpu.make_async_copy(h_ref.at[pl.ds(t, 1), :], xs_hbm.at[pl.ds(dest_ref[k, t], 1), :], sem).start()
        return carry

    lax.fori_loop(0, tm, body, 0)
    shared_ref[...] = _swiglu(hb_ref[...], wsg_ref[...], wsu_ref[...], wsd_ref[...])
    for k in range(TOP_K):
        pltpu.make_async_copy(h_ref, xs_hbm.at[pl.ds(0, tm), :], sem).wait()


def _dispatch(dest_t, hf, hb, wsg, wsu, wsd, n_slots):
    T = hf.shape[0]
    tm = min(DISPATCH_TM, T)
    row = pl.BlockSpec((tm, D_MODEL), lambda i: (i, 0))
    return pl.pallas_call(
        _dispatch_kernel,
        grid=(T // tm,),
        in_specs=[pl.BlockSpec((TOP_K, tm), lambda i: (0, i), memory_space=pltpu.SMEM), row, row,
                  _const_spec(wsg.shape), _const_spec(wsu.shape), _const_spec(wsd.shape)],
        out_specs=[pl.BlockSpec(memory_space=pl.ANY), row],
        out_shape=[jax.ShapeDtypeStruct((n_slots, D_MODEL), F32), jax.ShapeDtypeStruct((T, D_MODEL), F32)],
        scratch_shapes=[pltpu.SemaphoreType.DMA(())],
        compiler_params=pltpu.CompilerParams(dimension_semantics=("arbitrary",), vmem_limit_bytes=VMEM_LIMIT,
                                             has_side_effects=True),
        name="moe_dispatch",
    )(dest_t, hf, hb, wsg, wsu, wsd)


EXPERT_BLOCK = 256


def _swiglu(x, wg, wu, wd):
    g = jnp.dot(x, wg, preferred_element_type=F32)
    u = jnp.dot(x, wu, preferred_element_type=F32)
    hid = (g * jax.nn.sigmoid(g)) * u
    return jnp.dot(hid.astype(BF16), wd, preferred_element_type=F32)


def _expert_items(counts, n_items):
    ends = jnp.cumsum(counts)
    starts = ends - counts
    first_blk = starts // EXPERT_BLOCK
    last_blk = (ends - 1) // EXPERT_BLOCK
    per_expert = jnp.where(counts > 0, last_blk - first_blk + 1, 0)
    item_end = jnp.cumsum(per_expert)
    n_used = item_end[-1]
    i = jnp.minimum(jnp.arange(n_items, dtype=jnp.int32), jnp.maximum(n_used - 1, 0))
    e = jnp.sum((item_end[None, :] <= i[:, None]).astype(jnp.int32), axis=1)
    e = jnp.minimum(e, N_EXPERTS - 1)
    ids = jnp.arange(N_EXPERTS, dtype=jnp.int32)
    owner = e[:, None] == ids[None, :]
    of_item = lambda per_expert_value: jnp.sum(jnp.where(owner, per_expert_value[None, :], 0), axis=1)
    blk = of_item(first_blk) + i - of_item(item_end - per_expert)
    lo = jnp.clip(of_item(starts) - blk * EXPERT_BLOCK, 0, EXPERT_BLOCK)
    hi = jnp.clip(of_item(ends) - blk * EXPERT_BLOCK, 0, EXPERT_BLOCK)
    live = jnp.arange(n_items, dtype=jnp.int32) < n_used
    hi = jnp.where(live, hi, lo)
    new_expert = jnp.concatenate([jnp.ones((1,), jnp.bool_), e[1:] != e[:-1]])
    new_block = jnp.concatenate([jnp.ones((1,), jnp.bool_), blk[1:] != blk[:-1]])
    as_i32 = lambda a: a.astype(jnp.int32)
    used = counts > 0
    slot_of = (jnp.cumsum(as_i32(used)) - 1) % 2
    later = (ids[None, :] > ids[:, None]) & used[None, :]
    next_of = jnp.min(jnp.where(later, ids[None, :], N_EXPERTS), axis=1)
    next_of = jnp.where(next_of < N_EXPERTS, next_of, -1)
    return (as_i32(blk), as_i32(e), as_i32(lo), as_i32(hi), as_i32(new_expert & live),
            as_i32(new_block & live), as_i32(live), as_i32(of_item(slot_of)), as_i32(of_item(next_of)), starts)


def _expert_kernel(blk_ref, e_ref, lo_ref, hi_ref, newe_ref, newb_ref, live_ref, slot_ref, nexte_ref,
                   x_ref, wg_hbm, wu_hbm, wd_hbm, y_ref, wg32, wu32, wd32, wgb, wub, wdb, sem):
    i = pl.program_id(0)

    def weight_copies(e, slot):
        return (pltpu.make_async_copy(wg_hbm.at[e], wg32.at[slot], sem.at[slot]),
                pltpu.make_async_copy(wu_hbm.at[e], wu32.at[slot], sem.at[slot]),
                pltpu.make_async_copy(wd_hbm.at[e], wd32.at[slot], sem.at[slot]))

    @pl.when(live_ref[i] == 1)
    def _():
        @pl.when(newe_ref[i] == 1)
        def _():
            slot = slot_ref[i]

            @pl.when(i == 0)
            def _():
                for c in weight_copies(e_ref[i], slot):
                    c.start()

            for c in weight_copies(e_ref[i], slot):
                c.wait()
            wgb[...] = wg32[slot].astype(BF16)
            wub[...] = wu32[slot].astype(BF16)
            wdb[...] = wd32[slot].astype(BF16)

            @pl.when(nexte_ref[i] >= 0)
            def _():
                for c in weight_copies(nexte_ref[i], 1 - slot):
                    c.start()

        row = lax.broadcasted_iota(jnp.int32, (EXPERT_BLOCK, 1), 0)
        mine = (row >= lo_ref[i]) & (row < hi_ref[i])
        y = _swiglu(jnp.where(mine, x_ref[...], 0.0).astype(BF16), wgb[...], wub[...], wdb[...])

        @pl.when(newb_ref[i] == 1)
        def _():
            y_ref[...] = y

        @pl.when(newb_ref[i] == 0)
        def _():
            y_ref[...] += y


def _experts(items, xs, w_gate, w_up, w_down):
    n_items = items[0].shape[0]
    by_block = lambda i, blk, *_: (blk[i], 0)
    hbm = pl.BlockSpec(memory_space=pl.ANY)
    grid_spec = pltpu.PrefetchScalarGridSpec(
        num_scalar_prefetch=len(items),
        grid=(n_items,),
        in_specs=[pl.BlockSpec((EXPERT_BLOCK, D_MODEL), by_block), hbm, hbm, hbm],
        out_specs=pl.BlockSpec((EXPERT_BLOCK, D_MODEL), by_block),
        scratch_shapes=[pltpu.VMEM((2, D_MODEL, D_EXPERT), F32), pltpu.VMEM((2, D_MODEL, D_EXPERT), F32),
                        pltpu.VMEM((2, D_EXPERT, D_MODEL), F32),
                        pltpu.VMEM((D_MODEL, D_EXPERT), BF16), pltpu.VMEM((D_MODEL, D_EXPERT), BF16),
                        pltpu.VMEM((D_EXPERT, D_MODEL), BF16), pltpu.SemaphoreType.DMA((2,))],
    )
    return pl.pallas_call(
        _expert_kernel,
        grid_spec=grid_spec,
        out_shape=jax.ShapeDtypeStruct(xs.shape, F32),
        compiler_params=_params("arbitrary"),
        name="moe_experts",
    )(*items, xs, w_gate, w_up, w_down)


COMBINE_TM = 128


def _combine_kernel(dcur_ref, dnext_ref, w8_ref, shared_ref, x_ref, g_ref, gt_ref, y_hbm, o_ref, gbuf, sem):
    i = pl.program_id(0)
    n = pl.num_programs(0)
    tm = x_ref.shape[0]

    rows = TOP_K * tm

    def issue(d_ref, slot):
        base = slot * rows

        def body(t, carry):
            for k in range(TOP_K):
                pltpu.make_async_copy(y_hbm.at[pl.ds(d_ref[k, t], 1), :],
                                      gbuf.at[pl.ds(base + k * tm + t, 1), :], sem.at[slot]).start()
            return carry

        lax.fori_loop(0, tm, body, 0)

    @pl.when(i == 0)
    def _():
        issue(dcur_ref, 0)

    @pl.when(i + 1 < n)
    def _():
        issue(dnext_ref, (i + 1) % 2)

    slot = i % 2
    base = pl.multiple_of(slot * rows, rows)
    for k in range(TOP_K):
        pltpu.make_async_copy(y_hbm.at[pl.ds(0, tm), :], gbuf.at[pl.ds(base + k * tm, tm), :],
                              sem.at[slot]).wait()
    w8 = w8_ref[...]
    y = shared_ref[...]
    for k in range(TOP_K):
        y = y + w8[:, k:k + 1] * gbuf[pl.ds(base + k * tm, tm), :]
    o_ref[...] = x_ref[...] + gt_ref[...] * _rms(y, g_ref[...])


def _combine(dest_t, w8, shared, x1, g_post, mod, y_buf):
    T = x1.shape[0]
    tm = min(COMBINE_TM, T)
    n = T // tm
    row = pl.BlockSpec((tm, D_MODEL), lambda i: (i, 0))
    return pl.pallas_call(
        _combine_kernel,
        grid=(n,),
        in_specs=[pl.BlockSpec((TOP_K, tm), lambda i: (0, i), memory_space=pltpu.SMEM),
                  pl.BlockSpec((TOP_K, tm), lambda i: (0, jnp.minimum(i + 1, n - 1)), memory_space=pltpu.SMEM),
                  pl.BlockSpec((tm, TOP_K), lambda i: (i, 0)), row, row,
                  _const_spec((1, D_MODEL)), _mod_spec(5),
                  pl.BlockSpec(memory_space=pl.ANY)],
        out_specs=row,
        out_shape=jax.ShapeDtypeStruct((T, D_MODEL), F32),
        scratch_shapes=[pltpu.VMEM((2 * TOP_K * tm, D_MODEL), F32), pltpu.SemaphoreType.DMA((2,))],
        compiler_params=_params("arbitrary"),
        name="moe_combine",
    )(dest_t, dest_t, w8, shared, x1, g_post, mod, y_buf)


def _moe_block(x1, mod, g_pre, g_post, w_router, router_bias, w_exp_gate, w_exp_up, w_exp_down,
               w_sh_gate, w_sh_up, w_sh_down):
    T = x1.shape[0]
    hf, hb, sel_t, wd_t, rank_t, counts = _route(x1, g_pre, mod, w_router, router_bias)

    n_slots = T * TOP_K
    *items, starts = _expert_items(counts[:, 0].astype(jnp.int32), n_slots // EXPERT_BLOCK + N_EXPERTS)

    dest_t, w8 = _dest(sel_t, wd_t, rank_t, starts.astype(F32)[:, None])
    xs, shared = _dispatch(dest_t, hf, hb, w_sh_gate.astype(BF16), w_sh_up.astype(BF16),
                           w_sh_down.astype(BF16), n_slots)
    y_buf = _experts(items, xs, w_exp_gate, w_exp_up, w_exp_down)
    return _combine(dest_t, w8, shared, x1, g_post, mod, y_buf)


def kernel(x, c, positions, w_ada, b_ada, g_pre_mix, g_post_mix, g_pre_ffn, g_post_ffn, w_in, cmp_pos_k, cmp_k_w1, cmp_k_w2, cmp_pos_v, cmp_v_w1, cmp_v_w2, pool_w, pool_scale, w_pool_up, w_nsa_up, w_out, w_router, router_bias, w_exp_gate, w_exp_up, w_exp_down, w_sh_gate, w_sh_up, w_sh_down):
    B, T, D = x.shape
    assert B == 1 and D == D_MODEL
    x2 = x.reshape(T, D)
    pos_col = positions.reshape(T, 1)
    for l in range(w_ada.shape[0]):
        mod = _ada(c.reshape(D, 1), w_ada[l], b_ada[l][None, :])
        x2 = _token_mixer_block(x2, pos_col, mod, g_pre_mix[l][None, :], g_post_mix[l][None, :], w_in[l],
                                cmp_pos_k[l], cmp_k_w1[l], cmp_k_w2[l], cmp_pos_v[l], cmp_v_w1[l],
                                cmp_v_w2[l], pool_w[l], pool_scale[l][None, :], w_pool_up[l], w_nsa_up[l],
                                w_out[l])
        x2 = _moe_block(x2, mod, g_pre_ffn[l][None, :], g_post_ffn[l][None, :], w_router[l],
                        router_bias[l][None, :], w_exp_gate[l], w_exp_up[l], w_exp_down[l],
                        w_sh_gate[l], w_sh_up[l], w_sh_down[l])
    return x2.reshape(B, T, D)
```

```python
import functools

import numpy as np
import jax
import jax.numpy as jnp
from jax import lax
from jax.experimental import pallas as pl
from jax.experimental.pallas import tpu as pltpu

F32 = jnp.float32
BF16 = jnp.bfloat16

D_MODEL = 2048
POOL_WIDTH = 1024
POOL_WINDOWS = (2, 4, 8, 16)
POOL_GROUP = 256
POOL_HALO = 16
N_HEADS = 16
N_KV_GROUPS = 4
HEADS_PER_GROUP = 4
D_QK = 96
D_V = 64
HEAD_PAD = 128
ROT_HALF = 12
ROPE_THETA = 500000.0
CMP_LEN = 32
CMP_STRIDE = 16
SLC_LEN = 64
N_SELECT = 16
WINDOW = 512
Q_BLOCK = 512
N_EXPERTS = 64
TOP_K = 8
N_EXPERT_GROUPS = 8
TOPK_GROUPS = 4
D_EXPERT = 512
ROUTED_SCALE = 2.5
RMS_EPS = 1e-6
NEG_INF = -1e30
SEL_FORCE = 1e30
REMOVED = -3e38

VMEM_LIMIT = 56 * 1024 * 1024

C_GM = 0
C_Q = 4096
C_U = 6144
C_KC = 7168
C_KS = 7680
C_KW = 8192
C_VC = 8704
C_VS = 9216
C_VW = 9728
C_GN = 10240
N_COLS = 10752


def _params(*sem):
    return pltpu.CompilerParams(dimension_semantics=sem, vmem_limit_bytes=VMEM_LIMIT)


def _const_spec(shape):
    nd = len(shape)
    return pl.BlockSpec(shape, lambda *_: (0,) * nd)


ADA_TN = 1536
ADA_ROWS = 64


def _ada_kernel(c_ref, w_ref, b_ref, o_ref, s_ref):
    c = c_ref[...]
    s_ref[...] = c * jax.nn.sigmoid(c)

    def body(k, acc):
        r = pl.multiple_of(k * ADA_ROWS, ADA_ROWS)
        for j in range(ADA_ROWS // 8):
            acc = acc + w_ref[pl.ds(r + 8 * j, 8), :] * s_ref[pl.ds(r + 8 * j, 8), :]
        return acc

    acc = lax.fori_loop(0, D_MODEL // ADA_ROWS, body, jnp.zeros((8, ADA_TN), F32))
    o_ref[...] = jnp.sum(acc, axis=0, keepdims=True) + b_ref[...]


def _ada(c_col, w_ada, b_ada):
    n = w_ada.shape[1]
    return pl.pallas_call(
        _ada_kernel,
        grid=(n // ADA_TN,),
        in_specs=[_const_spec((D_MODEL, 1)),
                  pl.BlockSpec((D_MODEL, ADA_TN), lambda j: (0, j)),
                  pl.BlockSpec((1, ADA_TN), lambda j: (0, j))],
        out_specs=pl.BlockSpec((1, ADA_TN), lambda j: (0, j)),
        out_shape=jax.ShapeDtypeStruct((1, n), F32),
        scratch_shapes=[pltpu.VMEM((D_MODEL, 1), F32)],
        compiler_params=_params("arbitrary"),
        name="ada",
    )(c_col, w_ada, b_ada)


def _mod_spec(k):
    return pl.BlockSpec((1, D_MODEL), lambda *_: (0, k))


PROJ_TM = 512
PROJ_TN = 1536


def _modulated_norm(x, g, sc, sh):
    y = x * lax.rsqrt(jnp.mean(x * x, axis=-1, keepdims=True) + RMS_EPS)
    return (y * g) * (1.0 + sc) + sh


def _proj_kernel(x_ref, g_ref, sc_ref, sh_ref, w_ref, o_ref, h_ref):
    @pl.when(pl.program_id(1) == 0)
    def _():
        h_ref[...] = _modulated_norm(x_ref[...], g_ref[...], sc_ref[...], sh_ref[...]).astype(BF16)

    o_ref[...] = jnp.dot(h_ref[...], w_ref[...], preferred_element_type=F32)


def _proj(x2, g_pre, mod, w_in_p):
    T = x2.shape[0]
    tm = min(PROJ_TM, T)
    return pl.pallas_call(
        _proj_kernel,
        grid=(T // tm, N_COLS // PROJ_TN),
        in_specs=[pl.BlockSpec((tm, D_MODEL), lambda i, j: (i, 0)),
                  _const_spec((1, D_MODEL)), _mod_spec(1), _mod_spec(0),
                  pl.BlockSpec((D_MODEL, PROJ_TN), lambda i, j: (0, j))],
        out_specs=pl.BlockSpec((tm, PROJ_TN), lambda i, j: (i, j)),
        out_shape=jax.ShapeDtypeStruct((T, N_COLS), F32),
        scratch_shapes=[pltpu.VMEM((tm, D_MODEL), BF16)],
        compiler_params=_params("arbitrary", "arbitrary"),
        name="in_proj",
    )(x2, g_pre, mod, mod, w_in_p)


def _rope_tables(pos_col, invf):
    ang = pos_col.astype(F32) * invf
    cos = jnp.cos(ang)
    sin = jnp.sin(ang)
    lane = lax.broadcasted_iota(jnp.int32, ang.shape, 1)
    sin_lo = jnp.where(lane < ROT_HALF, -sin, 0.0)
    sin_hi = jnp.where((lane >= ROT_HALF) & (lane < 2 * ROT_HALF), sin, 0.0)
    return cos, sin_lo, sin_hi


def _rope_head(x, tabs):
    cos, sin_lo, sin_hi = tabs
    return (x * cos + pltpu.roll(x, HEAD_PAD - ROT_HALF, 1) * sin_lo
            + pltpu.roll(x, ROT_HALF, 1) * sin_hi)


PREP_TM = 256


LOG2E = 1.4426950408889634
Q_SCALE = (D_QK ** -0.5) * LOG2E
MASK_LANE = D_QK
MASK_ROWS = 16
ONES_ROW = D_V
V_ROWS = 80
MASK_BIG = 1e30


def _with_ones_row(vt):
    row = lax.broadcasted_iota(jnp.int32, (HEAD_PAD, 1), 0)
    return jnp.where(row == ONES_ROW, 1.0, vt)


def _prep_kernel(pos_ref, invf_ref, q_ref, ks_ref, kw_ref, vs_ref, vw_ref, gn_ref,
                 qt_ref, kso_ref, kwo_ref, vst_ref, vwt_ref, gt_ref):
    tm = q_ref.shape[0]
    tabs = _rope_tables(pos_ref[...], invf_ref[...])
    for h in range(N_HEADS):
        sl = slice(h * HEAD_PAD, (h + 1) * HEAD_PAD)
        qt_ref[sl, :] = (_rope_head(q_ref[:, sl], tabs) * Q_SCALE).T.astype(BF16)
    row = pl.program_id(0) * tm + lax.broadcasted_iota(jnp.int32, (tm, 1), 0)
    lane = lax.broadcasted_iota(jnp.int32, (1, HEAD_PAD), 1)
    block_tag = jnp.where(lane == MASK_LANE + ((row >> 6) & (MASK_ROWS - 1)), 1.0, 0.0)
    for g in range(N_KV_GROUPS):
        sl = slice(g * HEAD_PAD, (g + 1) * HEAD_PAD)
        kso_ref[:, sl] = (_rope_head(ks_ref[:, sl], tabs) + block_tag).astype(BF16)
        kwo_ref[:, sl] = _rope_head(kw_ref[:, sl], tabs).astype(BF16)
        vst_ref[sl, :] = _with_ones_row(vs_ref[:, sl].T).astype(BF16)
        vwt_ref[sl, :] = _with_ones_row(vw_ref[:, sl].T).astype(BF16)
    gt_ref[...] = jax.nn.sigmoid(gn_ref[...]).T


def _prep(proj, pos_col, invf):
    T = proj.shape[0]
    tm = min(PREP_TM, T)
    kv = N_KV_GROUPS * HEAD_PAD
    qw = N_HEADS * HEAD_PAD

    def col(width, off):
        return pl.BlockSpec((tm, width), lambda i: (i, off // width))

    rows = lambda width: pl.BlockSpec((tm, width), lambda i: (i, 0))
    cols = lambda height: pl.BlockSpec((height, tm), lambda i: (0, i))
    return pl.pallas_call(
        _prep_kernel,
        grid=(T // tm,),
        in_specs=[pl.BlockSpec((tm, 1), lambda i: (i, 0)), _const_spec((1, HEAD_PAD)),
                  col(qw, C_Q), col(kv, C_KS), col(kv, C_KW), col(kv, C_VS), col(kv, C_VW),
                  col(HEAD_PAD, C_GN)],
        out_specs=[cols(qw), rows(kv), rows(kv), cols(kv), cols(kv), cols(HEAD_PAD)],
        out_shape=[jax.ShapeDtypeStruct((qw, T), BF16), jax.ShapeDtypeStruct((T, kv), BF16),
                   jax.ShapeDtypeStruct((T, kv), BF16), jax.ShapeDtypeStruct((kv, T), BF16),
                   jax.ShapeDtypeStruct((kv, T), BF16), jax.ShapeDtypeStruct((HEAD_PAD, T), F32)],
        compiler_params=_params("arbitrary"),
        name="nsa_prep",
    )(pos_col, invf, proj, proj, proj, proj, proj, proj)


def _cmp_kernel(x_ref, pe_ref, w1_ref, w2_ref, pos_ref, invf_ref, o_ref, *, is_key):
    nchunk = pos_ref.shape[0]
    hid_dim = w1_ref.shape[2]
    first = jnp.zeros((nchunk, hid_dim), F32)
    second = jnp.zeros((nchunk, hid_dim), F32)
    for l in range(CMP_STRIDE):
        rows = x_ref[pl.ds(l, nchunk, stride=CMP_STRIDE), :]
        first += jnp.dot((rows + pe_ref[l:l + 1, :]).astype(BF16), w1_ref[l],
                         preferred_element_type=F32)
        second += jnp.dot((rows + pe_ref[CMP_STRIDE + l:CMP_STRIDE + l + 1, :]).astype(BF16),
                          w1_ref[CMP_STRIDE + l], preferred_element_type=F32)
    second = jnp.concatenate([second[1:], jnp.zeros((1, hid_dim), F32)], axis=0)
    pre = first + second
    hid = pre * jax.nn.sigmoid(pre)
    if is_key:
        out = jnp.dot(hid.astype(BF16), w2_ref[...], preferred_element_type=F32)
        o_ref[0] = _rope_head(out, _rope_tables(pos_ref[...], invf_ref[...])).astype(BF16)
    else:
        out_t = lax.dot_general(w2_ref[...], hid.astype(BF16), (((1,), (1,)), ((), ())),
                                preferred_element_type=F32)
        o_ref[0] = _with_ones_row(out_t).astype(BF16)


def _compress(proj, col_off, pe_p, w1_p, w2_p, pos_cmp, invf, is_key):
    T = proj.shape[0]
    nchunk = T // CMP_STRIDE
    hid = w1_p.shape[2]
    out_block = (1, nchunk, HEAD_PAD) if is_key else (1, HEAD_PAD, nchunk)
    return pl.pallas_call(
        functools.partial(_cmp_kernel, is_key=is_key),
        grid=(N_KV_GROUPS,),
        in_specs=[pl.BlockSpec((T, HEAD_PAD), lambda g: (0, col_off // HEAD_PAD + g)),
                  _const_spec((CMP_LEN, HEAD_PAD)), _const_spec((CMP_LEN, HEAD_PAD, hid)),
                  _const_spec(w2_p.shape), _const_spec((nchunk, 1)), _const_spec((1, HEAD_PAD))],
        out_specs=pl.BlockSpec(out_block, lambda g: (g, 0, 0)),
        out_shape=jax.ShapeDtypeStruct((N_KV_GROUPS,) + out_block[1:], BF16),
        compiler_params=_params("arbitrary"),
        name="compress",
    )(proj, pe_p, w1_p, w2_p, pos_cmp, invf)


SLC_TK = 1024
WIN_SUB = 128
WIN_KEYS = WINDOW + WIN_SUB
R = HEADS_PER_GROUP


def _lanes4(a):
    return jnp.concatenate([a] * R, axis=1)


def _attn_kernel(qt_ref, kc_ref, vct_ref, ks_ref, vst_ref, kw_ref, vwt_ref, gate_ref, c2st_ref, o_ref,
                 qa_ref, neg_ref, acc_ref, m_ref, s_ref, cmax_ref, *, n_sel):
    ci = pl.program_id(1)
    s = ci * Q_BLOCK
    qt = jnp.concatenate([qt_ref[r * HEAD_PAD:(r + 1) * HEAD_PAD, :] for r in range(R)], axis=1)
    tok = s + lax.broadcasted_iota(jnp.int32, (1, Q_BLOCK), 1)

    n_cmp = kc_ref.shape[1]
    cmp_end = lax.broadcasted_iota(jnp.int32, (n_cmp, 1), 0) * CMP_STRIDE + (CMP_LEN - 1)
    cmp_bias = jnp.where(cmp_end <= tok, 0.0, NEG_INF)
    cmp_live = tok >= CMP_LEN - 1

    def cmp_scores(r):
        return jnp.dot(kc_ref[0], qt[:, r * Q_BLOCK:(r + 1) * Q_BLOCK], preferred_element_type=F32) + cmp_bias

    o_cmp_heads = []
    p_sum = None
    sc_next = cmp_scores(0)
    for r in range(R):
        sc = sc_next
        if r + 1 < R:
            sc_next = cmp_scores(r + 1)
        p = jnp.exp2(sc - jnp.max(sc, axis=0, keepdims=True))
        p = p * jnp.where(cmp_live, 1.0 / jnp.sum(p, axis=0, keepdims=True), 0.0)
        o_cmp_heads.append(jnp.dot(vct_ref[0, 0:V_ROWS, :], p.astype(BF16), preferred_element_type=F32))
        p_sum = p if p_sum is None else p_sum + p
    o_cmp = jnp.concatenate(o_cmp_heads, axis=1)
    p_hi = p_sum.astype(BF16)
    p_lo = (p_sum - p_hi.astype(F32)).astype(BF16)
    imp = (jnp.dot(c2st_ref[...], p_hi, preferred_element_type=F32)
           + jnp.dot(c2st_ref[...], p_lo, preferred_element_type=F32))

    n_blk = c2st_ref.shape[0]
    jj = lax.broadcasted_iota(jnp.int32, (n_blk, 1), 0)
    cur = tok >> 6
    forced = (jj == 0) | (jj == cur) | (jj == cur - 1)
    score = jnp.where(jj <= cur, jnp.where(forced, SEL_FORCE, imp), NEG_INF)
    chosen = jnp.zeros((n_blk, Q_BLOCK), jnp.bool_)
    for _ in range(n_sel):
        best = jnp.max(score, axis=0, keepdims=True)
        hit = jj == jnp.min(jnp.where(score == best, jj, n_blk), axis=0, keepdims=True)
        chosen = chosen | (hit & (best > 0.5 * NEG_INF))
        score = jnp.where(hit, REMOVED, score)
    neg_ref[...] = _lanes4(jnp.where(chosen, 0.0, -MASK_BIG)).astype(BF16)

    qa_ref[...] = qt
    acc_ref[...] = jnp.zeros_like(acc_ref)
    m_ref[...] = jnp.full_like(m_ref, NEG_INF)

    def scores_head(jt, buf, r):
        ls = slice(r * Q_BLOCK, (r + 1) * Q_BLOCK)
        k0 = pl.multiple_of(jt * SLC_TK, SLC_TK)
        if r == 0:
            blk0 = pl.multiple_of((k0 // (SLC_LEN * MASK_ROWS)) * MASK_ROWS, MASK_ROWS)
            qa_ref[MASK_LANE:MASK_LANE + MASK_ROWS, :] = neg_ref[pl.ds(blk0, MASK_ROWS), :]
        st = jnp.dot(ks_ref[pl.ds(k0, SLC_TK), :], qa_ref[:, ls], preferred_element_type=F32)
        s_ref[buf, :, ls] = st
        cmax_ref[buf, :, ls] = jnp.max(st, axis=0, keepdims=True)

    def softmax_head(jt, buf, r, bias=None):
        ls = slice(r * Q_BLOCK, (r + 1) * Q_BLOCK)
        k0 = pl.multiple_of(jt * SLC_TK, SLC_TK)
        m_old = m_ref[:, ls]
        if bias is None:
            st = s_ref[buf, :, ls]
            m_new = jnp.maximum(m_old, cmax_ref[buf, :, ls])
        else:
            st = s_ref[buf, :, ls] + bias
            m_new = jnp.maximum(m_old, jnp.max(st, axis=0, keepdims=True))
        pt = jnp.exp2(st - m_new).astype(BF16)
        acc_ref[:, ls] = (jnp.exp2(m_old - m_new) * acc_ref[:, ls]
                          + jnp.dot(vst_ref[0:V_ROWS, pl.ds(k0, SLC_TK)], pt, preferred_element_type=F32))
        m_ref[:, ls] = m_new

    n_full = s // SLC_TK
    for r in range(R):
        scores_head(0, 0, r)

    def pair(i, carry):
        for nxt, cur in ((1, 0), (0, 1)):
            for r in range(R):
                scores_head(2 * i + cur + 1, nxt, r)
                softmax_head(2 * i + cur, cur, r)
        return carry

    lax.fori_loop(0, n_full // 2, pair, 0)

    def last_tile(buf):
        kpos = n_full * SLC_TK + lax.broadcasted_iota(jnp.int32, (SLC_TK, 1), 0)
        bias = jnp.where(kpos <= tok, 0.0, NEG_INF)
        for r in range(R):
            softmax_head(n_full, buf, r, bias)

    @pl.when(n_full % 2 == 1)
    def _():
        for r in range(R):
            scores_head(n_full, 1, r)
            softmax_head(n_full - 1, 0, r)
        last_tile(1)

    @pl.when(n_full % 2 == 0)
    def _():
        last_tile(0)

    acc = acc_ref[...]
    o_slc = acc * (1.0 / acc[ONES_ROW:ONES_ROW + 1, :])

    def win_scores(b):
        sb = s + b * WIN_SUB
        tok_b = sb + lax.broadcasted_iota(jnp.int32, (1, WIN_SUB), 1)
        qt_b = jnp.concatenate([qt[:, r * Q_BLOCK + b * WIN_SUB:r * Q_BLOCK + (b + 1) * WIN_SUB]
                                for r in range(R)], axis=1)
        w0 = pl.multiple_of(jnp.maximum(sb - WINDOW, 0), WIN_SUB)
        wpos = w0 + lax.broadcasted_iota(jnp.int32, (WIN_KEYS, 1), 0)
        sw = jnp.dot(kw_ref[pl.ds(w0, WIN_KEYS), :], qt_b, preferred_element_type=F32)
        return w0, sw + _lanes4(jnp.where((wpos <= tok_b) & (wpos > tok_b - WINDOW), 0.0, NEG_INF))

    n_win = Q_BLOCK // WIN_SUB
    win_parts = []
    nxt = win_scores(0)
    for b in range(n_win):
        w0, sw = nxt
        if b + 1 < n_win:
            nxt = win_scores(b + 1)
        pw = jnp.exp2(sw - jnp.max(sw, axis=0, keepdims=True)).astype(BF16)
        ow = jnp.dot(vwt_ref[0:V_ROWS, pl.ds(w0, WIN_KEYS)], pw, preferred_element_type=F32)
        win_parts.append(ow * (1.0 / ow[ONES_ROW:ONES_ROW + 1, :]))

    heads = []
    for r in range(R):
        ls = slice(r * Q_BLOCK, (r + 1) * Q_BLOCK)
        o_win = jnp.concatenate([part[:, r * WIN_SUB:(r + 1) * WIN_SUB] for part in win_parts], axis=1)
        g0 = 3 * (pl.program_id(0) * R + r)
        o = (gate_ref[pl.ds(g0, 1), :] * o_cmp[:, ls] + gate_ref[pl.ds(g0 + 1, 1), :] * o_slc[:, ls]
             + gate_ref[pl.ds(g0 + 2, 1), :] * o_win)
        heads.append(o[0:D_V, :])
    o_ref[...] = jnp.concatenate(heads, axis=0).T.astype(BF16)


def _attention(qt, k_cmp, v_cmp_t, ks_r, vs_t, kw_r, vw_t, gate_t, c2st):
    T = ks_r.shape[0]
    n_cmp = k_cmp.shape[1]
    n_blk = c2st.shape[0]
    n_sel = min(N_SELECT, T // SLC_LEN)
    gw = R * HEAD_PAD
    rows = lambda g, ci: (0, g)
    cols = lambda g, ci: (g, 0)
    return pl.pallas_call(
        functools.partial(_attn_kernel, n_sel=n_sel),
        grid=(N_KV_GROUPS, T // Q_BLOCK),
        in_specs=[pl.BlockSpec((gw, Q_BLOCK), lambda g, ci: (g, ci)),
                  pl.BlockSpec((1, n_cmp, HEAD_PAD), lambda g, ci: (g, 0, 0)),
                  pl.BlockSpec((1, HEAD_PAD, n_cmp), lambda g, ci: (g, 0, 0)),
                  pl.BlockSpec((T, HEAD_PAD), rows), pl.BlockSpec((HEAD_PAD, T), cols),
                  pl.BlockSpec((T, HEAD_PAD), rows), pl.BlockSpec((HEAD_PAD, T), cols),
                  pl.BlockSpec((HEAD_PAD, Q_BLOCK), lambda g, ci: (0, ci)),
                  _const_spec((n_blk, n_cmp))],
        out_specs=pl.BlockSpec((Q_BLOCK, R * D_V), lambda g, ci: (ci, g)),
        out_shape=jax.ShapeDtypeStruct((T, N_HEADS * D_V), BF16),
        scratch_shapes=[pltpu.VMEM((HEAD_PAD, R * Q_BLOCK), BF16), pltpu.VMEM((n_blk, R * Q_BLOCK), BF16),
                        pltpu.VMEM((V_ROWS, R * Q_BLOCK), F32), pltpu.VMEM((1, R * Q_BLOCK), F32),
                        pltpu.VMEM((2, SLC_TK, R * Q_BLOCK), F32), pltpu.VMEM((2, 1, R * Q_BLOCK), F32)],
        compiler_params=_params("arbitrary", "arbitrary"),
        name="nsa_attention",
    )(qt, k_cmp, v_cmp_t, ks_r, vs_t, kw_r, vw_t, gate_t, c2st)


MIX_TM = 256


def _mix_kernel(u_ref, up_ref, gm_ref, o_ref, pw_ref, ps_ref, wpu_ref, wnu_ref, y_ref):
    i = pl.program_id(0)
    tm = u_ref.shape[0]
    u = u_ref[...]
    prev = up_ref[...] * (i > 0).astype(F32)
    ext = jnp.concatenate([prev, u], axis=0)
    t = i * tm + lax.broadcasted_iota(jnp.int32, (tm, 1), 0)
    parts = []
    for g, win in enumerate(POOL_WINDOWS):
        cs = slice(g * POOL_GROUP, (g + 1) * POOL_GROUP)
        run = ext[:, cs]
        k = 1
        while k < win:
            run = run[k:] + run[:-k]
            k *= 2
        lo = POOL_HALO - (win - 1)
        total = run[lo:lo + tm]
        count = jnp.minimum(t + 1, win).astype(F32)
        d = total / count - u[:, cs]
        parts.append(jnp.dot(d.astype(BF16), pw_ref[g], preferred_element_type=F32) * ps_ref[:, cs])
    a = jnp.concatenate(parts, axis=1).astype(BF16)
    ya = jnp.dot(a, wpu_ref[...], preferred_element_type=F32)
    yb = jnp.dot(o_ref[...], wnu_ref[...], preferred_element_type=F32)
    gm = jax.nn.sigmoid(gm_ref[...])
    y_ref[...] = (gm[:, :D_MODEL] * ya + gm[:, D_MODEL:] * yb).astype(BF16)


def _mix(proj, o, pool_w_b, pool_scale, w_pool_up_b, w_nsa_up_p):
    T = proj.shape[0]
    tm = min(MIX_TM, T)
    halo_blocks = tm // POOL_HALO
    return pl.pallas_call(
        _mix_kernel,
        grid=(T // tm,),
        in_specs=[pl.BlockSpec((tm, POOL_WIDTH), lambda i: (i, C_U // POOL_WIDTH)),
                  pl.BlockSpec((POOL_HALO, POOL_WIDTH),
                               lambda i: (jnp.maximum(i * halo_blocks - 1, 0), C_U // POOL_WIDTH)),
                  pl.BlockSpec((tm, 2 * D_MODEL), lambda i: (i, 0)),
                  pl.BlockSpec((tm, N_HEADS * D_V), lambda i: (i, 0)),
                  _const_spec(pool_w_b.shape), _const_spec((1, POOL_WIDTH)),
                  _const_spec(w_pool_up_b.shape), _const_spec(w_nsa_up_p.shape)],
        out_specs=pl.BlockSpec((tm, D_MODEL), lambda i: (i, 0)),
        out_shape=jax.ShapeDtypeStruct((T, D_MODEL), BF16),
        compiler_params=_params("arbitrary"),
        name="pool_merge",
    )(proj, proj, proj, o, pool_w_b, pool_scale, w_pool_up_b, w_nsa_up_p)


OUT_TM = 256


def _rms(y, g):
    return y * lax.rsqrt(jnp.mean(y * y, axis=-1, keepdims=True) + RMS_EPS) * g


def _outproj_kernel(y_ref, w_ref, x_ref, g_ref, gt_ref, o_ref):
    z = jnp.dot(y_ref[...], w_ref[...], preferred_element_type=F32)
    o_ref[...] = x_ref[...] + gt_ref[...] * _rms(z, g_ref[...])


def _outproj(y, w_out_b, x2, g_post, mod):
    T = y.shape[0]
    tm = min(OUT_TM, T)
    row = pl.BlockSpec((tm, D_MODEL), lambda i: (i, 0))
    return pl.pallas_call(
        _outproj_kernel,
        grid=(T // tm,),
        in_specs=[row, _const_spec((D_MODEL, D_MODEL)), row, _const_spec((1, D_MODEL)), _mod_spec(2)],
        out_specs=row,
        out_shape=jax.ShapeDtypeStruct((T, D_MODEL), F32),
        compiler_params=_params("arbitrary"),
        name="out_proj",
    )(y, w_out_b, x2, g_post, mod)


def _pack_table():
    sizes = (POOL_WIDTH, N_HEADS * D_QK, N_KV_GROUPS * D_QK, N_KV_GROUPS * D_V, N_KV_GROUPS * D_QK,
             N_KV_GROUPS * D_V, N_KV_GROUPS * D_QK, N_KV_GROUPS * D_V, 3 * N_HEADS, 2 * D_MODEL)
    u, q, kc, vc, ks, vs, kw, vw, gn, gm = [int(o) for o in np.cumsum((0,) + sizes)[:-1]]
    segments = [(gm, HEAD_PAD, 2 * D_MODEL // HEAD_PAD), (q, D_QK, N_HEADS), (u, HEAD_PAD, POOL_WIDTH // HEAD_PAD),
                (kc, D_QK, N_KV_GROUPS), (ks, D_QK, N_KV_GROUPS), (kw, D_QK, N_KV_GROUPS),
                (vc, D_V, N_KV_GROUPS), (vs, D_V, N_KV_GROUPS), (vw, D_V, N_KV_GROUPS),
                (gn, 3 * N_HEADS, 1)]
    tiles = [(start + i * width, width) for start, width, count in segments for i in range(count)]
    tiles += [(0, 0)] * (N_COLS // HEAD_PAD - len(tiles))
    assert len(tiles) * HEAD_PAD == N_COLS
    return np.asarray(tiles, np.int32)


def _repack_kernel(blk_ref, shift_ref, width_ref, a_ref, b_ref, o_ref):
    j = pl.program_id(0)
    shift = shift_ref[j]
    lane = lax.broadcasted_iota(jnp.int32, (1, HEAD_PAD), 1)
    back = (HEAD_PAD - shift) & (HEAD_PAD - 1)
    merged = jnp.where(lane < HEAD_PAD - shift, pltpu.roll(a_ref[...], back, 1), pltpu.roll(b_ref[...], back, 1))
    o_ref[...] = jnp.where(lane < width_ref[j], merged, 0.0).astype(BF16)


def _pack_w_in(w_in):
    k, n_src = w_in.shape
    table = _pack_table()
    last = (n_src - 1) // HEAD_PAD
    grid_spec = pltpu.PrefetchScalarGridSpec(
        num_scalar_prefetch=3,
        grid=(table.shape[0],),
        in_specs=[pl.BlockSpec((k, HEAD_PAD), lambda j, blk, *_: (0, blk[j])),
                  pl.BlockSpec((k, HEAD_PAD), lambda j, blk, *_: (0, jnp.minimum(blk[j] + 1, last)))],
        out_specs=pl.BlockSpec((k, HEAD_PAD), lambda j, *_: (0, j)),
    )
    return pl.pallas_call(
        _repack_kernel,
        grid_spec=grid_spec,
        out_shape=jax.ShapeDtypeStruct((k, N_COLS), BF16),
        compiler_params=_params("arbitrary"),
        name="repack_w_in",
    )(jnp.asarray(table[:, 0] // HEAD_PAD), jnp.asarray(table[:, 0] % HEAD_PAD), jnp.asarray(table[:, 1]),
      w_in, w_in)


def _slc_from_cmp(n_blk, n_chunk):
    cs = np.arange(n_chunk)[None, :] * CMP_STRIDE
    ss = np.arange(n_blk)[:, None] * SLC_LEN
    ov = np.clip(np.minimum(cs + CMP_LEN, ss + SLC_LEN) - np.maximum(cs, ss), 0, None)
    return jnp.asarray(ov / CMP_LEN, dtype=BF16)


def _token_mixer_block(x2, pos_col, mod, g_pre, g_post, w_in, cmp_pos_k, cmp_k_w1, cmp_k_w2,
                       cmp_pos_v, cmp_v_w1, cmp_v_w2, pool_w, pool_scale, w_pool_up, w_nsa_up, w_out):
    T = x2.shape[0]
    half = jnp.arange(ROT_HALF, dtype=F32)
    inv = jnp.power(jnp.float32(ROPE_THETA), -half * 2.0 / (2 * ROT_HALF))
    invf = jnp.concatenate([inv, inv, jnp.zeros((HEAD_PAD - 2 * ROT_HALF,), F32)])[None, :]

    proj = _proj(x2, g_pre, mod, _pack_w_in(w_in))
    qt, ks_r, kw_r, vs_t, vw_t, gate_t = _prep(proj, pos_col, invf)

    n_chunk = T // CMP_STRIDE
    pos_cmp = jnp.pad(pos_col[CMP_LEN - 1::CMP_STRIDE], ((0, 1), (0, 0)))

    def pad_lanes(a, d):
        return jnp.pad(a, [(0, 0)] * (a.ndim - 1) + [(0, HEAD_PAD - d)])

    def w1_pack(w1, d):
        w1 = w1.reshape(CMP_LEN, d, w1.shape[1])
        return jnp.pad(w1, ((0, 0), (0, HEAD_PAD - d), (0, 0))).astype(BF16)

    k_cmp = _compress(proj, C_KC, pad_lanes(cmp_pos_k, D_QK), w1_pack(cmp_k_w1, D_QK),
                      pad_lanes(cmp_k_w2, D_QK).astype(BF16), pos_cmp, invf, True)
    v_cmp = _compress(proj, C_VC, pad_lanes(cmp_pos_v, D_V), w1_pack(cmp_v_w1, D_V),
                      pad_lanes(cmp_v_w2, D_V).T.astype(BF16), pos_cmp, invf, False)

    n_blk = max(T // SLC_LEN, MASK_ROWS)
    o = _attention(qt, k_cmp, v_cmp, ks_r, vs_t, kw_r, vw_t, gate_t, _slc_from_cmp(n_blk, n_chunk))

    y = _mix(proj, o, pool_w.astype(BF16), pool_scale, w_pool_up.astype(BF16), w_nsa_up.astype(BF16))
    return _outproj(y, w_out.astype(BF16), x2, g_post, mod)


ROUTE_TM = 256
EXPERTS_PER_GROUP = N_EXPERTS // N_EXPERT_GROUPS


def _split3(a):
    hi = a.astype(BF16)
    r1 = a - hi.astype(F32)
    mid = r1.astype(BF16)
    lo = (r1 - mid.astype(F32)).astype(BF16)
    return hi, mid, lo


def _dot_f32_nt(a, b):
    a3, b3 = _split3(a), _split3(b)
    out = None
    for i, j in ((2, 0), (1, 1), (0, 2), (1, 0), (0, 1), (0, 0)):
        term = lax.dot_general(a3[i], b3[j], (((1,), (1,)), ((), ())), preferred_element_type=F32)
        out = term if out is None else out + term
    return out


def _first_row_of(mask, row, n):
    return jnp.min(jnp.where(mask, row, n), axis=0, keepdims=True)


def _route_kernel(x_ref, g_ref, sc_ref, sh_ref, wr_ref, rb_ref,
                  hf_ref, hb_ref, sel_ref, wd_ref, rank_ref, cnt_ref, carry_ref):
    i = pl.program_id(0)
    tm = x_ref.shape[0]

    @pl.when(i == 0)
    def _():
        carry_ref[...] = jnp.zeros_like(carry_ref)

    h = _modulated_norm(x_ref[...], g_ref[...], sc_ref[...], sh_ref[...])
    hf_ref[...] = h
    hb_ref[...] = h.astype(BF16)
    logits = _dot_f32_nt(wr_ref[...], h)
    aff = jax.nn.sigmoid(logits)
    biased = aff + rb_ref[...]
    row = lax.broadcasted_iota(jnp.int32, (N_EXPERTS, 1), 0)
    row_g = lax.broadcasted_iota(jnp.int32, (EXPERTS_PER_GROUP, 1), 0)

    gs = []
    for gi in range(N_EXPERT_GROUPS):
        v = biased[gi * EXPERTS_PER_GROUP:(gi + 1) * EXPERTS_PER_GROUP, :]
        m1 = jnp.max(v, axis=0, keepdims=True)
        i1 = _first_row_of(v == m1, row_g, EXPERTS_PER_GROUP)
        m2 = jnp.max(jnp.where(row_g == i1, REMOVED, v), axis=0, keepdims=True)
        gs.append(m1 + m2)
    keep = []
    for gi in range(N_EXPERT_GROUPS):
        ahead = jnp.zeros((1, tm), jnp.int32)
        for gj in range(N_EXPERT_GROUPS):
            if gj < gi:
                ahead += (gs[gj] >= gs[gi]).astype(jnp.int32)
            elif gj > gi:
                ahead += (gs[gj] > gs[gi]).astype(jnp.int32)
        keep.append(jnp.broadcast_to(ahead < TOPK_GROUPS, (EXPERTS_PER_GROUP, tm)))
    masked = jnp.where(jnp.concatenate(keep, axis=0), biased, NEG_INF)
    sel = jnp.zeros((N_EXPERTS, tm), F32)
    for _ in range(TOP_K):
        best = jnp.max(masked, axis=0, keepdims=True)
        hit = row == _first_row_of(masked == best, row, N_EXPERTS)
        sel = jnp.where(hit, 1.0, sel)
        masked = jnp.where(hit, REMOVED, masked)
    w = aff * sel
    wd_ref[...] = w / jnp.sum(w, axis=0, keepdims=True) * ROUTED_SCALE
    sel_ref[...] = sel

    r_i = lax.broadcasted_iota(jnp.int32, (tm, tm), 0)
    c_i = lax.broadcasted_iota(jnp.int32, (tm, tm), 1)
    before = jnp.where(r_i < c_i, 1.0, 0.0).astype(BF16)
    carry = carry_ref[...]
    rank_ref[...] = jnp.dot(sel.astype(BF16), before, preferred_element_type=F32) + carry
    carry = carry + jnp.sum(sel, axis=1, keepdims=True)
    carry_ref[...] = carry
    cnt_ref[...] = carry


def _route(x1, g_pre, mod, w_router, router_bias):
    T = x1.shape[0]
    tm = min(ROUTE_TM, T)
    row = pl.BlockSpec((tm, D_MODEL), lambda i: (i, 0))
    ecol = pl.BlockSpec((N_EXPERTS, tm), lambda i: (0, i))
    e_shape = jax.ShapeDtypeStruct((N_EXPERTS, T), F32)
    return pl.pallas_call(
        _route_kernel,
        grid=(T // tm,),
        in_specs=[row, _const_spec((1, D_MODEL)), _mod_spec(4), _mod_spec(3),
                  _const_spec((N_EXPERTS, D_MODEL)), _const_spec((N_EXPERTS, 1))],
        out_specs=[row, row, ecol, ecol, ecol, _const_spec((N_EXPERTS, 1))],
        out_shape=[jax.ShapeDtypeStruct((T, D_MODEL), F32), jax.ShapeDtypeStruct((T, D_MODEL), BF16),
                   e_shape, e_shape, e_shape, jax.ShapeDtypeStruct((N_EXPERTS, 1), F32)],
        scratch_shapes=[pltpu.VMEM((N_EXPERTS, 1), F32)],
        compiler_params=_params("arbitrary"),
        name="moe_route",
    )(x1, g_pre, mod, mod, w_router.T, router_bias.reshape(N_EXPERTS, 1))


def _dest_kernel(sel_ref, wd_ref, rank_ref, ps_ref, dest_ref, w8_ref):
    tm = sel_ref.shape[1]
    slot = ps_ref[...] + rank_ref[...]
    wd = wd_ref[...]
    row = lax.broadcasted_iota(jnp.int32, (N_EXPERTS, 1), 0)
    dests, weights = [], []
    rem = sel_ref[...]
    for k in range(TOP_K):
        hit = row == _first_row_of(rem > 0.5, row, N_EXPERTS)
        dests.append(jnp.sum(jnp.where(hit, slot, 0.0), axis=0, keepdims=True))
        weights.append(jnp.sum(jnp.where(hit, wd, 0.0), axis=0, keepdims=True))
        rem = jnp.where(hit, 0.0, rem)
    dest_ref[...] = jnp.concatenate(dests, axis=0).astype(jnp.int32)
    w_pad = jnp.concatenate(weights + [jnp.zeros((HEAD_PAD - TOP_K, tm), F32)], axis=0)
    w8_ref[...] = w_pad.T[:, 0:TOP_K]


def _dest(sel_t, wd_t, rank_t, starts_col):
    T = sel_t.shape[1]
    tm = min(ROUTE_TM, T)
    ecol = pl.BlockSpec((N_EXPERTS, tm), lambda i: (0, i))
    return pl.pallas_call(
        _dest_kernel,
        grid=(T // tm,),
        in_specs=[ecol, ecol, ecol, _const_spec((N_EXPERTS, 1))],
        out_specs=[pl.BlockSpec((TOP_K, tm), lambda i: (0, i)), pl.BlockSpec((tm, TOP_K), lambda i: (i, 0))],
        out_shape=[jax.ShapeDtypeStruct((TOP_K, T), jnp.int32), jax.ShapeDtypeStruct((T, TOP_K), F32)],
        compiler_params=_params("arbitrary"),
        name="moe_dest",
    )(sel_t, wd_t, rank_t, starts_col)


DISPATCH_TM = 256


def _dispatch_kernel(dest_ref, h_ref, hb_ref, wsg_ref, wsu_ref, wsd_ref, xs_hbm, shared_ref, sem):
    tm = h_ref.shape[0]

    def body(t, carry):
        for k in range(TOP_K):
            pltpu.make_async_copy(h_ref.at[pl.ds(t, 1), :], xs_hbm.at[pl.ds(dest_ref[k, t], 1), :],
                                  sem).start(priority=k % 2)
        return carry

    lax.fori_loop(0, tm, body, 0)
    shared_ref[...] = _swiglu(hb_ref[...], wsg_ref[...], wsu_ref[...], wsd_ref[...])
    for k in range(TOP_K):
        pltpu.make_async_copy(h_ref, xs_hbm.at[pl.ds(0, tm), :], sem).wait()


def _dispatch(dest_t, hf, hb, wsg, wsu, wsd, n_slots):
    T = hf.shape[0]
    tm = min(DISPATCH_TM, T)
    row = pl.BlockSpec((tm, D_MODEL), lambda i: (i, 0))
    return pl.pallas_call(
        _dispatch_kernel,
        grid=(T // tm,),
        in_specs=[pl.BlockSpec((TOP_K, tm), lambda i: (0, i), memory_space=pltpu.SMEM), row, row,
                  _const_spec(wsg.shape), _const_spec(wsu.shape), _const_spec(wsd.shape)],
        out_specs=[pl.BlockSpec(memory_space=pl.ANY), row],
        out_shape=[jax.ShapeDtypeStruct((n_slots, D_MODEL), F32), jax.ShapeDtypeStruct((T, D_MODEL), F32)],
        scratch_shapes=[pltpu.SemaphoreType.DMA(())],
        compiler_params=pltpu.CompilerParams(dimension_semantics=("arbitrary",), vmem_limit_bytes=VMEM_LIMIT,
                                             has_side_effects=True),
        name="moe_dispatch",
    )(dest_t, hf, hb, wsg, wsu, wsd)


EXPERT_BLOCK = 256


def _swiglu(x, wg, wu, wd):
    g = jnp.dot(x, wg, preferred_element_type=F32)
    u = jnp.dot(x, wu, preferred_element_type=F32)
    hid = (g * jax.nn.sigmoid(g)) * u
    return jnp.dot(hid.astype(BF16), wd, preferred_element_type=F32)


def _expert_items(counts, n_items):
    ends = jnp.cumsum(counts)
    starts = ends - counts
    first_blk = starts // EXPERT_BLOCK
    last_blk = (ends - 1) // EXPERT_BLOCK
    per_expert = jnp.where(counts > 0, last_blk - first_blk + 1, 0)
    item_end = jnp.cumsum(per_expert)
    n_used = item_end[-1]
    i = jnp.minimum(jnp.arange(n_items, dtype=jnp.int32), jnp.maximum(n_used - 1, 0))
    e = jnp.sum((item_end[None, :] <= i[:, None]).astype(jnp.int32), axis=1)
    e = jnp.minimum(e, N_EXPERTS - 1)
    ids = jnp.arange(N_EXPERTS, dtype=jnp.int32)
    owner = e[:, None] == ids[None, :]
    of_item = lambda per_expert_value: jnp.sum(jnp.where(owner, per_expert_value[None, :], 0), axis=1)
    blk = of_item(first_blk) + i - of_item(item_end - per_expert)
    lo = jnp.clip(of_item(starts) - blk * EXPERT_BLOCK, 0, EXPERT_BLOCK)
    hi = jnp.clip(of_item(ends) - blk * EXPERT_BLOCK, 0, EXPERT_BLOCK)
    live = jnp.arange(n_items, dtype=jnp.int32) < n_used
    hi = jnp.where(live, hi, lo)
    new_expert = jnp.concatenate([jnp.ones((1,), jnp.bool_), e[1:] != e[:-1]])
    new_block = jnp.concatenate([jnp.ones((1,), jnp.bool_), blk[1:] != blk[:-1]])
    as_i32 = lambda a: a.astype(jnp.int32)
    used = counts > 0
    slot_of = (jnp.cumsum(as_i32(used)) - 1) % 2
    later = (ids[None, :] > ids[:, None]) & used[None, :]
    next_of = jnp.min(jnp.where(later, ids[None, :], N_EXPERTS), axis=1)
    next_of = jnp.where(next_of < N_EXPERTS, next_of, -1)
    return (as_i32(blk), as_i32(e), as_i32(lo), as_i32(hi), as_i32(new_expert & live),
            as_i32(new_block & live), as_i32(live), as_i32(of_item(slot_of)), as_i32(of_item(next_of)), starts)


def _expert_kernel(blk_ref, e_ref, lo_ref, hi_ref, newe_ref, newb_ref, live_ref, slot_ref, nexte_ref,
                   x_ref, wg_hbm, wu_hbm, wd_hbm, y_ref, wg32, wu32, wd32, wgb, wub, wdb, sem):
    i = pl.program_id(0)

    def weight_copies(e, slot):
        return (pltpu.make_async_copy(wg_hbm.at[e], wg32.at[slot], sem.at[slot]),
                pltpu.make_async_copy(wu_hbm.at[e], wu32.at[slot], sem.at[slot]),
                pltpu.make_async_copy(wd_hbm.at[e], wd32.at[slot], sem.at[slot]))

    @pl.when(live_ref[i] == 1)
    def _():
        @pl.when(newe_ref[i] == 1)
        def _():
            slot = slot_ref[i]

            @pl.when(i == 0)
            def _():
                for c in weight_copies(e_ref[i], slot):
                    c.start()

            for c in weight_copies(e_ref[i], slot):
                c.wait()
            wgb[...] = wg32[slot].astype(BF16)
            wub[...] = wu32[slot].astype(BF16)
            wdb[...] = wd32[slot].astype(BF16)

            @pl.when(nexte_ref[i] >= 0)
            def _():
                for c in weight_copies(nexte_ref[i], 1 - slot):
                    c.start()

        row = lax.broadcasted_iota(jnp.int32, (EXPERT_BLOCK, 1), 0)
        mine = (row >= lo_ref[i]) & (row < hi_ref[i])
        y = _swiglu(jnp.where(mine, x_ref[...], 0.0).astype(BF16), wgb[...], wub[...], wdb[...])

        @pl.when(newb_ref[i] == 1)
        def _():
            y_ref[...] = y

        @pl.when(newb_ref[i] == 0)
        def _():
            y_ref[...] += y


def _experts(items, xs, w_gate, w_up, w_down):
    n_items = items[0].shape[0]
    by_block = lambda i, blk, *_: (blk[i], 0)
    hbm = pl.BlockSpec(memory_space=pl.ANY)
    grid_spec = pltpu.PrefetchScalarGridSpec(
        num_scalar_prefetch=len(items),
        grid=(n_items,),
        in_specs=[pl.BlockSpec((EXPERT_BLOCK, D_MODEL), by_block), hbm, hbm, hbm],
        out_specs=pl.BlockSpec((EXPERT_BLOCK, D_MODEL), by_block),
        scratch_shapes=[pltpu.VMEM((2, D_MODEL, D_EXPERT), F32), pltpu.VMEM((2, D_MODEL, D_EXPERT), F32),
                        pltpu.VMEM((2, D_EXPERT, D_MODEL), F32),
                        pltpu.VMEM((D_MODEL, D_EXPERT), BF16), pltpu.VMEM((D_MODEL, D_EXPERT), BF16),
                        pltpu.VMEM((D_EXPERT, D_MODEL), BF16), pltpu.SemaphoreType.DMA((2,))],
    )
    return pl.pallas_call(
        _expert_kernel,
        grid_spec=grid_spec,
        out_shape=jax.ShapeDtypeStruct(xs.shape, F32),
        compiler_params=_params("arbitrary"),
        name="moe_experts",
    )(*items, xs, w_gate, w_up, w_down)


COMBINE_TM = 128


def _combine_kernel(dcur_ref, dnext_ref, w8_ref, shared_ref, x_ref, g_ref, gt_ref, y_hbm, o_ref, gbuf, sem):
    i = pl.program_id(0)
    n = pl.num_programs(0)
    tm = x_ref.shape[0]

    rows = TOP_K * tm

    def issue(d_ref, slot):
        base = slot * rows

        def body(t, carry):
            for k in range(TOP_K):
                pltpu.make_async_copy(y_hbm.at[pl.ds(d_ref[k, t], 1), :],
                                      gbuf.at[pl.ds(base + k * tm + t, 1), :], sem.at[slot]).start()
            return carry

        lax.fori_loop(0, tm, body, 0)

    @pl.when(i == 0)
    def _():
        issue(dcur_ref, 0)

    @pl.when(i + 1 < n)
    def _():
        issue(dnext_ref, (i + 1) % 2)

    slot = i % 2
    base = pl.multiple_of(slot * rows, rows)
    for k in range(TOP_K):
        pltpu.make_async_copy(y_hbm.at[pl.ds(0, tm), :], gbuf.at[pl.ds(base + k * tm, tm), :],
                              sem.at[slot]).wait()
    w8 = w8_ref[...]
    y = shared_ref[...]
    for k in range(TOP_K):
        y = y + w8[:, k:k + 1] * gbuf[pl.ds(base + k * tm, tm), :]
    o_ref[...] = x_ref[...] + gt_ref[...] * _rms(y, g_ref[...])


def _combine(dest_t, w8, shared, x1, g_post, mod, y_buf):
    T = x1.shape[0]
    tm = min(COMBINE_TM, T)
    n = T // tm
    row = pl.BlockSpec((tm, D_MODEL), lambda i: (i, 0))
    return pl.pallas_call(
        _combine_kernel,
        grid=(n,),
        in_specs=[pl.BlockSpec((TOP_K, tm), lambda i: (0, i), memory_space=pltpu.SMEM),
                  pl.BlockSpec((TOP_K, tm), lambda i: (0, jnp.minimum(i + 1, n - 1)), memory_space=pltpu.SMEM),
                  pl.BlockSpec((tm, TOP_K), lambda i: (i, 0)), row, row,
                  _const_spec((1, D_MODEL)), _mod_spec(5),
                  pl.BlockSpec(memory_space=pl.ANY)],
        out_specs=row,
        out_shape=jax.ShapeDtypeStruct((T, D_MODEL), F32),
        scratch_shapes=[pltpu.VMEM((2 * TOP_K * tm, D_MODEL), F32), pltpu.SemaphoreType.DMA((2,))],
        compiler_params=_params("arbitrary"),
        name="moe_combine",
    )(dest_t, dest_t, w8, shared, x1, g_post, mod, y_buf)


def _moe_block(x1, mod, g_pre, g_post, w_router, router_bias, w_exp_gate, w_exp_up, w_exp_down,
               w_sh_gate, w_sh_up, w_sh_down):
    T = x1.shape[0]
    hf, hb, sel_t, wd_t, rank_t, counts = _route(x1, g_pre, mod, w_router, router_bias)

    n_slots = T * TOP_K
    *items, starts = _expert_items(counts[:, 0].astype(jnp.int32), n_slots // EXPERT_BLOCK + N_EXPERTS)

    dest_t, w8 = _dest(sel_t, wd_t, rank_t, starts.astype(F32)[:, None])
    xs, shared = _dispatch(dest_t, hf, hb, w_sh_gate.astype(BF16), w_sh_up.astype(BF16),
                           w_sh_down.astype(BF16), n_slots)
    y_buf = _experts(items, xs, w_exp_gate, w_exp_up, w_exp_down)
    return _combine(dest_t, w8, shared, x1, g_post, mod, y_buf)


def kernel(x, c, positions, w_ada, b_ada, g_pre_mix, g_post_mix, g_pre_ffn, g_post_ffn, w_in, cmp_pos_k, cmp_k_w1, cmp_k_w2, cmp_pos_v, cmp_v_w1, cmp_v_w2, pool_w, pool_scale, w_pool_up, w_nsa_up, w_out, w_router, router_bias, w_exp_gate, w_exp_up, w_exp_down, w_sh_gate, w_sh_up, w_sh_down):
    B, T, D = x.shape
    assert B == 1 and D == D_MODEL
    x2 = x.reshape(T, D)
    pos_col = positions.reshape(T, 1)
    for l in range(w_ada.shape[0]):
        mod = _ada(c.reshape(D, 1), w_ada[l], b_ada[l][None, :])
        x2 = _token_mixer_block(x2, pos_col, mod, g_pre_mix[l][None, :], g_post_mix[l][None, :], w_in[l],
                                cmp_pos_k[l], cmp_k_w1[l], cmp_k_w2[l], cmp_pos_v[l], cmp_v_w1[l],
                                cmp_v_w2[l], pool_w[l], pool_scale[l][None, :], w_pool_up[l], w_nsa_up[l],
                                w_out[l])
        x2 = _moe_block(x2, mod, g_pre_ffn[l][None, :], g_post_ffn[l][None, :], w_router[l],
                        router_bias[l][None, :], w_exp_gate[l], w_exp_up[l], w_exp_down[l],
                        w_sh_gate[l], w_sh_up[l], w_sh_down[l])
    return x2.reshape(B, T, D)
```
